```python
import jax, jax.numpy as jnp
from jax import lax
import numpy as np

D_MODEL = 1024
BATCH = 8
SEQ = 2048
DEPTH = 2

GRID_W = 64
CTX_LEN = 256
N_MIXERS = 2
N_ATTN_LAYERS = (DEPTH + N_MIXERS - 1) // N_MIXERS
N_LRU_LAYERS = DEPTH // N_MIXERS
HEAD_DIM = 64
N_HEADS = D_MODEL // HEAD_DIM
N_KV_HEADS = 4
GQA_GROUP = N_HEADS // N_KV_HEADS
WINDOW = 128
BLOCK = 128
ROPE_BASE = 10000.0
D_RNN = 1280
LRU_BLOCK_W = 256
N_LRU_BLOCKS = D_RNN // LRU_BLOCK_W
CONV_W = 4
CONV_LEFT = 2
LRU_C = 8.0
D_FF = 4 * D_MODEL
N_MOD = 6
EPS = 1e-6
NEG_INF = -1e30

kernel_name = 'hybrid_swa_rglru_diffusion_block'


def rms_norm(x, g):
    xf = x.astype(jnp.float32)
    y = xf * lax.rsqrt(jnp.mean(xf * xf, axis=-1, keepdims=True) + EPS)
    return (y * g.astype(jnp.float32)).astype(x.dtype)


def modulate(h, shift, scale):
    return h * (1 + scale) + shift


def sqrelu_mlp(h, w1, w2):
    return jnp.square(jax.nn.relu(h @ w1)) @ w2


def axial_rope_tables(n):
    rows = n // GRID_W
    row = jnp.repeat(jnp.arange(rows, dtype=jnp.float32), GRID_W)
    col = jnp.tile(jnp.arange(GRID_W, dtype=jnp.float32), rows)
    half = HEAD_DIM // 2
    inv = ROPE_BASE ** (-jnp.arange(0, half, 2, dtype=jnp.float32) / half)
    ang_r = row[:, None] * inv[None, :]
    ang_c = col[:, None] * inv[None, :]
    ang = jnp.concatenate([ang_r, ang_r, ang_c, ang_c], axis=-1)
    return jnp.cos(ang), jnp.sin(ang)


def rotate_half_axial(x):
    a1, a2, b1, b2 = jnp.split(x, 4, axis=-1)
    return jnp.concatenate([-a2, a1, -b2, b1], axis=-1)


def apply_rope(x, cos, sin):
    xf = x.astype(jnp.float32)
    out = xf * cos[None, :, None, :] + rotate_half_axial(xf) * sin[None, :, None, :]
    return out.astype(x.dtype)


def banded_window_attention(q, k, v, k_ctx, v_ctx, sink):
    B, S = q.shape[0], q.shape[1]
    nb = S // BLOCK
    f32 = jnp.float32
    qb = (q.astype(f32) * (HEAD_DIM ** -0.5)).reshape(B, nb, BLOCK, N_KV_HEADS, GQA_GROUP, HEAD_DIM)

    def band(t):
        tp = jnp.pad(t.astype(f32), ((0, 0), (BLOCK, BLOCK), (0, 0), (0, 0)))
        tp = tp.reshape(B, nb + 2, BLOCK, N_KV_HEADS, HEAD_DIM)
        return jnp.concatenate([tp[:, :-2], tp[:, 1:-1], tp[:, 2:]], axis=2)

    kb, vb = band(k), band(v)
    blk = jnp.arange(nb)[:, None, None]
    qpos = blk * BLOCK + jnp.arange(BLOCK)[None, :, None]
    kpos = blk * BLOCK - BLOCK + jnp.arange(3 * BLOCK)[None, None, :]
    valid = (kpos >= 0) & (kpos < S) & (jnp.abs(kpos - qpos) <= WINDOW)
    kc = k_ctx.astype(f32)
    vc = v_ctx.astype(f32)
    n_ctx = kc.shape[1]
    n_loc = 3 * BLOCK
    sink_col = jnp.broadcast_to(sink.astype(f32).reshape(1, N_KV_HEADS, GQA_GROUP, 1, 1),
                                (B, N_KV_HEADS, GQA_GROUP, BLOCK, 1))

    def one_block(args):
        qi, ki, vi, mi = args
        s_loc = jnp.where(mi, jnp.einsum('bqkgd,bskd->bkgqs', qi, ki), NEG_INF)
        s_ctx = jnp.einsum('bqkgd,bckd->bkgqc', qi, kc)
        p = jax.nn.softmax(jnp.concatenate([s_loc, s_ctx, sink_col], axis=-1), axis=-1)
        return (jnp.einsum('bkgqs,bskd->bqkgd', p[..., :n_loc], vi)
                + jnp.einsum('bkgqc,bckd->bqkgd', p[..., n_loc:n_loc + n_ctx], vc))

    out = lax.map(one_block, (jnp.moveaxis(qb, 1, 0), jnp.moveaxis(kb, 1, 0),
                              jnp.moveaxis(vb, 1, 0), valid))
    out = jnp.moveaxis(out, 0, 1).reshape(B, S, N_HEADS * HEAD_DIM)
    return out.astype(q.dtype)


def context_attention(q_c, k_c, v_c, sink):
    B, C = q_c.shape[0], q_c.shape[1]
    f32 = jnp.float32
    qf = (q_c.astype(f32) * (HEAD_DIM ** -0.5)).reshape(B, C, N_KV_HEADS, GQA_GROUP, HEAD_DIM)
    s = jnp.einsum('bqkgd,bckd->bkgqc', qf, k_c.astype(f32))
    sink_col = jnp.broadcast_to(sink.astype(f32).reshape(1, N_KV_HEADS, GQA_GROUP, 1, 1),
                                (B, N_KV_HEADS, GQA_GROUP, C, 1))
    p = jax.nn.softmax(jnp.concatenate([s, sink_col], axis=-1), axis=-1)
    out = jnp.einsum('bkgqc,bckd->bqkgd', p[..., :C], v_c.astype(f32))
    return out.reshape(B, C, N_HEADS * HEAD_DIM).astype(q_c.dtype)


def attention_mixer(h_x, h_c, w_qkv, w_o, sink, cos, sin, need_ctx):
    nq = N_HEADS * HEAD_DIM
    w_q, w_kv = w_qkv[:, :nq], w_qkv[:, nq:]

    def kv(h):
        kvh = (h @ w_kv).reshape(h.shape[0], h.shape[1], 2, N_KV_HEADS, HEAD_DIM)
        return kvh[:, :, 0], kvh[:, :, 1]

    B, S = h_x.shape[0], h_x.shape[1]
    q_x = apply_rope((h_x @ w_q).reshape(B, S, N_HEADS, HEAD_DIM), cos, sin)
    k_x, v_x = kv(h_x)
    k_x = apply_rope(k_x, cos, sin)
    k_c, v_c = kv(h_c)
    y_x = banded_window_attention(q_x, k_x, v_x, k_c, v_c, sink) @ w_o
    y_c = None
    if need_ctx:
        q_c = (h_c @ w_q).reshape(h_c.shape[0], h_c.shape[1], N_HEADS, HEAD_DIM)
        y_c = context_attention(q_c, k_c, v_c, sink) @ w_o
    return y_x, y_c


def centred_dwconv(u, w, b):
    T = u.shape[1]
    up = jnp.pad(u, ((0, 0), (CONV_LEFT, CONV_W - 1 - CONV_LEFT), (0, 0)))
    out = b
    for tap in range(CONV_W):
        out = out + up[:, tap:tap + T] * w[tap]
    return out


def rglru_coeffs(u, w_a, b_a, w_i, b_i, lam):
    B, T, W = u.shape
    f32 = jnp.float32
    uf = u.astype(f32)
    ub = uf.reshape(B, T, N_LRU_BLOCKS, LRU_BLOCK_W)
    r = jax.nn.sigmoid(jnp.einsum('btnk,nkj->btnj', ub, w_a.astype(f32)).reshape(B, T, W) + b_a.astype(f32))
    i = jax.nn.sigmoid(jnp.einsum('btnk,nkj->btnj', ub, w_i.astype(f32)).reshape(B, T, W) + b_i.astype(f32))
    log_a = -LRU_C * jax.nn.softplus(-lam.astype(f32)) * r
    a = jnp.exp(log_a)
    bx = jnp.sqrt(-jnp.expm1(2 * log_a)) * (i * uf)
    return a, bx


def linear_scan(a, bx, h0, reverse, emit):
    def step(h, ab):
        a_t, b_t = ab
        h = a_t * h + b_t
        return h, (h if emit else None)

    h_last, ys = lax.scan(step, h0, (jnp.swapaxes(a, 0, 1), jnp.swapaxes(bx, 0, 1)), reverse=reverse)
    ys = jnp.swapaxes(ys, 0, 1) if emit else None
    return ys, h_last


def rglru_mixer(h_x, h_c, w_in, conv_w, conv_b, w_a, b_a, w_i, b_i, lam, w_out, need_ctx):
    w_gate, w_rec = w_in[:, :D_RNN], w_in[:, D_RNN:]
    u_x = centred_dwconv(h_x @ w_rec, conv_w, conv_b)
    u_c = centred_dwconv(h_c @ w_rec, conv_w, conv_b)
    B = h_x.shape[0]
    rec_x, rec_c = [], []
    for d in range(2):
        rev = d == 1
        a_c, bx_c = rglru_coeffs(u_c, w_a[d], b_a[d], w_i[d], b_i[d], lam[d])
        ys_c, hc_last = linear_scan(a_c, bx_c, jnp.zeros((B, D_RNN), jnp.float32), rev, need_ctx)
        a_x, bx_x = rglru_coeffs(u_x, w_a[d], b_a[d], w_i[d], b_i[d], lam[d])
        ys_x, _ = linear_scan(a_x, bx_x, hc_last, rev, True)
        rec_x.append(ys_x)
        rec_c.append(ys_c)
    y_x = (jax.nn.gelu(h_x @ w_gate) * (rec_x[0] + rec_x[1]).astype(h_x.dtype)) @ w_out
    y_c = None
    if need_ctx:
        y_c = (jax.nn.gelu(h_c @ w_gate) * (rec_c[0] + rec_c[1]).astype(h_c.dtype)) @ w_out
    return y_x, y_c


def setup_inputs(seed: int = 0) -> dict:
    key = jax.random.key(seed)
    ks = jax.random.split(key, 24)
    f32 = jnp.float32
    nrm = lambda k, shape, s: jax.random.normal(k, shape, f32) * s
    nqkv = (N_HEADS + 2 * N_KV_HEADS) * HEAD_DIM
    u = jax.random.uniform(ks[20], (N_LRU_LAYERS, 2, D_RNN), f32, 0.9, 0.999)
    s = u ** (1.0 / LRU_C)
    lam = jnp.log(s) - jnp.log1p(-s)
    return {
        'x': nrm(ks[0], (BATCH, SEQ, D_MODEL), 1.0),
        'c': nrm(ks[1], (BATCH, D_MODEL), 1.0),
        'ctx': nrm(ks[2], (BATCH, CTX_LEN, D_MODEL), 1.0),
        'c_ctx': nrm(ks[3], (D_MODEL,), 1.0),
        'ada_w': nrm(ks[4], (DEPTH, D_MODEL, N_MOD * D_MODEL), 0.5 * D_MODEL ** -0.5),
        'ada_b': nrm(ks[5], (DEPTH, N_MOD * D_MODEL), 0.01),
        'norm_g': 1.0 + nrm(ks[6], (DEPTH, 4, D_MODEL), 0.05),
        'mlp_w1': nrm(ks[7], (DEPTH, D_MODEL, D_FF), D_MODEL ** -0.5),
        'mlp_w2': nrm(ks[8], (DEPTH, D_FF, D_MODEL), D_FF ** -0.5),
        'attn_w_qkv': nrm(ks[9], (N_ATTN_LAYERS, D_MODEL, nqkv), D_MODEL ** -0.5),
        'attn_w_o': nrm(ks[10], (N_ATTN_LAYERS, N_HEADS * HEAD_DIM, D_MODEL), (N_HEADS * HEAD_DIM) ** -0.5),
        'attn_sink': nrm(ks[11], (N_ATTN_LAYERS, N_HEADS), 0.5),
        'lru_w_in': nrm(ks[12], (N_LRU_LAYERS, D_MODEL, 2 * D_RNN), D_MODEL ** -0.5),
        'lru_conv_w': nrm(ks[13], (N_LRU_LAYERS, CONV_W, D_RNN), CONV_W ** -0.5),
        'lru_conv_b': nrm(ks[14], (N_LRU_LAYERS, D_RNN), 0.01),
        'lru_w_a': nrm(ks[15], (N_LRU_LAYERS, 2, N_LRU_BLOCKS, LRU_BLOCK_W, LRU_BLOCK_W), LRU_BLOCK_W ** -0.5),
        'lru_b_a': nrm(ks[16], (N_LRU_LAYERS, 2, D_RNN), 0.01),
        'lru_w_i': nrm(ks[17], (N_LRU_LAYERS, 2, N_LRU_BLOCKS, LRU_BLOCK_W, LRU_BLOCK_W), LRU_BLOCK_W ** -0.5),
        'lru_b_i': nrm(ks[18], (N_LRU_LAYERS, 2, D_RNN), 0.01),
        'lru_lam': lam,
        'lru_w_out': nrm(ks[19], (N_LRU_LAYERS, D_RNN, D_MODEL), D_RNN ** -0.5),
    }


def reference(x, c, ctx, c_ctx, ada_w, ada_b, norm_g, mlp_w1, mlp_w2, attn_w_qkv, attn_w_o, attn_sink,
              lru_w_in, lru_conv_w, lru_conv_b, lru_w_a, lru_b_a, lru_w_i, lru_b_i, lru_lam, lru_w_out):
    n = x.shape[1]
    cos, sin = axial_rope_tables(n)
    silu_c = jax.nn.silu(c)
    silu_cc = jax.nn.silu(c_ctx)
    for i in range(DEPTH):
        last = i == DEPTH - 1
        mx = jnp.split((silu_c @ ada_w[i] + ada_b[i])[:, None, :], N_MOD, axis=-1)
        mc = jnp.split(silu_cc @ ada_w[i] + ada_b[i], N_MOD, axis=-1)
        g = norm_g[i]
        h_x = modulate(rms_norm(x, g[0]), mx[0], mx[1])
        h_c = modulate(rms_norm(ctx, g[0]), mc[0], mc[1])
        j = i // N_MIXERS
        if i % N_MIXERS == 0:
            y_x, y_c = attention_mixer(h_x, h_c, attn_w_qkv[j], attn_w_o[j], attn_sink[j], cos, sin, not last)
        else:
            y_x, y_c = rglru_mixer(h_x, h_c, lru_w_in[j], lru_conv_w[j], lru_conv_b[j], lru_w_a[j], lru_b_a[j],
                                   lru_w_i[j], lru_b_i[j], lru_lam[j], lru_w_out[j], not last)
        x = x + mx[2] * rms_norm(y_x, g[1])
        x = x + mx[5] * rms_norm(sqrelu_mlp(modulate(rms_norm(x, g[2]), mx[3], mx[4]), mlp_w1[i], mlp_w2[i]), g[3])
        if not last:
            ctx = ctx + mc[2] * rms_norm(y_c, g[1])
            ctx = ctx + mc[5] * rms_norm(sqrelu_mlp(modulate(rms_norm(ctx, g[2]), mc[3], mc[4]),
                                                    mlp_w1[i], mlp_w2[i]), g[3])
    return x
```

```python
import functools

import jax
import jax.numpy as jnp
from jax import lax
from jax.experimental import pallas as pl
from jax.experimental.pallas import tpu as pltpu

D_MODEL = 1024
BATCH = 8
SEQ = 2048
GRID_W = 64
CTX_LEN = 256
HEAD_DIM = 64
N_HEADS = 16
N_KV_HEADS = 4
GQA_GROUP = N_HEADS // N_KV_HEADS
WINDOW = 128
BLOCK = 128
ROPE_BASE = 10000.0
D_RNN = 1280
LRU_BLOCK_W = 256
N_LRU_BLOCKS = D_RNN // LRU_BLOCK_W
CONV_W = 4
LRU_C = 8.0
D_FF = 4 * D_MODEL
N_MOD = 6
EPS = 1e-6
NEG_INF = -1e30

D_Q = N_HEADS * HEAD_DIM
D_KV = N_KV_HEADS * HEAD_DIM
LANES = 128
SUBLANES = 8
TOKEN_TILE = 512
FF_CHUNK = 1024
SCAN_T = 64
VMEM_LIMIT = 56 * 1024 * 1024

F32 = jnp.float32
BF16 = jnp.bfloat16


def _rms(x, g):
    ms = jnp.mean(x * x, axis=-1, keepdims=True)
    return x * lax.rsqrt(ms + EPS) * g


def _slab(x):
    return x.reshape(x.shape[0] // SUBLANES, SUBLANES, x.shape[1])


def _modulate(h, shift8, scale8):
    out = _slab(h) * (1.0 + scale8)[None] + shift8[None]
    return out.reshape(h.shape)


def _gated_add(x, gate8, y):
    out = _slab(x) + gate8[None] * _slab(y)
    return out.reshape(x.shape)


def _const_spec(shape):
    n = len(shape)
    return pl.BlockSpec(shape, lambda *_: (0,) * n, pipeline_mode=pl.Buffered(1))


def _params(sem):
    return pltpu.CompilerParams(dimension_semantics=sem, vmem_limit_bytes=VMEM_LIMIT)


def _mod_kernel(c_ref, w_ref, b_ref, o_ref):
    s = jax.nn.silu(c_ref[...]).astype(BF16)
    o_ref[0] = jnp.dot(s, w_ref[0].astype(BF16), preferred_element_type=F32) + b_ref[0]


def _mod_call(c16, ada_w, ada_b):
    depth = ada_w.shape[0]
    nt = 1024
    return pl.pallas_call(
        _mod_kernel,
        grid=(depth, N_MOD * D_MODEL // nt),
        in_specs=[
            pl.BlockSpec((16, D_MODEL), lambda l, j: (0, 0)),
            pl.BlockSpec((1, D_MODEL, nt), lambda l, j: (l, 0, j)),
            pl.BlockSpec((1, 1, nt), lambda l, j: (l, 0, j)),
        ],
        out_specs=pl.BlockSpec((1, 16, nt), lambda l, j: (l, 0, j)),
        out_shape=jax.ShapeDtypeStruct((depth, 16, N_MOD * D_MODEL), F32),
        compiler_params=_params(("arbitrary", "arbitrary")),
        name="adaln_mod",
    )(c16, ada_w, ada_b.reshape(depth, 1, N_MOD * D_MODEL))


def _qkv_kernel(*refs, rope):
    if rope:
        x_ref, mod_ref, g_ref, w_ref, cos_ref, sa_ref, sb_ref, q_ref, k_ref, v_ref = refs
    else:
        x_ref, mod_ref, g_ref, w_ref, q_ref, k_ref, v_ref = refs
    h = _modulate(_rms(x_ref[...], g_ref[0:1, :]), mod_ref[0, 0], mod_ref[0, 1])
    y = jnp.dot(h.astype(BF16), w_ref[...], preferred_element_type=F32)
    if rope:
        cos, sa, sb = cos_ref[...], sa_ref[...], sb_ref[...]
    for c in range((D_Q + D_KV) // LANES):
        yc = y[:, c * LANES:(c + 1) * LANES]
        if rope:
            yc = yc * cos + pltpu.roll(yc, LANES - 16, 1) * sa + pltpu.roll(yc, 16, 1) * sb
        if c < D_Q // LANES:
            q_ref[:, c * LANES:(c + 1) * LANES] = yc.astype(BF16)
        else:
            c2 = c - D_Q // LANES
            k_ref[:, c2 * LANES:(c2 + 1) * LANES] = yc.astype(BF16)
    v_ref[...] = y[:, D_Q + D_KV:].astype(BF16)


def _qkv_call(x2, mod, g, w_qkv, tables, tiles_per_group):
    n = x2.shape[0]
    tm = TOKEN_TILE
    rope = tables is not None
    in_specs = [
        pl.BlockSpec((tm, D_MODEL), lambda i: (i, 0)),
        pl.BlockSpec((1, N_MOD, SUBLANES, D_MODEL), lambda i: (i // tiles_per_group, 0, 0, 0)),
        _const_spec((4, D_MODEL)),
        _const_spec((D_MODEL, D_Q + 2 * D_KV)),
    ]
    args = [x2, mod, g, w_qkv]
    if rope:
        nt = SEQ // tm
        in_specs += [pl.BlockSpec((tm, LANES), lambda i: (i % nt, 0))] * 3
        args += list(tables)
    return pl.pallas_call(
        functools.partial(_qkv_kernel, rope=rope),
        grid=(n // tm,),
        in_specs=in_specs,
        out_specs=[
            pl.BlockSpec((tm, D_Q), lambda i: (i, 0)),
            pl.BlockSpec((tm, D_KV), lambda i: (i, 0)),
            pl.BlockSpec((tm, D_KV), lambda i: (i, 0)),
        ],
        out_shape=[
            jax.ShapeDtypeStruct((n, D_Q), BF16),
            jax.ShapeDtypeStruct((n, D_KV), BF16),
            jax.ShapeDtypeStruct((n, D_KV), BF16),
        ],
        compiler_params=_params(("parallel",)),
        name="qkv_rope" if rope else "qkv_ctx",
    )(*args)


def _attn_kernel(*refs, local):
    if local:
        sink_ref, q_ref, k_ref, v_ref, kc_ref, vc_ref, o_ref = refs
    else:
        sink_ref, q_ref, kc_ref, vc_ref, o_ref = refs
    tq = q_ref.shape[0]
    if local:
        j = pl.program_id(1)
        n_loc = 3 * BLOCK
        start = pl.multiple_of(jnp.clip((j - 1) * BLOCK, 0, SEQ - n_loc), BLOCK)
        qpos = j * BLOCK + lax.broadcasted_iota(jnp.int32, (tq, n_loc), 0)
        kpos = start + lax.broadcasted_iota(jnp.int32, (tq, n_loc), 1)
        valid = jnp.abs(kpos - qpos) <= WINDOW
        valid = jnp.concatenate([valid] * GQA_GROUP, axis=0)
    for kh in range(N_KV_HEADS):
        cs = slice(kh * HEAD_DIM, (kh + 1) * HEAD_DIM)
        q4 = jnp.concatenate(
            [q_ref[:, (kh * GQA_GROUP + g) * HEAD_DIM:(kh * GQA_GROUP + g + 1) * HEAD_DIM] for g in range(GQA_GROUP)],
            axis=0)
        sink = jnp.concatenate(
            [jnp.full((tq, 1), sink_ref[kh * GQA_GROUP + g], F32) for g in range(GQA_GROUP)], axis=0)
        nt = (((1,), (1,)), ((), ()))
        s_ctx = lax.dot_general(q4, kc_ref[0, :, cs], nt, preferred_element_type=F32)
        m = jnp.maximum(jnp.max(s_ctx, axis=-1, keepdims=True), sink)
        if local:
            kb = k_ref[0, pl.ds(start, n_loc), cs]
            vb = v_ref[0, pl.ds(start, n_loc), cs]
            s_loc = lax.dot_general(q4, kb, nt, preferred_element_type=F32)
            s_loc = jnp.where(valid, s_loc, NEG_INF)
            m = jnp.maximum(m, jnp.max(s_loc, axis=-1, keepdims=True))
        e_ctx = jnp.exp(s_ctx - m)
        denom = jnp.sum(e_ctx, axis=-1, keepdims=True) + jnp.exp(sink - m)
        acc = jnp.dot(e_ctx.astype(BF16), vc_ref[0, :, cs], preferred_element_type=F32)
        if local:
            e_loc = jnp.exp(s_loc - m)
            denom = denom + jnp.sum(e_loc, axis=-1, keepdims=True)
            acc = acc + jnp.dot(e_loc.astype(BF16), vb, preferred_element_type=F32)
        out = acc * (1.0 / denom)
        for g in range(GQA_GROUP):
            h0 = (kh * GQA_GROUP + g) * HEAD_DIM
            o_ref[:, h0:h0 + HEAD_DIM] = out[g * tq:(g + 1) * tq].astype(o_ref.dtype)


def _attn_call(sink, q, k, v, kc, vc):
    nb = SEQ // BLOCK
    return pl.pallas_call(
        functools.partial(_attn_kernel, local=True),
        grid=(BATCH, nb),
        in_specs=[
            pl.BlockSpec(memory_space=pltpu.SMEM),
            pl.BlockSpec((BLOCK, D_Q), lambda b, j: (b * nb + j, 0)),
            pl.BlockSpec((1, SEQ, D_KV), lambda b, j: (b, 0, 0)),
            pl.BlockSpec((1, SEQ, D_KV), lambda b, j: (b, 0, 0)),
            pl.BlockSpec((1, CTX_LEN, D_KV), lambda b, j: (b, 0, 0)),
            pl.BlockSpec((1, CTX_LEN, D_KV), lambda b, j: (b, 0, 0)),
        ],
        out_specs=pl.BlockSpec((BLOCK, D_Q), lambda b, j: (b * nb + j, 0)),
        out_shape=jax.ShapeDtypeStruct((BATCH * SEQ, D_Q), BF16),
        compiler_params=_params(("parallel", "arbitrary")),
        name="band_attn",
    )(sink, q, k, v, kc, vc)


def _ctx_attn_call(sink, qc, kc, vc):
    return pl.pallas_call(
        functools.partial(_attn_kernel, local=False),
        grid=(BATCH,),
        in_specs=[
            pl.BlockSpec(memory_space=pltpu.SMEM),
            pl.BlockSpec((CTX_LEN, D_Q), lambda b: (b, 0)),
            pl.BlockSpec((1, CTX_LEN, D_KV), lambda b: (b, 0, 0)),
            pl.BlockSpec((1, CTX_LEN, D_KV), lambda b: (b, 0, 0)),
        ],
        out_specs=pl.BlockSpec((CTX_LEN, D_Q), lambda b: (b, 0)),
        out_shape=jax.ShapeDtypeStruct((BATCH * CTX_LEN, D_Q), BF16),
        compiler_params=_params(("parallel",)),
        name="ctx_attn",
    )(sink, qc, kc, vc)


def _post_kernel(*refs, lru):
    if lru:
        x_ref, gate_ref, yf_ref, yb_ref, mod_ref, g_ref, wf_ref, w1_ref, w2_ref, o_ref = refs
        front = (gate_ref[...].astype(F32) * (yf_ref[...] + yb_ref[...])).astype(BF16)
    else:
        x_ref, a_ref, mod_ref, g_ref, wf_ref, w1_ref, w2_ref, o_ref = refs
        front = a_ref[...]
    y = jnp.dot(front, wf_ref[...], preferred_element_type=F32)
    x1 = _gated_add(x_ref[...], mod_ref[0, 2], _rms(y, g_ref[1:2, :]))
    h = _modulate(_rms(x1, g_ref[2:3, :]), mod_ref[0, 3], mod_ref[0, 4]).astype(BF16)
    acc = jnp.zeros(x1.shape, F32)
    for c in range(D_FF // FF_CHUNK):
        hid = jnp.dot(h, w1_ref[:, c * FF_CHUNK:(c + 1) * FF_CHUNK], preferred_element_type=F32)
        hid = jnp.square(jnp.maximum(hid, 0.0)).astype(BF16)
        acc = acc + jnp.dot(hid, w2_ref[c * FF_CHUNK:(c + 1) * FF_CHUNK, :], preferred_element_type=F32)
    o_ref[...] = _gated_add(x1, mod_ref[0, 5], _rms(acc, g_ref[3:4, :]))


def _post_call(x2, fronts, mod, g, w_front, w1, w2, tiles_per_group, lru):
    n = x2.shape[0]
    tm = TOKEN_TILE
    row = lambda i: (i, 0)
    in_specs = [pl.BlockSpec((tm, D_MODEL), row)]
    in_specs += [pl.BlockSpec((tm, f.shape[1]), row) for f in fronts]
    in_specs += [
        pl.BlockSpec((1, N_MOD, SUBLANES, D_MODEL), lambda i: (i // tiles_per_group, 0, 0, 0)),
        _const_spec((4, D_MODEL)),
        _const_spec(w_front.shape),
        _const_spec(w1.shape),
        _const_spec(w2.shape),
    ]
    return pl.pallas_call(
        functools.partial(_post_kernel, lru=lru),
        grid=(n // tm,),
        in_specs=in_specs,
        out_specs=pl.BlockSpec((tm, D_MODEL), row),
        out_shape=jax.ShapeDtypeStruct((n, D_MODEL), F32),
        compiler_params=_params(("parallel",)),
        name="lru_out_mlp" if lru else "attn_out_mlp",
    )(x2, *fronts, mod, g, w_front, w1, w2)


def _lru_in_kernel(x_ref, mod_ref, g_ref, w_ref, gate_ref, v_ref):
    h = _modulate(_rms(x_ref[...], g_ref[0:1, :]), mod_ref[0, 0], mod_ref[0, 1]).astype(BF16)
    gate_ref[...] = jax.nn.gelu(jnp.dot(h, w_ref[:, :D_RNN], preferred_element_type=F32)).astype(BF16)
    v_ref[...] = jnp.dot(h, w_ref[:, D_RNN:], preferred_element_type=F32)


def _lru_in_call(x2, mod, g, w_in):
    n = x2.shape[0]
    tm = TOKEN_TILE
    row = lambda i: (i, 0)
    return pl.pallas_call(
        _lru_in_kernel,
        grid=(n // tm,),
        in_specs=[
            pl.BlockSpec((tm, D_MODEL), row),
            _const_spec((1, N_MOD, SUBLANES, D_MODEL)),
            _const_spec((4, D_MODEL)),
            _const_spec((D_MODEL, 2 * D_RNN)),
        ],
        out_specs=[pl.BlockSpec((tm, D_RNN), row), pl.BlockSpec((tm, D_RNN), row)],
        out_shape=[jax.ShapeDtypeStruct((n, D_RNN), BF16), jax.ShapeDtypeStruct((n, D_RNN), F32)],
        compiler_params=_params(("parallel",)),
        name="lru_in",
    )(x2, mod, g, w_in)


def _scan_kernel(vf_ref, vfp_ref, vfn_ref, vb_ref, vbp_ref, vbn_ref, h0_ref, cw_ref, cb_ref,
                 wa_ref, ba_ref, wi_ref, bi_ref, lam_ref, yf_ref, yb_ref, ht_ref,
                 u_sc, a_sc, bx_sc, h_sc):
    i = pl.program_id(0)
    n = pl.num_programs(0)
    rows = vf_ref.shape[0]
    nt = rows // SUBLANES
    s8 = SUBLANES

    @pl.when(i == 0)
    def _():
        h_sc[...] = h0_ref[...]

    dirs = ((vf_ref, vfp_ref, vfn_ref, i), (vb_ref, vbp_ref, vbn_ref, n - 1 - i))
    for d, (v_ref, vp_ref, vn_ref, tile) in enumerate(dirs):
        keep_prev = (tile > 0).astype(F32)
        keep_next = (tile < n - 1).astype(F32)
        for c in range(N_LRU_BLOCKS):
            cs = slice(c * LRU_BLOCK_W, (c + 1) * LRU_BLOCK_W)
            w0, w1, w2, w3 = (cw_ref[t:t + 1, cs] for t in range(CONV_W))
            bias = cb_ref[:, cs]
            prev = vp_ref[:, cs] * keep_prev
            nxt = vn_ref[:, cs] * keep_next
            u_sc[0:s8, cs] = (bias + w0 * prev[0:s8] + w1 * prev[s8:2 * s8]
                              + w2 * v_ref[0:s8, cs] + w3 * v_ref[s8:2 * s8, cs])
            u_sc[s8:2 * s8, cs] = (bias + w0 * prev[s8:2 * s8] + w1 * v_ref[0:s8, cs]
                                   + w2 * v_ref[s8:2 * s8, cs] + w3 * v_ref[2 * s8:3 * s8, cs])
            u_sc[2 * s8:rows - s8, cs] = (bias + w0 * v_ref[0:rows - 3 * s8, cs] + w1 * v_ref[s8:rows - 2 * s8, cs]
                                          + w2 * v_ref[2 * s8:rows - s8, cs] + w3 * v_ref[3 * s8:rows, cs])
            u_sc[rows - s8:rows, cs] = (bias + w0 * v_ref[rows - 3 * s8:rows - 2 * s8, cs]
                                        + w1 * v_ref[rows - 2 * s8:rows - s8, cs]
                                        + w2 * v_ref[rows - s8:rows, cs] + w3 * nxt)
            u = u_sc[:, cs]
            ub = u.astype(BF16)
            r = jax.nn.sigmoid(jnp.dot(ub, wa_ref[d, c], preferred_element_type=F32) + ba_ref[d, :, cs])
            ig = jax.nn.sigmoid(jnp.dot(ub, wi_ref[d, c], preferred_element_type=F32) + bi_ref[d, :, cs])
            neg_lam = -lam_ref[d, :, cs]
            softplus = jnp.maximum(neg_lam, 0.0) + jnp.log1p(jnp.exp(-jnp.abs(neg_lam)))
            a = jnp.exp((-LRU_C * softplus) * r)
            a_sc[d, :, cs] = a
            bx_sc[d, :, cs] = jnp.sqrt(1.0 - a * a) * (ig * u)

    def step(t, carry):
        hf, hb = carry
        rf = pl.multiple_of(t * s8, s8)
        rb = pl.multiple_of((nt - 1 - t) * s8, s8)
        hf = a_sc[0, pl.ds(rf, s8), :] * hf + bx_sc[0, pl.ds(rf, s8), :]
        yf_ref[pl.ds(rf, s8), :] = hf
        hb = a_sc[1, pl.ds(rb, s8), :] * hb + bx_sc[1, pl.ds(rb, s8), :]
        yb_ref[pl.ds(rb, s8), :] = hb
        return hf, hb

    hf, hb = lax.fori_loop(0, nt, step, (h_sc[0], h_sc[1]), unroll=4)
    h_sc[0] = hf
    h_sc[1] = hb

    @pl.when(i == n - 1)
    def _():
        ht_ref[...] = h_sc[...]


def _scan_call(v2, h0, conv_w, conv_b, w_a, b_a, w_i, b_i, lam):
    rows_total = v2.shape[0]
    rows = SCAN_T * SUBLANES
    n = rows_total // rows
    w = D_RNN
    fwd = lambda i: (i, 0)
    bwd = lambda i: (n - 1 - i, 0)
    prev_blocks = rows // (2 * SUBLANES)
    next_blocks = rows // SUBLANES
    last8 = rows_total // SUBLANES - 1
    fwd_prev = lambda i: (jnp.maximum(i * prev_blocks - 1, 0), 0)
    fwd_next = lambda i: (jnp.minimum((i + 1) * next_blocks, last8), 0)
    bwd_prev = lambda i: (jnp.maximum((n - 1 - i) * prev_blocks - 1, 0), 0)
    bwd_next = lambda i: (jnp.minimum((n - i) * next_blocks, last8), 0)
    return pl.pallas_call(
        _scan_kernel,
        grid=(n,),
        in_specs=[
            pl.BlockSpec((rows, w), fwd),
            pl.BlockSpec((2 * SUBLANES, w), fwd_prev),
            pl.BlockSpec((SUBLANES, w), fwd_next),
            pl.BlockSpec((rows, w), bwd),
            pl.BlockSpec((2 * SUBLANES, w), bwd_prev),
            pl.BlockSpec((SUBLANES, w), bwd_next),
            _const_spec((2, SUBLANES, w)),
            _const_spec((CONV_W, w)),
            _const_spec((1, w)),
            _const_spec((2, N_LRU_BLOCKS, LRU_BLOCK_W, LRU_BLOCK_W)),
            _const_spec((2, 1, w)),
            _const_spec((2, N_LRU_BLOCKS, LRU_BLOCK_W, LRU_BLOCK_W)),
            _const_spec((2, 1, w)),
            _const_spec((2, 1, w)),
        ],
        out_specs=[
            pl.BlockSpec((rows, w), fwd),
            pl.BlockSpec((rows, w), bwd),
            pl.BlockSpec((2, SUBLANES, w), lambda i: (0, 0, 0)),
        ],
        out_shape=[
            jax.ShapeDtypeStruct((rows_total, w), F32),
            jax.ShapeDtypeStruct((rows_total, w), F32),
            jax.ShapeDtypeStruct((2, SUBLANES, w), F32),
        ],
        scratch_shapes=[
            pltpu.VMEM((rows, w), F32),
            pltpu.VMEM((2, rows, w), F32),
            pltpu.VMEM((2, rows, w), F32),
            pltpu.VMEM((2, SUBLANES, w), F32),
        ],
        compiler_params=_params(("arbitrary",)),
        name="lru_scan",
    )(v2, v2, v2, v2, v2, v2, h0, conv_w, conv_b, w_a, b_a, w_i, b_i, lam)


def _rope_tables():
    t = jnp.arange(SEQ)
    row = (t // GRID_W).astype(F32)
    col = (t % GRID_W).astype(F32)
    half = HEAD_DIM // 2
    inv = ROPE_BASE ** (-jnp.arange(0, half, 2, dtype=F32) / half)
    ang_r = row[:, None] * inv[None, :]
    ang_c = col[:, None] * inv[None, :]
    ang = jnp.concatenate([ang_r, ang_r, ang_c, ang_c], axis=-1)
    ang = jnp.tile(ang, (1, LANES // HEAD_DIM))
    low = (jnp.arange(LANES) % 32) < 16
    sin = jnp.sin(ang)
    return jnp.cos(ang), jnp.where(low, -sin, 0.0), jnp.where(low, 0.0, sin)


def kernel(x, c, ctx, c_ctx, ada_w, ada_b, norm_g, mlp_w1, mlp_w2, attn_w_qkv, attn_w_o, attn_sink,
           lru_w_in, lru_conv_w, lru_conv_b, lru_w_a, lru_b_a, lru_w_i, lru_b_i, lru_lam, lru_w_out):
    n_lat = BATCH * SEQ
    n_ctx = BATCH * CTX_LEN

    c16 = jnp.zeros((16, D_MODEL), F32).at[:BATCH].set(c).at[BATCH].set(c_ctx)
    mods = _mod_call(c16, ada_w, ada_b).reshape(2, 16, N_MOD, D_MODEL)

    def slab_bmajor(m):
        return jnp.broadcast_to(m[:, :, None, :], (BATCH, N_MOD, SUBLANES, D_MODEL))

    def slab_ctx(m):
        return jnp.broadcast_to(m[None, :, None, :], (1, N_MOD, SUBLANES, D_MODEL))

    mod_x0 = slab_bmajor(mods[0, :BATCH])
    mod_c0 = slab_ctx(mods[0, BATCH])
    w_qkv = attn_w_qkv[0]
    w_qkv = jnp.concatenate([w_qkv[:, :D_Q] * (HEAD_DIM ** -0.5), w_qkv[:, D_Q:]], axis=1).astype(BF16)
    w_o = attn_w_o[0].astype(BF16)
    w1_0, w2_0 = mlp_w1[0].astype(BF16), mlp_w2[0].astype(BF16)
    g0 = norm_g[0]
    tiles_per_batch = SEQ // TOKEN_TILE

    x2 = x.reshape(n_lat, D_MODEL)
    c2 = ctx.reshape(n_ctx, D_MODEL)
    q, k, v = _qkv_call(x2, mod_x0, g0, w_qkv, _rope_tables(), tiles_per_batch)
    qc, kc, vc = _qkv_call(c2, mod_c0, g0, w_qkv, None, n_ctx // TOKEN_TILE)
    kc3 = kc.reshape(BATCH, CTX_LEN, D_KV)
    vc3 = vc.reshape(BATCH, CTX_LEN, D_KV)
    att = _attn_call(attn_sink[0], q, k.reshape(BATCH, SEQ, D_KV), v.reshape(BATCH, SEQ, D_KV), kc3, vc3)
    att_c = _ctx_attn_call(attn_sink[0], qc, kc3, vc3)
    x2 = _post_call(x2, [att], mod_x0, g0, w_o, w1_0, w2_0, tiles_per_batch, lru=False)
    c2 = _post_call(c2, [att_c], mod_c0, g0, w_o, w1_0, w2_0, n_ctx // TOKEN_TILE, lru=False)

    xt = x2.reshape(BATCH, SEQ, D_MODEL).transpose(1, 0, 2).reshape(n_lat, D_MODEL)
    ct = c2.reshape(BATCH, CTX_LEN, D_MODEL).transpose(1, 0, 2).reshape(n_ctx, D_MODEL)
    mod_x1 = mods[1, :BATCH].transpose(1, 0, 2)[None]
    mod_c1 = slab_ctx(mods[1, BATCH])
    g1 = norm_g[1]
    w_in = lru_w_in[0]
    w_in = jnp.concatenate([w_in[:, :D_RNN], w_in[:, D_RNN:]], axis=1).astype(BF16)
    conv_b = lru_conv_b[0].reshape(1, D_RNN)
    w_a, w_i = lru_w_a[0].astype(BF16), lru_w_i[0].astype(BF16)
    b_a, b_i = lru_b_a[0].reshape(2, 1, D_RNN), lru_b_i[0].reshape(2, 1, D_RNN)
    lam = lru_lam[0].reshape(2, 1, D_RNN)
    scan = functools.partial(_scan_call, conv_w=lru_conv_w[0], conv_b=conv_b, w_a=w_a, b_a=b_a,
                             w_i=w_i, b_i=b_i, lam=lam)

    _, v_c = _lru_in_call(ct, mod_c1, g1, w_in)
    _, _, h_ctx = scan(v_c, jnp.zeros((2, SUBLANES, D_RNN), F32))
    gate_x, v_x = _lru_in_call(xt, mod_x1, g1, w_in)
    yf, yb, _ = scan(v_x, h_ctx)
    xt = _post_call(xt, [gate_x, yf, yb], mod_x1, g1, lru_w_out[0].astype(BF16),
                    mlp_w1[1].astype(BF16), mlp_w2[1].astype(BF16), n_lat // TOKEN_TILE, lru=True)
    return xt.reshape(SEQ, BATCH, D_MODEL).transpose(1, 0, 2)
```

```python
import functools

import jax
import jax.numpy as jnp
from jax import lax
from jax.experimental import pallas as pl
from jax.experimental.pallas import tpu as pltpu

D_MODEL = 1024
BATCH = 8
SEQ = 2048
GRID_W = 64
CTX_LEN = 256
HEAD_DIM = 64
N_HEADS = 16
N_KV_HEADS = 4
GQA_GROUP = N_HEADS // N_KV_HEADS
WINDOW = 128
BLOCK = 128
ROPE_BASE = 10000.0
D_RNN = 1280
LRU_BLOCK_W = 256
N_LRU_BLOCKS = D_RNN // LRU_BLOCK_W
CONV_W = 4
LRU_C = 8.0
D_FF = 4 * D_MODEL
N_MOD = 6
EPS = 1e-6
NEG_INF = -1e30

D_Q = N_HEADS * HEAD_DIM
D_KV = N_KV_HEADS * HEAD_DIM
LANES = 128
SUBLANES = 8
TOKEN_TILE = 512
FF_CHUNK = 1024
SCAN_T = 64
VMEM_LIMIT = 56 * 1024 * 1024

F32 = jnp.float32
BF16 = jnp.bfloat16


def _rms(x, g):
    ms = jnp.mean(x * x, axis=-1, keepdims=True)
    return x * lax.rsqrt(ms + EPS) * g


def _slab(x):
    return x.reshape(x.shape[0] // SUBLANES, SUBLANES, x.shape[1])


def _modulate(h, shift8, scale8):
    out = _slab(h) * (1.0 + scale8)[None] + shift8[None]
    return out.reshape(h.shape)


def _gated_add(x, gate8, y):
    out = _slab(x) + gate8[None] * _slab(y)
    return out.reshape(x.shape)


def _const_spec(shape):
    n = len(shape)
    return pl.BlockSpec(shape, lambda *_: (0,) * n, pipeline_mode=pl.Buffered(1))


def _params(sem):
    return pltpu.CompilerParams(dimension_semantics=sem, vmem_limit_bytes=VMEM_LIMIT)


def _mod_kernel(c_ref, w_ref, b_ref, o_ref):
    s = jax.nn.silu(c_ref[...]).astype(BF16)
    o_ref[0] = jnp.dot(s, w_ref[0].astype(BF16), preferred_element_type=F32) + b_ref[0]


def _mod_call(c16, ada_w, ada_b):
    depth = ada_w.shape[0]
    nt = 1024
    return pl.pallas_call(
        _mod_kernel,
        grid=(depth, N_MOD * D_MODEL // nt),
        in_specs=[
            pl.BlockSpec((16, D_MODEL), lambda l, j: (0, 0)),
            pl.BlockSpec((1, D_MODEL, nt), lambda l, j: (l, 0, j)),
            pl.BlockSpec((1, 1, nt), lambda l, j: (l, 0, j)),
        ],
        out_specs=pl.BlockSpec((1, 16, nt), lambda l, j: (l, 0, j)),
        out_shape=jax.ShapeDtypeStruct((depth, 16, N_MOD * D_MODEL), F32),
        compiler_params=_params(("arbitrary", "arbitrary")),
        name="adaln_mod",
    )(c16, ada_w, ada_b.reshape(depth, 1, N_MOD * D_MODEL))


def _qkv_kernel(*refs, rope):
    if rope:
        x_ref, mod_ref, g_ref, w_ref, cos_ref, sa_ref, sb_ref, q_ref, k_ref, v_ref = refs
    else:
        x_ref, mod_ref, g_ref, w_ref, q_ref, k_ref, v_ref = refs
    h = _modulate(_rms(x_ref[...], g_ref[0:1, :]), mod_ref[0, 0], mod_ref[0, 1])
    y = jnp.dot(h.astype(BF16), w_ref[...], preferred_element_type=F32)
    if rope:
        cos, sa, sb = cos_ref[...], sa_ref[...], sb_ref[...]
    for c in range((D_Q + D_KV) // LANES):
        yc = y[:, c * LANES:(c + 1) * LANES]
        if rope:
            yc = yc * cos + pltpu.roll(yc, LANES - 16, 1) * sa + pltpu.roll(yc, 16, 1) * sb
        if c < D_Q // LANES:
            q_ref[:, c * LANES:(c + 1) * LANES] = yc.astype(BF16)
        else:
            c2 = c - D_Q // LANES
            k_ref[:, c2 * LANES:(c2 + 1) * LANES] = yc.astype(BF16)
    v_ref[...] = y[:, D_Q + D_KV:].astype(BF16)


def _qkv_call(x2, mod, g, w_qkv, tables, tiles_per_group):
    n = x2.shape[0]
    tm = TOKEN_TILE
    rope = tables is not None
    in_specs = [
        pl.BlockSpec((tm, D_MODEL), lambda i: (i, 0)),
        pl.BlockSpec((1, N_MOD, SUBLANES, D_MODEL), lambda i: (i // tiles_per_group, 0, 0, 0)),
        _const_spec((4, D_MODEL)),
        _const_spec((D_MODEL, D_Q + 2 * D_KV)),
    ]
    args = [x2, mod, g, w_qkv]
    if rope:
        nt = SEQ // tm
        in_specs += [pl.BlockSpec((tm, LANES), lambda i: (i % nt, 0))] * 3
        args += list(tables)
    return pl.pallas_call(
        functools.partial(_qkv_kernel, rope=rope),
        grid=(n // tm,),
        in_specs=in_specs,
        out_specs=[
            pl.BlockSpec((tm, D_Q), lambda i: (i, 0)),
            pl.BlockSpec((tm, D_KV), lambda i: (i, 0)),
            pl.BlockSpec((tm, D_KV), lambda i: (i, 0)),
        ],
        out_shape=[
            jax.ShapeDtypeStruct((n, D_Q), BF16),
            jax.ShapeDtypeStruct((n, D_KV), BF16),
            jax.ShapeDtypeStruct((n, D_KV), BF16),
        ],
        compiler_params=_params(("parallel",)),
        name="qkv_rope" if rope else "qkv_ctx",
    )(*args)


def _attn_kernel(*refs, local):
    if local:
        sink_ref, q_ref, k_ref, v_ref, kc_ref, vc_ref, o_ref, k2_sc, v2_sc = refs
    else:
        sink_ref, q_ref, kc_ref, vc_ref, o_ref, k2_sc, v2_sc = refs
    j = pl.program_id(1)
    seq_blocks = SEQ // BLOCK if local else 0
    ctx_blocks = CTX_LEN // BLOCK
    half = HEAD_DIM

    @pl.when((pl.program_id(0) == 0) & (j == 0))
    def _init_static():
        k2_sc[...] = jnp.zeros(k2_sc.shape, BF16)
        v2_sc[...] = jnp.zeros(v2_sc.shape, BF16)
        ones = jnp.ones((N_KV_HEADS, seq_blocks + ctx_blocks, BLOCK, half), BF16)
        v2_sc[:, :, 0, :, 2 * half:3 * half] = ones
        v2_sc[:, :, 1, :, 3 * half:4 * half] = ones

    def fill(src_k, src_v, n_blocks, first_block):
        def body(i, carry):
            r0 = pl.multiple_of(i * BLOCK, BLOCK)
            kb = src_k[0, pl.ds(r0, BLOCK), :]
            vb = src_v[0, pl.ds(r0, BLOCK), :]
            for kh in range(N_KV_HEADS):
                cs = slice(kh * half, (kh + 1) * half)
                k2_sc[kh, first_block + i, 0, :, 0:half] = kb[:, cs]
                k2_sc[kh, first_block + i, 1, :, half:2 * half] = kb[:, cs]
                v2_sc[kh, first_block + i, 0, :, 0:half] = vb[:, cs]
                v2_sc[kh, first_block + i, 1, :, half:2 * half] = vb[:, cs]
            return carry
        lax.fori_loop(0, n_blocks, body, 0)

    @pl.when(j == 0)
    def _build():
        if local:
            fill(k_ref, v_ref, seq_blocks, 0)
        fill(kc_ref, vc_ref, ctx_blocks, seq_blocks)

    nt = (((1,), (1,)), ((), ()))
    n_band = 3
    if local:
        blk0 = jnp.clip(j - 1, 0, seq_blocks - n_band)
        qpos = j * BLOCK + lax.broadcasted_iota(jnp.int32, (BLOCK, n_band * BLOCK), 0)
        kpos = blk0 * BLOCK + lax.broadcasted_iota(jnp.int32, (BLOCK, n_band * BLOCK), 1)
        bias = jnp.where(jnp.abs(kpos - qpos) <= WINDOW, 0.0, NEG_INF).astype(F32)
        bias_blocks = [bias[:, i * BLOCK:(i + 1) * BLOCK] for i in range(n_band)]
    low_lanes = lax.broadcasted_iota(jnp.int32, (BLOCK, LANES), 1) < half

    for c in range(D_Q // LANES):
        kh = c // (GQA_GROUP // 2)
        qc = q_ref[:, c * LANES:(c + 1) * LANES]
        k_ctx = k2_sc[kh, seq_blocks:seq_blocks + ctx_blocks].reshape(2 * CTX_LEN, LANES)
        s_ctx = lax.dot_general(qc, k_ctx, nt, preferred_element_type=F32)
        blocks = [s_ctx[:, i * BLOCK:(i + 1) * BLOCK] for i in range(2 * ctx_blocks)]
        if local:
            k_band = k2_sc[kh, pl.ds(blk0, n_band)].reshape(2 * n_band * BLOCK, LANES)
            s_band = lax.dot_general(qc, k_band, nt, preferred_element_type=F32)
            blocks += [s_band[:, i * BLOCK:(i + 1) * BLOCK] + bias_blocks[i // 2] for i in range(2 * n_band)]
        ms = []
        for par in range(2):
            mx = functools.reduce(jnp.maximum, blocks[par::2])
            ms.append(jnp.maximum(jnp.max(mx, axis=-1, keepdims=True), sink_ref[2 * c + par]))
        p = [jnp.exp2(blk - ms[i % 2]).astype(BF16) for i, blk in enumerate(blocks)]
        v_ctx = v2_sc[kh, seq_blocks:seq_blocks + ctx_blocks].reshape(2 * CTX_LEN, 2 * LANES)
        acc = jnp.dot(jnp.concatenate(p[:2 * ctx_blocks], axis=1), v_ctx, preferred_element_type=F32)
        if local:
            v_band = v2_sc[kh, pl.ds(blk0, n_band)].reshape(2 * n_band * BLOCK, 2 * LANES)
            acc = acc + jnp.dot(jnp.concatenate(p[2 * ctx_blocks:], axis=1), v_band, preferred_element_type=F32)
        sink_term = jnp.where(low_lanes, jnp.exp2(sink_ref[2 * c] - ms[0]), jnp.exp2(sink_ref[2 * c + 1] - ms[1]))
        o_ref[:, c * LANES:(c + 1) * LANES] = (acc[:, :LANES] / (acc[:, LANES:] + sink_term)).astype(o_ref.dtype)


def _attn_scratch(n_blocks):
    return [
        pltpu.VMEM((N_KV_HEADS, n_blocks, 2, BLOCK, LANES), BF16),
        pltpu.VMEM((N_KV_HEADS, n_blocks, 2, BLOCK, 2 * LANES), BF16),
    ]


def _attn_call(sink2, q, k, v, kc, vc):
    nb = SEQ // BLOCK
    return pl.pallas_call(
        functools.partial(_attn_kernel, local=True),
        grid=(BATCH, nb),
        in_specs=[
            pl.BlockSpec(memory_space=pltpu.SMEM),
            pl.BlockSpec((BLOCK, D_Q), lambda b, j: (b * nb + j, 0)),
            pl.BlockSpec((1, SEQ, D_KV), lambda b, j: (b, 0, 0)),
            pl.BlockSpec((1, SEQ, D_KV), lambda b, j: (b, 0, 0)),
            pl.BlockSpec((1, CTX_LEN, D_KV), lambda b, j: (b, 0, 0)),
            pl.BlockSpec((1, CTX_LEN, D_KV), lambda b, j: (b, 0, 0)),
        ],
        out_specs=pl.BlockSpec((BLOCK, D_Q), lambda b, j: (b * nb + j, 0)),
        out_shape=jax.ShapeDtypeStruct((BATCH * SEQ, D_Q), BF16),
        scratch_shapes=_attn_scratch(nb + CTX_LEN // BLOCK),
        compiler_params=_params(("arbitrary", "arbitrary")),
        name="band_attn",
    )(sink2, q, k, v, kc, vc)


def _ctx_attn_call(sink2, qc, kc, vc):
    nb = CTX_LEN // BLOCK
    return pl.pallas_call(
        functools.partial(_attn_kernel, local=False),
        grid=(BATCH, nb),
        in_specs=[
            pl.BlockSpec(memory_space=pltpu.SMEM),
            pl.BlockSpec((BLOCK, D_Q), lambda b, j: (b * nb + j, 0)),
            pl.BlockSpec((1, CTX_LEN, D_KV), lambda b, j: (b, 0, 0)),
            pl.BlockSpec((1, CTX_LEN, D_KV), lambda b, j: (b, 0, 0)),
        ],
        out_specs=pl.BlockSpec((BLOCK, D_Q), lambda b, j: (b * nb + j, 0)),
        out_shape=jax.ShapeDtypeStruct((BATCH * CTX_LEN, D_Q), BF16),
        scratch_shapes=_attn_scratch(nb),
        compiler_params=_params(("arbitrary", "arbitrary")),
        name="ctx_attn",
    )(sink2, qc, kc, vc)


def _post_kernel(*refs, lru):
    if lru:
        x_ref, gate_ref, yf_ref, yb_ref, mod_ref, g_ref, wf_ref, w1_ref, w2_ref, o_ref = refs
        front = (gate_ref[...].astype(F32) * (yf_ref[...] + yb_ref[...])).astype(BF16)
    else:
        x_ref, a_ref, mod_ref, g_ref, wf_ref, w1_ref, w2_ref, o_ref = refs
        front = a_ref[...]
    y = jnp.dot(front, wf_ref[...], preferred_element_type=F32)
    x1 = _gated_add(x_ref[...], mod_ref[0, 2], _rms(y, g_ref[1:2, :]))
    h = _modulate(_rms(x1, g_ref[2:3, :]), mod_ref[0, 3], mod_ref[0, 4]).astype(BF16)
    acc = jnp.zeros(x1.shape, F32)
    for c in range(D_FF // FF_CHUNK):
        hid = jnp.dot(h, w1_ref[:, c * FF_CHUNK:(c + 1) * FF_CHUNK], preferred_element_type=F32)
        hid = jnp.square(jnp.maximum(hid, 0.0)).astype(BF16)
        acc = acc + jnp.dot(hid, w2_ref[c * FF_CHUNK:(c + 1) * FF_CHUNK, :], preferred_element_type=F32)
    o_ref[...] = _gated_add(x1, mod_ref[0, 5], _rms(acc, g_ref[3:4, :]))


def _post_call(x2, fronts, mod, g, w_front, w1, w2, tiles_per_group, lru):
    n = x2.shape[0]
    tm = TOKEN_TILE
    row = lambda i: (i, 0)
    in_specs = [pl.BlockSpec((tm, D_MODEL), row)]
    in_specs += [pl.BlockSpec((tm, f.shape[1]), row) for f in fronts]
    in_specs += [
        pl.BlockSpec((1, N_MOD, SUBLANES, D_MODEL), lambda i: (i // tiles_per_group, 0, 0, 0)),
        _const_spec((4, D_MODEL)),
        _const_spec(w_front.shape),
        _const_spec(w1.shape),
        _const_spec(w2.shape),
    ]
    return pl.pallas_call(
        functools.partial(_post_kernel, lru=lru),
        grid=(n // tm,),
        in_specs=in_specs,
        out_specs=pl.BlockSpec((tm, D_MODEL), row),
        out_shape=jax.ShapeDtypeStruct((n, D_MODEL), F32),
        compiler_params=_params(("parallel",)),
        name="lru_out_mlp" if lru else "attn_out_mlp",
    )(x2, *fronts, mod, g, w_front, w1, w2)


def _lru_in_kernel(x_ref, mod_ref, g_ref, w_ref, gate_ref, v_ref):
    h = _modulate(_rms(x_ref[...], g_ref[0:1, :]), mod_ref[0, 0], mod_ref[0, 1]).astype(BF16)
    gate_ref[...] = jax.nn.gelu(jnp.dot(h, w_ref[:, :D_RNN], preferred_element_type=F32)).astype(BF16)
    v_ref[...] = jnp.dot(h, w_ref[:, D_RNN:], preferred_element_type=F32)


def _lru_in_call(x2, mod, g, w_in):
    n = x2.shape[0]
    tm = TOKEN_TILE
    row = lambda i: (i, 0)
    return pl.pallas_call(
        _lru_in_kernel,
        grid=(n // tm,),
        in_specs=[
            pl.BlockSpec((tm, D_MODEL), row),
            _const_spec((1, N_MOD, SUBLANES, D_MODEL)),
            _const_spec((4, D_MODEL)),
            _const_spec((D_MODEL, 2 * D_RNN)),
        ],
        out_specs=[pl.BlockSpec((tm, D_RNN), row), pl.BlockSpec((tm, D_RNN), row)],
        out_shape=[jax.ShapeDtypeStruct((n, D_RNN), BF16), jax.ShapeDtypeStruct((n, D_RNN), F32)],
        compiler_params=_params(("parallel",)),
        name="lru_in",
    )(x2, mod, g, w_in)


def _scan_kernel(vf_ref, vfp_ref, vfn_ref, vb_ref, vbp_ref, vbn_ref, h0_ref, cw_ref, cb_ref,
                 wa_ref, ba_ref, wi_ref, bi_ref, lam_ref, yf_ref, yb_ref, ht_ref,
                 u_sc, a_sc, bx_sc, h_sc):
    i = pl.program_id(0)
    n = pl.num_programs(0)
    rows = vf_ref.shape[0]
    nt = rows // SUBLANES
    s8 = SUBLANES

    @pl.when(i == 0)
    def _():
        h_sc[...] = h0_ref[...]

    dirs = ((vf_ref, vfp_ref, vfn_ref, i), (vb_ref, vbp_ref, vbn_ref, n - 1 - i))
    for d, (v_ref, vp_ref, vn_ref, tile) in enumerate(dirs):
        keep_prev = (tile > 0).astype(F32)
        keep_next = (tile < n - 1).astype(F32)
        for c in range(N_LRU_BLOCKS):
            cs = slice(c * LRU_BLOCK_W, (c + 1) * LRU_BLOCK_W)
            w0, w1, w2, w3 = (cw_ref[t:t + 1, cs] for t in range(CONV_W))
            bias = cb_ref[:, cs]
            prev = vp_ref[:, cs] * keep_prev
            nxt = vn_ref[:, cs] * keep_next
            u_sc[0:s8, cs] = (bias + w0 * prev[0:s8] + w1 * prev[s8:2 * s8]
                              + w2 * v_ref[0:s8, cs] + w3 * v_ref[s8:2 * s8, cs])
            u_sc[s8:2 * s8, cs] = (bias + w0 * prev[s8:2 * s8] + w1 * v_ref[0:s8, cs]
                                   + w2 * v_ref[s8:2 * s8, cs] + w3 * v_ref[2 * s8:3 * s8, cs])
            u_sc[2 * s8:rows - s8, cs] = (bias + w0 * v_ref[0:rows - 3 * s8, cs] + w1 * v_ref[s8:rows - 2 * s8, cs]
                                          + w2 * v_ref[2 * s8:rows - s8, cs] + w3 * v_ref[3 * s8:rows, cs])
            u_sc[rows - s8:rows, cs] = (bias + w0 * v_ref[rows - 3 * s8:rows - 2 * s8, cs]
                                        + w1 * v_ref[rows - 2 * s8:rows - s8, cs]
                                        + w2 * v_ref[rows - s8:rows, cs] + w3 * nxt)
            u = u_sc[:, cs]
            ub = u.astype(BF16)
            r = jax.nn.sigmoid(jnp.dot(ub, wa_ref[d, c], preferred_element_type=F32) + ba_ref[d, :, cs])
            ig = jax.nn.sigmoid(jnp.dot(ub, wi_ref[d, c], preferred_element_type=F32) + bi_ref[d, :, cs])
            neg_lam = -lam_ref[d, :, cs]
            softplus = jnp.maximum(neg_lam, 0.0) + jnp.log1p(jnp.exp(-jnp.abs(neg_lam)))
            a = jnp.exp((-LRU_C * softplus) * r)
            a_sc[d, :, cs] = a
            bx_sc[d, :, cs] = jnp.sqrt(1.0 - a * a) * (ig * u)

    def step(t, carry):
        hf, hb = carry
        rf = pl.multiple_of(t * s8, s8)
        rb = pl.multiple_of((nt - 1 - t) * s8, s8)
        hf = a_sc[0, pl.ds(rf, s8), :] * hf + bx_sc[0, pl.ds(rf, s8), :]
        yf_ref[pl.ds(rf, s8), :] = hf
        hb = a_sc[1, pl.ds(rb, s8), :] * hb + bx_sc[1, pl.ds(rb, s8), :]
        yb_ref[pl.ds(rb, s8), :] = hb
        return hf, hb

    hf, hb = lax.fori_loop(0, nt, step, (h_sc[0], h_sc[1]), unroll=4)
    h_sc[0] = hf
    h_sc[1] = hb

    @pl.when(i == n - 1)
    def _():
        ht_ref[...] = h_sc[...]


def _scan_call(v2, h0, conv_w, conv_b, w_a, b_a, w_i, b_i, lam):
    rows_total = v2.shape[0]
    rows = SCAN_T * SUBLANES
    n = rows_total // rows
    w = D_RNN
    fwd = lambda i: (i, 0)
    bwd = lambda i: (n - 1 - i, 0)
    prev_blocks = rows // (2 * SUBLANES)
    next_blocks = rows // SUBLANES
    last8 = rows_total // SUBLANES - 1
    fwd_prev = lambda i: (jnp.maximum(i * prev_blocks - 1, 0), 0)
    fwd_next = lambda i: (jnp.minimum((i + 1) * next_blocks, last8), 0)
    bwd_prev = lambda i: (jnp.maximum((n - 1 - i) * prev_blocks - 1, 0), 0)
    bwd_next = lambda i: (jnp.minimum((n - i) * next_blocks, last8), 0)
    return pl.pallas_call(
        _scan_kernel,
        grid=(n,),
        in_specs=[
            pl.BlockSpec((rows, w), fwd),
            pl.BlockSpec((2 * SUBLANES, w), fwd_prev),
            pl.BlockSpec((SUBLANES, w), fwd_next),
            pl.BlockSpec((rows, w), bwd),
            pl.BlockSpec((2 * SUBLANES, w), bwd_prev),
            pl.BlockSpec((SUBLANES, w), bwd_next),
            _const_spec((2, SUBLANES, w)),
            _const_spec((CONV_W, w)),
            _const_spec((1, w)),
            _const_spec((2, N_LRU_BLOCKS, LRU_BLOCK_W, LRU_BLOCK_W)),
            _const_spec((2, 1, w)),
            _const_spec((2, N_LRU_BLOCKS, LRU_BLOCK_W, LRU_BLOCK_W)),
            _const_spec((2, 1, w)),
            _const_spec((2, 1, w)),
        ],
        out_specs=[
            pl.BlockSpec((rows, w), fwd),
            pl.BlockSpec((rows, w), bwd),
            pl.BlockSpec((2, SUBLANES, w), lambda i: (0, 0, 0)),
        ],
        out_shape=[
            jax.ShapeDtypeStruct((rows_total, w), F32),
            jax.ShapeDtypeStruct((rows_total, w), F32),
            jax.ShapeDtypeStruct((2, SUBLANES, w), F32),
        ],
        scratch_shapes=[
            pltpu.VMEM((rows, w), F32),
            pltpu.VMEM((2, rows, w), F32),
            pltpu.VMEM((2, rows, w), F32),
            pltpu.VMEM((2, SUBLANES, w), F32),
        ],
        compiler_params=_params(("arbitrary",)),
        name="lru_scan",
    )(v2, v2, v2, v2, v2, v2, h0, conv_w, conv_b, w_a, b_a, w_i, b_i, lam)


def _rope_tables():
    t = jnp.arange(SEQ)
    row = (t // GRID_W).astype(F32)
    col = (t % GRID_W).astype(F32)
    half = HEAD_DIM // 2
    inv = ROPE_BASE ** (-jnp.arange(0, half, 2, dtype=F32) / half)
    ang_r = row[:, None] * inv[None, :]
    ang_c = col[:, None] * inv[None, :]
    ang = jnp.concatenate([ang_r, ang_r, ang_c, ang_c], axis=-1)
    ang = jnp.tile(ang, (1, LANES // HEAD_DIM))
    low = (jnp.arange(LANES) % 32) < 16
    sin = jnp.sin(ang)
    return jnp.cos(ang), jnp.where(low, -sin, 0.0), jnp.where(low, 0.0, sin)


def kernel(x, c, ctx, c_ctx, ada_w, ada_b, norm_g, mlp_w1, mlp_w2, attn_w_qkv, attn_w_o, attn_sink,
           lru_w_in, lru_conv_w, lru_conv_b, lru_w_a, lru_b_a, lru_w_i, lru_b_i, lru_lam, lru_w_out):
    n_lat = BATCH * SEQ
    n_ctx = BATCH * CTX_LEN

    c16 = jnp.zeros((16, D_MODEL), F32).at[:BATCH].set(c).at[BATCH].set(c_ctx)
    mods = _mod_call(c16, ada_w, ada_b).reshape(2, 16, N_MOD, D_MODEL)

    def slab_bmajor(m):
        return jnp.broadcast_to(m[:, :, None, :], (BATCH, N_MOD, SUBLANES, D_MODEL))

    def slab_ctx(m):
        return jnp.broadcast_to(m[None, :, None, :], (1, N_MOD, SUBLANES, D_MODEL))

    mod_x0 = slab_bmajor(mods[0, :BATCH])
    mod_c0 = slab_ctx(mods[0, BATCH])
    w_qkv = attn_w_qkv[0]
    log2e = 1.4426950408889634
    w_qkv = jnp.concatenate([w_qkv[:, :D_Q] * (HEAD_DIM ** -0.5 * log2e), w_qkv[:, D_Q:]], axis=1).astype(BF16)
    sink2 = attn_sink[0] * log2e
    w_o = attn_w_o[0].astype(BF16)
    w1_0, w2_0 = mlp_w1[0].astype(BF16), mlp_w2[0].astype(BF16)
    g0 = norm_g[0]
    tiles_per_batch = SEQ // TOKEN_TILE

    x2 = x.reshape(n_lat, D_MODEL)
    c2 = ctx.reshape(n_ctx, D_MODEL)
    q, k, v = _qkv_call(x2, mod_x0, g0, w_qkv, _rope_tables(), tiles_per_batch)
    qc, kc, vc = _qkv_call(c2, mod_c0, g0, w_qkv, None, n_ctx // TOKEN_TILE)
    kc3 = kc.reshape(BATCH, CTX_LEN, D_KV)
    vc3 = vc.reshape(BATCH, CTX_LEN, D_KV)
    att = _attn_call(sink2, q, k.reshape(BATCH, SEQ, D_KV), v.reshape(BATCH, SEQ, D_KV), kc3, vc3)
    att_c = _ctx_attn_call(sink2, qc, kc3, vc3)
    x2 = _post_call(x2, [att], mod_x0, g0, w_o, w1_0, w2_0, tiles_per_batch, lru=False)
    c2 = _post_call(c2, [att_c], mod_c0, g0, w_o, w1_0, w2_0, n_ctx // TOKEN_TILE, lru=False)

    xt = x2.reshape(BATCH, SEQ, D_MODEL).transpose(1, 0, 2).reshape(n_lat, D_MODEL)
    ct = c2.reshape(BATCH, CTX_LEN, D_MODEL).transpose(1, 0, 2).reshape(n_ctx, D_MODEL)
    mod_x1 = mods[1, :BATCH].transpose(1, 0, 2)[None]
    mod_c1 = slab_ctx(mods[1, BATCH])
    g1 = norm_g[1]
    w_in = lru_w_in[0]
    w_in = jnp.concatenate([w_in[:, :D_RNN], w_in[:, D_RNN:]], axis=1).astype(BF16)
    conv_b = lru_conv_b[0].reshape(1, D_RNN)
    w_a, w_i = lru_w_a[0].astype(BF16), lru_w_i[0].astype(BF16)
    b_a, b_i = lru_b_a[0].reshape(2, 1, D_RNN), lru_b_i[0].reshape(2, 1, D_RNN)
    lam = lru_lam[0].reshape(2, 1, D_RNN)
    scan = functools.partial(_scan_call, conv_w=lru_conv_w[0], conv_b=conv_b, w_a=w_a, b_a=b_a,
                             w_i=w_i, b_i=b_i, lam=lam)

    _, v_c = _lru_in_call(ct, mod_c1, g1, w_in)
    _, _, h_ctx = scan(v_c, jnp.zeros((2, SUBLANES, D_RNN), F32))
    gate_x, v_x = _lru_in_call(xt, mod_x1, g1, w_in)
    yf, yb, _ = scan(v_x, h_ctx)
    xt = _post_call(xt, [gate_x, yf, yb], mod_x1, g1, lru_w_out[0].astype(BF16),
                    mlp_w1[1].astype(BF16), mlp_w2[1].astype(BF16), n_lat // TOKEN_TILE, lru=True)
    return xt.reshape(SEQ, BATCH, D_MODEL).transpose(1, 0, 2)
```

```python
import functools

import jax
import jax.numpy as jnp
from jax import lax
from jax.experimental import pallas as pl
from jax.experimental.pallas import tpu as pltpu

D_MODEL = 1024
BATCH = 8
SEQ = 2048
GRID_W = 64
CTX_LEN = 256
HEAD_DIM = 64
N_HEADS = 16
N_KV_HEADS = 4
GQA_GROUP = N_HEADS // N_KV_HEADS
WINDOW = 128
BLOCK = 128
ROPE_BASE = 10000.0
D_RNN = 1280
LRU_BLOCK_W = 256
N_LRU_BLOCKS = D_RNN // LRU_BLOCK_W
CONV_W = 4
LRU_C = 8.0
D_FF = 4 * D_MODEL
N_MOD = 6
EPS = 1e-6
NEG_INF = -1e30

D_Q = N_HEADS * HEAD_DIM
D_KV = N_KV_HEADS * HEAD_DIM
LANES = 128
SUBLANES = 8
TOKEN_TILE = 512
FF_CHUNK = 1024
SCAN_T = 64
HALO = 16
U_DTYPE = jnp.float32
LOG2E = 1.4426950408889634
VMEM_LIMIT = 56 * 1024 * 1024

F32 = jnp.float32
BF16 = jnp.bfloat16


def _rms(x, g):
    ms = jnp.mean(x * x, axis=-1, keepdims=True)
    return x * lax.rsqrt(ms + EPS) * g


def _slab(x):
    return x.reshape(x.shape[0] // SUBLANES, SUBLANES, x.shape[1])


def _modulate(h, shift8, scale8):
    out = _slab(h) * (1.0 + scale8)[None] + shift8[None]
    return out.reshape(h.shape)


def _gated_add(x, gate8, y):
    out = _slab(x) + gate8[None] * _slab(y)
    return out.reshape(x.shape)


def _const_spec(shape):
    n = len(shape)
    return pl.BlockSpec(shape, lambda *_: (0,) * n, pipeline_mode=pl.Buffered(1))


def _params(sem):
    return pltpu.CompilerParams(dimension_semantics=sem, vmem_limit_bytes=VMEM_LIMIT)


def _mod_kernel(c_ref, w_ref, b_ref, o_ref):
    s = jax.nn.silu(c_ref[...]).astype(BF16)
    o_ref[0] = jnp.dot(s, w_ref[0].astype(BF16), preferred_element_type=F32) + b_ref[0]


def _mod_call(c16, ada_w, ada_b):
    depth = ada_w.shape[0]
    nt = 1024
    return pl.pallas_call(
        _mod_kernel,
        grid=(depth, N_MOD * D_MODEL // nt),
        in_specs=[
            pl.BlockSpec((16, D_MODEL), lambda l, j: (0, 0)),
            pl.BlockSpec((1, D_MODEL, nt), lambda l, j: (l, 0, j)),
            pl.BlockSpec((1, 1, nt), lambda l, j: (l, 0, j)),
        ],
        out_specs=pl.BlockSpec((1, 16, nt), lambda l, j: (l, 0, j)),
        out_shape=jax.ShapeDtypeStruct((depth, 16, N_MOD * D_MODEL), F32),
        compiler_params=_params(("arbitrary", "arbitrary")),
        name="adaln_mod",
    )(c16, ada_w, ada_b.reshape(depth, 1, N_MOD * D_MODEL))


def _qkv_kernel(*refs, rope):
    if rope:
        x_ref, mod_ref, g_ref, w_ref, cos_ref, sa_ref, sb_ref, q_ref, k_ref, v_ref = refs
    else:
        x_ref, mod_ref, g_ref, w_ref, q_ref, k_ref, v_ref = refs
    h = _modulate(_rms(x_ref[...], g_ref[0:1, :]), mod_ref[0, 0], mod_ref[0, 1])
    y = jnp.dot(h.astype(BF16), w_ref[...], preferred_element_type=F32)
    if rope:
        cos, sa, sb = cos_ref[...], sa_ref[...], sb_ref[...]
    for c in range((D_Q + D_KV) // LANES):
        yc = y[:, c * LANES:(c + 1) * LANES]
        if rope:
            yc = yc * cos + pltpu.roll(yc, LANES - 16, 1) * sa + pltpu.roll(yc, 16, 1) * sb
        if c < D_Q // LANES:
            q_ref[:, c * LANES:(c + 1) * LANES] = yc.astype(BF16)
        else:
            c2 = c - D_Q // LANES
            k_ref[:, c2 * LANES:(c2 + 1) * LANES] = yc.astype(BF16)
    v_ref[...] = y[:, D_Q + D_KV:].astype(BF16)


def _qkv_call(x2, mod, g, w_qkv, tables, tiles_per_group):
    n = x2.shape[0]
    tm = TOKEN_TILE
    rope = tables is not None
    in_specs = [
        pl.BlockSpec((tm, D_MODEL), lambda i: (i, 0)),
        pl.BlockSpec((1, N_MOD, SUBLANES, D_MODEL), lambda i: (i // tiles_per_group, 0, 0, 0)),
        _const_spec((4, D_MODEL)),
        _const_spec((D_MODEL, D_Q + 2 * D_KV)),
    ]
    args = [x2, mod, g, w_qkv]
    if rope:
        nt = SEQ // tm
        in_specs += [pl.BlockSpec((tm, LANES), lambda i: (i % nt, 0))] * 3
        args += list(tables)
    return pl.pallas_call(
        functools.partial(_qkv_kernel, rope=rope),
        grid=(n // tm,),
        in_specs=in_specs,
        out_specs=[
            pl.BlockSpec((tm, D_Q), lambda i: (i, 0)),
            pl.BlockSpec((tm, D_KV), lambda i: (i, 0)),
            pl.BlockSpec((tm, D_KV), lambda i: (i, 0)),
        ],
        out_shape=[
            jax.ShapeDtypeStruct((n, D_Q), BF16),
            jax.ShapeDtypeStruct((n, D_KV), BF16),
            jax.ShapeDtypeStruct((n, D_KV), BF16),
        ],
        compiler_params=_params(("parallel",)),
        name="qkv_rope" if rope else "qkv_ctx",
    )(*args)


def _attn_kernel(*refs, local):
    if local:
        sink_ref, q_ref, k_ref, v_ref, kc_ref, vc_ref, o_ref, k2_sc, v2_sc = refs
    else:
        sink_ref, q_ref, kc_ref, vc_ref, o_ref, k2_sc, v2_sc = refs
    j = pl.program_id(1)
    seq_blocks = SEQ // BLOCK if local else 0
    ctx_blocks = CTX_LEN // BLOCK
    half = HEAD_DIM

    @pl.when((pl.program_id(0) == 0) & (j == 0))
    def _init_static():
        k2_sc[...] = jnp.zeros(k2_sc.shape, BF16)
        v2_sc[...] = jnp.zeros(v2_sc.shape, BF16)
        ones = jnp.ones((N_KV_HEADS, seq_blocks + ctx_blocks, BLOCK, half), BF16)
        v2_sc[:, :, 0, :, 2 * half:3 * half] = ones
        v2_sc[:, :, 1, :, 3 * half:4 * half] = ones

    def fill(src_k, src_v, n_blocks, first_block):
        def body(i, carry):
            r0 = pl.multiple_of(i * BLOCK, BLOCK)
            kb = src_k[0, pl.ds(r0, BLOCK), :]
            vb = src_v[0, pl.ds(r0, BLOCK), :]
            for kh in range(N_KV_HEADS):
                cs = slice(kh * half, (kh + 1) * half)
                k2_sc[kh, first_block + i, 0, :, 0:half] = kb[:, cs]
                k2_sc[kh, first_block + i, 1, :, half:2 * half] = kb[:, cs]
                v2_sc[kh, first_block + i, 0, :, 0:half] = vb[:, cs]
                v2_sc[kh, first_block + i, 1, :, half:2 * half] = vb[:, cs]
            return carry
        lax.fori_loop(0, n_blocks, body, 0)

    @pl.when(j == 0)
    def _build():
        if local:
            fill(k_ref, v_ref, seq_blocks, 0)
        fill(kc_ref, vc_ref, ctx_blocks, seq_blocks)

    nt = (((1,), (1,)), ((), ()))
    n_band = 3
    if local:
        blk0 = jnp.clip(j - 1, 0, seq_blocks - n_band)
        qpos = j * BLOCK + lax.broadcasted_iota(jnp.int32, (BLOCK, n_band * BLOCK), 0)
        kpos = blk0 * BLOCK + lax.broadcasted_iota(jnp.int32, (BLOCK, n_band * BLOCK), 1)
        bias = jnp.where(jnp.abs(kpos - qpos) <= WINDOW, 0.0, NEG_INF).astype(F32)
        bias_blocks = [bias[:, i * BLOCK:(i + 1) * BLOCK] for i in range(n_band)]
    low_lanes = lax.broadcasted_iota(jnp.int32, (BLOCK, LANES), 1) < half

    for c in range(D_Q // LANES):
        kh = c // (GQA_GROUP // 2)
        qc = q_ref[:, c * LANES:(c + 1) * LANES]
        k_ctx = k2_sc[kh, seq_blocks:seq_blocks + ctx_blocks].reshape(2 * CTX_LEN, LANES)
        s_ctx = lax.dot_general(qc, k_ctx, nt, preferred_element_type=F32)
        blocks = [s_ctx[:, i * BLOCK:(i + 1) * BLOCK] for i in range(2 * ctx_blocks)]
        if local:
            k_band = k2_sc[kh, pl.ds(blk0, n_band)].reshape(2 * n_band * BLOCK, LANES)
            s_band = lax.dot_general(qc, k_band, nt, preferred_element_type=F32)
            blocks += [s_band[:, i * BLOCK:(i + 1) * BLOCK] + bias_blocks[i // 2] for i in range(2 * n_band)]
        ms = []
        for par in range(2):
            mx = functools.reduce(jnp.maximum, blocks[par::2])
            ms.append(jnp.maximum(jnp.max(mx, axis=-1, keepdims=True), sink_ref[2 * c + par]))
        p = [jnp.exp2(blk - ms[i % 2]).astype(BF16) for i, blk in enumerate(blocks)]
        v_ctx = v2_sc[kh, seq_blocks:seq_blocks + ctx_blocks].reshape(2 * CTX_LEN, 2 * LANES)
        acc = jnp.dot(jnp.concatenate(p[:2 * ctx_blocks], axis=1), v_ctx, preferred_element_type=F32)
        if local:
            v_band = v2_sc[kh, pl.ds(blk0, n_band)].reshape(2 * n_band * BLOCK, 2 * LANES)
            acc = acc + jnp.dot(jnp.concatenate(p[2 * ctx_blocks:], axis=1), v_band, preferred_element_type=F32)
        sink_term = jnp.where(low_lanes, jnp.exp2(sink_ref[2 * c] - ms[0]), jnp.exp2(sink_ref[2 * c + 1] - ms[1]))
        o_ref[:, c * LANES:(c + 1) * LANES] = (acc[:, :LANES] / (acc[:, LANES:] + sink_term)).astype(o_ref.dtype)


def _attn_scratch(n_blocks):
    return [
        pltpu.VMEM((N_KV_HEADS, n_blocks, 2, BLOCK, LANES), BF16),
        pltpu.VMEM((N_KV_HEADS, n_blocks, 2, BLOCK, 2 * LANES), BF16),
    ]


def _attn_call(sink2, q, k, v, kc, vc):
    nb = SEQ // BLOCK
    return pl.pallas_call(
        functools.partial(_attn_kernel, local=True),
        grid=(BATCH, nb),
        in_specs=[
            pl.BlockSpec(memory_space=pltpu.SMEM),
            pl.BlockSpec((BLOCK, D_Q), lambda b, j: (b * nb + j, 0)),
            pl.BlockSpec((1, SEQ, D_KV), lambda b, j: (b, 0, 0)),
            pl.BlockSpec((1, SEQ, D_KV), lambda b, j: (b, 0, 0)),
            pl.BlockSpec((1, CTX_LEN, D_KV), lambda b, j: (b, 0, 0)),
            pl.BlockSpec((1, CTX_LEN, D_KV), lambda b, j: (b, 0, 0)),
        ],
        out_specs=pl.BlockSpec((BLOCK, D_Q), lambda b, j: (b * nb + j, 0)),
        out_shape=jax.ShapeDtypeStruct((BATCH * SEQ, D_Q), BF16),
        scratch_shapes=_attn_scratch(nb + CTX_LEN // BLOCK),
        compiler_params=_params(("arbitrary", "arbitrary")),
        name="band_attn",
    )(sink2, q, k, v, kc, vc)


def _ctx_attn_call(sink2, qc, kc, vc):
    nb = CTX_LEN // BLOCK
    return pl.pallas_call(
        functools.partial(_attn_kernel, local=False),
        grid=(BATCH, nb),
        in_specs=[
            pl.BlockSpec(memory_space=pltpu.SMEM),
            pl.BlockSpec((BLOCK, D_Q), lambda b, j: (b * nb + j, 0)),
            pl.BlockSpec((1, CTX_LEN, D_KV), lambda b, j: (b, 0, 0)),
            pl.BlockSpec((1, CTX_LEN, D_KV), lambda b, j: (b, 0, 0)),
        ],
        out_specs=pl.BlockSpec((BLOCK, D_Q), lambda b, j: (b * nb + j, 0)),
        out_shape=jax.ShapeDtypeStruct((BATCH * CTX_LEN, D_Q), BF16),
        scratch_shapes=_attn_scratch(nb),
        compiler_params=_params(("arbitrary", "arbitrary")),
        name="ctx_attn",
    )(sink2, qc, kc, vc)


def _post_kernel(*refs, lru):
    if lru:
        x_ref, gate_ref, yf_ref, yb_ref, mod_ref, g_ref, wf_ref, w1_ref, w2_ref, o_ref = refs
        front = (gate_ref[...].astype(F32) * (yf_ref[...] + yb_ref[...])).astype(BF16)
    else:
        x_ref, a_ref, mod_ref, g_ref, wf_ref, w1_ref, w2_ref, o_ref = refs
        front = a_ref[...]
    y = jnp.dot(front, wf_ref[...], preferred_element_type=F32)
    x1 = _gated_add(x_ref[...], mod_ref[0, 2], _rms(y, g_ref[1:2, :]))
    h = _modulate(_rms(x1, g_ref[2:3, :]), mod_ref[0, 3], mod_ref[0, 4]).astype(BF16)
    acc = jnp.zeros(x1.shape, F32)
    for c in range(D_FF // FF_CHUNK):
        hid = jnp.dot(h, w1_ref[:, c * FF_CHUNK:(c + 1) * FF_CHUNK], preferred_element_type=F32)
        hid = jnp.square(jnp.maximum(hid, 0.0)).astype(BF16)
        acc = acc + jnp.dot(hid, w2_ref[c * FF_CHUNK:(c + 1) * FF_CHUNK, :], preferred_element_type=F32)
    o_ref[...] = _gated_add(x1, mod_ref[0, 5], _rms(acc, g_ref[3:4, :]))


def _post_call(x2, fronts, mod, g, w_front, w1, w2, tiles_per_group, lru):
    n = x2.shape[0]
    tm = TOKEN_TILE
    row = lambda i: (i, 0)
    in_specs = [pl.BlockSpec((tm, D_MODEL), row)]
    in_specs += [pl.BlockSpec((tm, f.shape[1]), row) for f in fronts]
    in_specs += [
        pl.BlockSpec((1, N_MOD, SUBLANES, D_MODEL), lambda i: (i // tiles_per_group, 0, 0, 0)),
        _const_spec((4, D_MODEL)),
        _const_spec(w_front.shape),
        _const_spec(w1.shape),
        _const_spec(w2.shape),
    ]
    return pl.pallas_call(
        functools.partial(_post_kernel, lru=lru),
        grid=(n // tm,),
        in_specs=in_specs,
        out_specs=pl.BlockSpec((tm, D_MODEL), row),
        out_shape=jax.ShapeDtypeStruct((n, D_MODEL), F32),
        compiler_params=_params(("parallel",)),
        name="lru_out_mlp" if lru else "attn_out_mlp",
    )(x2, *fronts, mod, g, w_front, w1, w2)


def _lru_in_kernel(x_ref, xp_ref, xn_ref, mod_ref, g_ref, w_ref, cw_ref, cb_ref, gate_ref, u_ref, v_sc):
    i = pl.program_id(0)
    n = pl.num_programs(0)
    rows = x_ref.shape[0]
    s8 = SUBLANES

    def pre(x):
        return _modulate(_rms(x, g_ref[0:1, :]), mod_ref[0, 0], mod_ref[0, 1]).astype(BF16)

    h = pre(x_ref[...])
    gate_ref[...] = jax.nn.gelu(jnp.dot(h, w_ref[:, :D_RNN], preferred_element_type=F32)).astype(BF16)
    h_ext = jnp.concatenate([pre(xp_ref[...]), h, pre(xn_ref[...])], axis=0)
    v_sc[...] = jnp.dot(h_ext, w_ref[:, D_RNN:], preferred_element_type=F32)
    v_sc[0:HALO] = v_sc[0:HALO] * (i > 0).astype(F32)
    v_sc[HALO + rows:HALO + rows + s8] = v_sc[HALO + rows:HALO + rows + s8] * (i < n - 1).astype(F32)
    u_ref[...] = (cb_ref[...]
                  + cw_ref[0:1, :] * v_sc[HALO - 2 * s8:HALO - 2 * s8 + rows]
                  + cw_ref[1:2, :] * v_sc[HALO - s8:HALO - s8 + rows]
                  + cw_ref[2:3, :] * v_sc[HALO:HALO + rows]
                  + cw_ref[3:4, :] * v_sc[HALO + s8:HALO + s8 + rows]).astype(u_ref.dtype)


def _lru_in_call(x2, mod, g, w_in, conv_w, conv_b):
    n = x2.shape[0]
    tm = TOKEN_TILE
    row = lambda i: (i, 0)
    per_tile = tm // HALO
    last = n // HALO - 1
    return pl.pallas_call(
        _lru_in_kernel,
        grid=(n // tm,),
        in_specs=[
            pl.BlockSpec((tm, D_MODEL), row),
            pl.BlockSpec((HALO, D_MODEL), lambda i: (jnp.maximum(i * per_tile - 1, 0), 0)),
            pl.BlockSpec((HALO, D_MODEL), lambda i: (jnp.minimum((i + 1) * per_tile, last), 0)),
            _const_spec((1, N_MOD, SUBLANES, D_MODEL)),
            _const_spec((4, D_MODEL)),
            _const_spec((D_MODEL, 2 * D_RNN)),
            _const_spec((CONV_W, D_RNN)),
            _const_spec((1, D_RNN)),
        ],
        out_specs=[pl.BlockSpec((tm, D_RNN), row), pl.BlockSpec((tm, D_RNN), row)],
        out_shape=[jax.ShapeDtypeStruct((n, D_RNN), BF16), jax.ShapeDtypeStruct((n, D_RNN), U_DTYPE)],
        scratch_shapes=[pltpu.VMEM((tm + 2 * HALO, D_RNN), F32)],
        compiler_params=_params(("parallel",)),
        name="lru_in",
    )(x2, x2, x2, mod, g, w_in, conv_w, conv_b)


def _scan_kernel(uf_ref, ub_ref, h0_ref, wa_ref, ba_ref, wi_ref, bi_ref, lam_ref, yf_ref, yb_ref, ht_ref,
                 a_sc, bx_sc, h_sc):
    i = pl.program_id(0)
    n = pl.num_programs(0)
    rows = uf_ref.shape[0]
    nt = rows // SUBLANES
    s8 = SUBLANES

    @pl.when(i == 0)
    def _():
        h_sc[...] = h0_ref[...]

    for d, u_ref in enumerate((uf_ref, ub_ref)):
        for c in range(N_LRU_BLOCKS):
            cs = slice(c * LRU_BLOCK_W, (c + 1) * LRU_BLOCK_W)
            u16 = u_ref[:, cs].astype(BF16)
            u = u_ref[:, cs].astype(F32)
            ta = jnp.tanh(jnp.dot(u16, wa_ref[d, c], preferred_element_type=F32) + 0.5 * ba_ref[d, :, cs])
            ti = jnp.tanh(jnp.dot(u16, wi_ref[d, c], preferred_element_type=F32) + 0.5 * bi_ref[d, :, cs])
            neg_lam = -lam_ref[d, :, cs]
            softplus = jnp.maximum(neg_lam, 0.0) + jnp.log1p(jnp.exp(-jnp.abs(neg_lam)))
            k = (-0.5 * LRU_C * LOG2E) * softplus
            a = jnp.exp2(k * ta + k)
            w = 1.0 - a * a
            root = w * lax.rsqrt(jnp.maximum(w, 1e-30))
            a_sc[d, :, cs] = a
            bx_sc[d, :, cs] = root * (ti * u + u)

    def step(t, carry):
        hf, hb = carry
        rf = pl.multiple_of(t * s8, s8)
        rb = pl.multiple_of((nt - 1 - t) * s8, s8)
        hf = a_sc[0, pl.ds(rf, s8), :] * hf + bx_sc[0, pl.ds(rf, s8), :]
        yf_ref[pl.ds(rf, s8), :] = hf
        hb = a_sc[1, pl.ds(rb, s8), :] * hb + bx_sc[1, pl.ds(rb, s8), :]
        yb_ref[pl.ds(rb, s8), :] = hb
        return hf, hb

    hf, hb = lax.fori_loop(0, nt, step, (h_sc[0], h_sc[1]), unroll=4)
    h_sc[0] = hf
    h_sc[1] = hb

    @pl.when(i == n - 1)
    def _():
        ht_ref[...] = h_sc[...]


def _scan_call(u2, h0, w_a, b_a, w_i, b_i, lam):
    rows_total = u2.shape[0]
    rows = SCAN_T * SUBLANES
    n = rows_total // rows
    w = D_RNN
    fwd = lambda i: (i, 0)
    bwd = lambda i: (n - 1 - i, 0)
    return pl.pallas_call(
        _scan_kernel,
        grid=(n,),
        in_specs=[
            pl.BlockSpec((rows, w), fwd),
            pl.BlockSpec((rows, w), bwd),
            _const_spec((2, SUBLANES, w)),
            _const_spec((2, N_LRU_BLOCKS, LRU_BLOCK_W, LRU_BLOCK_W)),
            _const_spec((2, 1, w)),
            _const_spec((2, N_LRU_BLOCKS, LRU_BLOCK_W, LRU_BLOCK_W)),
            _const_spec((2, 1, w)),
            _const_spec((2, 1, w)),
        ],
        out_specs=[
            pl.BlockSpec((rows, w), fwd),
            pl.BlockSpec((rows, w), bwd),
            pl.BlockSpec((2, SUBLANES, w), lambda i: (0, 0, 0)),
        ],
        out_shape=[
            jax.ShapeDtypeStruct((rows_total, w), F32),
            jax.ShapeDtypeStruct((rows_total, w), F32),
            jax.ShapeDtypeStruct((2, SUBLANES, w), F32),
        ],
        scratch_shapes=[
            pltpu.VMEM((2, rows, w), F32),
            pltpu.VMEM((2, rows, w), F32),
            pltpu.VMEM((2, SUBLANES, w), F32),
        ],
        compiler_params=_params(("arbitrary",)),
        name="lru_scan",
    )(u2, u2, h0, w_a, b_a, w_i, b_i, lam)


def _rope_tables():
    t = jnp.arange(SEQ)
    row = (t // GRID_W).astype(F32)
    col = (t % GRID_W).astype(F32)
    half = HEAD_DIM // 2
    inv = ROPE_BASE ** (-jnp.arange(0, half, 2, dtype=F32) / half)
    ang_r = row[:, None] * inv[None, :]
    ang_c = col[:, None] * inv[None, :]
    ang = jnp.concatenate([ang_r, ang_r, ang_c, ang_c], axis=-1)
    ang = jnp.tile(ang, (1, LANES // HEAD_DIM))
    low = (jnp.arange(LANES) % 32) < 16
    sin = jnp.sin(ang)
    return jnp.cos(ang), jnp.where(low, -sin, 0.0), jnp.where(low, 0.0, sin)


def kernel(x, c, ctx, c_ctx, ada_w, ada_b, norm_g, mlp_w1, mlp_w2, attn_w_qkv, attn_w_o, attn_sink,
           lru_w_in, lru_conv_w, lru_conv_b, lru_w_a, lru_b_a, lru_w_i, lru_b_i, lru_lam, lru_w_out):
    n_lat = BATCH * SEQ
    n_ctx = BATCH * CTX_LEN

    c16 = jnp.zeros((16, D_MODEL), F32).at[:BATCH].set(c).at[BATCH].set(c_ctx)
    mods = _mod_call(c16, ada_w, ada_b).reshape(2, 16, N_MOD, D_MODEL)

    def slab_bmajor(m):
        return jnp.broadcast_to(m[:, :, None, :], (BATCH, N_MOD, SUBLANES, D_MODEL))

    def slab_ctx(m):
        return jnp.broadcast_to(m[None, :, None, :], (1, N_MOD, SUBLANES, D_MODEL))

    mod_x0 = slab_bmajor(mods[0, :BATCH])
    mod_c0 = slab_ctx(mods[0, BATCH])
    w_qkv = attn_w_qkv[0]
    w_qkv = jnp.concatenate([w_qkv[:, :D_Q] * (HEAD_DIM ** -0.5 * LOG2E), w_qkv[:, D_Q:]], axis=1).astype(BF16)
    sink2 = attn_sink[0] * LOG2E
    w_o = attn_w_o[0].astype(BF16)
    w1_0, w2_0 = mlp_w1[0].astype(BF16), mlp_w2[0].astype(BF16)
    g0 = norm_g[0]
    tiles_per_batch = SEQ // TOKEN_TILE

    x2 = x.reshape(n_lat, D_MODEL)
    c2 = ctx.reshape(n_ctx, D_MODEL)
    q, k, v = _qkv_call(x2, mod_x0, g0, w_qkv, _rope_tables(), tiles_per_batch)
    qc, kc, vc = _qkv_call(c2, mod_c0, g0, w_qkv, None, n_ctx // TOKEN_TILE)
    kc3 = kc.reshape(BATCH, CTX_LEN, D_KV)
    vc3 = vc.reshape(BATCH, CTX_LEN, D_KV)
    att = _attn_call(sink2, q, k.reshape(BATCH, SEQ, D_KV), v.reshape(BATCH, SEQ, D_KV), kc3, vc3)
    att_c = _ctx_attn_call(sink2, qc, kc3, vc3)
    x2 = _post_call(x2, [att], mod_x0, g0, w_o, w1_0, w2_0, tiles_per_batch, lru=False)
    c2 = _post_call(c2, [att_c], mod_c0, g0, w_o, w1_0, w2_0, n_ctx // TOKEN_TILE, lru=False)

    xt = x2.reshape(BATCH, SEQ, D_MODEL).transpose(1, 0, 2).reshape(n_lat, D_MODEL)
    ct = c2.reshape(BATCH, CTX_LEN, D_MODEL).transpose(1, 0, 2).reshape(n_ctx, D_MODEL)
    mod_x1 = mods[1, :BATCH].transpose(1, 0, 2)[None]
    mod_c1 = slab_ctx(mods[1, BATCH])
    g1 = norm_g[1]
    w_in = lru_w_in[0].astype(BF16)
    conv_w = 0.5 * lru_conv_w[0]
    conv_b = 0.5 * lru_conv_b[0].reshape(1, D_RNN)
    w_a, w_i = lru_w_a[0].astype(BF16), lru_w_i[0].astype(BF16)
    b_a, b_i = lru_b_a[0].reshape(2, 1, D_RNN), lru_b_i[0].reshape(2, 1, D_RNN)
    lam = lru_lam[0].reshape(2, 1, D_RNN)
    scan = functools.partial(_scan_call, w_a=w_a, b_a=b_a, w_i=w_i, b_i=b_i, lam=lam)

    _, u_c = _lru_in_call(ct, mod_c1, g1, w_in, conv_w, conv_b)
    _, _, h_ctx = scan(u_c, jnp.zeros((2, SUBLANES, D_RNN), F32))
    gate_x, u_x = _lru_in_call(xt, mod_x1, g1, w_in, conv_w, conv_b)
    yf, yb, _ = scan(u_x, h_ctx)
    xt = _post_call(xt, [gate_x, yf, yb], mod_x1, g1, lru_w_out[0].astype(BF16),
                    mlp_w1[1].astype(BF16), mlp_w2[1].astype(BF16), n_lat // TOKEN_TILE, lru=True)
    return xt.reshape(SEQ, BATCH, D_MODEL).transpose(1, 0, 2)
```

```python
import functools

import jax
import jax.numpy as jnp
from jax import lax
from jax.experimental import pallas as pl
from jax.experimental.pallas import tpu as pltpu

D_MODEL = 1024
BATCH = 8
SEQ = 2048
GRID_W = 64
CTX_LEN = 256
HEAD_DIM = 64
N_HEADS = 16
N_KV_HEADS = 4
GQA_GROUP = N_HEADS // N_KV_HEADS
WINDOW = 128
BLOCK = 128
ROPE_BASE = 10000.0
D_RNN = 1280
LRU_BLOCK_W = 256
N_LRU_BLOCKS = D_RNN // LRU_BLOCK_W
CONV_W = 4
LRU_C = 8.0
D_FF = 4 * D_MODEL
N_MOD = 6
EPS = 1e-6
NEG_INF = -1e30

D_Q = N_HEADS * HEAD_DIM
D_KV = N_KV_HEADS * HEAD_DIM
LANES = 128
SUBLANES = 8
TOKEN_TILE = 512
FF_CHUNK = 1024
SCAN_T = 64
HALO = 16
U_DTYPE = jnp.bfloat16
Y_DTYPE = jnp.bfloat16
LOG2E = 1.4426950408889634
VMEM_LIMIT = 56 * 1024 * 1024

F32 = jnp.float32
BF16 = jnp.bfloat16


def _rms(x, g):
    ms = jnp.mean(x * x, axis=-1, keepdims=True)
    return x * lax.rsqrt(ms + EPS) * g


def _slab(x):
    return x.reshape(x.shape[0] // SUBLANES, SUBLANES, x.shape[1])


def _modulate(h, shift8, scale8):
    out = _slab(h) * (1.0 + scale8)[None] + shift8[None]
    return out.reshape(h.shape)


def _gated_add(x, gate8, y):
    out = _slab(x) + gate8[None] * _slab(y)
    return out.reshape(x.shape)


def _const_spec(shape):
    n = len(shape)
    return pl.BlockSpec(shape, lambda *_: (0,) * n, pipeline_mode=pl.Buffered(1))


def _params(sem):
    return pltpu.CompilerParams(dimension_semantics=sem, vmem_limit_bytes=VMEM_LIMIT)


def _mod_kernel(c_ref, w_ref, b_ref, o_ref):
    s = jax.nn.silu(c_ref[...]).astype(BF16)
    o_ref[0] = jnp.dot(s, w_ref[0].astype(BF16), preferred_element_type=F32) + b_ref[0]


def _mod_call(c16, ada_w, ada_b):
    depth = ada_w.shape[0]
    nt = 1024
    return pl.pallas_call(
        _mod_kernel,
        grid=(depth, N_MOD * D_MODEL // nt),
        in_specs=[
            pl.BlockSpec((16, D_MODEL), lambda l, j: (0, 0)),
            pl.BlockSpec((1, D_MODEL, nt), lambda l, j: (l, 0, j)),
            pl.BlockSpec((1, 1, nt), lambda l, j: (l, 0, j)),
        ],
        out_specs=pl.BlockSpec((1, 16, nt), lambda l, j: (l, 0, j)),
        out_shape=jax.ShapeDtypeStruct((depth, 16, N_MOD * D_MODEL), F32),
        compiler_params=_params(("arbitrary", "arbitrary")),
        name="adaln_mod",
    )(c16, ada_w, ada_b.reshape(depth, 1, N_MOD * D_MODEL))


def _qkv_kernel(*refs, rope):
    if rope:
        x_ref, mod_ref, g_ref, w_ref, cos_ref, sa_ref, sb_ref, q_ref, k_ref, v_ref = refs
    else:
        x_ref, mod_ref, g_ref, w_ref, q_ref, k_ref, v_ref = refs
    h = _modulate(_rms(x_ref[...], g_ref[0:1, :]), mod_ref[0, 0], mod_ref[0, 1])
    y = jnp.dot(h.astype(BF16), w_ref[...], preferred_element_type=F32)
    if rope:
        cos, sa, sb = cos_ref[...], sa_ref[...], sb_ref[...]
    for c in range((D_Q + D_KV) // LANES):
        yc = y[:, c * LANES:(c + 1) * LANES]
        if rope:
            yc = yc * cos + pltpu.roll(yc, LANES - 16, 1) * sa + pltpu.roll(yc, 16, 1) * sb
        if c < D_Q // LANES:
            q_ref[:, c * LANES:(c + 1) * LANES] = yc.astype(BF16)
        else:
            c2 = c - D_Q // LANES
            k_ref[:, c2 * LANES:(c2 + 1) * LANES] = yc.astype(BF16)
    v_ref[...] = y[:, D_Q + D_KV:].astype(BF16)


def _qkv_call(x2, mod, g, w_qkv, tables, tiles_per_group):
    n = x2.shape[0]
    tm = TOKEN_TILE
    rope = tables is not None
    in_specs = [
        pl.BlockSpec((tm, D_MODEL), lambda i: (i, 0)),
        pl.BlockSpec((1, N_MOD, SUBLANES, D_MODEL), lambda i: (i // tiles_per_group, 0, 0, 0)),
        _const_spec((4, D_MODEL)),
        _const_spec((D_MODEL, D_Q + 2 * D_KV)),
    ]
    args = [x2, mod, g, w_qkv]
    if rope:
        nt = SEQ // tm
        in_specs += [pl.BlockSpec((tm, LANES), lambda i: (i % nt, 0))] * 3
        args += list(tables)
    return pl.pallas_call(
        functools.partial(_qkv_kernel, rope=rope),
        grid=(n // tm,),
        in_specs=in_specs,
        out_specs=[
            pl.BlockSpec((tm, D_Q), lambda i: (i, 0)),
            pl.BlockSpec((tm, D_KV), lambda i: (i, 0)),
            pl.BlockSpec((tm, D_KV), lambda i: (i, 0)),
        ],
        out_shape=[
            jax.ShapeDtypeStruct((n, D_Q), BF16),
            jax.ShapeDtypeStruct((n, D_KV), BF16),
            jax.ShapeDtypeStruct((n, D_KV), BF16),
        ],
        compiler_params=_params(("parallel",)),
        name="qkv_rope" if rope else "qkv_ctx",
    )(*args)


def _attn_kernel(*refs, local):
    if local:
        sink_ref, q_ref, k_ref, v_ref, kc_ref, vc_ref, o_ref, k2_sc, v2_sc = refs
    else:
        sink_ref, q_ref, kc_ref, vc_ref, o_ref, k2_sc, v2_sc = refs
    j = pl.program_id(1)
    seq_blocks = SEQ // BLOCK if local else 0
    ctx_blocks = CTX_LEN // BLOCK
    half = HEAD_DIM

    @pl.when((pl.program_id(0) == 0) & (j == 0))
    def _init_static():
        k2_sc[...] = jnp.zeros(k2_sc.shape, BF16)
        v2_sc[...] = jnp.zeros(v2_sc.shape, BF16)
        ones = jnp.ones((N_KV_HEADS, seq_blocks + ctx_blocks, BLOCK, half), BF16)
        v2_sc[:, :, 0, :, 2 * half:3 * half] = ones
        v2_sc[:, :, 1, :, 3 * half:4 * half] = ones

    def fill(src_k, src_v, n_blocks, first_block):
        def body(i, carry):
            r0 = pl.multiple_of(i * BLOCK, BLOCK)
            kb = src_k[0, pl.ds(r0, BLOCK), :]
            vb = src_v[0, pl.ds(r0, BLOCK), :]
            for kh in range(N_KV_HEADS):
                cs = slice(kh * half, (kh + 1) * half)
                k2_sc[kh, first_block + i, 0, :, 0:half] = kb[:, cs]
                k2_sc[kh, first_block + i, 1, :, half:2 * half] = kb[:, cs]
                v2_sc[kh, first_block + i, 0, :, 0:half] = vb[:, cs]
                v2_sc[kh, first_block + i, 1, :, half:2 * half] = vb[:, cs]
            return carry
        lax.fori_loop(0, n_blocks, body, 0)

    @pl.when(j == 0)
    def _build():
        if local:
            fill(k_ref, v_ref, seq_blocks, 0)
        fill(kc_ref, vc_ref, ctx_blocks, seq_blocks)

    nt = (((1,), (1,)), ((), ()))
    n_band = 3
    if local:
        blk0 = jnp.clip(j - 1, 0, seq_blocks - n_band)
        qpos = j * BLOCK + lax.broadcasted_iota(jnp.int32, (BLOCK, n_band * BLOCK), 0)
        kpos = blk0 * BLOCK + lax.broadcasted_iota(jnp.int32, (BLOCK, n_band * BLOCK), 1)
        bias = jnp.where(jnp.abs(kpos - qpos) <= WINDOW, 0.0, NEG_INF).astype(F32)
        bias_blocks = [bias[:, i * BLOCK:(i + 1) * BLOCK] for i in range(n_band)]
    low_lanes = lax.broadcasted_iota(jnp.int32, (BLOCK, LANES), 1) < half

    for c in range(D_Q // LANES):
        kh = c // (GQA_GROUP // 2)
        qc = q_ref[:, c * LANES:(c + 1) * LANES]
        k_ctx = k2_sc[kh, seq_blocks:seq_blocks + ctx_blocks].reshape(2 * CTX_LEN, LANES)
        s_ctx = lax.dot_general(qc, k_ctx, nt, preferred_element_type=F32)
        blocks = [s_ctx[:, i * BLOCK:(i + 1) * BLOCK] for i in range(2 * ctx_blocks)]
        if local:
            k_band = k2_sc[kh, pl.ds(blk0, n_band)].reshape(2 * n_band * BLOCK, LANES)
            s_band = lax.dot_general(qc, k_band, nt, preferred_element_type=F32)
            blocks += [s_band[:, i * BLOCK:(i + 1) * BLOCK] + bias_blocks[i // 2] for i in range(2 * n_band)]
        ms = []
        for par in range(2):
            mx = functools.reduce(jnp.maximum, blocks[par::2])
            ms.append(jnp.maximum(jnp.max(mx, axis=-1, keepdims=True), sink_ref[2 * c + par]))
        p = [jnp.exp2(blk - ms[i % 2]).astype(BF16) for i, blk in enumerate(blocks)]
        v_ctx = v2_sc[kh, seq_blocks:seq_blocks + ctx_blocks].reshape(2 * CTX_LEN, 2 * LANES)
        acc = jnp.dot(jnp.concatenate(p[:2 * ctx_blocks], axis=1), v_ctx, preferred_element_type=F32)
        if local:
            v_band = v2_sc[kh, pl.ds(blk0, n_band)].reshape(2 * n_band * BLOCK, 2 * LANES)
            acc = acc + jnp.dot(jnp.concatenate(p[2 * ctx_blocks:], axis=1), v_band, preferred_element_type=F32)
        sink_term = jnp.where(low_lanes, jnp.exp2(sink_ref[2 * c] - ms[0]), jnp.exp2(sink_ref[2 * c + 1] - ms[1]))
        o_ref[:, c * LANES:(c + 1) * LANES] = (acc[:, :LANES] / (acc[:, LANES:] + sink_term)).astype(o_ref.dtype)


def _attn_scratch(n_blocks):
    return [
        pltpu.VMEM((N_KV_HEADS, n_blocks, 2, BLOCK, LANES), BF16),
        pltpu.VMEM((N_KV_HEADS, n_blocks, 2, BLOCK, 2 * LANES), BF16),
    ]


def _attn_call(sink2, q, k, v, kc, vc):
    nb = SEQ // BLOCK
    return pl.pallas_call(
        functools.partial(_attn_kernel, local=True),
        grid=(BATCH, nb),
        in_specs=[
            pl.BlockSpec(memory_space=pltpu.SMEM),
            pl.BlockSpec((BLOCK, D_Q), lambda b, j: (b * nb + j, 0)),
            pl.BlockSpec((1, SEQ, D_KV), lambda b, j: (b, 0, 0)),
            pl.BlockSpec((1, SEQ, D_KV), lambda b, j: (b, 0, 0)),
            pl.BlockSpec((1, CTX_LEN, D_KV), lambda b, j: (b, 0, 0)),
            pl.BlockSpec((1, CTX_LEN, D_KV), lambda b, j: (b, 0, 0)),
        ],
        out_specs=pl.BlockSpec((BLOCK, D_Q), lambda b, j: (b * nb + j, 0)),
        out_shape=jax.ShapeDtypeStruct((BATCH * SEQ, D_Q), BF16),
        scratch_shapes=_attn_scratch(nb + CTX_LEN // BLOCK),
        compiler_params=_params(("arbitrary", "arbitrary")),
        name="band_attn",
    )(sink2, q, k, v, kc, vc)


def _ctx_attn_call(sink2, qc, kc, vc):
    nb = CTX_LEN // BLOCK
    return pl.pallas_call(
        functools.partial(_attn_kernel, local=False),
        grid=(BATCH, nb),
        in_specs=[
            pl.BlockSpec(memory_space=pltpu.SMEM),
            pl.BlockSpec((BLOCK, D_Q), lambda b, j: (b * nb + j, 0)),
            pl.BlockSpec((1, CTX_LEN, D_KV), lambda b, j: (b, 0, 0)),
            pl.BlockSpec((1, CTX_LEN, D_KV), lambda b, j: (b, 0, 0)),
        ],
        out_specs=pl.BlockSpec((BLOCK, D_Q), lambda b, j: (b * nb + j, 0)),
        out_shape=jax.ShapeDtypeStruct((BATCH * CTX_LEN, D_Q), BF16),
        scratch_shapes=_attn_scratch(nb),
        compiler_params=_params(("arbitrary", "arbitrary")),
        name="ctx_attn",
    )(sink2, qc, kc, vc)


def _post_kernel(*refs, lru):
    if lru:
        x_ref, gate_ref, yf_ref, yb_ref, mod_ref, g_ref, wf_ref, w1_ref, w2_ref, o_ref = refs
        front = (gate_ref[...].astype(F32) * (yf_ref[...].astype(F32) + yb_ref[...].astype(F32))).astype(BF16)
    else:
        x_ref, a_ref, mod_ref, g_ref, wf_ref, w1_ref, w2_ref, o_ref = refs
        front = a_ref[...]
    y = jnp.dot(front, wf_ref[...], preferred_element_type=F32)
    x1 = _gated_add(x_ref[...], mod_ref[0, 2], _rms(y, g_ref[1:2, :]))
    h = _modulate(_rms(x1, g_ref[2:3, :]), mod_ref[0, 3], mod_ref[0, 4]).astype(BF16)
    acc = jnp.zeros(x1.shape, F32)
    for c in range(D_FF // FF_CHUNK):
        hid = jnp.dot(h, w1_ref[:, c * FF_CHUNK:(c + 1) * FF_CHUNK], preferred_element_type=F32)
        hid = jnp.square(jnp.maximum(hid, 0.0)).astype(BF16)
        acc = acc + jnp.dot(hid, w2_ref[c * FF_CHUNK:(c + 1) * FF_CHUNK, :], preferred_element_type=F32)
    o_ref[...] = _gated_add(x1, mod_ref[0, 5], _rms(acc, g_ref[3:4, :]))


def _post_call(x2, fronts, mod, g, w_front, w1, w2, tiles_per_group, lru):
    n = x2.shape[0]
    tm = TOKEN_TILE
    row = lambda i: (i, 0)
    in_specs = [pl.BlockSpec((tm, D_MODEL), row)]
    in_specs += [pl.BlockSpec((tm, f.shape[1]), row) for f in fronts]
    in_specs += [
        pl.BlockSpec((1, N_MOD, SUBLANES, D_MODEL), lambda i: (i // tiles_per_group, 0, 0, 0)),
        _const_spec((4, D_MODEL)),
        _const_spec(w_front.shape),
        _const_spec(w1.shape),
        _const_spec(w2.shape),
    ]
    return pl.pallas_call(
        functools.partial(_post_kernel, lru=lru),
        grid=(n // tm,),
        in_specs=in_specs,
        out_specs=pl.BlockSpec((tm, D_MODEL), row),
        out_shape=jax.ShapeDtypeStruct((n, D_MODEL), F32),
        compiler_params=_params(("parallel",)),
        name="lru_out_mlp" if lru else "attn_out_mlp",
    )(x2, *fronts, mod, g, w_front, w1, w2)


def _lru_in_kernel(x_ref, xp_ref, xn_ref, mod_ref, g_ref, w_ref, cw_ref, cb_ref, gate_ref, u_ref, v_sc):
    i = pl.program_id(0)
    n = pl.num_programs(0)
    rows = x_ref.shape[0]
    s8 = SUBLANES

    def pre(x):
        return _modulate(_rms(x, g_ref[0:1, :]), mod_ref[0, 0], mod_ref[0, 1]).astype(BF16)

    h = pre(x_ref[...])
    gate_ref[...] = jax.nn.gelu(jnp.dot(h, w_ref[:, :D_RNN], preferred_element_type=F32)).astype(BF16)
    h_ext = jnp.concatenate([pre(xp_ref[...]), h, pre(xn_ref[...])], axis=0)
    v_sc[...] = jnp.dot(h_ext, w_ref[:, D_RNN:], preferred_element_type=F32)
    v_sc[0:HALO] = v_sc[0:HALO] * (i > 0).astype(F32)
    v_sc[HALO + rows:HALO + rows + s8] = v_sc[HALO + rows:HALO + rows + s8] * (i < n - 1).astype(F32)
    u_ref[...] = (cb_ref[...]
                  + cw_ref[0:1, :] * v_sc[HALO - 2 * s8:HALO - 2 * s8 + rows]
                  + cw_ref[1:2, :] * v_sc[HALO - s8:HALO - s8 + rows]
                  + cw_ref[2:3, :] * v_sc[HALO:HALO + rows]
                  + cw_ref[3:4, :] * v_sc[HALO + s8:HALO + s8 + rows]).astype(u_ref.dtype)


def _lru_in_call(x2, mod, g, w_in, conv_w, conv_b):
    n = x2.shape[0]
    tm = TOKEN_TILE
    row = lambda i: (i, 0)
    per_tile = tm // HALO
    last = n // HALO - 1
    return pl.pallas_call(
        _lru_in_kernel,
        grid=(n // tm,),
        in_specs=[
            pl.BlockSpec((tm, D_MODEL), row),
            pl.BlockSpec((HALO, D_MODEL), lambda i: (jnp.maximum(i * per_tile - 1, 0), 0)),
            pl.BlockSpec((HALO, D_MODEL), lambda i: (jnp.minimum((i + 1) * per_tile, last), 0)),
            _const_spec((1, N_MOD, SUBLANES, D_MODEL)),
            _const_spec((4, D_MODEL)),
            _const_spec((D_MODEL, 2 * D_RNN)),
            _const_spec((CONV_W, D_RNN)),
            _const_spec((1, D_RNN)),
        ],
        out_specs=[pl.BlockSpec((tm, D_RNN), row), pl.BlockSpec((tm, D_RNN), row)],
        out_shape=[jax.ShapeDtypeStruct((n, D_RNN), BF16), jax.ShapeDtypeStruct((n, D_RNN), U_DTYPE)],
        scratch_shapes=[pltpu.VMEM((tm + 2 * HALO, D_RNN), F32)],
        compiler_params=_params(("parallel",)),
        name="lru_in",
    )(x2, x2, x2, mod, g, w_in, conv_w, conv_b)


def _scan_kernel(uf_ref, ub_ref, h0_ref, wa_ref, ba_ref, wi_ref, bi_ref, lam_ref, yf_ref, yb_ref, ht_ref,
                 a_sc, bx_sc, h_sc):
    i = pl.program_id(0)
    n = pl.num_programs(0)
    rows = uf_ref.shape[0]
    nt = rows // SUBLANES
    s8 = SUBLANES

    @pl.when(i == 0)
    def _():
        h_sc[...] = h0_ref[...]

    for d, u_ref in enumerate((uf_ref, ub_ref)):
        for c in range(N_LRU_BLOCKS):
            cs = slice(c * LRU_BLOCK_W, (c + 1) * LRU_BLOCK_W)
            u16 = u_ref[:, cs].astype(BF16)
            u = u_ref[:, cs].astype(F32)
            ta = jnp.tanh(jnp.dot(u16, wa_ref[d, c], preferred_element_type=F32) + 0.5 * ba_ref[d, :, cs])
            ti = jnp.tanh(jnp.dot(u16, wi_ref[d, c], preferred_element_type=F32) + 0.5 * bi_ref[d, :, cs])
            neg_lam = -lam_ref[d, :, cs]
            softplus = jnp.maximum(neg_lam, 0.0) + jnp.log1p(jnp.exp(-jnp.abs(neg_lam)))
            k = (-0.5 * LRU_C * LOG2E) * softplus
            a = jnp.exp2(k * ta + k)
            w = 1.0 - a * a
            root = w * lax.rsqrt(jnp.maximum(w, 1e-30))
            a_sc[d, :, cs] = a
            bx_sc[d, :, cs] = root * (ti * u + u)

    def step(t, carry):
        hf, hb = carry
        rf = pl.multiple_of(t * 2 * s8, 2 * s8)
        rb = pl.multiple_of((nt - 2 - 2 * t) * s8, 2 * s8)
        hf1 = a_sc[0, pl.ds(rf, s8), :] * hf + bx_sc[0, pl.ds(rf, s8), :]
        hf2 = a_sc[0, pl.ds(rf + s8, s8), :] * hf1 + bx_sc[0, pl.ds(rf + s8, s8), :]
        yf_ref[pl.ds(rf, 2 * s8), :] = jnp.concatenate([hf1, hf2], axis=0).astype(yf_ref.dtype)
        hb1 = a_sc[1, pl.ds(rb + s8, s8), :] * hb + bx_sc[1, pl.ds(rb + s8, s8), :]
        hb2 = a_sc[1, pl.ds(rb, s8), :] * hb1 + bx_sc[1, pl.ds(rb, s8), :]
        yb_ref[pl.ds(rb, 2 * s8), :] = jnp.concatenate([hb2, hb1], axis=0).astype(yb_ref.dtype)
        return hf2, hb2

    hf, hb = lax.fori_loop(0, nt // 2, step, (h_sc[0], h_sc[1]), unroll=2)
    h_sc[0] = hf
    h_sc[1] = hb

    @pl.when(i == n - 1)
    def _():
        ht_ref[...] = h_sc[...]


def _scan_call(u2, h0, w_a, b_a, w_i, b_i, lam):
    rows_total = u2.shape[0]
    rows = SCAN_T * SUBLANES
    n = rows_total // rows
    w = D_RNN
    fwd = lambda i: (i, 0)
    bwd = lambda i: (n - 1 - i, 0)
    return pl.pallas_call(
        _scan_kernel,
        grid=(n,),
        in_specs=[
            pl.BlockSpec((rows, w), fwd),
            pl.BlockSpec((rows, w), bwd),
            _const_spec((2, SUBLANES, w)),
            _const_spec((2, N_LRU_BLOCKS, LRU_BLOCK_W, LRU_BLOCK_W)),
            _const_spec((2, 1, w)),
            _const_spec((2, N_LRU_BLOCKS, LRU_BLOCK_W, LRU_BLOCK_W)),
            _const_spec((2, 1, w)),
            _const_spec((2, 1, w)),
        ],
        out_specs=[
            pl.BlockSpec((rows, w), fwd),
            pl.BlockSpec((rows, w), bwd),
            pl.BlockSpec((2, SUBLANES, w), lambda i: (0, 0, 0)),
        ],
        out_shape=[
            jax.ShapeDtypeStruct((rows_total, w), Y_DTYPE),
            jax.ShapeDtypeStruct((rows_total, w), Y_DTYPE),
            jax.ShapeDtypeStruct((2, SUBLANES, w), F32),
        ],
        scratch_shapes=[
            pltpu.VMEM((2, rows, w), F32),
            pltpu.VMEM((2, rows, w), F32),
            pltpu.VMEM((2, SUBLANES, w), F32),
        ],
        compiler_params=_params(("arbitrary",)),
        name="lru_scan",
    )(u2, u2, h0, w_a, b_a, w_i, b_i, lam)


def _rope_tables():
    t = jnp.arange(SEQ)
    row = (t // GRID_W).astype(F32)
    col = (t % GRID_W).astype(F32)
    half = HEAD_DIM // 2
    inv = ROPE_BASE ** (-jnp.arange(0, half, 2, dtype=F32) / half)
    ang_r = row[:, None] * inv[None, :]
    ang_c = col[:, None] * inv[None, :]
    ang = jnp.concatenate([ang_r, ang_r, ang_c, ang_c], axis=-1)
    ang = jnp.tile(ang, (1, LANES // HEAD_DIM))
    low = (jnp.arange(LANES) % 32) < 16
    sin = jnp.sin(ang)
    return jnp.cos(ang), jnp.where(low, -sin, 0.0), jnp.where(low, 0.0, sin)


def kernel(x, c, ctx, c_ctx, ada_w, ada_b, norm_g, mlp_w1, mlp_w2, attn_w_qkv, attn_w_o, attn_sink,
           lru_w_in, lru_conv_w, lru_conv_b, lru_w_a, lru_b_a, lru_w_i, lru_b_i, lru_lam, lru_w_out):
    n_lat = BATCH * SEQ
    n_ctx = BATCH * CTX_LEN

    c16 = jnp.zeros((16, D_MODEL), F32).at[:BATCH].set(c).at[BATCH].set(c_ctx)
    mods = _mod_call(c16, ada_w, ada_b).reshape(2, 16, N_MOD, D_MODEL)

    def slab_bmajor(m):
        return jnp.broadcast_to(m[:, :, None, :], (BATCH, N_MOD, SUBLANES, D_MODEL))

    def slab_ctx(m):
        return jnp.broadcast_to(m[None, :, None, :], (1, N_MOD, SUBLANES, D_MODEL))

    mod_x0 = slab_bmajor(mods[0, :BATCH])
    mod_c0 = slab_ctx(mods[0, BATCH])
    w_qkv = attn_w_qkv[0]
    w_qkv = jnp.concatenate([w_qkv[:, :D_Q] * (HEAD_DIM ** -0.5 * LOG2E), w_qkv[:, D_Q:]], axis=1).astype(BF16)
    sink2 = attn_sink[0] * LOG2E
    w_o = attn_w_o[0].astype(BF16)
    w1_0, w2_0 = mlp_w1[0].astype(BF16), mlp_w2[0].astype(BF16)
    g0 = norm_g[0]
    tiles_per_batch = SEQ // TOKEN_TILE

    x2 = x.reshape(n_lat, D_MODEL)
    c2 = ctx.reshape(n_ctx, D_MODEL)
    q, k, v = _qkv_call(x2, mod_x0, g0, w_qkv, _rope_tables(), tiles_per_batch)
    qc, kc, vc = _qkv_call(c2, mod_c0, g0, w_qkv, None, n_ctx // TOKEN_TILE)
    kc3 = kc.reshape(BATCH, CTX_LEN, D_KV)
    vc3 = vc.reshape(BATCH, CTX_LEN, D_KV)
    att = _attn_call(sink2, q, k.reshape(BATCH, SEQ, D_KV), v.reshape(BATCH, SEQ, D_KV), kc3, vc3)
    att_c = _ctx_attn_call(sink2, qc, kc3, vc3)
    x2 = _post_call(x2, [att], mod_x0, g0, w_o, w1_0, w2_0, tiles_per_batch, lru=False)
    c2 = _post_call(c2, [att_c], mod_c0, g0, w_o, w1_0, w2_0, n_ctx // TOKEN_TILE, lru=False)

    xt = x2.reshape(BATCH, SEQ, D_MODEL).transpose(1, 0, 2).reshape(n_lat, D_MODEL)
    ct = c2.reshape(BATCH, CTX_LEN, D_MODEL).transpose(1, 0, 2).reshape(n_ctx, D_MODEL)
    mod_x1 = mods[1, :BATCH].transpose(1, 0, 2)[None]
    mod_c1 = slab_ctx(mods[1, BATCH])
    g1 = norm_g[1]
    w_in = lru_w_in[0].astype(BF16)
    conv_w = 0.5 * lru_conv_w[0]
    conv_b = 0.5 * lru_conv_b[0].reshape(1, D_RNN)
    w_a, w_i = lru_w_a[0].astype(BF16), lru_w_i[0].astype(BF16)
    b_a, b_i = lru_b_a[0].reshape(2, 1, D_RNN), lru_b_i[0].reshape(2, 1, D_RNN)
    lam = lru_lam[0].reshape(2, 1, D_RNN)
    scan = functools.partial(_scan_call, w_a=w_a, b_a=b_a, w_i=w_i, b_i=b_i, lam=lam)

    _, u_c = _lru_in_call(ct, mod_c1, g1, w_in, conv_w, conv_b)
    _, _, h_ctx = scan(u_c, jnp.zeros((2, SUBLANES, D_RNN), F32))
    gate_x, u_x = _lru_in_call(xt, mod_x1, g1, w_in, conv_w, conv_b)
    yf, yb, _ = scan(u_x, h_ctx)
    xt = _post_call(xt, [gate_x, yf, yb], mod_x1, g1, lru_w_out[0].astype(BF16),
                    mlp_w1[1].astype(BF16), mlp_w2[1].astype(BF16), n_lat // TOKEN_TILE, lru=True)
    return xt.reshape(SEQ, BATCH, D_MODEL).transpose(1, 0, 2)
```

```python
import functools

import jax
import jax.numpy as jnp
from jax import lax
from jax.experimental import pallas as pl
from jax.experimental.pallas import tpu as pltpu

D_MODEL = 1024
BATCH = 8
SEQ = 2048
GRID_W = 64
CTX_LEN = 256
HEAD_DIM = 64
N_HEADS = 16
N_KV_HEADS = 4
GQA_GROUP = N_HEADS // N_KV_HEADS
WINDOW = 128
BLOCK = 128
ROPE_BASE = 10000.0
D_RNN = 1280
LRU_BLOCK_W = 256
N_LRU_BLOCKS = D_RNN // LRU_BLOCK_W
CONV_W = 4
LRU_C = 8.0
D_FF = 4 * D_MODEL
N_MOD = 6
EPS = 1e-6
NEG_INF = -1e30

D_Q = N_HEADS * HEAD_DIM
D_KV = N_KV_HEADS * HEAD_DIM
LANES = 128
SUBLANES = 8
N_SLAB = D_MODEL // LANES
TOKEN_TILE = 512
FF_CHUNK = 1024
SCAN_T = 64
HALO = 16
U_DTYPE = jnp.bfloat16
Y_DTYPE = jnp.bfloat16
LOG2E = 1.4426950408889634
VMEM_LIMIT = 56 * 1024 * 1024

F32 = jnp.float32
BF16 = jnp.bfloat16


def _rms(x, g):
    ms = jnp.mean(x * x, axis=-1, keepdims=True)
    return x * lax.rsqrt(ms + EPS) * g


def _slab(x):
    return x.reshape(x.shape[0] // SUBLANES, SUBLANES, x.shape[1])


def _modulate(h, shift8, scale8):
    out = _slab(h) * (1.0 + scale8)[None] + shift8[None]
    return out.reshape(h.shape)


def _gated_add(x, gate8, y):
    out = _slab(x) + gate8[None] * _slab(y)
    return out.reshape(x.shape)


def _const_spec(shape):
    n = len(shape)
    return pl.BlockSpec(shape, lambda *_: (0,) * n, pipeline_mode=pl.Buffered(1))


def _params(sem):
    return pltpu.CompilerParams(dimension_semantics=sem, vmem_limit_bytes=VMEM_LIMIT)


def _mod_kernel(c_ref, w_ref, b_ref, o_ref):
    s = jax.nn.silu(c_ref[...]).astype(BF16)
    o_ref[0] = jnp.dot(s, w_ref[0].astype(BF16), preferred_element_type=F32) + b_ref[0]


def _mod_call(c16, ada_w, ada_b):
    depth = ada_w.shape[0]
    nt = 1024
    return pl.pallas_call(
        _mod_kernel,
        grid=(depth, N_MOD * D_MODEL // nt),
        in_specs=[
            pl.BlockSpec((16, D_MODEL), lambda l, j: (0, 0)),
            pl.BlockSpec((1, D_MODEL, nt), lambda l, j: (l, 0, j)),
            pl.BlockSpec((1, 1, nt), lambda l, j: (l, 0, j)),
        ],
        out_specs=pl.BlockSpec((1, 16, nt), lambda l, j: (l, 0, j)),
        out_shape=jax.ShapeDtypeStruct((depth, 16, N_MOD * D_MODEL), F32),
        compiler_params=_params(("arbitrary", "arbitrary")),
        name="adaln_mod",
    )(c16, ada_w, ada_b.reshape(depth, 1, N_MOD * D_MODEL))


def _qkv_kernel(*refs, rope):
    if rope:
        x_ref, mod_ref, g_ref, w_ref, cos_ref, sa_ref, sb_ref, q_ref, k_ref, v_ref = refs
    else:
        x_ref, mod_ref, g_ref, w_ref, q_ref, k_ref, v_ref = refs
    h = _modulate(_rms(x_ref[...], g_ref[0:1, :]), mod_ref[0, 0], mod_ref[0, 1])
    y = jnp.dot(h.astype(BF16), w_ref[...], preferred_element_type=F32)
    if rope:
        cos, sa, sb = cos_ref[...], sa_ref[...], sb_ref[...]
    for c in range((D_Q + D_KV) // LANES):
        yc = y[:, c * LANES:(c + 1) * LANES]
        if rope:
            yc = yc * cos + pltpu.roll(yc, LANES - 16, 1) * sa + pltpu.roll(yc, 16, 1) * sb
        if c < D_Q // LANES:
            q_ref[:, c * LANES:(c + 1) * LANES] = yc.astype(BF16)
        else:
            c2 = c - D_Q // LANES
            k_ref[:, c2 * LANES:(c2 + 1) * LANES] = yc.astype(BF16)
    v_ref[...] = y[:, D_Q + D_KV:].astype(BF16)


def _qkv_call(x2, mod, g, w_qkv, tables, tiles_per_group):
    n = x2.shape[0]
    tm = TOKEN_TILE
    rope = tables is not None
    in_specs = [
        pl.BlockSpec((tm, D_MODEL), lambda i: (i, 0)),
        pl.BlockSpec((1, N_MOD, SUBLANES, D_MODEL), lambda i: (i // tiles_per_group, 0, 0, 0)),
        _const_spec((4, D_MODEL)),
        _const_spec((D_MODEL, D_Q + 2 * D_KV)),
    ]
    args = [x2, mod, g, w_qkv]
    if rope:
        nt = SEQ // tm
        in_specs += [pl.BlockSpec((tm, LANES), lambda i: (i % nt, 0))] * 3
        args += list(tables)
    return pl.pallas_call(
        functools.partial(_qkv_kernel, rope=rope),
        grid=(n // tm,),
        in_specs=in_specs,
        out_specs=[
            pl.BlockSpec((tm, D_Q), lambda i: (i, 0)),
            pl.BlockSpec((tm, D_KV), lambda i: (i, 0)),
            pl.BlockSpec((tm, D_KV), lambda i: (i, 0)),
        ],
        out_shape=[
            jax.ShapeDtypeStruct((n, D_Q), BF16),
            jax.ShapeDtypeStruct((n, D_KV), BF16),
            jax.ShapeDtypeStruct((n, D_KV), BF16),
        ],
        compiler_params=_params(("parallel",)),
        name="qkv_rope" if rope else "qkv_ctx",
    )(*args)


def _attn_kernel(*refs, local):
    if local:
        sink_ref, q_ref, k_ref, v_ref, kc_ref, vc_ref, o_ref, k2_sc, v2_sc = refs
    else:
        sink_ref, q_ref, kc_ref, vc_ref, o_ref, k2_sc, v2_sc = refs
    j = pl.program_id(1)
    seq_blocks = SEQ // BLOCK if local else 0
    ctx_blocks = CTX_LEN // BLOCK
    half = HEAD_DIM

    @pl.when((pl.program_id(0) == 0) & (j == 0))
    def _init_static():
        k2_sc[...] = jnp.zeros(k2_sc.shape, BF16)
        v2_sc[...] = jnp.zeros(v2_sc.shape, BF16)
        ones = jnp.ones((N_KV_HEADS, seq_blocks + ctx_blocks, BLOCK, half), BF16)
        v2_sc[:, :, 0, :, 2 * half:3 * half] = ones
        v2_sc[:, :, 1, :, 3 * half:4 * half] = ones

    def fill(src_k, src_v, n_blocks, first_block):
        def body(i, carry):
            r0 = pl.multiple_of(i * BLOCK, BLOCK)
            kb = src_k[0, pl.ds(r0, BLOCK), :]
            vb = src_v[0, pl.ds(r0, BLOCK), :]
            for kh in range(N_KV_HEADS):
                cs = slice(kh * half, (kh + 1) * half)
                k2_sc[kh, first_block + i, 0, :, 0:half] = kb[:, cs]
                k2_sc[kh, first_block + i, 1, :, half:2 * half] = kb[:, cs]
                v2_sc[kh, first_block + i, 0, :, 0:half] = vb[:, cs]
                v2_sc[kh, first_block + i, 1, :, half:2 * half] = vb[:, cs]
            return carry
        lax.fori_loop(0, n_blocks, body, 0)

    @pl.when(j == 0)
    def _build():
        if local:
            fill(k_ref, v_ref, seq_blocks, 0)
        fill(kc_ref, vc_ref, ctx_blocks, seq_blocks)

    nt = (((1,), (1,)), ((), ()))
    n_band = 3
    if local:
        blk0 = jnp.clip(j - 1, 0, seq_blocks - n_band)
        qpos = j * BLOCK + lax.broadcasted_iota(jnp.int32, (BLOCK, n_band * BLOCK), 0)
        kpos = blk0 * BLOCK + lax.broadcasted_iota(jnp.int32, (BLOCK, n_band * BLOCK), 1)
        bias = jnp.where(jnp.abs(kpos - qpos) <= WINDOW, 0.0, NEG_INF).astype(F32)
        bias_blocks = [bias[:, i * BLOCK:(i + 1) * BLOCK] for i in range(n_band)]
    low_lanes = lax.broadcasted_iota(jnp.int32, (BLOCK, LANES), 1) < half

    for c in range(D_Q // LANES):
        kh = c // (GQA_GROUP // 2)
        qc = q_ref[:, c * LANES:(c + 1) * LANES]
        k_ctx = k2_sc[kh, seq_blocks:seq_blocks + ctx_blocks].reshape(2 * CTX_LEN, LANES)
        s_ctx = lax.dot_general(qc, k_ctx, nt, preferred_element_type=F32)
        blocks = [s_ctx[:, i * BLOCK:(i + 1) * BLOCK] for i in range(2 * ctx_blocks)]
        if local:
            k_band = k2_sc[kh, pl.ds(blk0, n_band)].reshape(2 * n_band * BLOCK, LANES)
            s_band = lax.dot_general(qc, k_band, nt, preferred_element_type=F32)
            blocks += [s_band[:, i * BLOCK:(i + 1) * BLOCK] + bias_blocks[i // 2] for i in range(2 * n_band)]
        ms = []
        for par in range(2):
            mx = functools.reduce(jnp.maximum, blocks[par::2])
            ms.append(jnp.maximum(jnp.max(mx, axis=-1, keepdims=True), sink_ref[2 * c + par]))
        p = [jnp.exp2(blk - ms[i % 2]).astype(BF16) for i, blk in enumerate(blocks)]
        v_ctx = v2_sc[kh, seq_blocks:seq_blocks + ctx_blocks].reshape(2 * CTX_LEN, 2 * LANES)
        acc = jnp.dot(jnp.concatenate(p[:2 * ctx_blocks], axis=1), v_ctx, preferred_element_type=F32)
        if local:
            v_band = v2_sc[kh, pl.ds(blk0, n_band)].reshape(2 * n_band * BLOCK, 2 * LANES)
            acc = acc + jnp.dot(jnp.concatenate(p[2 * ctx_blocks:], axis=1), v_band, preferred_element_type=F32)
        sink_term = jnp.where(low_lanes, jnp.exp2(sink_ref[2 * c] - ms[0]), jnp.exp2(sink_ref[2 * c + 1] - ms[1]))
        o_ref[:, c * LANES:(c + 1) * LANES] = (acc[:, :LANES] / (acc[:, LANES:] + sink_term)).astype(o_ref.dtype)


def _attn_scratch(n_blocks):
    return [
        pltpu.VMEM((N_KV_HEADS, n_blocks, 2, BLOCK, LANES), BF16),
        pltpu.VMEM((N_KV_HEADS, n_blocks, 2, BLOCK, 2 * LANES), BF16),
    ]


def _attn_call(sink2, q, k, v, kc, vc):
    nb = SEQ // BLOCK
    return pl.pallas_call(
        functools.partial(_attn_kernel, local=True),
        grid=(BATCH, nb),
        in_specs=[
            pl.BlockSpec(memory_space=pltpu.SMEM),
            pl.BlockSpec((BLOCK, D_Q), lambda b, j: (b * nb + j, 0)),
            pl.BlockSpec((1, SEQ, D_KV), lambda b, j: (b, 0, 0)),
            pl.BlockSpec((1, SEQ, D_KV), lambda b, j: (b, 0, 0)),
            pl.BlockSpec((1, CTX_LEN, D_KV), lambda b, j: (b, 0, 0)),
            pl.BlockSpec((1, CTX_LEN, D_KV), lambda b, j: (b, 0, 0)),
        ],
        out_specs=pl.BlockSpec((BLOCK, D_Q), lambda b, j: (b * nb + j, 0)),
        out_shape=jax.ShapeDtypeStruct((BATCH * SEQ, D_Q), BF16),
        scratch_shapes=_attn_scratch(nb + CTX_LEN // BLOCK),
        compiler_params=_params(("arbitrary", "arbitrary")),
        name="band_attn",
    )(sink2, q, k, v, kc, vc)


def _ctx_attn_call(sink2, qc, kc, vc):
    nb = CTX_LEN // BLOCK
    return pl.pallas_call(
        functools.partial(_attn_kernel, local=False),
        grid=(BATCH, nb),
        in_specs=[
            pl.BlockSpec(memory_space=pltpu.SMEM),
            pl.BlockSpec((BLOCK, D_Q), lambda b, j: (b * nb + j, 0)),
            pl.BlockSpec((1, CTX_LEN, D_KV), lambda b, j: (b, 0, 0)),
            pl.BlockSpec((1, CTX_LEN, D_KV), lambda b, j: (b, 0, 0)),
        ],
        out_specs=pl.BlockSpec((BLOCK, D_Q), lambda b, j: (b * nb + j, 0)),
        out_shape=jax.ShapeDtypeStruct((BATCH * CTX_LEN, D_Q), BF16),
        scratch_shapes=_attn_scratch(nb),
        compiler_params=_params(("arbitrary", "arbitrary")),
        name="ctx_attn",
    )(sink2, qc, kc, vc)


def _to_time_major(src_ref, sc_ref):
    nt = src_ref.shape[1]
    for b in range(BATCH):
        for s in range(N_SLAB):
            sc_ref[s, pl.ds(b, nt, stride=SUBLANES), :] = src_ref[b, :, s * LANES:(s + 1) * LANES]
    return jnp.concatenate([sc_ref[s, 0:nt * SUBLANES, :] for s in range(N_SLAB)], axis=1)


def _from_time_major(val, sc_ref, dst_ref):
    nt = dst_ref.shape[1]
    for s in range(N_SLAB):
        sc_ref[s, 0:nt * SUBLANES, :] = val[:, s * LANES:(s + 1) * LANES]
    for b in range(BATCH):
        for s in range(N_SLAB):
            dst_ref[b, :, s * LANES:(s + 1) * LANES] = sc_ref[s, pl.ds(b, nt, stride=SUBLANES), :]


def _post_kernel(*refs, lru):
    if lru:
        x_ref, gate_ref, yf_ref, yb_ref, mod_ref, g_ref, wf_ref, w1_ref, w2_ref, o_ref, t_sc = refs
        front = (gate_ref[...].astype(F32) * (yf_ref[...].astype(F32) + yb_ref[...].astype(F32))).astype(BF16)
        x = _to_time_major(x_ref, t_sc)
    else:
        x_ref, a_ref, mod_ref, g_ref, wf_ref, w1_ref, w2_ref, o_ref = refs
        front = a_ref[...]
        x = x_ref[...]
    y = jnp.dot(front, wf_ref[...], preferred_element_type=F32)
    x1 = _gated_add(x, mod_ref[0, 2], _rms(y, g_ref[1:2, :]))
    h = _modulate(_rms(x1, g_ref[2:3, :]), mod_ref[0, 3], mod_ref[0, 4]).astype(BF16)
    acc = jnp.zeros(x1.shape, F32)
    for c in range(D_FF // FF_CHUNK):
        hid = jnp.dot(h, w1_ref[:, c * FF_CHUNK:(c + 1) * FF_CHUNK], preferred_element_type=F32)
        hid = jnp.square(jnp.maximum(hid, 0.0)).astype(BF16)
        acc = acc + jnp.dot(hid, w2_ref[c * FF_CHUNK:(c + 1) * FF_CHUNK, :], preferred_element_type=F32)
    out = _gated_add(x1, mod_ref[0, 5], _rms(acc, g_ref[3:4, :]))
    if lru:
        _from_time_major(out, t_sc, o_ref)
    else:
        o_ref[...] = out


def _post_call(x, fronts, mod, g, w_front, w1, w2, tiles_per_group, lru):
    tm = TOKEN_TILE
    row = lambda i: (i, 0)
    if lru:
        nt = tm // BATCH
        n = x.shape[0] * x.shape[1]
        x_spec = pl.BlockSpec((BATCH, nt, D_MODEL), lambda i: (0, i, 0))
        scratch = [pltpu.VMEM((N_SLAB, tm, LANES), F32)]
    else:
        n = x.shape[0]
        x_spec = pl.BlockSpec((tm, D_MODEL), row)
        scratch = []
    in_specs = [x_spec]
    in_specs += [pl.BlockSpec((tm, f.shape[1]), row) for f in fronts]
    in_specs += [
        pl.BlockSpec((1, N_MOD, SUBLANES, D_MODEL), lambda i: (i // tiles_per_group, 0, 0, 0)),
        _const_spec((4, D_MODEL)),
        _const_spec(w_front.shape),
        _const_spec(w1.shape),
        _const_spec(w2.shape),
    ]
    return pl.pallas_call(
        functools.partial(_post_kernel, lru=lru),
        grid=(n // tm,),
        in_specs=in_specs,
        out_specs=x_spec,
        out_shape=jax.ShapeDtypeStruct(x.shape, F32),
        scratch_shapes=scratch,
        compiler_params=_params(("parallel",)),
        name="lru_out_mlp" if lru else "attn_out_mlp",
    )(x, *fronts, mod, g, w_front, w1, w2)


def _lru_in_kernel(x_ref, xp_ref, xn_ref, mod_ref, g_ref, w_ref, cw_ref, cb_ref, gate_ref, u_ref,
                   v_sc, t_sc, tp_sc, tn_sc):
    i = pl.program_id(0)
    n = pl.num_programs(0)
    rows = x_ref.shape[0] * x_ref.shape[1]
    s8 = SUBLANES

    def pre(x):
        return _modulate(_rms(x, g_ref[0:1, :]), mod_ref[0, 0], mod_ref[0, 1]).astype(BF16)

    h = pre(_to_time_major(x_ref, t_sc))
    gate_ref[...] = jax.nn.gelu(jnp.dot(h, w_ref[:, :D_RNN], preferred_element_type=F32)).astype(BF16)
    x_prev = _to_time_major(xp_ref, tp_sc)[SUBLANES * SUBLANES - HALO:]
    x_next = _to_time_major(xn_ref, tn_sc)[:HALO]
    h_ext = jnp.concatenate([pre(x_prev), h, pre(x_next)], axis=0)
    v_sc[...] = jnp.dot(h_ext, w_ref[:, D_RNN:], preferred_element_type=F32)
    v_sc[0:HALO] = v_sc[0:HALO] * (i > 0).astype(F32)
    v_sc[HALO + rows:HALO + rows + s8] = v_sc[HALO + rows:HALO + rows + s8] * (i < n - 1).astype(F32)
    u_ref[...] = (cb_ref[...]
                  + cw_ref[0:1, :] * v_sc[HALO - 2 * s8:HALO - 2 * s8 + rows]
                  + cw_ref[1:2, :] * v_sc[HALO - s8:HALO - s8 + rows]
                  + cw_ref[2:3, :] * v_sc[HALO:HALO + rows]
                  + cw_ref[3:4, :] * v_sc[HALO + s8:HALO + s8 + rows]).astype(u_ref.dtype)


def _lru_in_call(x3, mod, g, w_in, conv_w, conv_b):
    t_total = x3.shape[1]
    n = BATCH * t_total
    tm = TOKEN_TILE
    nt = tm // BATCH
    row = lambda i: (i, 0)
    per_tile = nt // SUBLANES
    last = t_total // SUBLANES - 1
    halo_spec = lambda f: pl.BlockSpec((BATCH, SUBLANES, D_MODEL), f)
    return pl.pallas_call(
        _lru_in_kernel,
        grid=(n // tm,),
        in_specs=[
            pl.BlockSpec((BATCH, nt, D_MODEL), lambda i: (0, i, 0)),
            halo_spec(lambda i: (0, jnp.maximum(i * per_tile - 1, 0), 0)),
            halo_spec(lambda i: (0, jnp.minimum((i + 1) * per_tile, last), 0)),
            _const_spec((1, N_MOD, SUBLANES, D_MODEL)),
            _const_spec((4, D_MODEL)),
            _const_spec((D_MODEL, 2 * D_RNN)),
            _const_spec((CONV_W, D_RNN)),
            _const_spec((1, D_RNN)),
        ],
        out_specs=[pl.BlockSpec((tm, D_RNN), row), pl.BlockSpec((tm, D_RNN), row)],
        out_shape=[jax.ShapeDtypeStruct((n, D_RNN), BF16), jax.ShapeDtypeStruct((n, D_RNN), U_DTYPE)],
        scratch_shapes=[
            pltpu.VMEM((tm + 2 * HALO, D_RNN), F32),
            pltpu.VMEM((N_SLAB, tm, LANES), F32),
            pltpu.VMEM((N_SLAB, SUBLANES * SUBLANES, LANES), F32),
            pltpu.VMEM((N_SLAB, SUBLANES * SUBLANES, LANES), F32),
        ],
        compiler_params=_params(("parallel",)),
        name="lru_in",
    )(x3, x3, x3, mod, g, w_in, conv_w, conv_b)


def _scan_kernel(uf_ref, ub_ref, h0_ref, wa_ref, ba_ref, wi_ref, bi_ref, lam_ref, yf_ref, yb_ref, ht_ref,
                 a_sc, bx_sc, h_sc):
    i = pl.program_id(0)
    n = pl.num_programs(0)
    rows = uf_ref.shape[0]
    nt = rows // SUBLANES
    s8 = SUBLANES

    @pl.when(i == 0)
    def _():
        h_sc[...] = h0_ref[...]

    for d, u_ref in enumerate((uf_ref, ub_ref)):
        for c in range(N_LRU_BLOCKS):
            cs = slice(c * LRU_BLOCK_W, (c + 1) * LRU_BLOCK_W)
            u16 = u_ref[:, cs].astype(BF16)
            u = u_ref[:, cs].astype(F32)
            ta = jnp.tanh(jnp.dot(u16, wa_ref[d, c], preferred_element_type=F32) + 0.5 * ba_ref[d, :, cs])
            ti = jnp.tanh(jnp.dot(u16, wi_ref[d, c], preferred_element_type=F32) + 0.5 * bi_ref[d, :, cs])
            neg_lam = -lam_ref[d, :, cs]
            softplus = jnp.maximum(neg_lam, 0.0) + jnp.log1p(jnp.exp(-jnp.abs(neg_lam)))
            k = (-0.5 * LRU_C * LOG2E) * softplus
            a = jnp.exp2(k * ta + k)
            w = 1.0 - a * a
            root = w * lax.rsqrt(jnp.maximum(w, 1e-30))
            a_sc[d, :, cs] = a
            bx_sc[d, :, cs] = root * (ti * u + u)

    def step(t, carry):
        hf, hb = carry
        rf = pl.multiple_of(t * 2 * s8, 2 * s8)
        rb = pl.multiple_of((nt - 2 - 2 * t) * s8, 2 * s8)
        hf1 = a_sc[0, pl.ds(rf, s8), :] * hf + bx_sc[0, pl.ds(rf, s8), :]
        hf2 = a_sc[0, pl.ds(rf + s8, s8), :] * hf1 + bx_sc[0, pl.ds(rf + s8, s8), :]
        yf_ref[pl.ds(rf, 2 * s8), :] = jnp.concatenate([hf1, hf2], axis=0).astype(yf_ref.dtype)
        hb1 = a_sc[1, pl.ds(rb + s8, s8), :] * hb + bx_sc[1, pl.ds(rb + s8, s8), :]
        hb2 = a_sc[1, pl.ds(rb, s8), :] * hb1 + bx_sc[1, pl.ds(rb, s8), :]
        yb_ref[pl.ds(rb, 2 * s8), :] = jnp.concatenate([hb2, hb1], axis=0).astype(yb_ref.dtype)
        return hf2, hb2

    hf, hb = lax.fori_loop(0, nt // 2, step, (h_sc[0], h_sc[1]), unroll=2)
    h_sc[0] = hf
    h_sc[1] = hb

    @pl.when(i == n - 1)
    def _():
        ht_ref[...] = h_sc[...]


def _scan_call(u2, h0, w_a, b_a, w_i, b_i, lam):
    rows_total = u2.shape[0]
    rows = SCAN_T * SUBLANES
    n = rows_total // rows
    w = D_RNN
    fwd = lambda i: (i, 0)
    bwd = lambda i: (n - 1 - i, 0)
    return pl.pallas_call(
        _scan_kernel,
        grid=(n,),
        in_specs=[
            pl.BlockSpec((rows, w), fwd),
            pl.BlockSpec((rows, w), bwd),
            _const_spec((2, SUBLANES, w)),
            _const_spec((2, N_LRU_BLOCKS, LRU_BLOCK_W, LRU_BLOCK_W)),
            _const_spec((2, 1, w)),
            _const_spec((2, N_LRU_BLOCKS, LRU_BLOCK_W, LRU_BLOCK_W)),
            _const_spec((2, 1, w)),
            _const_spec((2, 1, w)),
        ],
        out_specs=[
            pl.BlockSpec((rows, w), fwd),
            pl.BlockSpec((rows, w), bwd),
            pl.BlockSpec((2, SUBLANES, w), lambda i: (0, 0, 0)),
        ],
        out_shape=[
            jax.ShapeDtypeStruct((rows_total, w), Y_DTYPE),
            jax.ShapeDtypeStruct((rows_total, w), Y_DTYPE),
            jax.ShapeDtypeStruct((2, SUBLANES, w), F32),
        ],
        scratch_shapes=[
            pltpu.VMEM((2, rows, w), F32),
            pltpu.VMEM((2, rows, w), F32),
            pltpu.VMEM((2, SUBLANES, w), F32),
        ],
        compiler_params=_params(("arbitrary",)),
        name="lru_scan",
    )(u2, u2, h0, w_a, b_a, w_i, b_i, lam)


def _rope_tables():
    t = jnp.arange(SEQ)
    row = (t // GRID_W).astype(F32)
    col = (t % GRID_W).astype(F32)
    half = HEAD_DIM // 2
    inv = ROPE_BASE ** (-jnp.arange(0, half, 2, dtype=F32) / half)
    ang_r = row[:, None] * inv[None, :]
    ang_c = col[:, None] * inv[None, :]
    ang = jnp.concatenate([ang_r, ang_r, ang_c, ang_c], axis=-1)
    ang = jnp.tile(ang, (1, LANES // HEAD_DIM))
    low = (jnp.arange(LANES) % 32) < 16
    sin = jnp.sin(ang)
    return jnp.cos(ang), jnp.where(low, -sin, 0.0), jnp.where(low, 0.0, sin)


def kernel(x, c, ctx, c_ctx, ada_w, ada_b, norm_g, mlp_w1, mlp_w2, attn_w_qkv, attn_w_o, attn_sink,
           lru_w_in, lru_conv_w, lru_conv_b, lru_w_a, lru_b_a, lru_w_i, lru_b_i, lru_lam, lru_w_out):
    n_lat = BATCH * SEQ
    n_ctx = BATCH * CTX_LEN

    c16 = jnp.zeros((16, D_MODEL), F32).at[:BATCH].set(c).at[BATCH].set(c_ctx)
    mods = _mod_call(c16, ada_w, ada_b).reshape(2, 16, N_MOD, D_MODEL)

    def slab_bmajor(m):
        return jnp.broadcast_to(m[:, :, None, :], (BATCH, N_MOD, SUBLANES, D_MODEL))

    def slab_ctx(m):
        return jnp.broadcast_to(m[None, :, None, :], (1, N_MOD, SUBLANES, D_MODEL))

    mod_x0 = slab_bmajor(mods[0, :BATCH])
    mod_c0 = slab_ctx(mods[0, BATCH])
    w_qkv = attn_w_qkv[0]
    w_qkv = jnp.concatenate([w_qkv[:, :D_Q] * (HEAD_DIM ** -0.5 * LOG2E), w_qkv[:, D_Q:]], axis=1).astype(BF16)
    sink2 = attn_sink[0] * LOG2E
    w_o = attn_w_o[0].astype(BF16)
    w1_0, w2_0 = mlp_w1[0].astype(BF16), mlp_w2[0].astype(BF16)
    g0 = norm_g[0]
    tiles_per_batch = SEQ // TOKEN_TILE

    x2 = x.reshape(n_lat, D_MODEL)
    c2 = ctx.reshape(n_ctx, D_MODEL)
    q, k, v = _qkv_call(x2, mod_x0, g0, w_qkv, _rope_tables(), tiles_per_batch)
    qc, kc, vc = _qkv_call(c2, mod_c0, g0, w_qkv, None, n_ctx // TOKEN_TILE)
    kc3 = kc.reshape(BATCH, CTX_LEN, D_KV)
    vc3 = vc.reshape(BATCH, CTX_LEN, D_KV)
    att = _attn_call(sink2, q, k.reshape(BATCH, SEQ, D_KV), v.reshape(BATCH, SEQ, D_KV), kc3, vc3)
    att_c = _ctx_attn_call(sink2, qc, kc3, vc3)
    x2 = _post_call(x2, [att], mod_x0, g0, w_o, w1_0, w2_0, tiles_per_batch, lru=False)
    c2 = _post_call(c2, [att_c], mod_c0, g0, w_o, w1_0, w2_0, n_ctx // TOKEN_TILE, lru=False)

    x3 = x2.reshape(BATCH, SEQ, D_MODEL)
    c3 = c2.reshape(BATCH, CTX_LEN, D_MODEL)
    mod_x1 = mods[1, :BATCH].transpose(1, 0, 2)[None]
    mod_c1 = slab_ctx(mods[1, BATCH])
    g1 = norm_g[1]
    w_in = lru_w_in[0].astype(BF16)
    conv_w = 0.5 * lru_conv_w[0]
    conv_b = 0.5 * lru_conv_b[0].reshape(1, D_RNN)
    w_a, w_i = lru_w_a[0].astype(BF16), lru_w_i[0].astype(BF16)
    b_a, b_i = lru_b_a[0].reshape(2, 1, D_RNN), lru_b_i[0].reshape(2, 1, D_RNN)
    lam = lru_lam[0].reshape(2, 1, D_RNN)
    scan = functools.partial(_scan_call, w_a=w_a, b_a=b_a, w_i=w_i, b_i=b_i, lam=lam)

    _, u_c = _lru_in_call(c3, mod_c1, g1, w_in, conv_w, conv_b)
    _, _, h_ctx = scan(u_c, jnp.zeros((2, SUBLANES, D_RNN), F32))
    gate_x, u_x = _lru_in_call(x3, mod_x1, g1, w_in, conv_w, conv_b)
    yf, yb, _ = scan(u_x, h_ctx)
    return _post_call(x3, [gate_x, yf, yb], mod_x1, g1, lru_w_out[0].astype(BF16),
                      mlp_w1[1].astype(BF16), mlp_w2[1].astype(BF16), n_lat // TOKEN_TILE, lru=True)
```

```python
import functools

import jax
import jax.numpy as jnp
from jax import lax
from jax.experimental import pallas as pl
from jax.experimental.pallas import tpu as pltpu

D_MODEL = 1024
BATCH = 8
SEQ = 2048
GRID_W = 64
CTX_LEN = 256
HEAD_DIM = 64
N_HEADS = 16
N_KV_HEADS = 4
GQA_GROUP = N_HEADS // N_KV_HEADS
WINDOW = 128
BLOCK = 128
ROPE_BASE = 10000.0
D_RNN = 1280
LRU_BLOCK_W = 256
N_LRU_BLOCKS = D_RNN // LRU_BLOCK_W
CONV_W = 4
LRU_C = 8.0
D_FF = 4 * D_MODEL
N_MOD = 6
EPS = 1e-6
NEG_INF = -1e30

D_Q = N_HEADS * HEAD_DIM
D_KV = N_KV_HEADS * HEAD_DIM
D_K2 = 2 * D_KV
LANES = 128
SUBLANES = 8
N_SLAB = D_MODEL // LANES
TOKEN_TILE = 512
FF_CHUNK = 1024
SCAN_T = 64
HALO = 16
U_DTYPE = jnp.bfloat16
Y_DTYPE = jnp.bfloat16
LOG2E = 1.4426950408889634
VMEM_LIMIT = 56 * 1024 * 1024

F32 = jnp.float32
BF16 = jnp.bfloat16


def _rms(x, g):
    ms = jnp.mean(x * x, axis=-1, keepdims=True)
    return x * lax.rsqrt(ms + EPS) * g


def _slab(x):
    return x.reshape(x.shape[0] // SUBLANES, SUBLANES, x.shape[1])


def _modulate(h, shift8, scale8):
    out = _slab(h) * (1.0 + scale8)[None] + shift8[None]
    return out.reshape(h.shape)


def _gated_add(x, gate8, y):
    out = _slab(x) + gate8[None] * _slab(y)
    return out.reshape(x.shape)


def _const_spec(shape):
    n = len(shape)
    return pl.BlockSpec(shape, lambda *_: (0,) * n, pipeline_mode=pl.Buffered(1))


def _params(sem):
    return pltpu.CompilerParams(dimension_semantics=sem, vmem_limit_bytes=VMEM_LIMIT)


def _mod_kernel(c_ref, w_ref, b_ref, o_ref):
    s = jax.nn.silu(c_ref[...]).astype(BF16)
    o_ref[0] = jnp.dot(s, w_ref[0].astype(BF16), preferred_element_type=F32) + b_ref[0]


def _mod_call(c16, ada_w, ada_b):
    depth = ada_w.shape[0]
    nt = 1024
    return pl.pallas_call(
        _mod_kernel,
        grid=(depth, N_MOD * D_MODEL // nt),
        in_specs=[
            pl.BlockSpec((16, D_MODEL), lambda l, j: (0, 0)),
            pl.BlockSpec((1, D_MODEL, nt), lambda l, j: (l, 0, j)),
            pl.BlockSpec((1, 1, nt), lambda l, j: (l, 0, j)),
        ],
        out_specs=pl.BlockSpec((1, 16, nt), lambda l, j: (l, 0, j)),
        out_shape=jax.ShapeDtypeStruct((depth, 16, N_MOD * D_MODEL), F32),
        compiler_params=_params(("arbitrary", "arbitrary")),
        name="adaln_mod",
    )(c16, ada_w, ada_b.reshape(depth, 1, N_MOD * D_MODEL))


def _qkv_kernel(*refs, rope):
    if rope:
        x_ref, mod_ref, g_ref, w_ref, cos_ref, sa_ref, sb_ref, q_ref, k_ref, v_ref = refs
    else:
        x_ref, mod_ref, g_ref, w_ref, q_ref, k_ref, v_ref = refs
    h = _modulate(_rms(x_ref[...], g_ref[0:1, :]), mod_ref[0, 0], mod_ref[0, 1])
    y = jnp.dot(h.astype(BF16), w_ref[...], preferred_element_type=F32)
    if rope:
        cos, sa, sb = cos_ref[...], sa_ref[...], sb_ref[...]
    for c in range((D_Q + D_K2) // LANES):
        yc = y[:, c * LANES:(c + 1) * LANES]
        if rope:
            yc = yc * cos + pltpu.roll(yc, LANES - 16, 1) * sa + pltpu.roll(yc, 16, 1) * sb
        if c < D_Q // LANES:
            q_ref[:, c * LANES:(c + 1) * LANES] = yc.astype(BF16)
        else:
            c2 = c - D_Q // LANES
            k_ref[:, c2 * LANES:(c2 + 1) * LANES] = yc.astype(BF16)
    v_ref[...] = y[:, D_Q + D_K2:].astype(BF16)


def _qkv_call(x2, mod, g, w_qkv, tables, tiles_per_group):
    n = x2.shape[0]
    tm = TOKEN_TILE
    rope = tables is not None
    in_specs = [
        pl.BlockSpec((tm, D_MODEL), lambda i: (i, 0)),
        pl.BlockSpec((1, N_MOD, SUBLANES, D_MODEL), lambda i: (i // tiles_per_group, 0, 0, 0)),
        _const_spec((4, D_MODEL)),
        _const_spec((D_MODEL, D_Q + D_K2 + D_KV)),
    ]
    args = [x2, mod, g, w_qkv]
    if rope:
        nt = SEQ // tm
        in_specs += [pl.BlockSpec((tm, LANES), lambda i: (i % nt, 0))] * 3
        args += list(tables)
    return pl.pallas_call(
        functools.partial(_qkv_kernel, rope=rope),
        grid=(n // tm,),
        in_specs=in_specs,
        out_specs=[
            pl.BlockSpec((tm, D_Q), lambda i: (i, 0)),
            pl.BlockSpec((tm, D_K2), lambda i: (i, 0)),
            pl.BlockSpec((tm, D_KV), lambda i: (i, 0)),
        ],
        out_shape=[
            jax.ShapeDtypeStruct((n, D_Q), BF16),
            jax.ShapeDtypeStruct((n, D_K2), BF16),
            jax.ShapeDtypeStruct((n, D_KV), BF16),
        ],
        compiler_params=_params(("parallel",)),
        name="qkv_rope" if rope else "qkv_ctx",
    )(*args)


VT_ROWS = HEAD_DIM + 16


def _attn_kernel(*refs, local):
    if local:
        sink_ref, q_ref, k_ref, v_ref, kc_ref, vc_ref, o_ref, vt_sc, s_sc, p_sc = refs
    else:
        sink_ref, q_ref, kc_ref, vc_ref, o_ref, vt_sc, s_sc, p_sc = refs
    j = pl.program_id(1)
    seq_blocks = SEQ // BLOCK if local else 0
    ctx_blocks = CTX_LEN // BLOCK

    @pl.when((pl.program_id(0) == 0) & (j == 0))
    def _init_static():
        row = lax.broadcasted_iota(jnp.int32, (VT_ROWS - HEAD_DIM, LANES), 0)
        pad = jnp.where(row == 0, 1.0, 0.0).astype(BF16)
        for kh in range(N_KV_HEADS):
            for blk in range(seq_blocks + ctx_blocks):
                vt_sc[kh, blk, HEAD_DIM:VT_ROWS, :] = pad

    def fill(src_v, n_blocks, first_block):
        def body(i, carry):
            r0 = pl.multiple_of(i * BLOCK, BLOCK)
            vt = src_v[0, pl.ds(r0, BLOCK), :].astype(F32).T
            for kh in range(N_KV_HEADS):
                vt_sc[kh, first_block + i, 0:HEAD_DIM, :] = vt[kh * HEAD_DIM:(kh + 1) * HEAD_DIM].astype(BF16)
            return carry
        lax.fori_loop(0, n_blocks, body, 0)

    @pl.when(j == 0)
    def _build():
        if local:
            fill(v_ref, seq_blocks, 0)
        fill(vc_ref, ctx_blocks, seq_blocks)

    nt = (((1,), (1,)), ((), ()))
    n_band = 3
    if local:
        blk0 = jnp.clip(j - 1, 0, seq_blocks - n_band)
        start = pl.multiple_of(blk0 * BLOCK, BLOCK)
        kpos = start + lax.broadcasted_iota(jnp.int32, (n_band * BLOCK, BLOCK), 0)
        qpos = j * BLOCK + lax.broadcasted_iota(jnp.int32, (n_band * BLOCK, BLOCK), 1)
        bias = jnp.where(jnp.abs(kpos - qpos) <= WINDOW, 0.0, NEG_INF).astype(F32)
        bias2 = jnp.concatenate([bias, bias], axis=1)
    lane = lax.broadcasted_iota(jnp.int32, (BLOCK, LANES), 1)
    first_head = lax.broadcasted_iota(jnp.int32, (1, 2 * BLOCK), 1) < BLOCK

    n_chunks = D_Q // LANES

    def scores(c):
        kcols = slice((c // 2) * LANES, (c // 2 + 1) * LANES)
        qc = q_ref[:, c * LANES:(c + 1) * LANES]
        zero = jnp.zeros_like(qc)
        q2 = jnp.concatenate([jnp.where(lane < HEAD_DIM, qc, zero), jnp.where(lane < HEAD_DIM, zero, qc)], axis=0)
        s_sc[c % 2, 0:CTX_LEN] = lax.dot_general(kc_ref[0, :, kcols], q2, nt, preferred_element_type=F32)
        if local:
            s_sc[c % 2, CTX_LEN:] = lax.dot_general(k_ref[0, pl.ds(start, n_band * BLOCK), kcols], q2, nt,
                                                    preferred_element_type=F32) + bias2

    sink_terms = {}

    def softmax(c):
        s = s_sc[c % 2]
        sink_row = jnp.where(first_head, sink_ref[2 * c], sink_ref[2 * c + 1])
        m = jnp.maximum(jnp.max(s, axis=0, keepdims=True), sink_row)
        p_sc[c % 2] = jnp.exp2(s - m).astype(BF16)
        sink_terms[c] = jnp.exp2(sink_row - m)

    def values(c):
        kh = c // 2
        vt = [vt_sc[kh, seq_blocks + i] for i in range(ctx_blocks)]
        if local:
            vt_band = vt_sc[kh, pl.ds(blk0, n_band)]
            vt += [vt_band[i] for i in range(n_band)]
        acc = jnp.dot(jnp.concatenate(vt, axis=1), p_sc[c % 2], preferred_element_type=F32)
        out_t = acc[0:HEAD_DIM] / (acc[HEAD_DIM:HEAD_DIM + 1] + sink_terms[c])
        both = jnp.concatenate([out_t[:, :BLOCK], out_t[:, BLOCK:]], axis=0)
        o_ref[:, c * LANES:(c + 1) * LANES] = both.T.astype(o_ref.dtype)

    scores(0)
    scores(1)
    softmax(0)
    for c in range(n_chunks):
        if c + 2 < n_chunks:
            scores(c + 2)
        if c + 1 < n_chunks:
            softmax(c + 1)
        values(c)


def _attn_scratch(n_blocks, n_keys):
    return [
        pltpu.VMEM((N_KV_HEADS, n_blocks, VT_ROWS, BLOCK), BF16),
        pltpu.VMEM((2, n_keys, 2 * BLOCK), F32),
        pltpu.VMEM((2, n_keys, 2 * BLOCK), BF16),
    ]


def _attn_call(sink2, q, k, v, kc, vc):
    nb = SEQ // BLOCK
    return pl.pallas_call(
        functools.partial(_attn_kernel, local=True),
        grid=(BATCH, nb),
        in_specs=[
            pl.BlockSpec(memory_space=pltpu.SMEM),
            pl.BlockSpec((BLOCK, D_Q), lambda b, j: (b * nb + j, 0)),
            pl.BlockSpec((1, SEQ, D_K2), lambda b, j: (b, 0, 0)),
            pl.BlockSpec((1, SEQ, D_KV), lambda b, j: (b, 0, 0)),
            pl.BlockSpec((1, CTX_LEN, D_K2), lambda b, j: (b, 0, 0)),
            pl.BlockSpec((1, CTX_LEN, D_KV), lambda b, j: (b, 0, 0)),
        ],
        out_specs=pl.BlockSpec((BLOCK, D_Q), lambda b, j: (b * nb + j, 0)),
        out_shape=jax.ShapeDtypeStruct((BATCH * SEQ, D_Q), BF16),
        scratch_shapes=_attn_scratch(nb + CTX_LEN // BLOCK, CTX_LEN + 3 * BLOCK),
        compiler_params=_params(("arbitrary", "arbitrary")),
        name="band_attn",
    )(sink2, q, k, v, kc, vc)


def _ctx_attn_call(sink2, qc, kc, vc):
    nb = CTX_LEN // BLOCK
    return pl.pallas_call(
        functools.partial(_attn_kernel, local=False),
        grid=(BATCH, nb),
        in_specs=[
            pl.BlockSpec(memory_space=pltpu.SMEM),
            pl.BlockSpec((BLOCK, D_Q), lambda b, j: (b * nb + j, 0)),
            pl.BlockSpec((1, CTX_LEN, D_K2), lambda b, j: (b, 0, 0)),
            pl.BlockSpec((1, CTX_LEN, D_KV), lambda b, j: (b, 0, 0)),
        ],
        out_specs=pl.BlockSpec((BLOCK, D_Q), lambda b, j: (b * nb + j, 0)),
        out_shape=jax.ShapeDtypeStruct((BATCH * CTX_LEN, D_Q), BF16),
        scratch_shapes=_attn_scratch(nb, CTX_LEN),
        compiler_params=_params(("arbitrary", "arbitrary")),
        name="ctx_attn",
    )(sink2, qc, kc, vc)


def _to_time_major(src_ref, sc_ref):
    nt = src_ref.shape[1]
    for b in range(BATCH):
        for s in range(N_SLAB):
            sc_ref[s, pl.ds(b, nt, stride=SUBLANES), :] = src_ref[b, :, s * LANES:(s + 1) * LANES]
    return jnp.concatenate([sc_ref[s, 0:nt * SUBLANES, :] for s in range(N_SLAB)], axis=1)


def _from_time_major(val, sc_ref, dst_ref):
    nt = dst_ref.shape[1]
    for s in range(N_SLAB):
        sc_ref[s, 0:nt * SUBLANES, :] = val[:, s * LANES:(s + 1) * LANES]
    for b in range(BATCH):
        for s in range(N_SLAB):
            dst_ref[b, :, s * LANES:(s + 1) * LANES] = sc_ref[s, pl.ds(b, nt, stride=SUBLANES), :]


def _post_kernel(*refs, lru):
    if lru:
        x_ref, gate_ref, yf_ref, yb_ref, mod_ref, g_ref, wf_ref, w1_ref, w2_ref, o_ref, t_sc = refs
        front = (gate_ref[...].astype(F32) * (yf_ref[...].astype(F32) + yb_ref[...].astype(F32))).astype(BF16)
        x = _to_time_major(x_ref, t_sc)
    else:
        x_ref, a_ref, mod_ref, g_ref, wf_ref, w1_ref, w2_ref, o_ref = refs
        front = a_ref[...]
        x = x_ref[...]
    y = jnp.dot(front, wf_ref[...], preferred_element_type=F32)
    x1 = _gated_add(x, mod_ref[0, 2], _rms(y, g_ref[1:2, :]))
    h = _modulate(_rms(x1, g_ref[2:3, :]), mod_ref[0, 3], mod_ref[0, 4]).astype(BF16)
    acc = jnp.zeros(x1.shape, F32)
    for c in range(D_FF // FF_CHUNK):
        hid = jnp.dot(h, w1_ref[:, c * FF_CHUNK:(c + 1) * FF_CHUNK], preferred_element_type=F32)
        hid = jnp.square(jnp.maximum(hid, 0.0)).astype(BF16)
        acc = acc + jnp.dot(hid, w2_ref[c * FF_CHUNK:(c + 1) * FF_CHUNK, :], preferred_element_type=F32)
    out = _gated_add(x1, mod_ref[0, 5], _rms(acc, g_ref[3:4, :]))
    if lru:
        _from_time_major(out, t_sc, o_ref)
    else:
        o_ref[...] = out


def _post_call(x, fronts, mod, g, w_front, w1, w2, tiles_per_group, lru):
    tm = TOKEN_TILE
    row = lambda i: (i, 0)
    if lru:
        nt = tm // BATCH
        n = x.shape[0] * x.shape[1]
        x_spec = pl.BlockSpec((BATCH, nt, D_MODEL), lambda i: (0, i, 0))
        scratch = [pltpu.VMEM((N_SLAB, tm, LANES), F32)]
    else:
        n = x.shape[0]
        x_spec = pl.BlockSpec((tm, D_MODEL), row)
        scratch = []
    in_specs = [x_spec]
    in_specs += [pl.BlockSpec((tm, f.shape[1]), row) for f in fronts]
    in_specs += [
        pl.BlockSpec((1, N_MOD, SUBLANES, D_MODEL), lambda i: (i // tiles_per_group, 0, 0, 0)),
        _const_spec((4, D_MODEL)),
        _const_spec(w_front.shape),
        _const_spec(w1.shape),
        _const_spec(w2.shape),
    ]
    return pl.pallas_call(
        functools.partial(_post_kernel, lru=lru),
        grid=(n // tm,),
        in_specs=in_specs,
        out_specs=x_spec,
        out_shape=jax.ShapeDtypeStruct(x.shape, F32),
        scratch_shapes=scratch,
        compiler_params=_params(("parallel",)),
        name="lru_out_mlp" if lru else "attn_out_mlp",
    )(x, *fronts, mod, g, w_front, w1, w2)


def _lru_in_kernel(x_ref, xp_ref, xn_ref, mod_ref, g_ref, w_ref, cw_ref, cb_ref, gate_ref, u_ref,
                   v_sc, t_sc, tp_sc, tn_sc):
    i = pl.program_id(0)
    n = pl.num_programs(0)
    rows = x_ref.shape[0] * x_ref.shape[1]
    s8 = SUBLANES

    def pre(x):
        return _modulate(_rms(x, g_ref[0:1, :]), mod_ref[0, 0], mod_ref[0, 1]).astype(BF16)

    h = pre(_to_time_major(x_ref, t_sc))
    gate_ref[...] = jax.nn.gelu(jnp.dot(h, w_ref[:, :D_RNN], preferred_element_type=F32)).astype(BF16)
    x_prev = _to_time_major(xp_ref, tp_sc)[SUBLANES * SUBLANES - HALO:]
    x_next = _to_time_major(xn_ref, tn_sc)[:HALO]
    h_ext = jnp.concatenate([pre(x_prev), h, pre(x_next)], axis=0)
    v_sc[...] = jnp.dot(h_ext, w_ref[:, D_RNN:], preferred_element_type=F32)
    v_sc[0:HALO] = v_sc[0:HALO] * (i > 0).astype(F32)
    v_sc[HALO + rows:HALO + rows + s8] = v_sc[HALO + rows:HALO + rows + s8] * (i < n - 1).astype(F32)
    u_ref[...] = (cb_ref[...]
                  + cw_ref[0:1, :] * v_sc[HALO - 2 * s8:HALO - 2 * s8 + rows]
                  + cw_ref[1:2, :] * v_sc[HALO - s8:HALO - s8 + rows]
                  + cw_ref[2:3, :] * v_sc[HALO:HALO + rows]
                  + cw_ref[3:4, :] * v_sc[HALO + s8:HALO + s8 + rows]).astype(u_ref.dtype)


def _lru_in_call(x3, mod, g, w_in, conv_w, conv_b):
    t_total = x3.shape[1]
    n = BATCH * t_total
    tm = TOKEN_TILE
    nt = tm // BATCH
    row = lambda i: (i, 0)
    per_tile = nt // SUBLANES
    last = t_total // SUBLANES - 1
    halo_spec = lambda f: pl.BlockSpec((BATCH, SUBLANES, D_MODEL), f)
    return pl.pallas_call(
        _lru_in_kernel,
        grid=(n // tm,),
        in_specs=[
            pl.BlockSpec((BATCH, nt, D_MODEL), lambda i: (0, i, 0)),
            halo_spec(lambda i: (0, jnp.maximum(i * per_tile - 1, 0), 0)),
            halo_spec(lambda i: (0, jnp.minimum((i + 1) * per_tile, last), 0)),
            _const_spec((1, N_MOD, SUBLANES, D_MODEL)),
            _const_spec((4, D_MODEL)),
            _const_spec((D_MODEL, 2 * D_RNN)),
            _const_spec((CONV_W, D_RNN)),
            _const_spec((1, D_RNN)),
        ],
        out_specs=[pl.BlockSpec((tm, D_RNN), row), pl.BlockSpec((tm, D_RNN), row)],
        out_shape=[jax.ShapeDtypeStruct((n, D_RNN), BF16), jax.ShapeDtypeStruct((n, D_RNN), U_DTYPE)],
        scratch_shapes=[
            pltpu.VMEM((tm + 2 * HALO, D_RNN), F32),
            pltpu.VMEM((N_SLAB, tm, LANES), F32),
            pltpu.VMEM((N_SLAB, SUBLANES * SUBLANES, LANES), F32),
            pltpu.VMEM((N_SLAB, SUBLANES * SUBLANES, LANES), F32),
        ],
        compiler_params=_params(("parallel",)),
        name="lru_in",
    )(x3, x3, x3, mod, g, w_in, conv_w, conv_b)


def _scan_kernel(uf_ref, ub_ref, h0_ref, wa_ref, ba_ref, wi_ref, bi_ref, lam_ref, yf_ref, yb_ref, ht_ref,
                 a_sc, bx_sc, h_sc):
    i = pl.program_id(0)
    n = pl.num_programs(0)
    rows = uf_ref.shape[0]
    nt = rows // SUBLANES
    s8 = SUBLANES

    @pl.when(i == 0)
    def _():
        h_sc[...] = h0_ref[...]

    for d, u_ref in enumerate((uf_ref, ub_ref)):
        for c in range(N_LRU_BLOCKS):
            cs = slice(c * LRU_BLOCK_W, (c + 1) * LRU_BLOCK_W)
            u16 = u_ref[:, cs].astype(BF16)
            u = u_ref[:, cs].astype(F32)
            ta = jnp.tanh(jnp.dot(u16, wa_ref[d, c], preferred_element_type=F32) + 0.5 * ba_ref[d, :, cs])
            ti = jnp.tanh(jnp.dot(u16, wi_ref[d, c], preferred_element_type=F32) + 0.5 * bi_ref[d, :, cs])
            neg_lam = -lam_ref[d, :, cs]
            softplus = jnp.maximum(neg_lam, 0.0) + jnp.log1p(jnp.exp(-jnp.abs(neg_lam)))
            k = (-0.5 * LRU_C * LOG2E) * softplus
            a = jnp.exp2(k * ta + k)
            w = 1.0 - a * a
            root = w * lax.rsqrt(jnp.maximum(w, 1e-30))
            a_sc[d, :, cs] = a
            bx_sc[d, :, cs] = root * (ti * u + u)

    def step(t, carry):
        hf, hb = carry
        rf = pl.multiple_of(t * 2 * s8, 2 * s8)
        rb = pl.multiple_of((nt - 2 - 2 * t) * s8, 2 * s8)
        hf1 = a_sc[0, pl.ds(rf, s8), :] * hf + bx_sc[0, pl.ds(rf, s8), :]
        hf2 = a_sc[0, pl.ds(rf + s8, s8), :] * hf1 + bx_sc[0, pl.ds(rf + s8, s8), :]
        yf_ref[pl.ds(rf, 2 * s8), :] = jnp.concatenate([hf1, hf2], axis=0).astype(yf_ref.dtype)
        hb1 = a_sc[1, pl.ds(rb + s8, s8), :] * hb + bx_sc[1, pl.ds(rb + s8, s8), :]
        hb2 = a_sc[1, pl.ds(rb, s8), :] * hb1 + bx_sc[1, pl.ds(rb, s8), :]
        yb_ref[pl.ds(rb, 2 * s8), :] = jnp.concatenate([hb2, hb1], axis=0).astype(yb_ref.dtype)
        return hf2, hb2

    hf, hb = lax.fori_loop(0, nt // 2, step, (h_sc[0], h_sc[1]), unroll=2)
    h_sc[0] = hf
    h_sc[1] = hb

    @pl.when(i == n - 1)
    def _():
        ht_ref[...] = h_sc[...]


def _scan_call(u2, h0, w_a, b_a, w_i, b_i, lam):
    rows_total = u2.shape[0]
    rows = SCAN_T * SUBLANES
    n = rows_total // rows
    w = D_RNN
    fwd = lambda i: (i, 0)
    bwd = lambda i: (n - 1 - i, 0)
    return pl.pallas_call(
        _scan_kernel,
        grid=(n,),
        in_specs=[
            pl.BlockSpec((rows, w), fwd),
            pl.BlockSpec((rows, w), bwd),
            _const_spec((2, SUBLANES, w)),
            _const_spec((2, N_LRU_BLOCKS, LRU_BLOCK_W, LRU_BLOCK_W)),
            _const_spec((2, 1, w)),
            _const_spec((2, N_LRU_BLOCKS, LRU_BLOCK_W, LRU_BLOCK_W)),
            _const_spec((2, 1, w)),
            _const_spec((2, 1, w)),
        ],
        out_specs=[
            pl.BlockSpec((rows, w), fwd),
            pl.BlockSpec((rows, w), bwd),
            pl.BlockSpec((2, SUBLANES, w), lambda i: (0, 0, 0)),
        ],
        out_shape=[
            jax.ShapeDtypeStruct((rows_total, w), Y_DTYPE),
            jax.ShapeDtypeStruct((rows_total, w), Y_DTYPE),
            jax.ShapeDtypeStruct((2, SUBLANES, w), F32),
        ],
        scratch_shapes=[
            pltpu.VMEM((2, rows, w), F32),
            pltpu.VMEM((2, rows, w), F32),
            pltpu.VMEM((2, SUBLANES, w), F32),
        ],
        compiler_params=_params(("arbitrary",)),
        name="lru_scan",
    )(u2, u2, h0, w_a, b_a, w_i, b_i, lam)


def _rope_tables():
    t = jnp.arange(SEQ)
    row = (t // GRID_W).astype(F32)
    col = (t % GRID_W).astype(F32)
    half = HEAD_DIM // 2
    inv = ROPE_BASE ** (-jnp.arange(0, half, 2, dtype=F32) / half)
    ang_r = row[:, None] * inv[None, :]
    ang_c = col[:, None] * inv[None, :]
    ang = jnp.concatenate([ang_r, ang_r, ang_c, ang_c], axis=-1)
    ang = jnp.tile(ang, (1, LANES // HEAD_DIM))
    low = (jnp.arange(LANES) % 32) < 16
    sin = jnp.sin(ang)
    return jnp.cos(ang), jnp.where(low, -sin, 0.0), jnp.where(low, 0.0, sin)


def kernel(x, c, ctx, c_ctx, ada_w, ada_b, norm_g, mlp_w1, mlp_w2, attn_w_qkv, attn_w_o, attn_sink,
           lru_w_in, lru_conv_w, lru_conv_b, lru_w_a, lru_b_a, lru_w_i, lru_b_i, lru_lam, lru_w_out):
    n_lat = BATCH * SEQ
    n_ctx = BATCH * CTX_LEN

    c16 = jnp.zeros((16, D_MODEL), F32).at[:BATCH].set(c).at[BATCH].set(c_ctx)
    mods = _mod_call(c16, ada_w, ada_b).reshape(2, 16, N_MOD, D_MODEL)

    def slab_bmajor(m):
        return jnp.broadcast_to(m[:, :, None, :], (BATCH, N_MOD, SUBLANES, D_MODEL))

    def slab_ctx(m):
        return jnp.broadcast_to(m[None, :, None, :], (1, N_MOD, SUBLANES, D_MODEL))

    mod_x0 = slab_bmajor(mods[0, :BATCH])
    mod_c0 = slab_ctx(mods[0, BATCH])
    w_qkv = attn_w_qkv[0]
    w_k = w_qkv[:, D_Q:D_Q + D_KV].reshape(D_MODEL, N_KV_HEADS, 1, HEAD_DIM)
    w_k2 = jnp.broadcast_to(w_k, (D_MODEL, N_KV_HEADS, 2, HEAD_DIM)).reshape(D_MODEL, D_K2)
    w_qkv = jnp.concatenate([w_qkv[:, :D_Q] * (HEAD_DIM ** -0.5 * LOG2E), w_k2, w_qkv[:, D_Q + D_KV:]],
                            axis=1).astype(BF16)
    sink2 = attn_sink[0] * LOG2E
    w_o = attn_w_o[0].astype(BF16)
    w1_0, w2_0 = mlp_w1[0].astype(BF16), mlp_w2[0].astype(BF16)
    g0 = norm_g[0]
    tiles_per_batch = SEQ // TOKEN_TILE

    x2 = x.reshape(n_lat, D_MODEL)
    c2 = ctx.reshape(n_ctx, D_MODEL)
    q, k, v = _qkv_call(x2, mod_x0, g0, w_qkv, _rope_tables(), tiles_per_batch)
    qc, kc, vc = _qkv_call(c2, mod_c0, g0, w_qkv, None, n_ctx // TOKEN_TILE)
    kc3 = kc.reshape(BATCH, CTX_LEN, D_K2)
    vc3 = vc.reshape(BATCH, CTX_LEN, D_KV)
    att = _attn_call(sink2, q, k.reshape(BATCH, SEQ, D_K2), v.reshape(BATCH, SEQ, D_KV), kc3, vc3)
    att_c = _ctx_attn_call(sink2, qc, kc3, vc3)
    x2 = _post_call(x2, [att], mod_x0, g0, w_o, w1_0, w2_0, tiles_per_batch, lru=False)
    c2 = _post_call(c2, [att_c], mod_c0, g0, w_o, w1_0, w2_0, n_ctx // TOKEN_TILE, lru=False)

    x3 = x2.reshape(BATCH, SEQ, D_MODEL)
    c3 = c2.reshape(BATCH, CTX_LEN, D_MODEL)
    mod_x1 = mods[1, :BATCH].transpose(1, 0, 2)[None]
    mod_c1 = slab_ctx(mods[1, BATCH])
    g1 = norm_g[1]
    w_in = lru_w_in[0].astype(BF16)
    conv_w = 0.5 * lru_conv_w[0]
    conv_b = 0.5 * lru_conv_b[0].reshape(1, D_RNN)
    w_a, w_i = lru_w_a[0].astype(BF16), lru_w_i[0].astype(BF16)
    b_a, b_i = lru_b_a[0].reshape(2, 1, D_RNN), lru_b_i[0].reshape(2, 1, D_RNN)
    lam = lru_lam[0].reshape(2, 1, D_RNN)
    scan = functools.partial(_scan_call, w_a=w_a, b_a=b_a, w_i=w_i, b_i=b_i, lam=lam)

    _, u_c = _lru_in_call(c3, mod_c1, g1, w_in, conv_w, conv_b)
    _, _, h_ctx = scan(u_c, jnp.zeros((2, SUBLANES, D_RNN), F32))
    gate_x, u_x = _lru_in_call(x3, mod_x1, g1, w_in, conv_w, conv_b)
    yf, yb, _ = scan(u_x, h_ctx)
    return _post_call(x3, [gate_x, yf, yb], mod_x1, g1, lru_w_out[0].astype(BF16),
                      mlp_w1[1].astype(BF16), mlp_w2[1].astype(BF16), n_lat // TOKEN_TILE, lru=True)
```

```python
import functools

import jax
import jax.numpy as jnp
from jax import lax
from jax.experimental import pallas as pl
from jax.experimental.pallas import tpu as pltpu

D_MODEL = 1024
BATCH = 8
SEQ = 2048
GRID_W = 64
CTX_LEN = 256
HEAD_DIM = 64
N_HEADS = 16
N_KV_HEADS = 4
GQA_GROUP = N_HEADS // N_KV_HEADS
WINDOW = 128
BLOCK = 128
ROPE_BASE = 10000.0
D_RNN = 1280
LRU_BLOCK_W = 256
N_LRU_BLOCKS = D_RNN // LRU_BLOCK_W
CONV_W = 4
LRU_C = 8.0
D_FF = 4 * D_MODEL
N_MOD = 6
EPS = 1e-6
NEG_INF = -1e30

D_Q = N_HEADS * HEAD_DIM
D_KV = N_KV_HEADS * HEAD_DIM
D_K2 = 2 * D_KV
LANES = 128
SUBLANES = 8
N_SLAB = D_MODEL // LANES
TOKEN_TILE = 512
FF_CHUNK = 1024
SCAN_T = 64
HALO = 16
U_DTYPE = jnp.bfloat16
Y_DTYPE = jnp.bfloat16
LOG2E = 1.4426950408889634
VMEM_LIMIT = 60 * 1024 * 1024

F32 = jnp.float32
BF16 = jnp.bfloat16


def _rms(x, g):
    ms = jnp.mean(x * x, axis=-1, keepdims=True)
    return x * lax.rsqrt(ms + EPS) * g


def _slab(x):
    return x.reshape(x.shape[0] // SUBLANES, SUBLANES, x.shape[1])


def _modulate(h, shift8, scale8):
    out = _slab(h) * (1.0 + scale8)[None] + shift8[None]
    return out.reshape(h.shape)


def _gated_add(x, gate8, y):
    out = _slab(x) + gate8[None] * _slab(y)
    return out.reshape(x.shape)


def _const_spec(shape):
    n = len(shape)
    return pl.BlockSpec(shape, lambda *_: (0,) * n, pipeline_mode=pl.Buffered(1))


def _params(sem):
    return pltpu.CompilerParams(dimension_semantics=sem, vmem_limit_bytes=VMEM_LIMIT)


def _mod_kernel(c_ref, w_ref, b_ref, o_ref):
    s = jax.nn.silu(c_ref[...]).astype(BF16)
    o_ref[0] = jnp.dot(s, w_ref[0].astype(BF16), preferred_element_type=F32) + b_ref[0]


def _mod_call(c16, ada_w, ada_b):
    depth = ada_w.shape[0]
    nt = 1024
    return pl.pallas_call(
        _mod_kernel,
        grid=(depth, N_MOD * D_MODEL // nt),
        in_specs=[
            pl.BlockSpec((16, D_MODEL), lambda l, j: (0, 0)),
            pl.BlockSpec((1, D_MODEL, nt), lambda l, j: (l, 0, j)),
            pl.BlockSpec((1, 1, nt), lambda l, j: (l, 0, j)),
        ],
        out_specs=pl.BlockSpec((1, 16, nt), lambda l, j: (l, 0, j)),
        out_shape=jax.ShapeDtypeStruct((depth, 16, N_MOD * D_MODEL), F32),
        compiler_params=_params(("arbitrary", "arbitrary")),
        name="adaln_mod",
    )(c16, ada_w, ada_b.reshape(depth, 1, N_MOD * D_MODEL))


def _qkv_kernel(*refs, rope):
    if rope:
        x_ref, mod_ref, g_ref, w_ref, cos_ref, sa_ref, sb_ref, q_ref, k_ref, v_ref = refs
    else:
        x_ref, mod_ref, g_ref, w_ref, q_ref, k_ref, v_ref = refs
    h = _modulate(_rms(x_ref[...], g_ref[0:1, :]), mod_ref[0, 0], mod_ref[0, 1])
    y = jnp.dot(h.astype(BF16), w_ref[...], preferred_element_type=F32)
    if rope:
        cos, sa, sb = cos_ref[...], sa_ref[...], sb_ref[...]
    low = lax.broadcasted_iota(jnp.int32, (x_ref.shape[0], LANES), 1) < HEAD_DIM
    for c in range((D_Q + D_KV) // LANES):
        yc = y[:, c * LANES:(c + 1) * LANES]
        if rope:
            yc = yc * cos + pltpu.roll(yc, LANES - 16, 1) * sa + pltpu.roll(yc, 16, 1) * sb
        if c < D_Q // LANES:
            q_ref[:, c * LANES:(c + 1) * LANES] = yc.astype(BF16)
        else:
            c2 = 2 * (c - D_Q // LANES)
            swapped = pltpu.roll(yc, HEAD_DIM, 1)
            k_ref[:, c2 * LANES:(c2 + 1) * LANES] = jnp.where(low, yc, swapped).astype(BF16)
            k_ref[:, (c2 + 1) * LANES:(c2 + 2) * LANES] = jnp.where(low, swapped, yc).astype(BF16)
    v_ref[...] = y[:, D_Q + D_KV:].astype(BF16)


def _qkv_call(x2, mod, g, w_qkv, tables, tiles_per_group):
    n = x2.shape[0]
    tm = TOKEN_TILE
    rope = tables is not None
    in_specs = [
        pl.BlockSpec((tm, D_MODEL), lambda i: (i, 0)),
        pl.BlockSpec((1, N_MOD, SUBLANES, D_MODEL), lambda i: (i // tiles_per_group, 0, 0, 0)),
        _const_spec((4, D_MODEL)),
        _const_spec((D_MODEL, D_Q + 2 * D_KV)),
    ]
    args = [x2, mod, g, w_qkv]
    if rope:
        nt = SEQ // tm
        in_specs += [pl.BlockSpec((tm, LANES), lambda i: (i % nt, 0))] * 3
        args += list(tables)
    return pl.pallas_call(
        functools.partial(_qkv_kernel, rope=rope),
        grid=(n // tm,),
        in_specs=in_specs,
        out_specs=[
            pl.BlockSpec((tm, D_Q), lambda i: (i, 0)),
            pl.BlockSpec((tm, D_K2), lambda i: (i, 0)),
            pl.BlockSpec((tm, D_KV), lambda i: (i, 0)),
        ],
        out_shape=[
            jax.ShapeDtypeStruct((n, D_Q), BF16),
            jax.ShapeDtypeStruct((n, D_K2), BF16),
            jax.ShapeDtypeStruct((n, D_KV), BF16),
        ],
        compiler_params=_params(("parallel",)),
        name="qkv_rope" if rope else "qkv_ctx",
    )(*args)


VT_ROWS = HEAD_DIM + 16


def _attn_kernel(*refs, local):
    if local:
        sink_ref, q_ref, k_ref, v_ref, kc_ref, vc_ref, o_ref, vt_sc, s_sc, p_sc = refs
    else:
        sink_ref, q_ref, kc_ref, vc_ref, o_ref, vt_sc, s_sc, p_sc = refs
    j = pl.program_id(1)
    seq_blocks = SEQ // BLOCK if local else 0
    ctx_blocks = CTX_LEN // BLOCK

    @pl.when((pl.program_id(0) == 0) & (j == 0))
    def _init_static():
        row = lax.broadcasted_iota(jnp.int32, (VT_ROWS - HEAD_DIM, LANES), 0)
        pad = jnp.where(row == 0, 1.0, 0.0).astype(BF16)
        for kh in range(N_KV_HEADS):
            for blk in range(seq_blocks + ctx_blocks):
                vt_sc[kh, blk, HEAD_DIM:VT_ROWS, :] = pad

    def fill(src_v, n_blocks, first_block):
        def body(i, carry):
            r0 = pl.multiple_of(i * BLOCK, BLOCK)
            vt = src_v[0, pl.ds(r0, BLOCK), :].astype(F32).T
            for kh in range(N_KV_HEADS):
                vt_sc[kh, first_block + i, 0:HEAD_DIM, :] = vt[kh * HEAD_DIM:(kh + 1) * HEAD_DIM].astype(BF16)
            return carry
        lax.fori_loop(0, n_blocks, body, 0)

    @pl.when(j == 0)
    def _build():
        if local:
            fill(v_ref, seq_blocks, 0)
        fill(vc_ref, ctx_blocks, seq_blocks)

    nt = (((1,), (1,)), ((), ()))
    n_band = 3
    if local:
        blk0 = jnp.clip(j - 1, 0, seq_blocks - n_band)
        start = pl.multiple_of(blk0 * BLOCK, BLOCK)
        kpos = start + lax.broadcasted_iota(jnp.int32, (n_band * BLOCK, BLOCK), 0)
        qpos = j * BLOCK + lax.broadcasted_iota(jnp.int32, (n_band * BLOCK, BLOCK), 1)
        bias = jnp.where(jnp.abs(kpos - qpos) <= WINDOW, 0.0, NEG_INF).astype(F32)
        bias2 = jnp.concatenate([bias, bias], axis=1)
    lane = lax.broadcasted_iota(jnp.int32, (BLOCK, LANES), 1)
    first_head = lax.broadcasted_iota(jnp.int32, (1, 2 * BLOCK), 1) < BLOCK

    n_chunks = D_Q // LANES

    def scores(c):
        kcols = slice((c // 2) * LANES, (c // 2 + 1) * LANES)
        qc = q_ref[:, c * LANES:(c + 1) * LANES]
        zero = jnp.zeros_like(qc)
        q2 = jnp.concatenate([jnp.where(lane < HEAD_DIM, qc, zero), jnp.where(lane < HEAD_DIM, zero, qc)], axis=0)
        s_sc[c % 2, 0:CTX_LEN] = lax.dot_general(kc_ref[0, :, kcols], q2, nt, preferred_element_type=F32)
        if local:
            s_sc[c % 2, CTX_LEN:] = lax.dot_general(k_ref[0, pl.ds(start, n_band * BLOCK), kcols], q2, nt,
                                                    preferred_element_type=F32) + bias2

    sink_terms = {}

    def softmax(c):
        s = s_sc[c % 2]
        sink_row = jnp.where(first_head, sink_ref[2 * c], sink_ref[2 * c + 1])
        m = jnp.maximum(jnp.max(s, axis=0, keepdims=True), sink_row)
        p_sc[c % 2] = jnp.exp2(s - m).astype(BF16)
        sink_terms[c] = jnp.exp2(sink_row - m)

    def values(c):
        kh = c // 2
        vt = [vt_sc[kh, seq_blocks + i] for i in range(ctx_blocks)]
        if local:
            vt_band = vt_sc[kh, pl.ds(blk0, n_band)]
            vt += [vt_band[i] for i in range(n_band)]
        acc = jnp.dot(jnp.concatenate(vt, axis=1), p_sc[c % 2], preferred_element_type=F32)
        out_t = acc[0:HEAD_DIM] / (acc[HEAD_DIM:HEAD_DIM + 1] + sink_terms[c])
        both = jnp.concatenate([out_t[:, :BLOCK], out_t[:, BLOCK:]], axis=0)
        o_ref[:, c * LANES:(c + 1) * LANES] = both.T.astype(o_ref.dtype)

    scores(0)
    scores(1)
    softmax(0)
    for c in range(n_chunks):
        if c + 2 < n_chunks:
            scores(c + 2)
        if c + 1 < n_chunks:
            softmax(c + 1)
        values(c)


def _attn_scratch(n_blocks, n_keys):
    return [
        pltpu.VMEM((N_KV_HEADS, n_blocks, VT_ROWS, BLOCK), BF16),
        pltpu.VMEM((2, n_keys, 2 * BLOCK), F32),
        pltpu.VMEM((2, n_keys, 2 * BLOCK), BF16),
    ]


def _attn_call(sink2, q, k, v, kc, vc):
    nb = SEQ // BLOCK
    return pl.pallas_call(
        functools.partial(_attn_kernel, local=True),
        grid=(BATCH, nb),
        in_specs=[
            pl.BlockSpec(memory_space=pltpu.SMEM),
            pl.BlockSpec((BLOCK, D_Q), lambda b, j: (b * nb + j, 0)),
            pl.BlockSpec((1, SEQ, D_K2), lambda b, j: (b, 0, 0)),
            pl.BlockSpec((1, SEQ, D_KV), lambda b, j: (b, 0, 0)),
            pl.BlockSpec((1, CTX_LEN, D_K2), lambda b, j: (b, 0, 0)),
            pl.BlockSpec((1, CTX_LEN, D_KV), lambda b, j: (b, 0, 0)),
        ],
        out_specs=pl.BlockSpec((BLOCK, D_Q), lambda b, j: (b * nb + j, 0)),
        out_shape=jax.ShapeDtypeStruct((BATCH * SEQ, D_Q), BF16),
        scratch_shapes=_attn_scratch(nb + CTX_LEN // BLOCK, CTX_LEN + 3 * BLOCK),
        compiler_params=_params(("arbitrary", "arbitrary")),
        name="band_attn",
    )(sink2, q, k, v, kc, vc)


def _ctx_attn_call(sink2, qc, kc, vc):
    nb = CTX_LEN // BLOCK
    return pl.pallas_call(
        functools.partial(_attn_kernel, local=False),
        grid=(BATCH, nb),
        in_specs=[
            pl.BlockSpec(memory_space=pltpu.SMEM),
            pl.BlockSpec((BLOCK, D_Q), lambda b, j: (b * nb + j, 0)),
            pl.BlockSpec((1, CTX_LEN, D_K2), lambda b, j: (b, 0, 0)),
            pl.BlockSpec((1, CTX_LEN, D_KV), lambda b, j: (b, 0, 0)),
        ],
        out_specs=pl.BlockSpec((BLOCK, D_Q), lambda b, j: (b * nb + j, 0)),
        out_shape=jax.ShapeDtypeStruct((BATCH * CTX_LEN, D_Q), BF16),
        scratch_shapes=_attn_scratch(nb, CTX_LEN),
        compiler_params=_params(("arbitrary", "arbitrary")),
        name="ctx_attn",
    )(sink2, qc, kc, vc)


def _to_time_major(src_ref, sc_ref):
    nt = src_ref.shape[1]
    for b in range(BATCH):
        for s in range(N_SLAB):
            sc_ref[s, pl.ds(b, nt, stride=SUBLANES), :] = src_ref[b, :, s * LANES:(s + 1) * LANES]
    return jnp.concatenate([sc_ref[s, 0:nt * SUBLANES, :] for s in range(N_SLAB)], axis=1)


def _from_time_major(val, sc_ref, dst_ref):
    nt = dst_ref.shape[1]
    for s in range(N_SLAB):
        sc_ref[s, 0:nt * SUBLANES, :] = val[:, s * LANES:(s + 1) * LANES]
    for b in range(BATCH):
        for s in range(N_SLAB):
            dst_ref[b, :, s * LANES:(s + 1) * LANES] = sc_ref[s, pl.ds(b, nt, stride=SUBLANES), :]


def _post_kernel(*refs, lru):
    if lru:
        x_ref, gate_ref, yf_ref, yb_ref, mod_ref, g_ref, wf_ref, w1_ref, w2_ref, o_ref, x1_sc, h_sc, acc_sc, t_sc = refs
        front = (gate_ref[...].astype(F32) * (yf_ref[...].astype(F32) + yb_ref[...].astype(F32))).astype(BF16)
        x = _to_time_major(x_ref, t_sc)
    else:
        x_ref, a_ref, mod_ref, g_ref, wf_ref, w1_ref, w2_ref, o_ref, x1_sc, h_sc, acc_sc = refs
        front = a_ref[...]
        x = x_ref[...]
    half_rows = x.shape[0] // 2

    def head(r):
        rs = slice(r * half_rows, (r + 1) * half_rows)
        y = jnp.dot(front[rs], wf_ref[...], preferred_element_type=F32)
        x1 = _gated_add(x[rs], mod_ref[0, 2], _rms(y, g_ref[1:2, :]))
        x1_sc[r] = x1
        h_sc[r] = _modulate(_rms(x1, g_ref[2:3, :]), mod_ref[0, 3], mod_ref[0, 4]).astype(BF16)

    def mlp(r):
        acc = jnp.zeros((half_rows, D_MODEL), F32)
        for c in range(D_FF // FF_CHUNK):
            hid = jnp.dot(h_sc[r], w1_ref[:, c * FF_CHUNK:(c + 1) * FF_CHUNK], preferred_element_type=F32)
            hid = jnp.square(jnp.maximum(hid, 0.0)).astype(BF16)
            acc = acc + jnp.dot(hid, w2_ref[c * FF_CHUNK:(c + 1) * FF_CHUNK, :], preferred_element_type=F32)
        acc_sc[r] = acc

    def tail(r):
        return _gated_add(x1_sc[r], mod_ref[0, 5], _rms(acc_sc[r], g_ref[3:4, :]))

    head(0)
    head(1)
    mlp(0)
    out0 = tail(0)
    mlp(1)
    out = jnp.concatenate([out0, tail(1)], axis=0)
    if lru:
        _from_time_major(out, t_sc, o_ref)
    else:
        o_ref[...] = out


def _post_call(x, fronts, mod, g, w_front, w1, w2, tiles_per_group, lru):
    tm = TOKEN_TILE
    row = lambda i: (i, 0)
    if lru:
        nt = tm // BATCH
        n = x.shape[0] * x.shape[1]
        x_spec = pl.BlockSpec((BATCH, nt, D_MODEL), lambda i: (0, i, 0))
        scratch = [pltpu.VMEM((N_SLAB, tm, LANES), F32)]
    else:
        n = x.shape[0]
        x_spec = pl.BlockSpec((tm, D_MODEL), row)
        scratch = []
    scratch = [
        pltpu.VMEM((2, tm // 2, D_MODEL), F32),
        pltpu.VMEM((2, tm // 2, D_MODEL), BF16),
        pltpu.VMEM((2, tm // 2, D_MODEL), F32),
    ] + scratch
    in_specs = [x_spec]
    in_specs += [pl.BlockSpec((tm, f.shape[1]), row) for f in fronts]
    in_specs += [
        pl.BlockSpec((1, N_MOD, SUBLANES, D_MODEL), lambda i: (i // tiles_per_group, 0, 0, 0)),
        _const_spec((4, D_MODEL)),
        _const_spec(w_front.shape),
        _const_spec(w1.shape),
        _const_spec(w2.shape),
    ]
    return pl.pallas_call(
        functools.partial(_post_kernel, lru=lru),
        grid=(n // tm,),
        in_specs=in_specs,
        out_specs=x_spec,
        out_shape=jax.ShapeDtypeStruct(x.shape, F32),
        scratch_shapes=scratch,
        compiler_params=_params(("parallel",)),
        name="lru_out_mlp" if lru else "attn_out_mlp",
    )(x, *fronts, mod, g, w_front, w1, w2)


def _lru_in_kernel(x_ref, xp_ref, xn_ref, mod_ref, g_ref, w_ref, cw_ref, cb_ref, gate_ref, u_ref,
                   v_sc, t_sc, tp_sc, tn_sc):
    i = pl.program_id(0)
    n = pl.num_programs(0)
    rows = x_ref.shape[0] * x_ref.shape[1]
    s8 = SUBLANES

    def pre(x):
        return _modulate(_rms(x, g_ref[0:1, :]), mod_ref[0, 0], mod_ref[0, 1]).astype(BF16)

    h = pre(_to_time_major(x_ref, t_sc))
    gate_ref[...] = jax.nn.gelu(jnp.dot(h, w_ref[:, :D_RNN], preferred_element_type=F32)).astype(BF16)
    x_prev = _to_time_major(xp_ref, tp_sc)[SUBLANES * SUBLANES - HALO:]
    x_next = _to_time_major(xn_ref, tn_sc)[:HALO]
    h_ext = jnp.concatenate([pre(x_prev), h, pre(x_next)], axis=0)
    v_sc[...] = jnp.dot(h_ext, w_ref[:, D_RNN:], preferred_element_type=F32)
    v_sc[0:HALO] = v_sc[0:HALO] * (i > 0).astype(F32)
    v_sc[HALO + rows:HALO + rows + s8] = v_sc[HALO + rows:HALO + rows + s8] * (i < n - 1).astype(F32)
    u_ref[...] = (cb_ref[...]
                  + cw_ref[0:1, :] * v_sc[HALO - 2 * s8:HALO - 2 * s8 + rows]
                  + cw_ref[1:2, :] * v_sc[HALO - s8:HALO - s8 + rows]
                  + cw_ref[2:3, :] * v_sc[HALO:HALO + rows]
                  + cw_ref[3:4, :] * v_sc[HALO + s8:HALO + s8 + rows]).astype(u_ref.dtype)


def _lru_in_call(x3, mod, g, w_in, conv_w, conv_b):
    t_total = x3.shape[1]
    n = BATCH * t_total
    tm = TOKEN_TILE
    nt = tm // BATCH
    row = lambda i: (i, 0)
    per_tile = nt // SUBLANES
    last = t_total // SUBLANES - 1
    halo_spec = lambda f: pl.BlockSpec((BATCH, SUBLANES, D_MODEL), f)
    return pl.pallas_call(
        _lru_in_kernel,
        grid=(n // tm,),
        in_specs=[
            pl.BlockSpec((BATCH, nt, D_MODEL), lambda i: (0, i, 0)),
            halo_spec(lambda i: (0, jnp.maximum(i * per_tile - 1, 0), 0)),
            halo_spec(lambda i: (0, jnp.minimum((i + 1) * per_tile, last), 0)),
            _const_spec((1, N_MOD, SUBLANES, D_MODEL)),
            _const_spec((4, D_MODEL)),
            _const_spec((D_MODEL, 2 * D_RNN)),
            _const_spec((CONV_W, D_RNN)),
            _const_spec((1, D_RNN)),
        ],
        out_specs=[pl.BlockSpec((tm, D_RNN), row), pl.BlockSpec((tm, D_RNN), row)],
        out_shape=[jax.ShapeDtypeStruct((n, D_RNN), BF16), jax.ShapeDtypeStruct((n, D_RNN), U_DTYPE)],
        scratch_shapes=[
            pltpu.VMEM((tm + 2 * HALO, D_RNN), F32),
            pltpu.VMEM((N_SLAB, tm, LANES), F32),
            pltpu.VMEM((N_SLAB, SUBLANES * SUBLANES, LANES), F32),
            pltpu.VMEM((N_SLAB, SUBLANES * SUBLANES, LANES), F32),
        ],
        compiler_params=_params(("parallel",)),
        name="lru_in",
    )(x3, x3, x3, mod, g, w_in, conv_w, conv_b)


def _scan_kernel(uf_ref, ub_ref, h0_ref, wa_ref, ba_ref, wi_ref, bi_ref, lam_ref, yf_ref, yb_ref, ht_ref,
                 a_sc, bx_sc, h_sc):
    i = pl.program_id(0)
    n = pl.num_programs(0)
    rows = uf_ref.shape[0]
    nt = rows // SUBLANES
    s8 = SUBLANES

    @pl.when(i == 0)
    def _():
        h_sc[...] = h0_ref[...]

    for d, u_ref in enumerate((uf_ref, ub_ref)):
        for c in range(N_LRU_BLOCKS):
            cs = slice(c * LRU_BLOCK_W, (c + 1) * LRU_BLOCK_W)
            u16 = u_ref[:, cs].astype(BF16)
            u = u_ref[:, cs].astype(F32)
            ta = jnp.tanh(jnp.dot(u16, wa_ref[d, c], preferred_element_type=F32) + 0.5 * ba_ref[d, :, cs])
            ti = jnp.tanh(jnp.dot(u16, wi_ref[d, c], preferred_element_type=F32) + 0.5 * bi_ref[d, :, cs])
            neg_lam = -lam_ref[d, :, cs]
            softplus = jnp.maximum(neg_lam, 0.0) + jnp.log1p(jnp.exp(-jnp.abs(neg_lam)))
            k = (-0.5 * LRU_C * LOG2E) * softplus
            a = jnp.exp2(k * ta + k)
            w = 1.0 - a * a
            root = w * lax.rsqrt(jnp.maximum(w, 1e-30))
            a_sc[d, :, cs] = a
            bx_sc[d, :, cs] = root * (ti * u + u)

    def step(t, carry):
        hf, hb = carry
        rf = pl.multiple_of(t * 2 * s8, 2 * s8)
        rb = pl.multiple_of((nt - 2 - 2 * t) * s8, 2 * s8)
        hf1 = a_sc[0, pl.ds(rf, s8), :] * hf + bx_sc[0, pl.ds(rf, s8), :]
        hf2 = a_sc[0, pl.ds(rf + s8, s8), :] * hf1 + bx_sc[0, pl.ds(rf + s8, s8), :]
        yf_ref[pl.ds(rf, 2 * s8), :] = jnp.concatenate([hf1, hf2], axis=0).astype(yf_ref.dtype)
        hb1 = a_sc[1, pl.ds(rb + s8, s8), :] * hb + bx_sc[1, pl.ds(rb + s8, s8), :]
        hb2 = a_sc[1, pl.ds(rb, s8), :] * hb1 + bx_sc[1, pl.ds(rb, s8), :]
        yb_ref[pl.ds(rb, 2 * s8), :] = jnp.concatenate([hb2, hb1], axis=0).astype(yb_ref.dtype)
        return hf2, hb2

    hf, hb = lax.fori_loop(0, nt // 2, step, (h_sc[0], h_sc[1]), unroll=2)
    h_sc[0] = hf
    h_sc[1] = hb

    @pl.when(i == n - 1)
    def _():
        ht_ref[...] = h_sc[...]


def _scan_call(u2, h0, w_a, b_a, w_i, b_i, lam):
    rows_total = u2.shape[0]
    rows = SCAN_T * SUBLANES
    n = rows_total // rows
    w = D_RNN
    fwd = lambda i: (i, 0)
    bwd = lambda i: (n - 1 - i, 0)
    return pl.pallas_call(
        _scan_kernel,
        grid=(n,),
        in_specs=[
            pl.BlockSpec((rows, w), fwd),
            pl.BlockSpec((rows, w), bwd),
            _const_spec((2, SUBLANES, w)),
            _const_spec((2, N_LRU_BLOCKS, LRU_BLOCK_W, LRU_BLOCK_W)),
            _const_spec((2, 1, w)),
            _const_spec((2, N_LRU_BLOCKS, LRU_BLOCK_W, LRU_BLOCK_W)),
            _const_spec((2, 1, w)),
            _const_spec((2, 1, w)),
        ],
        out_specs=[
            pl.BlockSpec((rows, w), fwd),
            pl.BlockSpec((rows, w), bwd),
            pl.BlockSpec((2, SUBLANES, w), lambda i: (0, 0, 0)),
        ],
        out_shape=[
            jax.ShapeDtypeStruct((rows_total, w), Y_DTYPE),
            jax.ShapeDtypeStruct((rows_total, w), Y_DTYPE),
            jax.ShapeDtypeStruct((2, SUBLANES, w), F32),
        ],
        scratch_shapes=[
            pltpu.VMEM((2, rows, w), F32),
            pltpu.VMEM((2, rows, w), F32),
            pltpu.VMEM((2, SUBLANES, w), F32),
        ],
        compiler_params=_params(("arbitrary",)),
        name="lru_scan",
    )(u2, u2, h0, w_a, b_a, w_i, b_i, lam)


def _rope_tables():
    t = jnp.arange(SEQ)
    row = (t // GRID_W).astype(F32)
    col = (t % GRID_W).astype(F32)
    half = HEAD_DIM // 2
    inv = ROPE_BASE ** (-jnp.arange(0, half, 2, dtype=F32) / half)
    ang_r = row[:, None] * inv[None, :]
    ang_c = col[:, None] * inv[None, :]
    ang = jnp.concatenate([ang_r, ang_r, ang_c, ang_c], axis=-1)
    ang = jnp.tile(ang, (1, LANES // HEAD_DIM))
    low = (jnp.arange(LANES) % 32) < 16
    sin = jnp.sin(ang)
    return jnp.cos(ang), jnp.where(low, -sin, 0.0), jnp.where(low, 0.0, sin)


def kernel(x, c, ctx, c_ctx, ada_w, ada_b, norm_g, mlp_w1, mlp_w2, attn_w_qkv, attn_w_o, attn_sink,
           lru_w_in, lru_conv_w, lru_conv_b, lru_w_a, lru_b_a, lru_w_i, lru_b_i, lru_lam, lru_w_out):
    n_lat = BATCH * SEQ
    n_ctx = BATCH * CTX_LEN

    c16 = jnp.zeros((16, D_MODEL), F32).at[:BATCH].set(c).at[BATCH].set(c_ctx)
    mods = _mod_call(c16, ada_w, ada_b).reshape(2, 16, N_MOD, D_MODEL)

    def slab_bmajor(m):
        return jnp.broadcast_to(m[:, :, None, :], (BATCH, N_MOD, SUBLANES, D_MODEL))

    def slab_ctx(m):
        return jnp.broadcast_to(m[None, :, None, :], (1, N_MOD, SUBLANES, D_MODEL))

    mod_x0 = slab_bmajor(mods[0, :BATCH])
    mod_c0 = slab_ctx(mods[0, BATCH])
    w_qkv = attn_w_qkv[0]
    w_qkv = jnp.concatenate([w_qkv[:, :D_Q] * (HEAD_DIM ** -0.5 * LOG2E), w_qkv[:, D_Q:]], axis=1).astype(BF16)
    sink2 = attn_sink[0] * LOG2E
    w_o = attn_w_o[0].astype(BF16)
    w1_0, w2_0 = mlp_w1[0].astype(BF16), mlp_w2[0].astype(BF16)
    g0 = norm_g[0]
    tiles_per_batch = SEQ // TOKEN_TILE

    x2 = x.reshape(n_lat, D_MODEL)
    c2 = ctx.reshape(n_ctx, D_MODEL)
    q, k, v = _qkv_call(x2, mod_x0, g0, w_qkv, _rope_tables(), tiles_per_batch)
    qc, kc, vc = _qkv_call(c2, mod_c0, g0, w_qkv, None, n_ctx // TOKEN_TILE)
    kc3 = kc.reshape(BATCH, CTX_LEN, D_K2)
    vc3 = vc.reshape(BATCH, CTX_LEN, D_KV)
    att = _attn_call(sink2, q, k.reshape(BATCH, SEQ, D_K2), v.reshape(BATCH, SEQ, D_KV), kc3, vc3)
    att_c = _ctx_attn_call(sink2, qc, kc3, vc3)
    x2 = _post_call(x2, [att], mod_x0, g0, w_o, w1_0, w2_0, tiles_per_batch, lru=False)
    c2 = _post_call(c2, [att_c], mod_c0, g0, w_o, w1_0, w2_0, n_ctx // TOKEN_TILE, lru=False)

    x3 = x2.reshape(BATCH, SEQ, D_MODEL)
    c3 = c2.reshape(BATCH, CTX_LEN, D_MODEL)
    mod_x1 = mods[1, :BATCH].transpose(1, 0, 2)[None]
    mod_c1 = slab_ctx(mods[1, BATCH])
    g1 = norm_g[1]
    w_in = lru_w_in[0].astype(BF16)
    conv_w = 0.5 * lru_conv_w[0]
    conv_b = 0.5 * lru_conv_b[0].reshape(1, D_RNN)
    w_a, w_i = lru_w_a[0].astype(BF16), lru_w_i[0].astype(BF16)
    b_a, b_i = lru_b_a[0].reshape(2, 1, D_RNN), lru_b_i[0].reshape(2, 1, D_RNN)
    lam = lru_lam[0].reshape(2, 1, D_RNN)
    scan = functools.partial(_scan_call, w_a=w_a, b_a=b_a, w_i=w_i, b_i=b_i, lam=lam)

    _, u_c = _lru_in_call(c3, mod_c1, g1, w_in, conv_w, conv_b)
    _, _, h_ctx = scan(u_c, jnp.zeros((2, SUBLANES, D_RNN), F32))
    gate_x, u_x = _lru_in_call(x3, mod_x1, g1, w_in, conv_w, conv_b)
    yf, yb, _ = scan(u_x, h_ctx)
    return _post_call(x3, [gate_x, yf, yb], mod_x1, g1, lru_w_out[0].astype(BF16),
                      mlp_w1[1].astype(BF16), mlp_w2[1].astype(BF16), n_lat // TOKEN_TILE, lru=True)
```

```python
import functools

import jax
import jax.numpy as jnp
from jax import lax
from jax.experimental import pallas as pl
from jax.experimental.pallas import tpu as pltpu

D_MODEL = 1024
BATCH = 8
SEQ = 2048
GRID_W = 64
CTX_LEN = 256
HEAD_DIM = 64
N_HEADS = 16
N_KV_HEADS = 4
GQA_GROUP = N_HEADS // N_KV_HEADS
WINDOW = 128
BLOCK = 128
ROPE_BASE = 10000.0
D_RNN = 1280
LRU_BLOCK_W = 256
N_LRU_BLOCKS = D_RNN // LRU_BLOCK_W
CONV_W = 4
LRU_C = 8.0
D_FF = 4 * D_MODEL
N_MOD = 6
EPS = 1e-6
NEG_INF = -1e30

D_Q = N_HEADS * HEAD_DIM
D_KV = N_KV_HEADS * HEAD_DIM
D_K2 = 2 * D_KV
LANES = 128
SUBLANES = 8
N_SLAB = D_MODEL // LANES
TOKEN_TILE = 512
FF_CHUNK = 1024
ATTN_Q_PER_STEP = 4
SCAN_T = 128
HALO = 16
U_DTYPE = jnp.bfloat16
Y_DTYPE = jnp.bfloat16
LOG2E = 1.4426950408889634
VMEM_LIMIT = 60 * 1024 * 1024

F32 = jnp.float32
BF16 = jnp.bfloat16


def _rms(x, g):
    ms = jnp.mean(x * x, axis=-1, keepdims=True)
    return x * lax.rsqrt(ms + EPS) * g


def _slab(x):
    return x.reshape(x.shape[0] // SUBLANES, SUBLANES, x.shape[1])


def _modulate(h, shift8, scale8):
    out = _slab(h) * (1.0 + scale8)[None] + shift8[None]
    return out.reshape(h.shape)


def _gated_add(x, gate8, y):
    out = _slab(x) + gate8[None] * _slab(y)
    return out.reshape(x.shape)


def _const_spec(shape):
    n = len(shape)
    return pl.BlockSpec(shape, lambda *_: (0,) * n, pipeline_mode=pl.Buffered(1))


def _params(sem):
    return pltpu.CompilerParams(dimension_semantics=sem, vmem_limit_bytes=VMEM_LIMIT)


def _mod_kernel(c_ref, w_ref, b_ref, o_ref):
    s = jax.nn.silu(c_ref[...]).astype(BF16)
    o_ref[0] = jnp.dot(s, w_ref[0].astype(BF16), preferred_element_type=F32) + b_ref[0]


def _mod_call(c16, ada_w, ada_b):
    depth = ada_w.shape[0]
    nt = 1024
    return pl.pallas_call(
        _mod_kernel,
        grid=(depth, N_MOD * D_MODEL // nt),
        in_specs=[
            pl.BlockSpec((16, D_MODEL), lambda l, j: (0, 0)),
            pl.BlockSpec((1, D_MODEL, nt), lambda l, j: (l, 0, j)),
            pl.BlockSpec((1, 1, nt), lambda l, j: (l, 0, j)),
        ],
        out_specs=pl.BlockSpec((1, 16, nt), lambda l, j: (l, 0, j)),
        out_shape=jax.ShapeDtypeStruct((depth, 16, N_MOD * D_MODEL), F32),
        compiler_params=_params(("arbitrary", "arbitrary")),
        name="adaln_mod",
    )(c16, ada_w, ada_b.reshape(depth, 1, N_MOD * D_MODEL))


def _qkv_kernel(*refs, rope):
    if rope:
        x_ref, mod_ref, g_ref, w_ref, cos_ref, sa_ref, sb_ref, q_ref, k_ref, v_ref = refs
    else:
        x_ref, mod_ref, g_ref, w_ref, q_ref, k_ref, v_ref = refs
    h = _modulate(_rms(x_ref[...], g_ref[0:1, :]), mod_ref[0, 0], mod_ref[0, 1])
    y = jnp.dot(h.astype(BF16), w_ref[...], preferred_element_type=F32)
    if rope:
        cos, sa, sb = cos_ref[...], sa_ref[...], sb_ref[...]
    low = lax.broadcasted_iota(jnp.int32, (x_ref.shape[0], LANES), 1) < HEAD_DIM
    for c in range((D_Q + D_KV) // LANES):
        yc = y[:, c * LANES:(c + 1) * LANES]
        if rope:
            yc = yc * cos + pltpu.roll(yc, LANES - 16, 1) * sa + pltpu.roll(yc, 16, 1) * sb
        if c < D_Q // LANES:
            q_ref[:, c * LANES:(c + 1) * LANES] = yc.astype(BF16)
        else:
            c2 = 2 * (c - D_Q // LANES)
            swapped = pltpu.roll(yc, HEAD_DIM, 1)
            k_ref[:, c2 * LANES:(c2 + 1) * LANES] = jnp.where(low, yc, swapped).astype(BF16)
            k_ref[:, (c2 + 1) * LANES:(c2 + 2) * LANES] = jnp.where(low, swapped, yc).astype(BF16)
    v_ref[...] = y[:, D_Q + D_KV:].astype(BF16)


def _qkv_call(x2, mod, g, w_qkv, tables, tiles_per_group):
    n = x2.shape[0]
    tm = TOKEN_TILE
    rope = tables is not None
    in_specs = [
        pl.BlockSpec((tm, D_MODEL), lambda i: (i, 0)),
        pl.BlockSpec((1, N_MOD, SUBLANES, D_MODEL), lambda i: (i // tiles_per_group, 0, 0, 0)),
        _const_spec((4, D_MODEL)),
        _const_spec((D_MODEL, D_Q + 2 * D_KV)),
    ]
    args = [x2, mod, g, w_qkv]
    if rope:
        nt = SEQ // tm
        in_specs += [pl.BlockSpec((tm, LANES), lambda i: (i % nt, 0))] * 3
        args += list(tables)
    return pl.pallas_call(
        functools.partial(_qkv_kernel, rope=rope),
        grid=(n // tm,),
        in_specs=in_specs,
        out_specs=[
            pl.BlockSpec((tm, D_Q), lambda i: (i, 0)),
            pl.BlockSpec((tm, D_K2), lambda i: (i, 0)),
            pl.BlockSpec((tm, D_KV), lambda i: (i, 0)),
        ],
        out_shape=[
            jax.ShapeDtypeStruct((n, D_Q), BF16),
            jax.ShapeDtypeStruct((n, D_K2), BF16),
            jax.ShapeDtypeStruct((n, D_KV), BF16),
        ],
        compiler_params=_params(("parallel",)),
        name="qkv_rope" if rope else "qkv_ctx",
    )(*args)


VT_ROWS = HEAD_DIM + 16


def _attn_kernel(*refs, local, q_per_step):
    if local:
        sink_ref, q_ref, k_ref, v_ref, kc_ref, vc_ref, o_ref, vt_sc, s_sc, p_sc = refs
    else:
        sink_ref, q_ref, kc_ref, vc_ref, o_ref, vt_sc, s_sc, p_sc = refs
    j = pl.program_id(1)
    seq_blocks = SEQ // BLOCK if local else 0
    ctx_blocks = CTX_LEN // BLOCK

    @pl.when((pl.program_id(0) == 0) & (j == 0))
    def _init_static():
        row = lax.broadcasted_iota(jnp.int32, (VT_ROWS - HEAD_DIM, LANES), 0)
        pad = jnp.where(row == 0, 1.0, 0.0).astype(BF16)
        for kh in range(N_KV_HEADS):
            for blk in range(seq_blocks + ctx_blocks):
                vt_sc[kh, blk, HEAD_DIM:VT_ROWS, :] = pad

    def fill(src_v, n_blocks, first_block):
        def body(i, carry):
            r0 = pl.multiple_of(i * BLOCK, BLOCK)
            vt = src_v[0, pl.ds(r0, BLOCK), :].astype(F32).T
            for kh in range(N_KV_HEADS):
                vt_sc[kh, first_block + i, 0:HEAD_DIM, :] = vt[kh * HEAD_DIM:(kh + 1) * HEAD_DIM].astype(BF16)
            return carry
        lax.fori_loop(0, n_blocks, body, 0)

    @pl.when(j == 0)
    def _build():
        if local:
            fill(v_ref, seq_blocks, 0)
        fill(vc_ref, ctx_blocks, seq_blocks)

    nt = (((1,), (1,)), ((), ()))
    n_band = 3
    n_chunks = D_Q // LANES
    lane = lax.broadcasted_iota(jnp.int32, (BLOCK, LANES), 1)
    first_head = lax.broadcasted_iota(jnp.int32, (1, 2 * BLOCK), 1) < BLOCK

    def query_block(qb, carry):
        jq = j * q_per_step + qb
        rows = pl.ds(pl.multiple_of(qb * BLOCK, BLOCK), BLOCK)
        if local:
            blk0 = jnp.clip(jq - 1, 0, seq_blocks - n_band)
            start = pl.multiple_of(blk0 * BLOCK, BLOCK)
            kpos = start + lax.broadcasted_iota(jnp.int32, (n_band * BLOCK, BLOCK), 0)
            qpos = jq * BLOCK + lax.broadcasted_iota(jnp.int32, (n_band * BLOCK, BLOCK), 1)
            bias = jnp.where(jnp.abs(kpos - qpos) <= WINDOW, 0.0, NEG_INF).astype(F32)
            bias2 = jnp.concatenate([bias, bias], axis=1)

        def scores(c):
            kcols = slice((c // 2) * LANES, (c // 2 + 1) * LANES)
            qc = q_ref[rows, c * LANES:(c + 1) * LANES]
            zero = jnp.zeros_like(qc)
            q2 = jnp.concatenate([jnp.where(lane < HEAD_DIM, qc, zero), jnp.where(lane < HEAD_DIM, zero, qc)],
                                 axis=0)
            s_sc[c % 2, 0:CTX_LEN] = lax.dot_general(kc_ref[0, :, kcols], q2, nt, preferred_element_type=F32)
            if local:
                s_sc[c % 2, CTX_LEN:] = lax.dot_general(k_ref[0, pl.ds(start, n_band * BLOCK), kcols], q2, nt,
                                                        preferred_element_type=F32) + bias2

        sink_terms = {}

        def softmax(c):
            s = s_sc[c % 2]
            sink_row = jnp.where(first_head, sink_ref[2 * c], sink_ref[2 * c + 1])
            m = jnp.maximum(jnp.max(s, axis=0, keepdims=True), sink_row)
            p_sc[c % 2] = jnp.exp2(s - m).astype(BF16)
            sink_terms[c] = jnp.exp2(sink_row - m)

        def values(c):
            kh = c // 2
            vt = [vt_sc[kh, seq_blocks + i] for i in range(ctx_blocks)]
            if local:
                vt_band = vt_sc[kh, pl.ds(blk0, n_band)]
                vt += [vt_band[i] for i in range(n_band)]
            acc = jnp.dot(jnp.concatenate(vt, axis=1), p_sc[c % 2], preferred_element_type=F32)
            out_t = acc[0:HEAD_DIM] / (acc[HEAD_DIM:HEAD_DIM + 1] + sink_terms[c])
            both = jnp.concatenate([out_t[:, :BLOCK], out_t[:, BLOCK:]], axis=0)
            o_ref[rows, c * LANES:(c + 1) * LANES] = both.T.astype(o_ref.dtype)

        scores(0)
        scores(1)
        softmax(0)
        for c in range(n_chunks):
            if c + 2 < n_chunks:
                scores(c + 2)
            if c + 1 < n_chunks:
                softmax(c + 1)
            values(c)
        return carry

    lax.fori_loop(0, q_per_step, query_block, 0)


def _attn_scratch(n_blocks, n_keys):
    return [
        pltpu.VMEM((N_KV_HEADS, n_blocks, VT_ROWS, BLOCK), BF16),
        pltpu.VMEM((2, n_keys, 2 * BLOCK), F32),
        pltpu.VMEM((2, n_keys, 2 * BLOCK), BF16),
    ]


def _attn_call(sink2, q, k, v, kc, vc):
    nb = SEQ // BLOCK
    qps = ATTN_Q_PER_STEP
    steps = nb // qps
    return pl.pallas_call(
        functools.partial(_attn_kernel, local=True, q_per_step=qps),
        grid=(BATCH, steps),
        in_specs=[
            pl.BlockSpec(memory_space=pltpu.SMEM),
            pl.BlockSpec((qps * BLOCK, D_Q), lambda b, j: (b * steps + j, 0)),
            pl.BlockSpec((1, SEQ, D_K2), lambda b, j: (b, 0, 0)),
            pl.BlockSpec((1, SEQ, D_KV), lambda b, j: (b, 0, 0)),
            pl.BlockSpec((1, CTX_LEN, D_K2), lambda b, j: (b, 0, 0)),
            pl.BlockSpec((1, CTX_LEN, D_KV), lambda b, j: (b, 0, 0)),
        ],
        out_specs=pl.BlockSpec((qps * BLOCK, D_Q), lambda b, j: (b * steps + j, 0)),
        out_shape=jax.ShapeDtypeStruct((BATCH * SEQ, D_Q), BF16),
        scratch_shapes=_attn_scratch(nb + CTX_LEN // BLOCK, CTX_LEN + 3 * BLOCK),
        compiler_params=_params(("arbitrary", "arbitrary")),
        name="band_attn",
    )(sink2, q, k, v, kc, vc)


def _ctx_attn_call(sink2, qc, kc, vc):
    nb = CTX_LEN // BLOCK
    return pl.pallas_call(
        functools.partial(_attn_kernel, local=False, q_per_step=nb),
        grid=(BATCH, 1),
        in_specs=[
            pl.BlockSpec(memory_space=pltpu.SMEM),
            pl.BlockSpec((CTX_LEN, D_Q), lambda b, j: (b, 0)),
            pl.BlockSpec((1, CTX_LEN, D_K2), lambda b, j: (b, 0, 0)),
            pl.BlockSpec((1, CTX_LEN, D_KV), lambda b, j: (b, 0, 0)),
        ],
        out_specs=pl.BlockSpec((CTX_LEN, D_Q), lambda b, j: (b, 0)),
        out_shape=jax.ShapeDtypeStruct((BATCH * CTX_LEN, D_Q), BF16),
        scratch_shapes=_attn_scratch(nb, CTX_LEN),
        compiler_params=_params(("arbitrary", "arbitrary")),
        name="ctx_attn",
    )(sink2, qc, kc, vc)


def _to_time_major(src_ref, sc_ref):
    nt = src_ref.shape[1]
    for b in range(BATCH):
        for s in range(N_SLAB):
            sc_ref[s, pl.ds(b, nt, stride=SUBLANES), :] = src_ref[b, :, s * LANES:(s + 1) * LANES]
    return jnp.concatenate([sc_ref[s, 0:nt * SUBLANES, :] for s in range(N_SLAB)], axis=1)


def _from_time_major(val, sc_ref, dst_ref):
    nt = dst_ref.shape[1]
    for s in range(N_SLAB):
        sc_ref[s, 0:nt * SUBLANES, :] = val[:, s * LANES:(s + 1) * LANES]
    for b in range(BATCH):
        for s in range(N_SLAB):
            dst_ref[b, :, s * LANES:(s + 1) * LANES] = sc_ref[s, pl.ds(b, nt, stride=SUBLANES), :]


def _post_kernel(*refs, lru):
    if lru:
        x_ref, gate_ref, yf_ref, yb_ref, mod_ref, g_ref, wf_ref, w1_ref, w2_ref, o_ref, x1_sc, h_sc, acc_sc, t_sc = refs
        front = (gate_ref[...].astype(F32) * (yf_ref[...].astype(F32) + yb_ref[...].astype(F32))).astype(BF16)
        x = _to_time_major(x_ref, t_sc)
    else:
        x_ref, a_ref, mod_ref, g_ref, wf_ref, w1_ref, w2_ref, o_ref, x1_sc, h_sc, acc_sc = refs
        front = a_ref[...]
        x = x_ref[...]
    half_rows = x.shape[0] // 2

    def head(r):
        rs = slice(r * half_rows, (r + 1) * half_rows)
        y = jnp.dot(front[rs], wf_ref[...], preferred_element_type=F32)
        x1 = _gated_add(x[rs], mod_ref[0, 2], _rms(y, g_ref[1:2, :]))
        x1_sc[r] = x1
        h_sc[r] = _modulate(_rms(x1, g_ref[2:3, :]), mod_ref[0, 3], mod_ref[0, 4]).astype(BF16)

    def mlp(r):
        acc = jnp.zeros((half_rows, D_MODEL), F32)
        for c in range(D_FF // FF_CHUNK):
            hid = jnp.dot(h_sc[r], w1_ref[:, c * FF_CHUNK:(c + 1) * FF_CHUNK], preferred_element_type=F32)
            hid = jnp.square(jnp.maximum(hid, 0.0)).astype(BF16)
            acc = acc + jnp.dot(hid, w2_ref[c * FF_CHUNK:(c + 1) * FF_CHUNK, :], preferred_element_type=F32)
        acc_sc[r] = acc

    def tail(r):
        return _gated_add(x1_sc[r], mod_ref[0, 5], _rms(acc_sc[r], g_ref[3:4, :]))

    head(0)
    head(1)
    mlp(0)
    out0 = tail(0)
    mlp(1)
    out = jnp.concatenate([out0, tail(1)], axis=0)
    if lru:
        _from_time_major(out, t_sc, o_ref)
    else:
        o_ref[...] = out


def _post_call(x, fronts, mod, g, w_front, w1, w2, tiles_per_group, lru):
    tm = TOKEN_TILE
    row = lambda i: (i, 0)
    if lru:
        nt = tm // BATCH
        n = x.shape[0] * x.shape[1]
        x_spec = pl.BlockSpec((BATCH, nt, D_MODEL), lambda i: (0, i, 0))
        scratch = [pltpu.VMEM((N_SLAB, tm, LANES), F32)]
    else:
        n = x.shape[0]
        x_spec = pl.BlockSpec((tm, D_MODEL), row)
        scratch = []
    scratch = [
        pltpu.VMEM((2, tm // 2, D_MODEL), F32),
        pltpu.VMEM((2, tm // 2, D_MODEL), BF16),
        pltpu.VMEM((2, tm // 2, D_MODEL), F32),
    ] + scratch
    in_specs = [x_spec]
    in_specs += [pl.BlockSpec((tm, f.shape[1]), row) for f in fronts]
    in_specs += [
        pl.BlockSpec((1, N_MOD, SUBLANES, D_MODEL), lambda i: (i // tiles_per_group, 0, 0, 0)),
        _const_spec((4, D_MODEL)),
        _const_spec(w_front.shape),
        _const_spec(w1.shape),
        _const_spec(w2.shape),
    ]
    return pl.pallas_call(
        functools.partial(_post_kernel, lru=lru),
        grid=(n // tm,),
        in_specs=in_specs,
        out_specs=x_spec,
        out_shape=jax.ShapeDtypeStruct(x.shape, F32),
        scratch_shapes=scratch,
        compiler_params=_params(("parallel",)),
        name="lru_out_mlp" if lru else "attn_out_mlp",
    )(x, *fronts, mod, g, w_front, w1, w2)


def _lru_in_kernel(x_ref, xp_ref, xn_ref, mod_ref, g_ref, w_ref, cw_ref, cb_ref, gate_ref, u_ref,
                   v_sc, t_sc, tp_sc, tn_sc):
    i = pl.program_id(0)
    n = pl.num_programs(0)
    rows = x_ref.shape[0] * x_ref.shape[1]
    s8 = SUBLANES

    def pre(x):
        return _modulate(_rms(x, g_ref[0:1, :]), mod_ref[0, 0], mod_ref[0, 1]).astype(BF16)

    h = pre(_to_time_major(x_ref, t_sc))
    gate_ref[...] = jax.nn.gelu(jnp.dot(h, w_ref[:, :D_RNN], preferred_element_type=F32)).astype(BF16)
    x_prev = _to_time_major(xp_ref, tp_sc)[SUBLANES * SUBLANES - HALO:]
    x_next = _to_time_major(xn_ref, tn_sc)[:HALO]
    h_ext = jnp.concatenate([pre(x_prev), h, pre(x_next)], axis=0)
    v_sc[...] = jnp.dot(h_ext, w_ref[:, D_RNN:], preferred_element_type=F32)
    v_sc[0:HALO] = v_sc[0:HALO] * (i > 0).astype(F32)
    v_sc[HALO + rows:HALO + rows + s8] = v_sc[HALO + rows:HALO + rows + s8] * (i < n - 1).astype(F32)
    u_ref[...] = (cb_ref[...]
                  + cw_ref[0:1, :] * v_sc[HALO - 2 * s8:HALO - 2 * s8 + rows]
                  + cw_ref[1:2, :] * v_sc[HALO - s8:HALO - s8 + rows]
                  + cw_ref[2:3, :] * v_sc[HALO:HALO + rows]
                  + cw_ref[3:4, :] * v_sc[HALO + s8:HALO + s8 + rows]).astype(u_ref.dtype)


def _lru_in_call(x3, mod, g, w_in, conv_w, conv_b):
    t_total = x3.shape[1]
    n = BATCH * t_total
    tm = TOKEN_TILE
    nt = tm // BATCH
    row = lambda i: (i, 0)
    per_tile = nt // SUBLANES
    last = t_total // SUBLANES - 1
    halo_spec = lambda f: pl.BlockSpec((BATCH, SUBLANES, D_MODEL), f)
    return pl.pallas_call(
        _lru_in_kernel,
        grid=(n // tm,),
        in_specs=[
            pl.BlockSpec((BATCH, nt, D_MODEL), lambda i: (0, i, 0)),
            halo_spec(lambda i: (0, jnp.maximum(i * per_tile - 1, 0), 0)),
            halo_spec(lambda i: (0, jnp.minimum((i + 1) * per_tile, last), 0)),
            _const_spec((1, N_MOD, SUBLANES, D_MODEL)),
            _const_spec((4, D_MODEL)),
            _const_spec((D_MODEL, 2 * D_RNN)),
            _const_spec((CONV_W, D_RNN)),
            _const_spec((1, D_RNN)),
        ],
        out_specs=[pl.BlockSpec((tm, D_RNN), row), pl.BlockSpec((tm, D_RNN), row)],
        out_shape=[jax.ShapeDtypeStruct((n, D_RNN), BF16), jax.ShapeDtypeStruct((n, D_RNN), U_DTYPE)],
        scratch_shapes=[
            pltpu.VMEM((tm + 2 * HALO, D_RNN), F32),
            pltpu.VMEM((N_SLAB, tm, LANES), F32),
            pltpu.VMEM((N_SLAB, SUBLANES * SUBLANES, LANES), F32),
            pltpu.VMEM((N_SLAB, SUBLANES * SUBLANES, LANES), F32),
        ],
        compiler_params=_params(("parallel",)),
        name="lru_in",
    )(x3, x3, x3, mod, g, w_in, conv_w, conv_b)


def _scan_kernel(uf_ref, ub_ref, h0_ref, wa_ref, ba_ref, wi_ref, bi_ref, lam_ref, yf_ref, yb_ref, ht_ref,
                 a_sc, bx_sc, h_sc):
    i = pl.program_id(0)
    n = pl.num_programs(0)
    rows = uf_ref.shape[0]
    nt = rows // SUBLANES
    s8 = SUBLANES

    @pl.when(i == 0)
    def _():
        h_sc[...] = h0_ref[...]

    for d, u_ref in enumerate((uf_ref, ub_ref)):
        for c in range(N_LRU_BLOCKS):
            cs = slice(c * LRU_BLOCK_W, (c + 1) * LRU_BLOCK_W)
            u16 = u_ref[:, cs].astype(BF16)
            u = u_ref[:, cs].astype(F32)
            ta = jnp.tanh(jnp.dot(u16, wa_ref[d, c], preferred_element_type=F32) + 0.5 * ba_ref[d, :, cs])
            ti = jnp.tanh(jnp.dot(u16, wi_ref[d, c], preferred_element_type=F32) + 0.5 * bi_ref[d, :, cs])
            neg_lam = -lam_ref[d, :, cs]
            softplus = jnp.maximum(neg_lam, 0.0) + jnp.log1p(jnp.exp(-jnp.abs(neg_lam)))
            k = (-0.5 * LRU_C * LOG2E) * softplus
            a = jnp.exp2(k * ta + k)
            w = 1.0 - a * a
            root = w * lax.rsqrt(jnp.maximum(w, 1e-30))
            a_sc[d, :, cs] = a
            bx_sc[d, :, cs] = root * (ti * u + u)

    def step(t, carry):
        hf, hb = carry
        rf = pl.multiple_of(t * 2 * s8, 2 * s8)
        rb = pl.multiple_of((nt - 2 - 2 * t) * s8, 2 * s8)
        hf1 = a_sc[0, pl.ds(rf, s8), :] * hf + bx_sc[0, pl.ds(rf, s8), :]
        hf2 = a_sc[0, pl.ds(rf + s8, s8), :] * hf1 + bx_sc[0, pl.ds(rf + s8, s8), :]
        yf_ref[pl.ds(rf, 2 * s8), :] = jnp.concatenate([hf1, hf2], axis=0).astype(yf_ref.dtype)
        hb1 = a_sc[1, pl.ds(rb + s8, s8), :] * hb + bx_sc[1, pl.ds(rb + s8, s8), :]
        hb2 = a_sc[1, pl.ds(rb, s8), :] * hb1 + bx_sc[1, pl.ds(rb, s8), :]
        yb_ref[pl.ds(rb, 2 * s8), :] = jnp.concatenate([hb2, hb1], axis=0).astype(yb_ref.dtype)
        return hf2, hb2

    hf, hb = lax.fori_loop(0, nt // 2, step, (h_sc[0], h_sc[1]), unroll=2)
    h_sc[0] = hf
    h_sc[1] = hb

    @pl.when(i == n - 1)
    def _():
        ht_ref[...] = h_sc[...]


def _scan_call(u2, h0, w_a, b_a, w_i, b_i, lam):
    rows_total = u2.shape[0]
    rows = SCAN_T * SUBLANES
    n = rows_total // rows
    w = D_RNN
    fwd = lambda i: (i, 0)
    bwd = lambda i: (n - 1 - i, 0)
    return pl.pallas_call(
        _scan_kernel,
        grid=(n,),
        in_specs=[
            pl.BlockSpec((rows, w), fwd),
            pl.BlockSpec((rows, w), bwd),
            _const_spec((2, SUBLANES, w)),
            _const_spec((2, N_LRU_BLOCKS, LRU_BLOCK_W, LRU_BLOCK_W)),
            _const_spec((2, 1, w)),
            _const_spec((2, N_LRU_BLOCKS, LRU_BLOCK_W, LRU_BLOCK_W)),
            _const_spec((2, 1, w)),
            _const_spec((2, 1, w)),
        ],
        out_specs=[
            pl.BlockSpec((rows, w), fwd),
            pl.BlockSpec((rows, w), bwd),
            pl.BlockSpec((2, SUBLANES, w), lambda i: (0, 0, 0)),
        ],
        out_shape=[
            jax.ShapeDtypeStruct((rows_total, w), Y_DTYPE),
            jax.ShapeDtypeStruct((rows_total, w), Y_DTYPE),
            jax.ShapeDtypeStruct((2, SUBLANES, w), F32),
        ],
        scratch_shapes=[
            pltpu.VMEM((2, rows, w), F32),
            pltpu.VMEM((2, rows, w), F32),
            pltpu.VMEM((2, SUBLANES, w), F32),
        ],
        compiler_params=_params(("arbitrary",)),
        name="lru_scan",
    )(u2, u2, h0, w_a, b_a, w_i, b_i, lam)


def _rope_tables():
    t = jnp.arange(SEQ)
    row = (t // GRID_W).astype(F32)
    col = (t % GRID_W).astype(F32)
    half = HEAD_DIM // 2
    inv = ROPE_BASE ** (-jnp.arange(0, half, 2, dtype=F32) / half)
    ang_r = row[:, None] * inv[None, :]
    ang_c = col[:, None] * inv[None, :]
    ang = jnp.concatenate([ang_r, ang_r, ang_c, ang_c], axis=-1)
    ang = jnp.tile(ang, (1, LANES // HEAD_DIM))
    low = (jnp.arange(LANES) % 32) < 16
    sin = jnp.sin(ang)
    return jnp.cos(ang), jnp.where(low, -sin, 0.0), jnp.where(low, 0.0, sin)


def kernel(x, c, ctx, c_ctx, ada_w, ada_b, norm_g, mlp_w1, mlp_w2, attn_w_qkv, attn_w_o, attn_sink,
           lru_w_in, lru_conv_w, lru_conv_b, lru_w_a, lru_b_a, lru_w_i, lru_b_i, lru_lam, lru_w_out):
    n_lat = BATCH * SEQ
    n_ctx = BATCH * CTX_LEN

    c16 = jnp.zeros((16, D_MODEL), F32).at[:BATCH].set(c).at[BATCH].set(c_ctx)
    mods = _mod_call(c16, ada_w, ada_b).reshape(2, 16, N_MOD, D_MODEL)

    def slab_bmajor(m):
        return jnp.broadcast_to(m[:, :, None, :], (BATCH, N_MOD, SUBLANES, D_MODEL))

    def slab_ctx(m):
        return jnp.broadcast_to(m[None, :, None, :], (1, N_MOD, SUBLANES, D_MODEL))

    mod_x0 = slab_bmajor(mods[0, :BATCH])
    mod_c0 = slab_ctx(mods[0, BATCH])
    w_qkv = attn_w_qkv[0]
    w_qkv = jnp.concatenate([w_qkv[:, :D_Q] * (HEAD_DIM ** -0.5 * LOG2E), w_qkv[:, D_Q:]], axis=1).astype(BF16)
    sink2 = attn_sink[0] * LOG2E
    w_o = attn_w_o[0].astype(BF16)
    w1_0, w2_0 = mlp_w1[0].astype(BF16), mlp_w2[0].astype(BF16)
    g0 = norm_g[0]
    tiles_per_batch = SEQ // TOKEN_TILE

    x2 = x.reshape(n_lat, D_MODEL)
    c2 = ctx.reshape(n_ctx, D_MODEL)
    q, k, v = _qkv_call(x2, mod_x0, g0, w_qkv, _rope_tables(), tiles_per_batch)
    qc, kc, vc = _qkv_call(c2, mod_c0, g0, w_qkv, None, n_ctx // TOKEN_TILE)
    kc3 = kc.reshape(BATCH, CTX_LEN, D_K2)
    vc3 = vc.reshape(BATCH, CTX_LEN, D_KV)
    att = _attn_call(sink2, q, k.reshape(BATCH, SEQ, D_K2), v.reshape(BATCH, SEQ, D_KV), kc3, vc3)
    att_c = _ctx_attn_call(sink2, qc, kc3, vc3)
    x2 = _post_call(x2, [att], mod_x0, g0, w_o, w1_0, w2_0, tiles_per_batch, lru=False)
    c2 = _post_call(c2, [att_c], mod_c0, g0, w_o, w1_0, w2_0, n_ctx // TOKEN_TILE, lru=False)

    x3 = x2.reshape(BATCH, SEQ, D_MODEL)
    c3 = c2.reshape(BATCH, CTX_LEN, D_MODEL)
    mod_x1 = mods[1, :BATCH].transpose(1, 0, 2)[None]
    mod_c1 = slab_ctx(mods[1, BATCH])
    g1 = norm_g[1]
    w_in = lru_w_in[0].astype(BF16)
    conv_w = 0.5 * lru_conv_w[0]
    conv_b = 0.5 * lru_conv_b[0].reshape(1, D_RNN)
    w_a, w_i = lru_w_a[0].astype(BF16), lru_w_i[0].astype(BF16)
    b_a, b_i = lru_b_a[0].reshape(2, 1, D_RNN), lru_b_i[0].reshape(2, 1, D_RNN)
    lam = lru_lam[0].reshape(2, 1, D_RNN)
    scan = functools.partial(_scan_call, w_a=w_a, b_a=b_a, w_i=w_i, b_i=b_i, lam=lam)

    _, u_c = _lru_in_call(c3, mod_c1, g1, w_in, conv_w, conv_b)
    _, _, h_ctx = scan(u_c, jnp.zeros((2, SUBLANES, D_RNN), F32))
    gate_x, u_x = _lru_in_call(x3, mod_x1, g1, w_in, conv_w, conv_b)
    yf, yb, _ = scan(u_x, h_ctx)
    return _post_call(x3, [gate_x, yf, yb], mod_x1, g1, lru_w_out[0].astype(BF16),
                      mlp_w1[1].astype(BF16), mlp_w2[1].astype(BF16), n_lat // TOKEN_TILE, lru=True)
```

```python
import functools

import jax
import jax.numpy as jnp
from jax import lax
from jax.experimental import pallas as pl
from jax.experimental.pallas import tpu as pltpu

D_MODEL = 1024
BATCH = 8
SEQ = 2048
GRID_W = 64
CTX_LEN = 256
HEAD_DIM = 64
N_HEADS = 16
N_KV_HEADS = 4
GQA_GROUP = N_HEADS // N_KV_HEADS
WINDOW = 128
BLOCK = 128
ROPE_BASE = 10000.0
D_RNN = 1280
LRU_BLOCK_W = 256
N_LRU_BLOCKS = D_RNN // LRU_BLOCK_W
CONV_W = 4
LRU_C = 8.0
D_FF = 4 * D_MODEL
N_MOD = 6
EPS = 1e-6
NEG_INF = -1e30

D_Q = N_HEADS * HEAD_DIM
D_KV = N_KV_HEADS * HEAD_DIM
D_K2 = 2 * D_KV
LANES = 128
SUBLANES = 8
N_SLAB = D_MODEL // LANES
TOKEN_TILE = 512
FF_CHUNK = 1024
ATTN_Q_PER_STEP = 4
SCAN_T = 128
HALO = 16
U_DTYPE = jnp.bfloat16
Y_DTYPE = jnp.bfloat16
LOG2E = 1.4426950408889634
VMEM_LIMIT = 60 * 1024 * 1024

F32 = jnp.float32
BF16 = jnp.bfloat16


def _rms(x, g):
    ms = jnp.mean(x * x, axis=-1, keepdims=True)
    return x * lax.rsqrt(ms + EPS) * g


def _slab(x):
    return x.reshape(x.shape[0] // SUBLANES, SUBLANES, x.shape[1])


def _modulate(h, shift8, scale8):
    out = _slab(h) * (1.0 + scale8)[None] + shift8[None]
    return out.reshape(h.shape)


def _gated_add(x, gate8, y):
    out = _slab(x) + gate8[None] * _slab(y)
    return out.reshape(x.shape)


def _const_spec(shape):
    n = len(shape)
    return pl.BlockSpec(shape, lambda *_: (0,) * n, pipeline_mode=pl.Buffered(1))


def _params(sem):
    return pltpu.CompilerParams(dimension_semantics=sem, vmem_limit_bytes=VMEM_LIMIT)


def _mod_kernel(c_ref, w_ref, b_ref, o_ref):
    s = jax.nn.silu(c_ref[...]).astype(BF16)
    o_ref[0] = jnp.dot(s, w_ref[0].astype(BF16), preferred_element_type=F32) + b_ref[0]


def _mod_call(c16, ada_w, ada_b):
    depth = ada_w.shape[0]
    nt = 1024
    return pl.pallas_call(
        _mod_kernel,
        grid=(depth, N_MOD * D_MODEL // nt),
        in_specs=[
            pl.BlockSpec((16, D_MODEL), lambda l, j: (0, 0)),
            pl.BlockSpec((1, D_MODEL, nt), lambda l, j: (l, 0, j)),
            pl.BlockSpec((1, 1, nt), lambda l, j: (l, 0, j)),
        ],
        out_specs=pl.BlockSpec((1, 16, nt), lambda l, j: (l, 0, j)),
        out_shape=jax.ShapeDtypeStruct((depth, 16, N_MOD * D_MODEL), F32),
        compiler_params=_params(("arbitrary", "arbitrary")),
        name="adaln_mod",
    )(c16, ada_w, ada_b.reshape(depth, 1, N_MOD * D_MODEL))


def _qkv_kernel(*refs, rope):
    if rope:
        x_ref, mod_ref, g_ref, w_ref, cos_ref, sa_ref, sb_ref, q_ref, k_ref, v_ref = refs
    else:
        x_ref, mod_ref, g_ref, w_ref, q_ref, k_ref, v_ref = refs
    h = _modulate(_rms(x_ref[...], g_ref[0:1, :]), mod_ref[0, 0], mod_ref[0, 1])
    y = jnp.dot(h.astype(BF16), w_ref[...], preferred_element_type=F32)
    if rope:
        cos, sa, sb = cos_ref[...], sa_ref[...], sb_ref[...]
    low = lax.broadcasted_iota(jnp.int32, (x_ref.shape[0], LANES), 1) < HEAD_DIM
    for c in range((D_Q + D_KV) // LANES):
        yc = y[:, c * LANES:(c + 1) * LANES]
        if rope:
            yc = yc * cos + pltpu.roll(yc, LANES - 16, 1) * sa + pltpu.roll(yc, 16, 1) * sb
        if c < D_Q // LANES:
            q_ref[:, c * LANES:(c + 1) * LANES] = yc.astype(BF16)
        else:
            c2 = 2 * (c - D_Q // LANES)
            swapped = pltpu.roll(yc, HEAD_DIM, 1)
            k_ref[:, c2 * LANES:(c2 + 1) * LANES] = jnp.where(low, yc, swapped).astype(BF16)
            k_ref[:, (c2 + 1) * LANES:(c2 + 2) * LANES] = jnp.where(low, swapped, yc).astype(BF16)
    v_ref[...] = y[:, D_Q + D_KV:].astype(BF16)


def _qkv_call(x2, mod, g, w_qkv, tables, tiles_per_group):
    n = x2.shape[0]
    tm = TOKEN_TILE
    rope = tables is not None
    in_specs = [
        pl.BlockSpec((tm, D_MODEL), lambda i: (i, 0)),
        pl.BlockSpec((1, N_MOD, SUBLANES, D_MODEL), lambda i: (i // tiles_per_group, 0, 0, 0)),
        _const_spec((4, D_MODEL)),
        _const_spec((D_MODEL, D_Q + 2 * D_KV)),
    ]
    args = [x2, mod, g, w_qkv]
    if rope:
        nt = SEQ // tm
        in_specs += [pl.BlockSpec((tm, LANES), lambda i: (i % nt, 0))] * 3
        args += list(tables)
    return pl.pallas_call(
        functools.partial(_qkv_kernel, rope=rope),
        grid=(n // tm,),
        in_specs=in_specs,
        out_specs=[
            pl.BlockSpec((tm, D_Q), lambda i: (i, 0)),
            pl.BlockSpec((tm, D_K2), lambda i: (i, 0)),
            pl.BlockSpec((tm, D_KV), lambda i: (i, 0)),
        ],
        out_shape=[
            jax.ShapeDtypeStruct((n, D_Q), BF16),
            jax.ShapeDtypeStruct((n, D_K2), BF16),
            jax.ShapeDtypeStruct((n, D_KV), BF16),
        ],
        compiler_params=_params(("parallel",)),
        name="qkv_rope" if rope else "qkv_ctx",
    )(*args)


VT_ROWS = HEAD_DIM + 16


def _attn_kernel(*refs, local, q_per_step):
    if local:
        sink_ref, q_ref, k_ref, v_ref, kc_ref, vc_ref, o_ref, vt_sc, s_sc, p_sc = refs
    else:
        sink_ref, q_ref, kc_ref, vc_ref, o_ref, vt_sc, s_sc, p_sc = refs
    j = pl.program_id(1)
    seq_blocks = SEQ // BLOCK if local else 0
    ctx_blocks = CTX_LEN // BLOCK

    @pl.when((pl.program_id(0) == 0) & (j == 0))
    def _init_static():
        row = lax.broadcasted_iota(jnp.int32, (VT_ROWS - HEAD_DIM, LANES), 0)
        pad = jnp.where(row == 0, 1.0, 0.0).astype(BF16)
        for kh in range(N_KV_HEADS):
            for blk in range(seq_blocks + ctx_blocks):
                vt_sc[kh, blk, HEAD_DIM:VT_ROWS, :] = pad

    def fill(src_v, n_blocks, first_block):
        def body(i, carry):
            r0 = pl.multiple_of(i * BLOCK, BLOCK)
            vt = src_v[0, pl.ds(r0, BLOCK), :].astype(F32).T
            for kh in range(N_KV_HEADS):
                vt_sc[kh, first_block + i, 0:HEAD_DIM, :] = vt[kh * HEAD_DIM:(kh + 1) * HEAD_DIM].astype(BF16)
            return carry
        lax.fori_loop(0, n_blocks, body, 0)

    @pl.when(j == 0)
    def _build():
        if local:
            fill(v_ref, seq_blocks, 0)
        fill(vc_ref, ctx_blocks, seq_blocks)

    nt = (((1,), (1,)), ((), ()))
    n_band = 3
    n_chunks = D_Q // LANES
    lane = lax.broadcasted_iota(jnp.int32, (BLOCK, LANES), 1)
    first_head = lax.broadcasted_iota(jnp.int32, (1, 2 * BLOCK), 1) < BLOCK

    def block_params(qb):
        jq = j * q_per_step + qb
        rows = pl.ds(pl.multiple_of(qb * BLOCK, BLOCK), BLOCK)
        if not local:
            return rows, None, None, None
        blk0 = jnp.clip(jq - 1, 0, seq_blocks - n_band)
        start = pl.multiple_of(blk0 * BLOCK, BLOCK)
        kpos = start + lax.broadcasted_iota(jnp.int32, (n_band * BLOCK, BLOCK), 0)
        qpos = jq * BLOCK + lax.broadcasted_iota(jnp.int32, (n_band * BLOCK, BLOCK), 1)
        bias = jnp.where(jnp.abs(kpos - qpos) <= WINDOW, 0.0, NEG_INF).astype(F32)
        return rows, blk0, start, jnp.concatenate([bias, bias], axis=1)

    def scores(c, params):
        rows, _, start, bias2 = params
        kcols = slice((c // 2) * LANES, (c // 2 + 1) * LANES)
        qc = q_ref[rows, c * LANES:(c + 1) * LANES]
        zero = jnp.zeros_like(qc)
        q2 = jnp.concatenate([jnp.where(lane < HEAD_DIM, qc, zero), jnp.where(lane < HEAD_DIM, zero, qc)], axis=0)
        s_sc[c % 2, 0:CTX_LEN] = lax.dot_general(kc_ref[0, :, kcols], q2, nt, preferred_element_type=F32)
        if local:
            s_sc[c % 2, CTX_LEN:] = lax.dot_general(k_ref[0, pl.ds(start, n_band * BLOCK), kcols], q2, nt,
                                                    preferred_element_type=F32) + bias2

    def softmax(c):
        s = s_sc[c % 2]
        sink_row = jnp.where(first_head, sink_ref[2 * c], sink_ref[2 * c + 1])
        m = jnp.maximum(jnp.max(s, axis=0, keepdims=True), sink_row)
        p_sc[c % 2] = jnp.exp2(s - m).astype(BF16)
        return jnp.exp2(sink_row - m)

    def values(c, params, sink_term):
        rows, blk0, _, _ = params
        kh = c // 2
        vt = [vt_sc[kh, seq_blocks + i] for i in range(ctx_blocks)]
        if local:
            vt_band = vt_sc[kh, pl.ds(blk0, n_band)]
            vt += [vt_band[i] for i in range(n_band)]
        acc = jnp.dot(jnp.concatenate(vt, axis=1), p_sc[c % 2], preferred_element_type=F32)
        out_t = acc[0:HEAD_DIM] / (acc[HEAD_DIM:HEAD_DIM + 1] + sink_term)
        both = jnp.concatenate([out_t[:, :BLOCK], out_t[:, BLOCK:]], axis=0)
        o_ref[rows, c * LANES:(c + 1) * LANES] = both.T.astype(o_ref.dtype)

    def query_block(qb, sink_term0):
        cur = block_params(qb)
        nxt = block_params(jnp.minimum(qb + 1, q_per_step - 1))
        sink_terms = {0: sink_term0}
        for c in range(n_chunks):
            if c + 2 < n_chunks:
                scores(c + 2, cur)
            else:
                scores(c + 2 - n_chunks, nxt)
            sink_terms[c + 1] = softmax((c + 1) % n_chunks)
            values(c, cur, sink_terms[c])
        return sink_terms[n_chunks]

    first = block_params(0)
    scores(0, first)
    scores(1, first)
    lax.fori_loop(0, q_per_step, query_block, softmax(0))


def _attn_scratch(n_blocks, n_keys):
    return [
        pltpu.VMEM((N_KV_HEADS, n_blocks, VT_ROWS, BLOCK), BF16),
        pltpu.VMEM((2, n_keys, 2 * BLOCK), F32),
        pltpu.VMEM((2, n_keys, 2 * BLOCK), BF16),
    ]


def _attn_call(sink2, q, k, v, kc, vc):
    nb = SEQ // BLOCK
    qps = ATTN_Q_PER_STEP
    steps = nb // qps
    return pl.pallas_call(
        functools.partial(_attn_kernel, local=True, q_per_step=qps),
        grid=(BATCH, steps),
        in_specs=[
            pl.BlockSpec(memory_space=pltpu.SMEM),
            pl.BlockSpec((qps * BLOCK, D_Q), lambda b, j: (b * steps + j, 0)),
            pl.BlockSpec((1, SEQ, D_K2), lambda b, j: (b, 0, 0)),
            pl.BlockSpec((1, SEQ, D_KV), lambda b, j: (b, 0, 0)),
            pl.BlockSpec((1, CTX_LEN, D_K2), lambda b, j: (b, 0, 0)),
            pl.BlockSpec((1, CTX_LEN, D_KV), lambda b, j: (b, 0, 0)),
        ],
        out_specs=pl.BlockSpec((qps * BLOCK, D_Q), lambda b, j: (b * steps + j, 0)),
        out_shape=jax.ShapeDtypeStruct((BATCH * SEQ, D_Q), BF16),
        scratch_shapes=_attn_scratch(nb + CTX_LEN // BLOCK, CTX_LEN + 3 * BLOCK),
        compiler_params=_params(("arbitrary", "arbitrary")),
        name="band_attn",
    )(sink2, q, k, v, kc, vc)


def _ctx_attn_call(sink2, qc, kc, vc):
    nb = CTX_LEN // BLOCK
    return pl.pallas_call(
        functools.partial(_attn_kernel, local=False, q_per_step=nb),
        grid=(BATCH, 1),
        in_specs=[
            pl.BlockSpec(memory_space=pltpu.SMEM),
            pl.BlockSpec((CTX_LEN, D_Q), lambda b, j: (b, 0)),
            pl.BlockSpec((1, CTX_LEN, D_K2), lambda b, j: (b, 0, 0)),
            pl.BlockSpec((1, CTX_LEN, D_KV), lambda b, j: (b, 0, 0)),
        ],
        out_specs=pl.BlockSpec((CTX_LEN, D_Q), lambda b, j: (b, 0)),
        out_shape=jax.ShapeDtypeStruct((BATCH * CTX_LEN, D_Q), BF16),
        scratch_shapes=_attn_scratch(nb, CTX_LEN),
        compiler_params=_params(("arbitrary", "arbitrary")),
        name="ctx_attn",
    )(sink2, qc, kc, vc)


def _to_time_major(src_ref, sc_ref):
    nt = src_ref.shape[1]
    for b in range(BATCH):
        for s in range(N_SLAB):
            sc_ref[s, pl.ds(b, nt, stride=SUBLANES), :] = src_ref[b, :, s * LANES:(s + 1) * LANES]
    return jnp.concatenate([sc_ref[s, 0:nt * SUBLANES, :] for s in range(N_SLAB)], axis=1)


def _from_time_major(val, sc_ref, dst_ref):
    nt = dst_ref.shape[1]
    for s in range(N_SLAB):
        sc_ref[s, 0:nt * SUBLANES, :] = val[:, s * LANES:(s + 1) * LANES]
    for b in range(BATCH):
        for s in range(N_SLAB):
            dst_ref[b, :, s * LANES:(s + 1) * LANES] = sc_ref[s, pl.ds(b, nt, stride=SUBLANES), :]


def _post_kernel(*refs, lru):
    if lru:
        x_ref, gate_ref, yf_ref, yb_ref, mod_ref, g_ref, wf_ref, w1_ref, w2_ref, o_ref, x1_sc, h_sc, acc_sc, t_sc = refs
        front = (gate_ref[...].astype(F32) * (yf_ref[...].astype(F32) + yb_ref[...].astype(F32))).astype(BF16)
        x = _to_time_major(x_ref, t_sc)
    else:
        x_ref, a_ref, mod_ref, g_ref, wf_ref, w1_ref, w2_ref, o_ref, x1_sc, h_sc, acc_sc = refs
        front = a_ref[...]
        x = x_ref[...]
    half_rows = x.shape[0] // 2

    def head(r):
        rs = slice(r * half_rows, (r + 1) * half_rows)
        y = jnp.dot(front[rs], wf_ref[...], preferred_element_type=F32)
        x1 = _gated_add(x[rs], mod_ref[0, 2], _rms(y, g_ref[1:2, :]))
        x1_sc[r] = x1
        h_sc[r] = _modulate(_rms(x1, g_ref[2:3, :]), mod_ref[0, 3], mod_ref[0, 4]).astype(BF16)

    def mlp(r):
        acc = jnp.zeros((half_rows, D_MODEL), F32)
        for c in range(D_FF // FF_CHUNK):
            hid = jnp.dot(h_sc[r], w1_ref[:, c * FF_CHUNK:(c + 1) * FF_CHUNK], preferred_element_type=F32)
            hid = jnp.square(jnp.maximum(hid, 0.0)).astype(BF16)
            acc = acc + jnp.dot(hid, w2_ref[c * FF_CHUNK:(c + 1) * FF_CHUNK, :], preferred_element_type=F32)
        acc_sc[r] = acc

    def tail(r):
        return _gated_add(x1_sc[r], mod_ref[0, 5], _rms(acc_sc[r], g_ref[3:4, :]))

    head(0)
    head(1)
    mlp(0)
    out0 = tail(0)
    mlp(1)
    out = jnp.concatenate([out0, tail(1)], axis=0)
    if lru:
        _from_time_major(out, t_sc, o_ref)
    else:
        o_ref[...] = out


def _post_call(x, fronts, mod, g, w_front, w1, w2, tiles_per_group, lru):
    tm = TOKEN_TILE
    row = lambda i: (i, 0)
    if lru:
        nt = tm // BATCH
        n = x.shape[0] * x.shape[1]
        x_spec = pl.BlockSpec((BATCH, nt, D_MODEL), lambda i: (0, i, 0))
        scratch = [pltpu.VMEM((N_SLAB, tm, LANES), F32)]
    else:
        n = x.shape[0]
        x_spec = pl.BlockSpec((tm, D_MODEL), row)
        scratch = []
    scratch = [
        pltpu.VMEM((2, tm // 2, D_MODEL), F32),
        pltpu.VMEM((2, tm // 2, D_MODEL), BF16),
        pltpu.VMEM((2, tm // 2, D_MODEL), F32),
    ] + scratch
    in_specs = [x_spec]
    in_specs += [pl.BlockSpec((tm, f.shape[1]), row) for f in fronts]
    in_specs += [
        pl.BlockSpec((1, N_MOD, SUBLANES, D_MODEL), lambda i: (i // tiles_per_group, 0, 0, 0)),
        _const_spec((4, D_MODEL)),
        _const_spec(w_front.shape),
        _const_spec(w1.shape),
        _const_spec(w2.shape),
    ]
    return pl.pallas_call(
        functools.partial(_post_kernel, lru=lru),
        grid=(n // tm,),
        in_specs=in_specs,
        out_specs=x_spec,
        out_shape=jax.ShapeDtypeStruct(x.shape, F32),
        scratch_shapes=scratch,
        compiler_params=_params(("parallel",)),
        name="lru_out_mlp" if lru else "attn_out_mlp",
    )(x, *fronts, mod, g, w_front, w1, w2)


def _lru_in_kernel(x_ref, xp_ref, xn_ref, mod_ref, g_ref, w_ref, cw_ref, cb_ref, gate_ref, u_ref,
                   v_sc, t_sc, tp_sc, tn_sc):
    i = pl.program_id(0)
    n = pl.num_programs(0)
    rows = x_ref.shape[0] * x_ref.shape[1]
    s8 = SUBLANES

    def pre(x):
        return _modulate(_rms(x, g_ref[0:1, :]), mod_ref[0, 0], mod_ref[0, 1]).astype(BF16)

    h = pre(_to_time_major(x_ref, t_sc))
    gate_ref[...] = jax.nn.gelu(jnp.dot(h, w_ref[:, :D_RNN], preferred_element_type=F32)).astype(BF16)
    x_prev = _to_time_major(xp_ref, tp_sc)[SUBLANES * SUBLANES - HALO:]
    x_next = _to_time_major(xn_ref, tn_sc)[:HALO]
    h_ext = jnp.concatenate([pre(x_prev), h, pre(x_next)], axis=0)
    v_sc[...] = jnp.dot(h_ext, w_ref[:, D_RNN:], preferred_element_type=F32)
    v_sc[0:HALO] = v_sc[0:HALO] * (i > 0).astype(F32)
    v_sc[HALO + rows:HALO + rows + s8] = v_sc[HALO + rows:HALO + rows + s8] * (i < n - 1).astype(F32)
    u_ref[...] = (cb_ref[...]
                  + cw_ref[0:1, :] * v_sc[HALO - 2 * s8:HALO - 2 * s8 + rows]
                  + cw_ref[1:2, :] * v_sc[HALO - s8:HALO - s8 + rows]
                  + cw_ref[2:3, :] * v_sc[HALO:HALO + rows]
                  + cw_ref[3:4, :] * v_sc[HALO + s8:HALO + s8 + rows]).astype(u_ref.dtype)


def _lru_in_call(x3, mod, g, w_in, conv_w, conv_b):
    t_total = x3.shape[1]
    n = BATCH * t_total
    tm = TOKEN_TILE
    nt = tm // BATCH
    row = lambda i: (i, 0)
    per_tile = nt // SUBLANES
    last = t_total // SUBLANES - 1
    halo_spec = lambda f: pl.BlockSpec((BATCH, SUBLANES, D_MODEL), f)
    return pl.pallas_call(
        _lru_in_kernel,
        grid=(n // tm,),
        in_specs=[
            pl.BlockSpec((BATCH, nt, D_MODEL), lambda i: (0, i, 0)),
            halo_spec(lambda i: (0, jnp.maximum(i * per_tile - 1, 0), 0)),
            halo_spec(lambda i: (0, jnp.minimum((i + 1) * per_tile, last), 0)),
            _const_spec((1, N_MOD, SUBLANES, D_MODEL)),
            _const_spec((4, D_MODEL)),
            _const_spec((D_MODEL, 2 * D_RNN)),
            _const_spec((CONV_W, D_RNN)),
            _const_spec((1, D_RNN)),
        ],
        out_specs=[pl.BlockSpec((tm, D_RNN), row), pl.BlockSpec((tm, D_RNN), row)],
        out_shape=[jax.ShapeDtypeStruct((n, D_RNN), BF16), jax.ShapeDtypeStruct((n, D_RNN), U_DTYPE)],
        scratch_shapes=[
            pltpu.VMEM((tm + 2 * HALO, D_RNN), F32),
            pltpu.VMEM((N_SLAB, tm, LANES), F32),
            pltpu.VMEM((N_SLAB, SUBLANES * SUBLANES, LANES), F32),
            pltpu.VMEM((N_SLAB, SUBLANES * SUBLANES, LANES), F32),
        ],
        compiler_params=_params(("parallel",)),
        name="lru_in",
    )(x3, x3, x3, mod, g, w_in, conv_w, conv_b)


def _scan_kernel(uf_ref, ub_ref, h0_ref, wa_ref, ba_ref, wi_ref, bi_ref, lam_ref, yf_ref, yb_ref, ht_ref,
                 a_sc, bx_sc, h_sc):
    i = pl.program_id(0)
    n = pl.num_programs(0)
    rows = uf_ref.shape[0]
    nt = rows // SUBLANES
    s8 = SUBLANES

    @pl.when(i == 0)
    def _():
        h_sc[...] = h0_ref[...]

    for d, u_ref in enumerate((uf_ref, ub_ref)):
        for c in range(N_LRU_BLOCKS):
            cs = slice(c * LRU_BLOCK_W, (c + 1) * LRU_BLOCK_W)
            u16 = u_ref[:, cs].astype(BF16)
            u = u_ref[:, cs].astype(F32)
            ta = jnp.tanh(jnp.dot(u16, wa_ref[d, c], preferred_element_type=F32) + 0.5 * ba_ref[d, :, cs])
            ti = jnp.tanh(jnp.dot(u16, wi_ref[d, c], preferred_element_type=F32) + 0.5 * bi_ref[d, :, cs])
            neg_lam = -lam_ref[d, :, cs]
            softplus = jnp.maximum(neg_lam, 0.0) + jnp.log1p(jnp.exp(-jnp.abs(neg_lam)))
            k = (-0.5 * LRU_C * LOG2E) * softplus
            a = jnp.exp2(k * ta + k)
            w = 1.0 - a * a
            root = w * lax.rsqrt(jnp.maximum(w, 1e-30))
            a_sc[d, :, cs] = a
            bx_sc[d, :, cs] = root * (ti * u + u)

    def step(t, carry):
        hf, hb = carry
        rf = pl.multiple_of(t * 2 * s8, 2 * s8)
        rb = pl.multiple_of((nt - 2 - 2 * t) * s8, 2 * s8)
        hf1 = a_sc[0, pl.ds(rf, s8), :] * hf + bx_sc[0, pl.ds(rf, s8), :]
        hf2 = a_sc[0, pl.ds(rf + s8, s8), :] * hf1 + bx_sc[0, pl.ds(rf + s8, s8), :]
        yf_ref[pl.ds(rf, 2 * s8), :] = jnp.concatenate([hf1, hf2], axis=0).astype(yf_ref.dtype)
        hb1 = a_sc[1, pl.ds(rb + s8, s8), :] * hb + bx_sc[1, pl.ds(rb + s8, s8), :]
        hb2 = a_sc[1, pl.ds(rb, s8), :] * hb1 + bx_sc[1, pl.ds(rb, s8), :]
        yb_ref[pl.ds(rb, 2 * s8), :] = jnp.concatenate([hb2, hb1], axis=0).astype(yb_ref.dtype)
        return hf2, hb2

    hf, hb = lax.fori_loop(0, nt // 2, step, (h_sc[0], h_sc[1]), unroll=2)
    h_sc[0] = hf
    h_sc[1] = hb

    @pl.when(i == n - 1)
    def _():
        ht_ref[...] = h_sc[...]


def _scan_call(u2, h0, w_a, b_a, w_i, b_i, lam):
    rows_total = u2.shape[0]
    rows = SCAN_T * SUBLANES
    n = rows_total // rows
    w = D_RNN
    fwd = lambda i: (i, 0)
    bwd = lambda i: (n - 1 - i, 0)
    return pl.pallas_call(
        _scan_kernel,
        grid=(n,),
        in_specs=[
            pl.BlockSpec((rows, w), fwd),
            pl.BlockSpec((rows, w), bwd),
            _const_spec((2, SUBLANES, w)),
            _const_spec((2, N_LRU_BLOCKS, LRU_BLOCK_W, LRU_BLOCK_W)),
            _const_spec((2, 1, w)),
            _const_spec((2, N_LRU_BLOCKS, LRU_BLOCK_W, LRU_BLOCK_W)),
            _const_spec((2, 1, w)),
            _const_spec((2, 1, w)),
        ],
        out_specs=[
            pl.BlockSpec((rows, w), fwd),
            pl.BlockSpec((rows, w), bwd),
            pl.BlockSpec((2, SUBLANES, w), lambda i: (0, 0, 0)),
        ],
        out_shape=[
            jax.ShapeDtypeStruct((rows_total, w), Y_DTYPE),
            jax.ShapeDtypeStruct((rows_total, w), Y_DTYPE),
            jax.ShapeDtypeStruct((2, SUBLANES, w), F32),
        ],
        scratch_shapes=[
            pltpu.VMEM((2, rows, w), F32),
            pltpu.VMEM((2, rows, w), F32),
            pltpu.VMEM((2, SUBLANES, w), F32),
        ],
        compiler_params=_params(("arbitrary",)),
        name="lru_scan",
    )(u2, u2, h0, w_a, b_a, w_i, b_i, lam)


def _rope_tables():
    t = jnp.arange(SEQ)
    row = (t // GRID_W).astype(F32)
    col = (t % GRID_W).astype(F32)
    half = HEAD_DIM // 2
    inv = ROPE_BASE ** (-jnp.arange(0, half, 2, dtype=F32) / half)
    ang_r = row[:, None] * inv[None, :]
    ang_c = col[:, None] * inv[None, :]
    ang = jnp.concatenate([ang_r, ang_r, ang_c, ang_c], axis=-1)
    ang = jnp.tile(ang, (1, LANES // HEAD_DIM))
    low = (jnp.arange(LANES) % 32) < 16
    sin = jnp.sin(ang)
    return jnp.cos(ang), jnp.where(low, -sin, 0.0), jnp.where(low, 0.0, sin)


def kernel(x, c, ctx, c_ctx, ada_w, ada_b, norm_g, mlp_w1, mlp_w2, attn_w_qkv, attn_w_o, attn_sink,
           lru_w_in, lru_conv_w, lru_conv_b, lru_w_a, lru_b_a, lru_w_i, lru_b_i, lru_lam, lru_w_out):
    n_lat = BATCH * SEQ
    n_ctx = BATCH * CTX_LEN

    c16 = jnp.zeros((16, D_MODEL), F32).at[:BATCH].set(c).at[BATCH].set(c_ctx)
    mods = _mod_call(c16, ada_w, ada_b).reshape(2, 16, N_MOD, D_MODEL)

    def slab_bmajor(m):
        return jnp.broadcast_to(m[:, :, None, :], (BATCH, N_MOD, SUBLANES, D_MODEL))

    def slab_ctx(m):
        return jnp.broadcast_to(m[None, :, None, :], (1, N_MOD, SUBLANES, D_MODEL))

    mod_x0 = slab_bmajor(mods[0, :BATCH])
    mod_c0 = slab_ctx(mods[0, BATCH])
    w_qkv = attn_w_qkv[0]
    w_qkv = jnp.concatenate([w_qkv[:, :D_Q] * (HEAD_DIM ** -0.5 * LOG2E), w_qkv[:, D_Q:]], axis=1).astype(BF16)
    sink2 = attn_sink[0] * LOG2E
    w_o = attn_w_o[0].astype(BF16)
    w1_0, w2_0 = mlp_w1[0].astype(BF16), mlp_w2[0].astype(BF16)
    g0 = norm_g[0]
    tiles_per_batch = SEQ // TOKEN_TILE

    x2 = x.reshape(n_lat, D_MODEL)
    c2 = ctx.reshape(n_ctx, D_MODEL)
    q, k, v = _qkv_call(x2, mod_x0, g0, w_qkv, _rope_tables(), tiles_per_batch)
    qc, kc, vc = _qkv_call(c2, mod_c0, g0, w_qkv, None, n_ctx // TOKEN_TILE)
    kc3 = kc.reshape(BATCH, CTX_LEN, D_K2)
    vc3 = vc.reshape(BATCH, CTX_LEN, D_KV)
    att = _attn_call(sink2, q, k.reshape(BATCH, SEQ, D_K2), v.reshape(BATCH, SEQ, D_KV), kc3, vc3)
    att_c = _ctx_attn_call(sink2, qc, kc3, vc3)
    x2 = _post_call(x2, [att], mod_x0, g0, w_o, w1_0, w2_0, tiles_per_batch, lru=False)
    c2 = _post_call(c2, [att_c], mod_c0, g0, w_o, w1_0, w2_0, n_ctx // TOKEN_TILE, lru=False)

    x3 = x2.reshape(BATCH, SEQ, D_MODEL)
    c3 = c2.reshape(BATCH, CTX_LEN, D_MODEL)
    mod_x1 = mods[1, :BATCH].transpose(1, 0, 2)[None]
    mod_c1 = slab_ctx(mods[1, BATCH])
    g1 = norm_g[1]
    w_in = lru_w_in[0].astype(BF16)
    conv_w = 0.5 * lru_conv_w[0]
    conv_b = 0.5 * lru_conv_b[0].reshape(1, D_RNN)
    w_a, w_i = lru_w_a[0].astype(BF16), lru_w_i[0].astype(BF16)
    b_a, b_i = lru_b_a[0].reshape(2, 1, D_RNN), lru_b_i[0].reshape(2, 1, D_RNN)
    lam = lru_lam[0].reshape(2, 1, D_RNN)
    scan = functools.partial(_scan_call, w_a=w_a, b_a=b_a, w_i=w_i, b_i=b_i, lam=lam)

    _, u_c = _lru_in_call(c3, mod_c1, g1, w_in, conv_w, conv_b)
    _, _, h_ctx = scan(u_c, jnp.zeros((2, SUBLANES, D_RNN), F32))
    gate_x, u_x = _lru_in_call(x3, mod_x1, g1, w_in, conv_w, conv_b)
    yf, yb, _ = scan(u_x, h_ctx)
    return _post_call(x3, [gate_x, yf, yb], mod_x1, g1, lru_w_out[0].astype(BF16),
                      mlp_w1[1].astype(BF16), mlp_w2[1].astype(BF16), n_lat // TOKEN_TILE, lru=True)
```

```python
import functools

import jax
import jax.numpy as jnp
import numpy as np
from jax import lax
from jax.experimental import pallas as pl
from jax.experimental.pallas import tpu as pltpu

D_MODEL = 1024
BATCH = 8
SEQ = 2048
GRID_W = 64
CTX_LEN = 256
HEAD_DIM = 64
N_HEADS = 16
N_KV_HEADS = 4
GQA_GROUP = N_HEADS // N_KV_HEADS
WINDOW = 128
BLOCK = 128
ROPE_BASE = 10000.0
D_RNN = 1280
LRU_BLOCK_W = 256
N_LRU_BLOCKS = D_RNN // LRU_BLOCK_W
CONV_W = 4
LRU_C = 8.0
D_FF = 4 * D_MODEL
N_MOD = 6
EPS = 1e-6
NEG_INF = -1e30

D_Q = N_HEADS * HEAD_DIM
D_KV = N_KV_HEADS * HEAD_DIM
D_K2 = 2 * D_KV
LANES = 128
SUBLANES = 8
N_SLAB = D_MODEL // LANES
TOKEN_TILE = 512
FF_CHUNK = 1024
ATTN_Q_PER_STEP = 4
SCAN_T = 128
HALO = 16
U_DTYPE = jnp.bfloat16
Y_DTYPE = jnp.bfloat16
LOG2E = 1.4426950408889634
VMEM_LIMIT = 60 * 1024 * 1024

F32 = jnp.float32
BF16 = jnp.bfloat16


def _rms(x, g):
    ms = jnp.mean(x * x, axis=-1, keepdims=True)
    return x * lax.rsqrt(ms + EPS) * g


def _slab(x):
    return x.reshape(x.shape[0] // SUBLANES, SUBLANES, x.shape[1])


def _modulate(h, shift8, scale8):
    out = _slab(h) * (1.0 + scale8)[None] + shift8[None]
    return out.reshape(h.shape)


def _gated_add(x, gate8, y):
    out = _slab(x) + gate8[None] * _slab(y)
    return out.reshape(x.shape)


def _const_spec(shape):
    n = len(shape)
    return pl.BlockSpec(shape, lambda *_: (0,) * n, pipeline_mode=pl.Buffered(1))


def _params(sem):
    return pltpu.CompilerParams(dimension_semantics=sem, vmem_limit_bytes=VMEM_LIMIT)


def _mod_kernel(c_ref, w_ref, b_ref, o_ref):
    s = jax.nn.silu(c_ref[...]).astype(BF16)
    o_ref[0] = jnp.dot(s, w_ref[0].astype(BF16), preferred_element_type=F32) + b_ref[0]


def _mod_call(c16, ada_w, ada_b):
    depth = ada_w.shape[0]
    nt = 1024
    return pl.pallas_call(
        _mod_kernel,
        grid=(depth, N_MOD * D_MODEL // nt),
        in_specs=[
            pl.BlockSpec((16, D_MODEL), lambda l, j: (0, 0)),
            pl.BlockSpec((1, D_MODEL, nt), lambda l, j: (l, 0, j)),
            pl.BlockSpec((1, 1, nt), lambda l, j: (l, 0, j)),
        ],
        out_specs=pl.BlockSpec((1, 16, nt), lambda l, j: (l, 0, j)),
        out_shape=jax.ShapeDtypeStruct((depth, 16, N_MOD * D_MODEL), F32),
        compiler_params=_params(("arbitrary", "arbitrary")),
        name="adaln_mod",
    )(c16, ada_w, ada_b.reshape(depth, 1, N_MOD * D_MODEL))


def _qkv_kernel(*refs, rope):
    if rope:
        x_ref, mod_ref, g_ref, w_ref, cos_ref, sa_ref, sb_ref, q_ref, k_ref, v_ref = refs
    else:
        x_ref, mod_ref, g_ref, w_ref, q_ref, k_ref, v_ref = refs
    h = _modulate(_rms(x_ref[...], g_ref[0:1, :]), mod_ref[0, 0], mod_ref[0, 1])
    y = jnp.dot(h.astype(BF16), w_ref[...], preferred_element_type=F32)
    if rope:
        cos, sa, sb = cos_ref[...], sa_ref[...], sb_ref[...]
    low = lax.broadcasted_iota(jnp.int32, (x_ref.shape[0], LANES), 1) < HEAD_DIM
    for c in range((D_Q + D_KV) // LANES):
        yc = y[:, c * LANES:(c + 1) * LANES]
        if rope:
            yc = yc * cos + pltpu.roll(yc, LANES - 16, 1) * sa + pltpu.roll(yc, 16, 1) * sb
        if c < D_Q // LANES:
            q_ref[:, c * LANES:(c + 1) * LANES] = yc.astype(BF16)
        else:
            c2 = 2 * (c - D_Q // LANES)
            swapped = pltpu.roll(yc, HEAD_DIM, 1)
            k_ref[:, c2 * LANES:(c2 + 1) * LANES] = jnp.where(low, yc, swapped).astype(BF16)
            k_ref[:, (c2 + 1) * LANES:(c2 + 2) * LANES] = jnp.where(low, swapped, yc).astype(BF16)
    v_ref[...] = y[:, D_Q + D_KV:].astype(BF16)


def _qkv_call(x2, mod, g, w_qkv, tables, tiles_per_group):
    n = x2.shape[0]
    tm = TOKEN_TILE
    rope = tables is not None
    in_specs = [
        pl.BlockSpec((tm, D_MODEL), lambda i: (i, 0)),
        pl.BlockSpec((1, N_MOD, SUBLANES, D_MODEL), lambda i: (i // tiles_per_group, 0, 0, 0)),
        _const_spec((4, D_MODEL)),
        _const_spec((D_MODEL, D_Q + 2 * D_KV)),
    ]
    args = [x2, mod, g, w_qkv]
    if rope:
        nt = SEQ // tm
        in_specs += [pl.BlockSpec((tm, LANES), lambda i: (i % nt, 0))] * 3
        args += list(tables)
    return pl.pallas_call(
        functools.partial(_qkv_kernel, rope=rope),
        grid=(n // tm,),
        in_specs=in_specs,
        out_specs=[
            pl.BlockSpec((tm, D_Q), lambda i: (i, 0)),
            pl.BlockSpec((tm, D_K2), lambda i: (i, 0)),
            pl.BlockSpec((tm, D_KV), lambda i: (i, 0)),
        ],
        out_shape=[
            jax.ShapeDtypeStruct((n, D_Q), BF16),
            jax.ShapeDtypeStruct((n, D_K2), BF16),
            jax.ShapeDtypeStruct((n, D_KV), BF16),
        ],
        compiler_params=_params(("parallel",)),
        name="qkv_rope" if rope else "qkv_ctx",
    )(*args)


VT_ROWS = HEAD_DIM + 16


def _attn_kernel(*refs, local, q_per_step):
    if local:
        sink_ref, q_ref, k_ref, v_ref, kc_ref, vc_ref, o_ref, vt_sc, s_sc, p_sc = refs
    else:
        sink_ref, q_ref, kc_ref, vc_ref, o_ref, vt_sc, s_sc, p_sc = refs
    j = pl.program_id(1)
    seq_blocks = SEQ // BLOCK if local else 0
    ctx_blocks = CTX_LEN // BLOCK

    @pl.when((pl.program_id(0) == 0) & (j == 0))
    def _init_static():
        row = lax.broadcasted_iota(jnp.int32, (VT_ROWS - HEAD_DIM, LANES), 0)
        pad = jnp.where(row == 0, 1.0, 0.0).astype(BF16)
        for kh in range(N_KV_HEADS):
            for blk in range(seq_blocks + ctx_blocks):
                vt_sc[kh, blk, HEAD_DIM:VT_ROWS, :] = pad

    def fill(src_v, n_blocks, first_block):
        def body(i, carry):
            r0 = pl.multiple_of(i * BLOCK, BLOCK)
            vt = src_v[0, pl.ds(r0, BLOCK), :].astype(F32).T
            for kh in range(N_KV_HEADS):
                vt_sc[kh, first_block + i, 0:HEAD_DIM, :] = vt[kh * HEAD_DIM:(kh + 1) * HEAD_DIM].astype(BF16)
            return carry
        lax.fori_loop(0, n_blocks, body, 0)

    @pl.when(j == 0)
    def _build():
        if local:
            fill(v_ref, seq_blocks, 0)
        fill(vc_ref, ctx_blocks, seq_blocks)

    nt = (((1,), (1,)), ((), ()))
    n_band = 3
    n_chunks = D_Q // LANES
    lane = lax.broadcasted_iota(jnp.int32, (BLOCK, LANES), 1)
    first_head = lax.broadcasted_iota(jnp.int32, (1, 2 * BLOCK), 1) < BLOCK

    def block_params(qb):
        jq = j * q_per_step + qb
        rows = pl.ds(pl.multiple_of(qb * BLOCK, BLOCK), BLOCK)
        if not local:
            return rows, None, None, None
        blk0 = jnp.clip(jq - 1, 0, seq_blocks - n_band)
        start = pl.multiple_of(blk0 * BLOCK, BLOCK)
        kpos = start + lax.broadcasted_iota(jnp.int32, (n_band * BLOCK, BLOCK), 0)
        qpos = jq * BLOCK + lax.broadcasted_iota(jnp.int32, (n_band * BLOCK, BLOCK), 1)
        bias = jnp.where(jnp.abs(kpos - qpos) <= WINDOW, 0.0, NEG_INF).astype(F32)
        return rows, blk0, start, jnp.concatenate([bias, bias], axis=1)

    def scores(c, params):
        rows, _, start, bias2 = params
        kcols = slice((c // 2) * LANES, (c // 2 + 1) * LANES)
        qc = q_ref[rows, c * LANES:(c + 1) * LANES]
        zero = jnp.zeros_like(qc)
        q2 = jnp.concatenate([jnp.where(lane < HEAD_DIM, qc, zero), jnp.where(lane < HEAD_DIM, zero, qc)], axis=0)
        s_sc[c % 2, 0:CTX_LEN] = lax.dot_general(kc_ref[0, :, kcols], q2, nt, preferred_element_type=F32)
        if local:
            s_sc[c % 2, CTX_LEN:] = lax.dot_general(k_ref[0, pl.ds(start, n_band * BLOCK), kcols], q2, nt,
                                                    preferred_element_type=F32) + bias2

    def softmax(c):
        s = s_sc[c % 2]
        sink_row = jnp.where(first_head, sink_ref[2 * c], sink_ref[2 * c + 1])
        m = jnp.maximum(jnp.max(s, axis=0, keepdims=True), sink_row)
        p_sc[c % 2] = jnp.exp2(s - m).astype(BF16)
        return jnp.exp2(sink_row - m)

    def values(c, params, sink_term):
        rows, blk0, _, _ = params
        kh = c // 2
        vt = [vt_sc[kh, seq_blocks + i] for i in range(ctx_blocks)]
        if local:
            vt_band = vt_sc[kh, pl.ds(blk0, n_band)]
            vt += [vt_band[i] for i in range(n_band)]
        acc = jnp.dot(jnp.concatenate(vt, axis=1), p_sc[c % 2], preferred_element_type=F32)
        out_t = acc[0:HEAD_DIM] / (acc[HEAD_DIM:HEAD_DIM + 1] + sink_term)
        both = jnp.concatenate([out_t[:, :BLOCK], out_t[:, BLOCK:]], axis=0)
        o_ref[rows, c * LANES:(c + 1) * LANES] = both.T.astype(o_ref.dtype)

    def query_block(qb, sink_term0):
        cur = block_params(qb)
        nxt = block_params(jnp.minimum(qb + 1, q_per_step - 1))
        sink_terms = {0: sink_term0}
        for c in range(n_chunks):
            if c + 2 < n_chunks:
                scores(c + 2, cur)
            else:
                scores(c + 2 - n_chunks, nxt)
            sink_terms[c + 1] = softmax((c + 1) % n_chunks)
            values(c, cur, sink_terms[c])
        return sink_terms[n_chunks]

    first = block_params(0)
    scores(0, first)
    scores(1, first)
    lax.fori_loop(0, q_per_step, query_block, softmax(0))


def _attn_scratch(n_blocks, n_keys):
    return [
        pltpu.VMEM((N_KV_HEADS, n_blocks, VT_ROWS, BLOCK), BF16),
        pltpu.VMEM((2, n_keys, 2 * BLOCK), F32),
        pltpu.VMEM((2, n_keys, 2 * BLOCK), BF16),
    ]


def _attn_call(sink2, q, k, v, kc, vc):
    nb = SEQ // BLOCK
    qps = ATTN_Q_PER_STEP
    steps = nb // qps
    return pl.pallas_call(
        functools.partial(_attn_kernel, local=True, q_per_step=qps),
        grid=(BATCH, steps),
        in_specs=[
            pl.BlockSpec(memory_space=pltpu.SMEM),
            pl.BlockSpec((qps * BLOCK, D_Q), lambda b, j: (b * steps + j, 0)),
            pl.BlockSpec((1, SEQ, D_K2), lambda b, j: (b, 0, 0)),
            pl.BlockSpec((1, SEQ, D_KV), lambda b, j: (b, 0, 0)),
            pl.BlockSpec((1, CTX_LEN, D_K2), lambda b, j: (b, 0, 0)),
            pl.BlockSpec((1, CTX_LEN, D_KV), lambda b, j: (b, 0, 0)),
        ],
        out_specs=pl.BlockSpec((qps * BLOCK, D_Q), lambda b, j: (b * steps + j, 0)),
        out_shape=jax.ShapeDtypeStruct((BATCH * SEQ, D_Q), BF16),
        scratch_shapes=_attn_scratch(nb + CTX_LEN // BLOCK, CTX_LEN + 3 * BLOCK),
        compiler_params=_params(("arbitrary", "arbitrary")),
        name="band_attn",
    )(sink2, q, k, v, kc, vc)


def _ctx_attn_call(sink2, qc, kc, vc):
    nb = CTX_LEN // BLOCK
    return pl.pallas_call(
        functools.partial(_attn_kernel, local=False, q_per_step=nb),
        grid=(BATCH, 1),
        in_specs=[
            pl.BlockSpec(memory_space=pltpu.SMEM),
            pl.BlockSpec((CTX_LEN, D_Q), lambda b, j: (b, 0)),
            pl.BlockSpec((1, CTX_LEN, D_K2), lambda b, j: (b, 0, 0)),
            pl.BlockSpec((1, CTX_LEN, D_KV), lambda b, j: (b, 0, 0)),
        ],
        out_specs=pl.BlockSpec((CTX_LEN, D_Q), lambda b, j: (b, 0)),
        out_shape=jax.ShapeDtypeStruct((BATCH * CTX_LEN, D_Q), BF16),
        scratch_shapes=_attn_scratch(nb, CTX_LEN),
        compiler_params=_params(("arbitrary", "arbitrary")),
        name="ctx_attn",
    )(sink2, qc, kc, vc)


def _to_time_major(src_ref, sc_ref):
    nt = src_ref.shape[1]
    for b in range(BATCH):
        for s in range(N_SLAB):
            sc_ref[s, pl.ds(b, nt, stride=SUBLANES), :] = src_ref[b, :, s * LANES:(s + 1) * LANES]
    return jnp.concatenate([sc_ref[s, 0:nt * SUBLANES, :] for s in range(N_SLAB)], axis=1)


def _from_time_major(val, sc_ref, dst_ref):
    nt = dst_ref.shape[1]
    for s in range(N_SLAB):
        sc_ref[s, 0:nt * SUBLANES, :] = val[:, s * LANES:(s + 1) * LANES]
    for b in range(BATCH):
        for s in range(N_SLAB):
            dst_ref[b, :, s * LANES:(s + 1) * LANES] = sc_ref[s, pl.ds(b, nt, stride=SUBLANES), :]


def _post_kernel(*refs, lru):
    if lru:
        x_ref, gate_ref, yf_ref, yb_ref, mod_ref, g_ref, wf_ref, w1_ref, w2_ref, o_ref, x1_sc, h_sc, acc_sc, t_sc = refs
        front = (gate_ref[...].astype(F32) * (yf_ref[...].astype(F32) + yb_ref[...].astype(F32))).astype(BF16)
        x = _to_time_major(x_ref, t_sc)
    else:
        x_ref, a_ref, mod_ref, g_ref, wf_ref, w1_ref, w2_ref, o_ref, x1_sc, h_sc, acc_sc = refs
        front = a_ref[...]
        x = x_ref[...]
    half_rows = x.shape[0] // 2

    def head(r):
        rs = slice(r * half_rows, (r + 1) * half_rows)
        y = jnp.dot(front[rs], wf_ref[...], preferred_element_type=F32)
        x1 = _gated_add(x[rs], mod_ref[0, 2], _rms(y, g_ref[1:2, :]))
        x1_sc[r] = x1
        h_sc[r] = _modulate(_rms(x1, g_ref[2:3, :]), mod_ref[0, 3], mod_ref[0, 4]).astype(BF16)

    def mlp(r):
        acc = jnp.zeros((half_rows, D_MODEL), F32)
        for c in range(D_FF // FF_CHUNK):
            hid = jnp.dot(h_sc[r], w1_ref[:, c * FF_CHUNK:(c + 1) * FF_CHUNK], preferred_element_type=F32)
            hid = jnp.square(jnp.maximum(hid, 0.0)).astype(BF16)
            acc = acc + jnp.dot(hid, w2_ref[c * FF_CHUNK:(c + 1) * FF_CHUNK, :], preferred_element_type=F32)
        acc_sc[r] = acc

    def tail(r):
        return _gated_add(x1_sc[r], mod_ref[0, 5], _rms(acc_sc[r], g_ref[3:4, :]))

    head(0)
    head(1)
    mlp(0)
    out0 = tail(0)
    mlp(1)
    out = jnp.concatenate([out0, tail(1)], axis=0)
    if lru:
        _from_time_major(out, t_sc, o_ref)
    else:
        o_ref[...] = out


def _post_call(x, fronts, mod, g, w_front, w1, w2, layer, tiles_per_group, lru):
    tm = TOKEN_TILE
    layer_spec = lambda shape: pl.BlockSpec((None,) + shape[1:], lambda i: (layer, 0, 0),
                                            pipeline_mode=pl.Buffered(1))
    row = lambda i: (i, 0)
    if lru:
        nt = tm // BATCH
        n = x.shape[0] * x.shape[1]
        x_spec = pl.BlockSpec((BATCH, nt, D_MODEL), lambda i: (0, i, 0))
        scratch = [pltpu.VMEM((N_SLAB, tm, LANES), F32)]
    else:
        n = x.shape[0]
        x_spec = pl.BlockSpec((tm, D_MODEL), row)
        scratch = []
    scratch = [
        pltpu.VMEM((2, tm // 2, D_MODEL), F32),
        pltpu.VMEM((2, tm // 2, D_MODEL), BF16),
        pltpu.VMEM((2, tm // 2, D_MODEL), F32),
    ] + scratch
    in_specs = [x_spec]
    in_specs += [pl.BlockSpec((tm, f.shape[1]), row) for f in fronts]
    in_specs += [
        pl.BlockSpec((1, N_MOD, SUBLANES, D_MODEL), lambda i: (i // tiles_per_group, 0, 0, 0)),
        _const_spec((4, D_MODEL)),
        _const_spec(w_front.shape),
        layer_spec(w1.shape),
        layer_spec(w2.shape),
    ]
    return pl.pallas_call(
        functools.partial(_post_kernel, lru=lru),
        grid=(n // tm,),
        in_specs=in_specs,
        out_specs=x_spec,
        out_shape=jax.ShapeDtypeStruct(x.shape, F32),
        scratch_shapes=scratch,
        compiler_params=_params(("parallel",)),
        name="lru_out_mlp" if lru else "attn_out_mlp",
    )(x, *fronts, mod, g, w_front, w1, w2)


def _lru_in_kernel(x_ref, xp_ref, xn_ref, mod_ref, g_ref, w_ref, cw_ref, cb_ref, gate_ref, u_ref,
                   v_sc, t_sc, tp_sc, tn_sc):
    i = pl.program_id(0)
    n = pl.num_programs(0)
    rows = x_ref.shape[0] * x_ref.shape[1]
    s8 = SUBLANES

    def pre(x):
        return _modulate(_rms(x, g_ref[0:1, :]), mod_ref[0, 0], mod_ref[0, 1]).astype(BF16)

    h = pre(_to_time_major(x_ref, t_sc))
    gate_ref[...] = jax.nn.gelu(jnp.dot(h, w_ref[:, :D_RNN], preferred_element_type=F32)).astype(BF16)
    x_prev = _to_time_major(xp_ref, tp_sc)[SUBLANES * SUBLANES - HALO:]
    x_next = _to_time_major(xn_ref, tn_sc)[:HALO]
    h_ext = jnp.concatenate([pre(x_prev), h, pre(x_next)], axis=0)
    v_sc[...] = jnp.dot(h_ext, w_ref[:, D_RNN:], preferred_element_type=F32)
    v_sc[0:HALO] = v_sc[0:HALO] * (i > 0).astype(F32)
    v_sc[HALO + rows:HALO + rows + s8] = v_sc[HALO + rows:HALO + rows + s8] * (i < n - 1).astype(F32)
    u_ref[...] = (cb_ref[...]
                  + cw_ref[0:1, :] * v_sc[HALO - 2 * s8:HALO - 2 * s8 + rows]
                  + cw_ref[1:2, :] * v_sc[HALO - s8:HALO - s8 + rows]
                  + cw_ref[2:3, :] * v_sc[HALO:HALO + rows]
                  + cw_ref[3:4, :] * v_sc[HALO + s8:HALO + s8 + rows]).astype(u_ref.dtype)


def _lru_in_call(x3, mod, g, w_in, conv_w, conv_b):
    t_total = x3.shape[1]
    n = BATCH * t_total
    tm = TOKEN_TILE
    nt = tm // BATCH
    row = lambda i: (i, 0)
    per_tile = nt // SUBLANES
    last = t_total // SUBLANES - 1
    halo_spec = lambda f: pl.BlockSpec((BATCH, SUBLANES, D_MODEL), f)
    return pl.pallas_call(
        _lru_in_kernel,
        grid=(n // tm,),
        in_specs=[
            pl.BlockSpec((BATCH, nt, D_MODEL), lambda i: (0, i, 0)),
            halo_spec(lambda i: (0, jnp.maximum(i * per_tile - 1, 0), 0)),
            halo_spec(lambda i: (0, jnp.minimum((i + 1) * per_tile, last), 0)),
            _const_spec((1, N_MOD, SUBLANES, D_MODEL)),
            _const_spec((4, D_MODEL)),
            _const_spec((D_MODEL, 2 * D_RNN)),
            _const_spec((CONV_W, D_RNN)),
            _const_spec((1, D_RNN)),
        ],
        out_specs=[pl.BlockSpec((tm, D_RNN), row), pl.BlockSpec((tm, D_RNN), row)],
        out_shape=[jax.ShapeDtypeStruct((n, D_RNN), BF16), jax.ShapeDtypeStruct((n, D_RNN), U_DTYPE)],
        scratch_shapes=[
            pltpu.VMEM((tm + 2 * HALO, D_RNN), F32),
            pltpu.VMEM((N_SLAB, tm, LANES), F32),
            pltpu.VMEM((N_SLAB, SUBLANES * SUBLANES, LANES), F32),
            pltpu.VMEM((N_SLAB, SUBLANES * SUBLANES, LANES), F32),
        ],
        compiler_params=_params(("parallel",)),
        name="lru_in",
    )(x3, x3, x3, mod, g, w_in, conv_w, conv_b)


def _scan_kernel(uf_ref, ub_ref, h0_ref, wa_ref, ba_ref, wi_ref, bi_ref, lam_ref, yf_ref, yb_ref, ht_ref,
                 a_sc, bx_sc, h_sc):
    i = pl.program_id(0)
    n = pl.num_programs(0)
    rows = uf_ref.shape[0]
    nt = rows // SUBLANES
    s8 = SUBLANES

    @pl.when(i == 0)
    def _():
        h_sc[...] = h0_ref[...]

    for d, u_ref in enumerate((uf_ref, ub_ref)):
        for c in range(N_LRU_BLOCKS):
            cs = slice(c * LRU_BLOCK_W, (c + 1) * LRU_BLOCK_W)
            u16 = u_ref[:, cs].astype(BF16)
            u = u_ref[:, cs].astype(F32)
            ta = jnp.tanh(jnp.dot(u16, wa_ref[d, c], preferred_element_type=F32) + 0.5 * ba_ref[d, :, cs])
            ti = jnp.tanh(jnp.dot(u16, wi_ref[d, c], preferred_element_type=F32) + 0.5 * bi_ref[d, :, cs])
            neg_lam = -lam_ref[d, :, cs]
            softplus = jnp.maximum(neg_lam, 0.0) + jnp.log1p(jnp.exp(-jnp.abs(neg_lam)))
            k = (-0.5 * LRU_C * LOG2E) * softplus
            a = jnp.exp2(k * ta + k)
            w = 1.0 - a * a
            root = w * lax.rsqrt(jnp.maximum(w, 1e-30))
            a_sc[d, :, cs] = a
            bx_sc[d, :, cs] = root * (ti * u + u)

    def step(t, carry):
        hf, hb = carry
        rf = pl.multiple_of(t * 2 * s8, 2 * s8)
        rb = pl.multiple_of((nt - 2 - 2 * t) * s8, 2 * s8)
        hf1 = a_sc[0, pl.ds(rf, s8), :] * hf + bx_sc[0, pl.ds(rf, s8), :]
        hf2 = a_sc[0, pl.ds(rf + s8, s8), :] * hf1 + bx_sc[0, pl.ds(rf + s8, s8), :]
        yf_ref[pl.ds(rf, 2 * s8), :] = jnp.concatenate([hf1, hf2], axis=0).astype(yf_ref.dtype)
        hb1 = a_sc[1, pl.ds(rb + s8, s8), :] * hb + bx_sc[1, pl.ds(rb + s8, s8), :]
        hb2 = a_sc[1, pl.ds(rb, s8), :] * hb1 + bx_sc[1, pl.ds(rb, s8), :]
        yb_ref[pl.ds(rb, 2 * s8), :] = jnp.concatenate([hb2, hb1], axis=0).astype(yb_ref.dtype)
        return hf2, hb2

    hf, hb = lax.fori_loop(0, nt // 2, step, (h_sc[0], h_sc[1]), unroll=2)
    h_sc[0] = hf
    h_sc[1] = hb

    @pl.when(i == n - 1)
    def _():
        ht_ref[...] = h_sc[...]


def _scan_call(u2, h0, w_a, b_a, w_i, b_i, lam):
    rows_total = u2.shape[0]
    rows = SCAN_T * SUBLANES
    n = rows_total // rows
    w = D_RNN
    fwd = lambda i: (i, 0)
    bwd = lambda i: (n - 1 - i, 0)
    return pl.pallas_call(
        _scan_kernel,
        grid=(n,),
        in_specs=[
            pl.BlockSpec((rows, w), fwd),
            pl.BlockSpec((rows, w), bwd),
            _const_spec((2, SUBLANES, w)),
            _const_spec((2, N_LRU_BLOCKS, LRU_BLOCK_W, LRU_BLOCK_W)),
            _const_spec((2, 1, w)),
            _const_spec((2, N_LRU_BLOCKS, LRU_BLOCK_W, LRU_BLOCK_W)),
            _const_spec((2, 1, w)),
            _const_spec((2, 1, w)),
        ],
        out_specs=[
            pl.BlockSpec((rows, w), fwd),
            pl.BlockSpec((rows, w), bwd),
            pl.BlockSpec((2, SUBLANES, w), lambda i: (0, 0, 0)),
        ],
        out_shape=[
            jax.ShapeDtypeStruct((rows_total, w), Y_DTYPE),
            jax.ShapeDtypeStruct((rows_total, w), Y_DTYPE),
            jax.ShapeDtypeStruct((2, SUBLANES, w), F32),
        ],
        scratch_shapes=[
            pltpu.VMEM((2, rows, w), F32),
            pltpu.VMEM((2, rows, w), F32),
            pltpu.VMEM((2, SUBLANES, w), F32),
        ],
        compiler_params=_params(("arbitrary",)),
        name="lru_scan",
    )(u2, u2, h0, w_a, b_a, w_i, b_i, lam)


def _rope_tables():
    t = np.arange(SEQ)
    row = (t // GRID_W).astype(np.float64)
    col = (t % GRID_W).astype(np.float64)
    half = HEAD_DIM // 2
    inv = ROPE_BASE ** (-np.arange(0, half, 2, dtype=np.float64) / half)
    ang_r = row[:, None] * inv[None, :]
    ang_c = col[:, None] * inv[None, :]
    ang = np.concatenate([ang_r, ang_r, ang_c, ang_c], axis=-1)
    ang = np.tile(ang, (1, LANES // HEAD_DIM))
    low = (np.arange(LANES) % 32) < 16
    sin = np.sin(ang)
    tables = (np.cos(ang), np.where(low, -sin, 0.0), np.where(low, 0.0, sin))
    return tuple(jnp.asarray(a, dtype=F32) for a in tables)


def kernel(x, c, ctx, c_ctx, ada_w, ada_b, norm_g, mlp_w1, mlp_w2, attn_w_qkv, attn_w_o, attn_sink,
           lru_w_in, lru_conv_w, lru_conv_b, lru_w_a, lru_b_a, lru_w_i, lru_b_i, lru_lam, lru_w_out):
    n_lat = BATCH * SEQ
    n_ctx = BATCH * CTX_LEN

    c16 = jnp.zeros((16, D_MODEL), F32).at[:BATCH].set(c).at[BATCH].set(c_ctx)
    mods = _mod_call(c16, ada_w, ada_b).reshape(2, 16, N_MOD, D_MODEL)

    def slab_bmajor(m):
        return jnp.broadcast_to(m[:, :, None, :], (BATCH, N_MOD, SUBLANES, D_MODEL))

    def slab_ctx(m):
        return jnp.broadcast_to(m[None, :, None, :], (1, N_MOD, SUBLANES, D_MODEL))

    mod_x0 = slab_bmajor(mods[0, :BATCH])
    mod_c0 = slab_ctx(mods[0, BATCH])
    w_qkv = attn_w_qkv[0]
    w_qkv = jnp.concatenate([w_qkv[:, :D_Q] * (HEAD_DIM ** -0.5 * LOG2E), w_qkv[:, D_Q:]], axis=1).astype(BF16)
    sink2 = attn_sink[0] * LOG2E
    w_o = attn_w_o[0].astype(BF16)
    w1_all, w2_all = mlp_w1.astype(BF16), mlp_w2.astype(BF16)
    g0 = norm_g[0]
    tiles_per_batch = SEQ // TOKEN_TILE

    x2 = x.reshape(n_lat, D_MODEL)
    c2 = ctx.reshape(n_ctx, D_MODEL)
    q, k, v = _qkv_call(x2, mod_x0, g0, w_qkv, _rope_tables(), tiles_per_batch)
    qc, kc, vc = _qkv_call(c2, mod_c0, g0, w_qkv, None, n_ctx // TOKEN_TILE)
    kc3 = kc.reshape(BATCH, CTX_LEN, D_K2)
    vc3 = vc.reshape(BATCH, CTX_LEN, D_KV)
    att = _attn_call(sink2, q, k.reshape(BATCH, SEQ, D_K2), v.reshape(BATCH, SEQ, D_KV), kc3, vc3)
    att_c = _ctx_attn_call(sink2, qc, kc3, vc3)
    x2 = _post_call(x2, [att], mod_x0, g0, w_o, w1_all, w2_all, 0, tiles_per_batch, lru=False)
    c2 = _post_call(c2, [att_c], mod_c0, g0, w_o, w1_all, w2_all, 0, n_ctx // TOKEN_TILE, lru=False)

    x3 = x2.reshape(BATCH, SEQ, D_MODEL)
    c3 = c2.reshape(BATCH, CTX_LEN, D_MODEL)
    mod_x1 = mods[1, :BATCH].transpose(1, 0, 2)[None]
    mod_c1 = slab_ctx(mods[1, BATCH])
    g1 = norm_g[1]
    w_in = lru_w_in[0].astype(BF16)
    conv_w = 0.5 * lru_conv_w[0]
    conv_b = 0.5 * lru_conv_b[0].reshape(1, D_RNN)
    w_a, w_i = lru_w_a[0].astype(BF16), lru_w_i[0].astype(BF16)
    b_a, b_i = lru_b_a[0].reshape(2, 1, D_RNN), lru_b_i[0].reshape(2, 1, D_RNN)
    lam = lru_lam[0].reshape(2, 1, D_RNN)
    scan = functools.partial(_scan_call, w_a=w_a, b_a=b_a, w_i=w_i, b_i=b_i, lam=lam)

    _, u_c = _lru_in_call(c3, mod_c1, g1, w_in, conv_w, conv_b)
    _, _, h_ctx = scan(u_c, jnp.zeros((2, SUBLANES, D_RNN), F32))
    gate_x, u_x = _lru_in_call(x3, mod_x1, g1, w_in, conv_w, conv_b)
    yf, yb, _ = scan(u_x, h_ctx)
    return _post_call(x3, [gate_x, yf, yb], mod_x1, g1, lru_w_out[0].astype(BF16),
                      w1_all, w2_all, 1, n_lat // TOKEN_TILE, lru=True)
```

```python
import functools

import jax
import jax.numpy as jnp
import numpy as np
from jax import lax
from jax.experimental import pallas as pl
from jax.experimental.pallas import tpu as pltpu

D_MODEL = 1024
BATCH = 8
SEQ = 2048
GRID_W = 64
CTX_LEN = 256
HEAD_DIM = 64
N_HEADS = 16
N_KV_HEADS = 4
GQA_GROUP = N_HEADS // N_KV_HEADS
WINDOW = 128
BLOCK = 128
ROPE_BASE = 10000.0
D_RNN = 1280
LRU_BLOCK_W = 256
N_LRU_BLOCKS = D_RNN // LRU_BLOCK_W
CONV_W = 4
LRU_C = 8.0
D_FF = 4 * D_MODEL
N_MOD = 6
EPS = 1e-6
NEG_INF = -1e30

D_Q = N_HEADS * HEAD_DIM
D_KV = N_KV_HEADS * HEAD_DIM
D_K2 = 2 * D_KV
LANES = 128
SUBLANES = 8
N_SLAB = D_MODEL // LANES
TOKEN_TILE = 512
FF_CHUNK = 1024
ATTN_Q_PER_STEP = 16
SCAN_T = 128
HALO = 16
U_DTYPE = jnp.bfloat16
Y_DTYPE = jnp.bfloat16
LOG2E = 1.4426950408889634
VMEM_LIMIT = 60 * 1024 * 1024

F32 = jnp.float32
BF16 = jnp.bfloat16


def _rms(x, g):
    ms = jnp.mean(x * x, axis=-1, keepdims=True)
    return x * lax.rsqrt(ms + EPS) * g


def _slab(x):
    return x.reshape(x.shape[0] // SUBLANES, SUBLANES, x.shape[1])


def _modulate(h, shift8, scale8):
    out = _slab(h) * (1.0 + scale8)[None] + shift8[None]
    return out.reshape(h.shape)


def _gated_add(x, gate8, y):
    out = _slab(x) + gate8[None] * _slab(y)
    return out.reshape(x.shape)


def _const_spec(shape):
    n = len(shape)
    return pl.BlockSpec(shape, lambda *_: (0,) * n, pipeline_mode=pl.Buffered(1))


def _params(sem):
    return pltpu.CompilerParams(dimension_semantics=sem, vmem_limit_bytes=VMEM_LIMIT)


def _mod_kernel(c_ref, w_ref, b_ref, o_ref):
    s = jax.nn.silu(c_ref[...]).astype(BF16)
    o_ref[0] = jnp.dot(s, w_ref[0].astype(BF16), preferred_element_type=F32) + b_ref[0]


def _mod_call(c16, ada_w, ada_b):
    depth = ada_w.shape[0]
    nt = 1024
    return pl.pallas_call(
        _mod_kernel,
        grid=(depth, N_MOD * D_MODEL // nt),
        in_specs=[
            pl.BlockSpec((16, D_MODEL), lambda l, j: (0, 0)),
            pl.BlockSpec((1, D_MODEL, nt), lambda l, j: (l, 0, j)),
            pl.BlockSpec((1, 1, nt), lambda l, j: (l, 0, j)),
        ],
        out_specs=pl.BlockSpec((1, 16, nt), lambda l, j: (l, 0, j)),
        out_shape=jax.ShapeDtypeStruct((depth, 16, N_MOD * D_MODEL), F32),
        compiler_params=_params(("arbitrary", "arbitrary")),
        name="adaln_mod",
    )(c16, ada_w, ada_b.reshape(depth, 1, N_MOD * D_MODEL))


def _qkv_kernel(*refs, rope):
    if rope:
        x_ref, mod_ref, g_ref, w_ref, cos_ref, sa_ref, sb_ref, q_ref, k_ref, v_ref = refs
    else:
        x_ref, mod_ref, g_ref, w_ref, q_ref, k_ref, v_ref = refs
    h = _modulate(_rms(x_ref[...], g_ref[0:1, :]), mod_ref[0, 0], mod_ref[0, 1])
    y = jnp.dot(h.astype(BF16), w_ref[...], preferred_element_type=F32)
    if rope:
        cos, sa, sb = cos_ref[...], sa_ref[...], sb_ref[...]
    low = lax.broadcasted_iota(jnp.int32, (x_ref.shape[0], LANES), 1) < HEAD_DIM
    for c in range((D_Q + D_KV) // LANES):
        yc = y[:, c * LANES:(c + 1) * LANES]
        if rope:
            yc = yc * cos + pltpu.roll(yc, LANES - 16, 1) * sa + pltpu.roll(yc, 16, 1) * sb
        if c < D_Q // LANES:
            q_ref[:, c * LANES:(c + 1) * LANES] = yc.astype(BF16)
        else:
            c2 = 2 * (c - D_Q // LANES)
            swapped = pltpu.roll(yc, HEAD_DIM, 1)
            k_ref[:, c2 * LANES:(c2 + 1) * LANES] = jnp.where(low, yc, swapped).astype(BF16)
            k_ref[:, (c2 + 1) * LANES:(c2 + 2) * LANES] = jnp.where(low, swapped, yc).astype(BF16)
    v_ref[...] = y[:, D_Q + D_KV:].astype(BF16)


def _qkv_call(x2, mod, g, w_qkv, tables, tiles_per_group):
    n = x2.shape[0]
    tm = TOKEN_TILE
    rope = tables is not None
    in_specs = [
        pl.BlockSpec((tm, D_MODEL), lambda i: (i, 0)),
        pl.BlockSpec((1, N_MOD, SUBLANES, D_MODEL), lambda i: (i // tiles_per_group, 0, 0, 0)),
        _const_spec((4, D_MODEL)),
        _const_spec((D_MODEL, D_Q + 2 * D_KV)),
    ]
    args = [x2, mod, g, w_qkv]
    if rope:
        nt = SEQ // tm
        in_specs += [pl.BlockSpec((tm, LANES), lambda i: (i % nt, 0))] * 3
        args += list(tables)
    return pl.pallas_call(
        functools.partial(_qkv_kernel, rope=rope),
        grid=(n // tm,),
        in_specs=in_specs,
        out_specs=[
            pl.BlockSpec((tm, D_Q), lambda i: (i, 0)),
            pl.BlockSpec((tm, D_K2), lambda i: (i, 0)),
            pl.BlockSpec((tm, D_KV), lambda i: (i, 0)),
        ],
        out_shape=[
            jax.ShapeDtypeStruct((n, D_Q), BF16),
            jax.ShapeDtypeStruct((n, D_K2), BF16),
            jax.ShapeDtypeStruct((n, D_KV), BF16),
        ],
        compiler_params=_params(("parallel",)),
        name="qkv_rope" if rope else "qkv_ctx",
    )(*args)


VT_ROWS = HEAD_DIM + 16


def _attn_kernel(*refs, local, q_per_step):
    if local:
        sink_ref, q_ref, k_ref, v_ref, kc_ref, vc_ref, o_ref, vt_sc, s_sc, p_sc = refs
    else:
        sink_ref, q_ref, kc_ref, vc_ref, o_ref, vt_sc, s_sc, p_sc = refs
    j = pl.program_id(1)
    seq_blocks = SEQ // BLOCK if local else 0
    ctx_blocks = CTX_LEN // BLOCK

    @pl.when((pl.program_id(0) == 0) & (j == 0))
    def _init_static():
        row = lax.broadcasted_iota(jnp.int32, (VT_ROWS - HEAD_DIM, LANES), 0)
        pad = jnp.where(row == 0, 1.0, 0.0).astype(BF16)
        for kh in range(N_KV_HEADS):
            for blk in range(seq_blocks + ctx_blocks):
                vt_sc[kh, blk, HEAD_DIM:VT_ROWS, :] = pad

    def fill(src_v, n_blocks, first_block):
        def body(i, carry):
            r0 = pl.multiple_of(i * BLOCK, BLOCK)
            vt = src_v[0, pl.ds(r0, BLOCK), :].astype(F32).T
            for kh in range(N_KV_HEADS):
                vt_sc[kh, first_block + i, 0:HEAD_DIM, :] = vt[kh * HEAD_DIM:(kh + 1) * HEAD_DIM].astype(BF16)
            return carry
        lax.fori_loop(0, n_blocks, body, 0)

    @pl.when(j == 0)
    def _build():
        if local:
            fill(v_ref, seq_blocks, 0)
        fill(vc_ref, ctx_blocks, seq_blocks)

    nt = (((1,), (1,)), ((), ()))
    n_band = 3
    n_chunks = D_Q // LANES
    lane = lax.broadcasted_iota(jnp.int32, (BLOCK, LANES), 1)
    first_head = lax.broadcasted_iota(jnp.int32, (1, 2 * BLOCK), 1) < BLOCK

    def block_params(qb):
        jq = j * q_per_step + qb
        rows = pl.ds(pl.multiple_of(qb * BLOCK, BLOCK), BLOCK)
        if not local:
            return rows, None, None, None
        blk0 = jnp.clip(jq - 1, 0, seq_blocks - n_band)
        start = pl.multiple_of(blk0 * BLOCK, BLOCK)
        kpos = start + lax.broadcasted_iota(jnp.int32, (n_band * BLOCK, BLOCK), 0)
        qpos = jq * BLOCK + lax.broadcasted_iota(jnp.int32, (n_band * BLOCK, BLOCK), 1)
        bias = jnp.where(jnp.abs(kpos - qpos) <= WINDOW, 0.0, NEG_INF).astype(F32)
        return rows, blk0, start, jnp.concatenate([bias, bias], axis=1)

    def scores(c, params):
        rows, _, start, bias2 = params
        kcols = slice((c // 2) * LANES, (c // 2 + 1) * LANES)
        qc = q_ref[rows, c * LANES:(c + 1) * LANES]
        zero = jnp.zeros_like(qc)
        q2 = jnp.concatenate([jnp.where(lane < HEAD_DIM, qc, zero), jnp.where(lane < HEAD_DIM, zero, qc)], axis=0)
        s_sc[c % 2, 0:CTX_LEN] = lax.dot_general(kc_ref[0, :, kcols], q2, nt, preferred_element_type=F32)
        if local:
            s_sc[c % 2, CTX_LEN:] = lax.dot_general(k_ref[0, pl.ds(start, n_band * BLOCK), kcols], q2, nt,
                                                    preferred_element_type=F32) + bias2

    def softmax(c):
        s = s_sc[c % 2]
        sink_row = jnp.where(first_head, sink_ref[2 * c], sink_ref[2 * c + 1])
        m = jnp.maximum(jnp.max(s, axis=0, keepdims=True), sink_row)
        p_sc[c % 2] = jnp.exp2(s - m).astype(BF16)
        return jnp.exp2(sink_row - m)

    def values(c, params, sink_term):
        rows, blk0, _, _ = params
        kh = c // 2
        vt = [vt_sc[kh, seq_blocks + i] for i in range(ctx_blocks)]
        if local:
            vt_band = vt_sc[kh, pl.ds(blk0, n_band)]
            vt += [vt_band[i] for i in range(n_band)]
        acc = jnp.dot(jnp.concatenate(vt, axis=1), p_sc[c % 2], preferred_element_type=F32)
        out_t = acc[0:HEAD_DIM] / (acc[HEAD_DIM:HEAD_DIM + 1] + sink_term)
        both = jnp.concatenate([out_t[:, :BLOCK], out_t[:, BLOCK:]], axis=0)
        o_ref[rows, c * LANES:(c + 1) * LANES] = both.T.astype(o_ref.dtype)

    def query_block(qb, sink_term0):
        cur = block_params(qb)
        nxt = block_params(jnp.minimum(qb + 1, q_per_step - 1))
        sink_terms = {0: sink_term0}
        for c in range(n_chunks):
            if c + 2 < n_chunks:
                scores(c + 2, cur)
            else:
                scores(c + 2 - n_chunks, nxt)
            sink_terms[c + 1] = softmax((c + 1) % n_chunks)
            values(c, cur, sink_terms[c])
        return sink_terms[n_chunks]

    first = block_params(0)
    scores(0, first)
    scores(1, first)
    lax.fori_loop(0, q_per_step, query_block, softmax(0))


def _attn_scratch(n_blocks, n_keys):
    return [
        pltpu.VMEM((N_KV_HEADS, n_blocks, VT_ROWS, BLOCK), BF16),
        pltpu.VMEM((2, n_keys, 2 * BLOCK), F32),
        pltpu.VMEM((2, n_keys, 2 * BLOCK), BF16),
    ]


def _attn_call(sink2, q, k, v, kc, vc):
    nb = SEQ // BLOCK
    qps = ATTN_Q_PER_STEP
    steps = nb // qps
    return pl.pallas_call(
        functools.partial(_attn_kernel, local=True, q_per_step=qps),
        grid=(BATCH, steps),
        in_specs=[
            pl.BlockSpec(memory_space=pltpu.SMEM),
            pl.BlockSpec((qps * BLOCK, D_Q), lambda b, j: (b * steps + j, 0)),
            pl.BlockSpec((1, SEQ, D_K2), lambda b, j: (b, 0, 0)),
            pl.BlockSpec((1, SEQ, D_KV), lambda b, j: (b, 0, 0)),
            pl.BlockSpec((1, CTX_LEN, D_K2), lambda b, j: (b, 0, 0)),
            pl.BlockSpec((1, CTX_LEN, D_KV), lambda b, j: (b, 0, 0)),
        ],
        out_specs=pl.BlockSpec((qps * BLOCK, D_Q), lambda b, j: (b * steps + j, 0)),
        out_shape=jax.ShapeDtypeStruct((BATCH * SEQ, D_Q), BF16),
        scratch_shapes=_attn_scratch(nb + CTX_LEN // BLOCK, CTX_LEN + 3 * BLOCK),
        compiler_params=_params(("arbitrary", "arbitrary")),
        name="band_attn",
    )(sink2, q, k, v, kc, vc)


def _ctx_attn_call(sink2, qc, kc, vc):
    nb = CTX_LEN // BLOCK
    return pl.pallas_call(
        functools.partial(_attn_kernel, local=False, q_per_step=nb),
        grid=(BATCH, 1),
        in_specs=[
            pl.BlockSpec(memory_space=pltpu.SMEM),
            pl.BlockSpec((CTX_LEN, D_Q), lambda b, j: (b, 0)),
            pl.BlockSpec((1, CTX_LEN, D_K2), lambda b, j: (b, 0, 0)),
            pl.BlockSpec((1, CTX_LEN, D_KV), lambda b, j: (b, 0, 0)),
        ],
        out_specs=pl.BlockSpec((CTX_LEN, D_Q), lambda b, j: (b, 0)),
        out_shape=jax.ShapeDtypeStruct((BATCH * CTX_LEN, D_Q), BF16),
        scratch_shapes=_attn_scratch(nb, CTX_LEN),
        compiler_params=_params(("arbitrary", "arbitrary")),
        name="ctx_attn",
    )(sink2, qc, kc, vc)


def _to_time_major(src_ref, sc_ref):
    nt = src_ref.shape[1]
    for b in range(BATCH):
        for s in range(N_SLAB):
            sc_ref[s, pl.ds(b, nt, stride=SUBLANES), :] = src_ref[b, :, s * LANES:(s + 1) * LANES]
    return jnp.concatenate([sc_ref[s, 0:nt * SUBLANES, :] for s in range(N_SLAB)], axis=1)


def _from_time_major(val, sc_ref, dst_ref):
    nt = dst_ref.shape[1]
    for s in range(N_SLAB):
        sc_ref[s, 0:nt * SUBLANES, :] = val[:, s * LANES:(s + 1) * LANES]
    for b in range(BATCH):
        for s in range(N_SLAB):
            dst_ref[b, :, s * LANES:(s + 1) * LANES] = sc_ref[s, pl.ds(b, nt, stride=SUBLANES), :]


def _post_kernel(*refs, lru):
    if lru:
        x_ref, gate_ref, yf_ref, yb_ref, mod_ref, g_ref, wf_ref, w1_ref, w2_ref, o_ref, x1_sc, h_sc, acc_sc, t_sc = refs
        front = (gate_ref[...].astype(F32) * (yf_ref[...].astype(F32) + yb_ref[...].astype(F32))).astype(BF16)
        x = _to_time_major(x_ref, t_sc)
    else:
        x_ref, a_ref, mod_ref, g_ref, wf_ref, w1_ref, w2_ref, o_ref, x1_sc, h_sc, acc_sc = refs
        front = a_ref[...]
        x = x_ref[...]
    half_rows = x.shape[0] // 2

    def head(r):
        rs = slice(r * half_rows, (r + 1) * half_rows)
        y = jnp.dot(front[rs], wf_ref[...], preferred_element_type=F32)
        x1 = _gated_add(x[rs], mod_ref[0, 2], _rms(y, g_ref[1:2, :]))
        x1_sc[r] = x1
        h_sc[r] = _modulate(_rms(x1, g_ref[2:3, :]), mod_ref[0, 3], mod_ref[0, 4]).astype(BF16)

    def mlp(r):
        acc = jnp.zeros((half_rows, D_MODEL), F32)
        for c in range(D_FF // FF_CHUNK):
            hid = jnp.dot(h_sc[r], w1_ref[:, c * FF_CHUNK:(c + 1) * FF_CHUNK], preferred_element_type=F32)
            hid = jnp.square(jnp.maximum(hid, 0.0)).astype(BF16)
            acc = acc + jnp.dot(hid, w2_ref[c * FF_CHUNK:(c + 1) * FF_CHUNK, :], preferred_element_type=F32)
        acc_sc[r] = acc

    def tail(r):
        return _gated_add(x1_sc[r], mod_ref[0, 5], _rms(acc_sc[r], g_ref[3:4, :]))

    head(0)
    head(1)
    mlp(0)
    out0 = tail(0)
    mlp(1)
    out = jnp.concatenate([out0, tail(1)], axis=0)
    if lru:
        _from_time_major(out, t_sc, o_ref)
    else:
        o_ref[...] = out


def _post_call(x, fronts, mod, g, w_front, w1, w2, layer, tiles_per_group, lru):
    tm = TOKEN_TILE
    layer_spec = lambda shape: pl.BlockSpec((None,) + shape[1:], lambda i: (layer, 0, 0),
                                            pipeline_mode=pl.Buffered(1))
    row = lambda i: (i, 0)
    if lru:
        nt = tm // BATCH
        n = x.shape[0] * x.shape[1]
        x_spec = pl.BlockSpec((BATCH, nt, D_MODEL), lambda i: (0, i, 0))
        scratch = [pltpu.VMEM((N_SLAB, tm, LANES), F32)]
    else:
        n = x.shape[0]
        x_spec = pl.BlockSpec((tm, D_MODEL), row)
        scratch = []
    scratch = [
        pltpu.VMEM((2, tm // 2, D_MODEL), F32),
        pltpu.VMEM((2, tm // 2, D_MODEL), BF16),
        pltpu.VMEM((2, tm // 2, D_MODEL), F32),
    ] + scratch
    in_specs = [x_spec]
    in_specs += [pl.BlockSpec((tm, f.shape[1]), row) for f in fronts]
    in_specs += [
        pl.BlockSpec((1, N_MOD, SUBLANES, D_MODEL), lambda i: (i // tiles_per_group, 0, 0, 0)),
        _const_spec((4, D_MODEL)),
        _const_spec(w_front.shape),
        layer_spec(w1.shape),
        layer_spec(w2.shape),
    ]
    return pl.pallas_call(
        functools.partial(_post_kernel, lru=lru),
        grid=(n // tm,),
        in_specs=in_specs,
        out_specs=x_spec,
        out_shape=jax.ShapeDtypeStruct(x.shape, F32),
        scratch_shapes=scratch,
        compiler_params=_params(("parallel",)),
        name="lru_out_mlp" if lru else "attn_out_mlp",
    )(x, *fronts, mod, g, w_front, w1, w2)


def _lru_in_kernel(*refs, need_gate):
    if need_gate:
        x_ref, xp_ref, xn_ref, mod_ref, g_ref, w_ref, cw_ref, cb_ref, gate_ref, u_ref, v_sc, t_sc, tp_sc, tn_sc = refs
    else:
        x_ref, xp_ref, xn_ref, mod_ref, g_ref, w_ref, cw_ref, cb_ref, u_ref, v_sc, t_sc, tp_sc, tn_sc = refs
    i = pl.program_id(0)
    n = pl.num_programs(0)
    rows = x_ref.shape[0] * x_ref.shape[1]
    s8 = SUBLANES

    def pre(x):
        return _modulate(_rms(x, g_ref[0:1, :]), mod_ref[0, 0], mod_ref[0, 1]).astype(BF16)

    h = pre(_to_time_major(x_ref, t_sc))
    if need_gate:
        gate_ref[...] = jax.nn.gelu(jnp.dot(h, w_ref[:, :D_RNN], preferred_element_type=F32)).astype(BF16)
    x_prev = _to_time_major(xp_ref, tp_sc)[SUBLANES * SUBLANES - HALO:]
    x_next = _to_time_major(xn_ref, tn_sc)[:HALO]
    h_ext = jnp.concatenate([pre(x_prev), h, pre(x_next)], axis=0)
    v_sc[...] = jnp.dot(h_ext, w_ref[:, D_RNN:], preferred_element_type=F32)
    v_sc[0:HALO] = v_sc[0:HALO] * (i > 0).astype(F32)
    v_sc[HALO + rows:HALO + rows + s8] = v_sc[HALO + rows:HALO + rows + s8] * (i < n - 1).astype(F32)
    u_ref[...] = (cb_ref[...]
                  + cw_ref[0:1, :] * v_sc[HALO - 2 * s8:HALO - 2 * s8 + rows]
                  + cw_ref[1:2, :] * v_sc[HALO - s8:HALO - s8 + rows]
                  + cw_ref[2:3, :] * v_sc[HALO:HALO + rows]
                  + cw_ref[3:4, :] * v_sc[HALO + s8:HALO + s8 + rows]).astype(u_ref.dtype)


def _lru_in_call(x3, mod, g, w_in, conv_w, conv_b, need_gate):
    t_total = x3.shape[1]
    n = BATCH * t_total
    tm = TOKEN_TILE
    nt = tm // BATCH
    row = lambda i: (i, 0)
    per_tile = nt // SUBLANES
    last = t_total // SUBLANES - 1
    halo_spec = lambda f: pl.BlockSpec((BATCH, SUBLANES, D_MODEL), f)
    out_specs = [pl.BlockSpec((tm, D_RNN), row), pl.BlockSpec((tm, D_RNN), row)]
    out_shape = [jax.ShapeDtypeStruct((n, D_RNN), BF16), jax.ShapeDtypeStruct((n, D_RNN), U_DTYPE)]
    if not need_gate:
        out_specs, out_shape = out_specs[1:], out_shape[1:]
    return pl.pallas_call(
        functools.partial(_lru_in_kernel, need_gate=need_gate),
        grid=(n // tm,),
        in_specs=[
            pl.BlockSpec((BATCH, nt, D_MODEL), lambda i: (0, i, 0)),
            halo_spec(lambda i: (0, jnp.maximum(i * per_tile - 1, 0), 0)),
            halo_spec(lambda i: (0, jnp.minimum((i + 1) * per_tile, last), 0)),
            _const_spec((1, N_MOD, SUBLANES, D_MODEL)),
            _const_spec((4, D_MODEL)),
            _const_spec((D_MODEL, 2 * D_RNN)),
            _const_spec((CONV_W, D_RNN)),
            _const_spec((1, D_RNN)),
        ],
        out_specs=out_specs,
        out_shape=out_shape,
        scratch_shapes=[
            pltpu.VMEM((tm + 2 * HALO, D_RNN), F32),
            pltpu.VMEM((N_SLAB, tm, LANES), F32),
            pltpu.VMEM((N_SLAB, SUBLANES * SUBLANES, LANES), F32),
            pltpu.VMEM((N_SLAB, SUBLANES * SUBLANES, LANES), F32),
        ],
        compiler_params=_params(("parallel",)),
        name="lru_in",
    )(x3, x3, x3, mod, g, w_in, conv_w, conv_b)


def _scan_kernel(uf_ref, ub_ref, h0_ref, wa_ref, ba_ref, wi_ref, bi_ref, lam_ref, yf_ref, yb_ref, ht_ref,
                 a_sc, bx_sc, h_sc):
    i = pl.program_id(0)
    n = pl.num_programs(0)
    rows = uf_ref.shape[0]
    nt = rows // SUBLANES
    s8 = SUBLANES

    @pl.when(i == 0)
    def _():
        h_sc[...] = h0_ref[...]

    for d, u_ref in enumerate((uf_ref, ub_ref)):
        for c in range(N_LRU_BLOCKS):
            cs = slice(c * LRU_BLOCK_W, (c + 1) * LRU_BLOCK_W)
            u16 = u_ref[:, cs].astype(BF16)
            u = u_ref[:, cs].astype(F32)
            ta = jnp.tanh(jnp.dot(u16, wa_ref[d, c], preferred_element_type=F32) + 0.5 * ba_ref[d, :, cs])
            ti = jnp.tanh(jnp.dot(u16, wi_ref[d, c], preferred_element_type=F32) + 0.5 * bi_ref[d, :, cs])
            neg_lam = -lam_ref[d, :, cs]
            softplus = jnp.maximum(neg_lam, 0.0) + jnp.log1p(jnp.exp(-jnp.abs(neg_lam)))
            k = (-0.5 * LRU_C * LOG2E) * softplus
            a = jnp.exp2(k * ta + k)
            w = 1.0 - a * a
            root = w * lax.rsqrt(jnp.maximum(w, 1e-30))
            a_sc[d, :, cs] = a
            bx_sc[d, :, cs] = root * (ti * u + u)

    def step(t, carry):
        hf, hb = carry
        rf = pl.multiple_of(t * 2 * s8, 2 * s8)
        rb = pl.multiple_of((nt - 2 - 2 * t) * s8, 2 * s8)
        hf1 = a_sc[0, pl.ds(rf, s8), :] * hf + bx_sc[0, pl.ds(rf, s8), :]
        hf2 = a_sc[0, pl.ds(rf + s8, s8), :] * hf1 + bx_sc[0, pl.ds(rf + s8, s8), :]
        yf_ref[pl.ds(rf, 2 * s8), :] = jnp.concatenate([hf1, hf2], axis=0).astype(yf_ref.dtype)
        hb1 = a_sc[1, pl.ds(rb + s8, s8), :] * hb + bx_sc[1, pl.ds(rb + s8, s8), :]
        hb2 = a_sc[1, pl.ds(rb, s8), :] * hb1 + bx_sc[1, pl.ds(rb, s8), :]
        yb_ref[pl.ds(rb, 2 * s8), :] = jnp.concatenate([hb2, hb1], axis=0).astype(yb_ref.dtype)
        return hf2, hb2

    hf, hb = lax.fori_loop(0, nt // 2, step, (h_sc[0], h_sc[1]), unroll=2)
    h_sc[0] = hf
    h_sc[1] = hb

    @pl.when(i == n - 1)
    def _():
        ht_ref[...] = h_sc[...]


def _scan_call(u2, h0, w_a, b_a, w_i, b_i, lam):
    rows_total = u2.shape[0]
    rows = SCAN_T * SUBLANES
    n = rows_total // rows
    w = D_RNN
    fwd = lambda i: (i, 0)
    bwd = lambda i: (n - 1 - i, 0)
    return pl.pallas_call(
        _scan_kernel,
        grid=(n,),
        in_specs=[
            pl.BlockSpec((rows, w), fwd),
            pl.BlockSpec((rows, w), bwd),
            _const_spec((2, SUBLANES, w)),
            _const_spec((2, N_LRU_BLOCKS, LRU_BLOCK_W, LRU_BLOCK_W)),
            _const_spec((2, 1, w)),
            _const_spec((2, N_LRU_BLOCKS, LRU_BLOCK_W, LRU_BLOCK_W)),
            _const_spec((2, 1, w)),
            _const_spec((2, 1, w)),
        ],
        out_specs=[
            pl.BlockSpec((rows, w), fwd),
            pl.BlockSpec((rows, w), bwd),
            pl.BlockSpec((2, SUBLANES, w), lambda i: (0, 0, 0)),
        ],
        out_shape=[
            jax.ShapeDtypeStruct((rows_total, w), Y_DTYPE),
            jax.ShapeDtypeStruct((rows_total, w), Y_DTYPE),
            jax.ShapeDtypeStruct((2, SUBLANES, w), F32),
        ],
        scratch_shapes=[
            pltpu.VMEM((2, rows, w), F32),
            pltpu.VMEM((2, rows, w), F32),
            pltpu.VMEM((2, SUBLANES, w), F32),
        ],
        compiler_params=_params(("arbitrary",)),
        name="lru_scan",
    )(u2, u2, h0, w_a, b_a, w_i, b_i, lam)


def _rope_tables():
    t = np.arange(SEQ)
    row = (t // GRID_W).astype(np.float64)
    col = (t % GRID_W).astype(np.float64)
    half = HEAD_DIM // 2
    inv = ROPE_BASE ** (-np.arange(0, half, 2, dtype=np.float64) / half)
    ang_r = row[:, None] * inv[None, :]
    ang_c = col[:, None] * inv[None, :]
    ang = np.concatenate([ang_r, ang_r, ang_c, ang_c], axis=-1)
    ang = np.tile(ang, (1, LANES // HEAD_DIM))
    low = (np.arange(LANES) % 32) < 16
    sin = np.sin(ang)
    tables = (np.cos(ang), np.where(low, -sin, 0.0), np.where(low, 0.0, sin))
    return tuple(jnp.asarray(a, dtype=F32) for a in tables)


def kernel(x, c, ctx, c_ctx, ada_w, ada_b, norm_g, mlp_w1, mlp_w2, attn_w_qkv, attn_w_o, attn_sink,
           lru_w_in, lru_conv_w, lru_conv_b, lru_w_a, lru_b_a, lru_w_i, lru_b_i, lru_lam, lru_w_out):
    n_lat = BATCH * SEQ
    n_ctx = BATCH * CTX_LEN

    c16 = jnp.zeros((16, D_MODEL), F32).at[:BATCH].set(c).at[BATCH].set(c_ctx)
    mods = _mod_call(c16, ada_w, ada_b).reshape(2, 16, N_MOD, D_MODEL)

    def slab_bmajor(m):
        return jnp.broadcast_to(m[:, :, None, :], (BATCH, N_MOD, SUBLANES, D_MODEL))

    def slab_ctx(m):
        return jnp.broadcast_to(m[None, :, None, :], (1, N_MOD, SUBLANES, D_MODEL))

    mod_x0 = slab_bmajor(mods[0, :BATCH])
    mod_c0 = slab_ctx(mods[0, BATCH])
    w_qkv = attn_w_qkv[0]
    w_qkv = jnp.concatenate([w_qkv[:, :D_Q] * (HEAD_DIM ** -0.5 * LOG2E), w_qkv[:, D_Q:]], axis=1).astype(BF16)
    sink2 = attn_sink[0] * LOG2E
    w_o = attn_w_o[0].astype(BF16)
    w1_all, w2_all = mlp_w1.astype(BF16), mlp_w2.astype(BF16)
    g0 = norm_g[0]
    tiles_per_batch = SEQ // TOKEN_TILE

    x2 = x.reshape(n_lat, D_MODEL)
    c2 = ctx.reshape(n_ctx, D_MODEL)
    q, k, v = _qkv_call(x2, mod_x0, g0, w_qkv, _rope_tables(), tiles_per_batch)
    qc, kc, vc = _qkv_call(c2, mod_c0, g0, w_qkv, None, n_ctx // TOKEN_TILE)
    kc3 = kc.reshape(BATCH, CTX_LEN, D_K2)
    vc3 = vc.reshape(BATCH, CTX_LEN, D_KV)
    att = _attn_call(sink2, q, k.reshape(BATCH, SEQ, D_K2), v.reshape(BATCH, SEQ, D_KV), kc3, vc3)
    att_c = _ctx_attn_call(sink2, qc, kc3, vc3)
    x2 = _post_call(x2, [att], mod_x0, g0, w_o, w1_all, w2_all, 0, tiles_per_batch, lru=False)
    c2 = _post_call(c2, [att_c], mod_c0, g0, w_o, w1_all, w2_all, 0, n_ctx // TOKEN_TILE, lru=False)

    x3 = x2.reshape(BATCH, SEQ, D_MODEL)
    c3 = c2.reshape(BATCH, CTX_LEN, D_MODEL)
    mod_x1 = mods[1, :BATCH].transpose(1, 0, 2)[None]
    mod_c1 = slab_ctx(mods[1, BATCH])
    g1 = norm_g[1]
    w_in = lru_w_in[0].astype(BF16)
    conv_w = 0.5 * lru_conv_w[0]
    conv_b = 0.5 * lru_conv_b[0].reshape(1, D_RNN)
    w_a, w_i = lru_w_a[0].astype(BF16), lru_w_i[0].astype(BF16)
    b_a, b_i = lru_b_a[0].reshape(2, 1, D_RNN), lru_b_i[0].reshape(2, 1, D_RNN)
    lam = lru_lam[0].reshape(2, 1, D_RNN)
    scan = functools.partial(_scan_call, w_a=w_a, b_a=b_a, w_i=w_i, b_i=b_i, lam=lam)

    (u_c,) = _lru_in_call(c3, mod_c1, g1, w_in, conv_w, conv_b, need_gate=False)
    _, _, h_ctx = scan(u_c, jnp.zeros((2, SUBLANES, D_RNN), F32))
    gate_x, u_x = _lru_in_call(x3, mod_x1, g1, w_in, conv_w, conv_b, need_gate=True)
    yf, yb, _ = scan(u_x, h_ctx)
    return _post_call(x3, [gate_x, yf, yb], mod_x1, g1, lru_w_out[0].astype(BF16),
                      w1_all, w2_all, 1, n_lat // TOKEN_TILE, lru=True)
```

```python
import functools

import jax
import jax.numpy as jnp
import numpy as np
from jax import lax
from jax.experimental import pallas as pl
from jax.experimental.pallas import tpu as pltpu

D_MODEL = 1024
BATCH = 8
SEQ = 2048
GRID_W = 64
CTX_LEN = 256
HEAD_DIM = 64
N_HEADS = 16
N_KV_HEADS = 4
GQA_GROUP = N_HEADS // N_KV_HEADS
WINDOW = 128
BLOCK = 128
ROPE_BASE = 10000.0
D_RNN = 1280
LRU_BLOCK_W = 256
N_LRU_BLOCKS = D_RNN // LRU_BLOCK_W
CONV_W = 4
LRU_C = 8.0
D_FF = 4 * D_MODEL
N_MOD = 6
EPS = 1e-6
NEG_INF = -1e30

D_Q = N_HEADS * HEAD_DIM
D_KV = N_KV_HEADS * HEAD_DIM
D_K2 = 2 * D_KV
LANES = 128
SUBLANES = 8
N_SLAB = D_MODEL // LANES
TOKEN_TILE = 512
FF_CHUNK = 1024
ATTN_Q_PER_STEP = 16
SCAN_T = 128
HALO = 16
U_DTYPE = jnp.bfloat16
Y_DTYPE = jnp.bfloat16
LOG2E = 1.4426950408889634
VMEM_LIMIT = 60 * 1024 * 1024

F32 = jnp.float32
BF16 = jnp.bfloat16


def _rms(x, g):
    ms = jnp.mean(x * x, axis=-1, keepdims=True)
    return x * lax.rsqrt(ms + EPS) * g


def _slab(x):
    return x.reshape(x.shape[0] // SUBLANES, SUBLANES, x.shape[1])


def _modulate(h, shift8, scale8):
    out = _slab(h) * (1.0 + scale8)[None] + shift8[None]
    return out.reshape(h.shape)


def _gated_add(x, gate8, y):
    out = _slab(x) + gate8[None] * _slab(y)
    return out.reshape(x.shape)


def _const_spec(shape):
    n = len(shape)
    return pl.BlockSpec(shape, lambda *_: (0,) * n, pipeline_mode=pl.Buffered(1))


def _params(sem):
    return pltpu.CompilerParams(dimension_semantics=sem, vmem_limit_bytes=VMEM_LIMIT)


def _mod_kernel(c_ref, w_ref, b_ref, o_ref):
    s = jax.nn.silu(c_ref[...]).astype(BF16)
    o_ref[0] = jnp.dot(s, w_ref[0].astype(BF16), preferred_element_type=F32) + b_ref[0]


def _mod_call(c16, ada_w, ada_b):
    depth = ada_w.shape[0]
    nt = 1024
    return pl.pallas_call(
        _mod_kernel,
        grid=(depth, N_MOD * D_MODEL // nt),
        in_specs=[
            pl.BlockSpec((16, D_MODEL), lambda l, j: (0, 0)),
            pl.BlockSpec((1, D_MODEL, nt), lambda l, j: (l, 0, j)),
            pl.BlockSpec((1, 1, nt), lambda l, j: (l, 0, j)),
        ],
        out_specs=pl.BlockSpec((1, 16, nt), lambda l, j: (l, 0, j)),
        out_shape=jax.ShapeDtypeStruct((depth, 16, N_MOD * D_MODEL), F32),
        compiler_params=_params(("arbitrary", "arbitrary")),
        name="adaln_mod",
    )(c16, ada_w, ada_b.reshape(depth, 1, N_MOD * D_MODEL))


def _qkv_kernel(*refs, rope):
    if rope:
        x_ref, mod_ref, g_ref, w_ref, cos_ref, sa_ref, sb_ref, q_ref, k_ref, v_ref = refs
    else:
        x_ref, mod_ref, g_ref, w_ref, q_ref, k_ref, v_ref = refs
    h = _modulate(_rms(x_ref[...], g_ref[0:1, :]), mod_ref[0, 0], mod_ref[0, 1])
    y = jnp.dot(h.astype(BF16), w_ref[...], preferred_element_type=F32)
    if rope:
        cos, sa, sb = cos_ref[...], sa_ref[...], sb_ref[...]
    low = lax.broadcasted_iota(jnp.int32, (x_ref.shape[0], LANES), 1) < HEAD_DIM
    for c in range((D_Q + D_KV) // LANES):
        yc = y[:, c * LANES:(c + 1) * LANES]
        if rope:
            yc = yc * cos + pltpu.roll(yc, LANES - 16, 1) * sa + pltpu.roll(yc, 16, 1) * sb
        if c < D_Q // LANES:
            q_ref[:, c * LANES:(c + 1) * LANES] = yc.astype(BF16)
        else:
            c2 = 2 * (c - D_Q // LANES)
            swapped = pltpu.roll(yc, HEAD_DIM, 1)
            k_ref[:, c2 * LANES:(c2 + 1) * LANES] = jnp.where(low, yc, swapped).astype(BF16)
            k_ref[:, (c2 + 1) * LANES:(c2 + 2) * LANES] = jnp.where(low, swapped, yc).astype(BF16)
    for blk in range(x_ref.shape[0] // BLOCK):
        v_ref[blk] = y[blk * BLOCK:(blk + 1) * BLOCK, D_Q + D_KV:].T.astype(BF16)


def _qkv_call(x2, mod, g, w_qkv, tables, tiles_per_group):
    n = x2.shape[0]
    tm = TOKEN_TILE
    rope = tables is not None
    in_specs = [
        pl.BlockSpec((tm, D_MODEL), lambda i: (i, 0)),
        pl.BlockSpec((1, N_MOD, SUBLANES, D_MODEL), lambda i: (i // tiles_per_group, 0, 0, 0)),
        _const_spec((4, D_MODEL)),
        _const_spec((D_MODEL, D_Q + 2 * D_KV)),
    ]
    args = [x2, mod, g, w_qkv]
    if rope:
        nt = SEQ // tm
        in_specs += [pl.BlockSpec((tm, LANES), lambda i: (i % nt, 0))] * 3
        args += list(tables)
    return pl.pallas_call(
        functools.partial(_qkv_kernel, rope=rope),
        grid=(n // tm,),
        in_specs=in_specs,
        out_specs=[
            pl.BlockSpec((tm, D_Q), lambda i: (i, 0)),
            pl.BlockSpec((tm, D_K2), lambda i: (i, 0)),
            pl.BlockSpec((tm // BLOCK, D_KV, BLOCK), lambda i: (i, 0, 0)),
        ],
        out_shape=[
            jax.ShapeDtypeStruct((n, D_Q), BF16),
            jax.ShapeDtypeStruct((n, D_K2), BF16),
            jax.ShapeDtypeStruct((n // BLOCK, D_KV, BLOCK), BF16),
        ],
        compiler_params=_params(("parallel",)),
        name="qkv_rope" if rope else "qkv_ctx",
    )(*args)


VT_ROWS = HEAD_DIM + 16


def _attn_kernel(*refs, local, q_per_step):
    if local:
        sink_ref, q_ref, k_ref, v_ref, kc_ref, vc_ref, o_ref, s_sc, p_sc = refs
    else:
        sink_ref, q_ref, kc_ref, vc_ref, o_ref, s_sc, p_sc = refs
    j = pl.program_id(1)
    seq_blocks = SEQ // BLOCK if local else 0
    ctx_blocks = CTX_LEN // BLOCK
    n_keys = s_sc.shape[1]
    pad_row = lax.broadcasted_iota(jnp.int32, (VT_ROWS - HEAD_DIM, n_keys), 0)
    vt_pad = jnp.where(pad_row == 0, 1.0, 0.0).astype(BF16)

    nt = (((1,), (1,)), ((), ()))
    n_band = 3
    n_chunks = D_Q // LANES
    lane = lax.broadcasted_iota(jnp.int32, (BLOCK, LANES), 1)
    first_head = lax.broadcasted_iota(jnp.int32, (1, 2 * BLOCK), 1) < BLOCK

    def block_params(qb):
        jq = j * q_per_step + qb
        rows = pl.ds(pl.multiple_of(qb * BLOCK, BLOCK), BLOCK)
        if not local:
            return rows, None, None, None
        blk0 = jnp.clip(jq - 1, 0, seq_blocks - n_band)
        start = pl.multiple_of(blk0 * BLOCK, BLOCK)
        kpos = start + lax.broadcasted_iota(jnp.int32, (n_band * BLOCK, BLOCK), 0)
        qpos = jq * BLOCK + lax.broadcasted_iota(jnp.int32, (n_band * BLOCK, BLOCK), 1)
        bias = jnp.where(jnp.abs(kpos - qpos) <= WINDOW, 0.0, NEG_INF).astype(F32)
        return rows, blk0, start, jnp.concatenate([bias, bias], axis=1)

    def scores(c, params):
        rows, _, start, bias2 = params
        kcols = slice((c // 2) * LANES, (c // 2 + 1) * LANES)
        qc = q_ref[rows, c * LANES:(c + 1) * LANES]
        zero = jnp.zeros_like(qc)
        q2 = jnp.concatenate([jnp.where(lane < HEAD_DIM, qc, zero), jnp.where(lane < HEAD_DIM, zero, qc)], axis=0)
        s_sc[c % 2, 0:CTX_LEN] = lax.dot_general(kc_ref[0, :, kcols], q2, nt, preferred_element_type=F32)
        if local:
            s_sc[c % 2, CTX_LEN:] = lax.dot_general(k_ref[0, pl.ds(start, n_band * BLOCK), kcols], q2, nt,
                                                    preferred_element_type=F32) + bias2

    def softmax(c):
        s = s_sc[c % 2]
        sink_row = jnp.where(first_head, sink_ref[2 * c], sink_ref[2 * c + 1])
        m = jnp.maximum(jnp.max(s, axis=0, keepdims=True), sink_row)
        p_sc[c % 2] = jnp.exp2(s - m).astype(BF16)
        return jnp.exp2(sink_row - m)

    def values(c, params, sink_term):
        rows, blk0, _, _ = params
        hrows = slice((c // 2) * HEAD_DIM, (c // 2 + 1) * HEAD_DIM)
        vt = [vc_ref[i, hrows, :] for i in range(ctx_blocks)]
        if local:
            vt_band = v_ref[pl.ds(blk0, n_band), hrows, :]
            vt += [vt_band[i] for i in range(n_band)]
        vt_aug = jnp.concatenate([jnp.concatenate(vt, axis=1), vt_pad], axis=0)
        acc = jnp.dot(vt_aug, p_sc[c % 2], preferred_element_type=F32)
        out_t = acc[0:HEAD_DIM] / (acc[HEAD_DIM:HEAD_DIM + 1] + sink_term)
        both = jnp.concatenate([out_t[:, :BLOCK], out_t[:, BLOCK:]], axis=0)
        o_ref[rows, c * LANES:(c + 1) * LANES] = both.T.astype(o_ref.dtype)

    def query_block(qb, sink_term0):
        cur = block_params(qb)
        nxt = block_params(jnp.minimum(qb + 1, q_per_step - 1))
        sink_terms = {0: sink_term0}
        for c in range(n_chunks):
            if c + 2 < n_chunks:
                scores(c + 2, cur)
            else:
                scores(c + 2 - n_chunks, nxt)
            sink_terms[c + 1] = softmax((c + 1) % n_chunks)
            values(c, cur, sink_terms[c])
        return sink_terms[n_chunks]

    first = block_params(0)
    scores(0, first)
    scores(1, first)
    lax.fori_loop(0, q_per_step, query_block, softmax(0))


def _attn_scratch(n_keys):
    return [
        pltpu.VMEM((2, n_keys, 2 * BLOCK), F32),
        pltpu.VMEM((2, n_keys, 2 * BLOCK), BF16),
    ]


def _attn_call(sink2, q, k, vt, kc, vtc):
    nb = SEQ // BLOCK
    nbc = CTX_LEN // BLOCK
    qps = ATTN_Q_PER_STEP
    steps = nb // qps
    return pl.pallas_call(
        functools.partial(_attn_kernel, local=True, q_per_step=qps),
        grid=(BATCH, steps),
        in_specs=[
            pl.BlockSpec(memory_space=pltpu.SMEM),
            pl.BlockSpec((qps * BLOCK, D_Q), lambda b, j: (b * steps + j, 0)),
            pl.BlockSpec((1, SEQ, D_K2), lambda b, j: (b, 0, 0)),
            pl.BlockSpec((nb, D_KV, BLOCK), lambda b, j: (b, 0, 0)),
            pl.BlockSpec((1, CTX_LEN, D_K2), lambda b, j: (b, 0, 0)),
            pl.BlockSpec((nbc, D_KV, BLOCK), lambda b, j: (b, 0, 0)),
        ],
        out_specs=pl.BlockSpec((qps * BLOCK, D_Q), lambda b, j: (b * steps + j, 0)),
        out_shape=jax.ShapeDtypeStruct((BATCH * SEQ, D_Q), BF16),
        scratch_shapes=_attn_scratch(CTX_LEN + 3 * BLOCK),
        compiler_params=_params(("parallel", "arbitrary")),
        name="band_attn",
    )(sink2, q, k, vt, kc, vtc)


def _ctx_attn_call(sink2, qc, kc, vtc):
    nb = CTX_LEN // BLOCK
    return pl.pallas_call(
        functools.partial(_attn_kernel, local=False, q_per_step=nb),
        grid=(BATCH, 1),
        in_specs=[
            pl.BlockSpec(memory_space=pltpu.SMEM),
            pl.BlockSpec((CTX_LEN, D_Q), lambda b, j: (b, 0)),
            pl.BlockSpec((1, CTX_LEN, D_K2), lambda b, j: (b, 0, 0)),
            pl.BlockSpec((nb, D_KV, BLOCK), lambda b, j: (b, 0, 0)),
        ],
        out_specs=pl.BlockSpec((CTX_LEN, D_Q), lambda b, j: (b, 0)),
        out_shape=jax.ShapeDtypeStruct((BATCH * CTX_LEN, D_Q), BF16),
        scratch_shapes=_attn_scratch(CTX_LEN),
        compiler_params=_params(("parallel", "arbitrary")),
        name="ctx_attn",
    )(sink2, qc, kc, vtc)


def _to_time_major(src_ref, sc_ref):
    nt = src_ref.shape[1]
    for b in range(BATCH):
        for s in range(N_SLAB):
            sc_ref[s, pl.ds(b, nt, stride=SUBLANES), :] = src_ref[b, :, s * LANES:(s + 1) * LANES]
    return jnp.concatenate([sc_ref[s, 0:nt * SUBLANES, :] for s in range(N_SLAB)], axis=1)


def _from_time_major(val, sc_ref, dst_ref):
    nt = dst_ref.shape[1]
    for s in range(N_SLAB):
        sc_ref[s, 0:nt * SUBLANES, :] = val[:, s * LANES:(s + 1) * LANES]
    for b in range(BATCH):
        for s in range(N_SLAB):
            dst_ref[b, :, s * LANES:(s + 1) * LANES] = sc_ref[s, pl.ds(b, nt, stride=SUBLANES), :]


def _post_kernel(*refs, lru):
    if lru:
        x_ref, gate_ref, yf_ref, yb_ref, mod_ref, g_ref, wf_ref, w1_ref, w2_ref, o_ref, x1_sc, h_sc, acc_sc, t_sc = refs
        front = (gate_ref[...].astype(F32) * (yf_ref[...].astype(F32) + yb_ref[...].astype(F32))).astype(BF16)
        x = _to_time_major(x_ref, t_sc)
    else:
        x_ref, a_ref, mod_ref, g_ref, wf_ref, w1_ref, w2_ref, o_ref, x1_sc, h_sc, acc_sc = refs
        front = a_ref[...]
        x = x_ref[...]
    half_rows = x.shape[0] // 2

    def head(r):
        rs = slice(r * half_rows, (r + 1) * half_rows)
        y = jnp.dot(front[rs], wf_ref[...], preferred_element_type=F32)
        x1 = _gated_add(x[rs], mod_ref[0, 2], _rms(y, g_ref[1:2, :]))
        x1_sc[r] = x1
        h_sc[r] = _modulate(_rms(x1, g_ref[2:3, :]), mod_ref[0, 3], mod_ref[0, 4]).astype(BF16)

    def mlp(r):
        acc = jnp.zeros((half_rows, D_MODEL), F32)
        for c in range(D_FF // FF_CHUNK):
            hid = jnp.dot(h_sc[r], w1_ref[:, c * FF_CHUNK:(c + 1) * FF_CHUNK], preferred_element_type=F32)
            hid = jnp.square(jnp.maximum(hid, 0.0)).astype(BF16)
            acc = acc + jnp.dot(hid, w2_ref[c * FF_CHUNK:(c + 1) * FF_CHUNK, :], preferred_element_type=F32)
        acc_sc[r] = acc

    def tail(r):
        return _gated_add(x1_sc[r], mod_ref[0, 5], _rms(acc_sc[r], g_ref[3:4, :]))

    head(0)
    head(1)
    mlp(0)
    out0 = tail(0)
    mlp(1)
    out = jnp.concatenate([out0, tail(1)], axis=0)
    if lru:
        _from_time_major(out, t_sc, o_ref)
    else:
        o_ref[...] = out


def _post_call(x, fronts, mod, g, w_front, w1, w2, layer, tiles_per_group, lru):
    tm = TOKEN_TILE
    layer_spec = lambda shape: pl.BlockSpec((None,) + shape[1:], lambda i: (layer, 0, 0),
                                            pipeline_mode=pl.Buffered(1))
    row = lambda i: (i, 0)
    if lru:
        nt = tm // BATCH
        n = x.shape[0] * x.shape[1]
        x_spec = pl.BlockSpec((BATCH, nt, D_MODEL), lambda i: (0, i, 0))
        scratch = [pltpu.VMEM((N_SLAB, tm, LANES), F32)]
    else:
        n = x.shape[0]
        x_spec = pl.BlockSpec((tm, D_MODEL), row)
        scratch = []
    scratch = [
        pltpu.VMEM((2, tm // 2, D_MODEL), F32),
        pltpu.VMEM((2, tm // 2, D_MODEL), BF16),
        pltpu.VMEM((2, tm // 2, D_MODEL), F32),
    ] + scratch
    in_specs = [x_spec]
    in_specs += [pl.BlockSpec((tm, f.shape[1]), row) for f in fronts]
    in_specs += [
        pl.BlockSpec((1, N_MOD, SUBLANES, D_MODEL), lambda i: (i // tiles_per_group, 0, 0, 0)),
        _const_spec((4, D_MODEL)),
        _const_spec(w_front.shape),
        layer_spec(w1.shape),
        layer_spec(w2.shape),
    ]
    return pl.pallas_call(
        functools.partial(_post_kernel, lru=lru),
        grid=(n // tm,),
        in_specs=in_specs,
        out_specs=x_spec,
        out_shape=jax.ShapeDtypeStruct(x.shape, F32),
        scratch_shapes=scratch,
        compiler_params=_params(("parallel",)),
        name="lru_out_mlp" if lru else "attn_out_mlp",
    )(x, *fronts, mod, g, w_front, w1, w2)


def _lru_in_kernel(*refs, need_gate):
    if need_gate:
        x_ref, xp_ref, xn_ref, mod_ref, g_ref, w_ref, cw_ref, cb_ref, gate_ref, u_ref, v_sc, t_sc, tp_sc, tn_sc = refs
    else:
        x_ref, xp_ref, xn_ref, mod_ref, g_ref, w_ref, cw_ref, cb_ref, u_ref, v_sc, t_sc, tp_sc, tn_sc = refs
    i = pl.program_id(0)
    n = pl.num_programs(0)
    rows = x_ref.shape[0] * x_ref.shape[1]
    s8 = SUBLANES

    def pre(x):
        return _modulate(_rms(x, g_ref[0:1, :]), mod_ref[0, 0], mod_ref[0, 1]).astype(BF16)

    h = pre(_to_time_major(x_ref, t_sc))
    if need_gate:
        gate_ref[...] = jax.nn.gelu(jnp.dot(h, w_ref[:, :D_RNN], preferred_element_type=F32)).astype(BF16)
    x_prev = _to_time_major(xp_ref, tp_sc)[SUBLANES * SUBLANES - HALO:]
    x_next = _to_time_major(xn_ref, tn_sc)[:HALO]
    h_ext = jnp.concatenate([pre(x_prev), h, pre(x_next)], axis=0)
    v_sc[...] = jnp.dot(h_ext, w_ref[:, D_RNN:], preferred_element_type=F32)
    v_sc[0:HALO] = v_sc[0:HALO] * (i > 0).astype(F32)
    v_sc[HALO + rows:HALO + rows + s8] = v_sc[HALO + rows:HALO + rows + s8] * (i < n - 1).astype(F32)
    u_ref[...] = (cb_ref[...]
                  + cw_ref[0:1, :] * v_sc[HALO - 2 * s8:HALO - 2 * s8 + rows]
                  + cw_ref[1:2, :] * v_sc[HALO - s8:HALO - s8 + rows]
                  + cw_ref[2:3, :] * v_sc[HALO:HALO + rows]
                  + cw_ref[3:4, :] * v_sc[HALO + s8:HALO + s8 + rows]).astype(u_ref.dtype)


def _lru_in_call(x3, mod, g, w_in, conv_w, conv_b, need_gate):
    t_total = x3.shape[1]
    n = BATCH * t_total
    tm = TOKEN_TILE
    nt = tm // BATCH
    row = lambda i: (i, 0)
    per_tile = nt // SUBLANES
    last = t_total // SUBLANES - 1
    halo_spec = lambda f: pl.BlockSpec((BATCH, SUBLANES, D_MODEL), f)
    out_specs = [pl.BlockSpec((tm, D_RNN), row), pl.BlockSpec((tm, D_RNN), row)]
    out_shape = [jax.ShapeDtypeStruct((n, D_RNN), BF16), jax.ShapeDtypeStruct((n, D_RNN), U_DTYPE)]
    if not need_gate:
        out_specs, out_shape = out_specs[1:], out_shape[1:]
    return pl.pallas_call(
        functools.partial(_lru_in_kernel, need_gate=need_gate),
        grid=(n // tm,),
        in_specs=[
            pl.BlockSpec((BATCH, nt, D_MODEL), lambda i: (0, i, 0)),
            halo_spec(lambda i: (0, jnp.maximum(i * per_tile - 1, 0), 0)),
            halo_spec(lambda i: (0, jnp.minimum((i + 1) * per_tile, last), 0)),
            _const_spec((1, N_MOD, SUBLANES, D_MODEL)),
            _const_spec((4, D_MODEL)),
            _const_spec((D_MODEL, 2 * D_RNN)),
            _const_spec((CONV_W, D_RNN)),
            _const_spec((1, D_RNN)),
        ],
        out_specs=out_specs,
        out_shape=out_shape,
        scratch_shapes=[
            pltpu.VMEM((tm + 2 * HALO, D_RNN), F32),
            pltpu.VMEM((N_SLAB, tm, LANES), F32),
            pltpu.VMEM((N_SLAB, SUBLANES * SUBLANES, LANES), F32),
            pltpu.VMEM((N_SLAB, SUBLANES * SUBLANES, LANES), F32),
        ],
        compiler_params=_params(("parallel",)),
        name="lru_in",
    )(x3, x3, x3, mod, g, w_in, conv_w, conv_b)


def _scan_kernel(uf_ref, ub_ref, h0_ref, wa_ref, ba_ref, wi_ref, bi_ref, lam_ref, yf_ref, yb_ref, ht_ref,
                 a_sc, bx_sc, h_sc):
    i = pl.program_id(0)
    n = pl.num_programs(0)
    rows = uf_ref.shape[0]
    nt = rows // SUBLANES
    s8 = SUBLANES

    @pl.when(i == 0)
    def _():
        h_sc[...] = h0_ref[...]

    for d, u_ref in enumerate((uf_ref, ub_ref)):
        for c in range(N_LRU_BLOCKS):
            cs = slice(c * LRU_BLOCK_W, (c + 1) * LRU_BLOCK_W)
            u16 = u_ref[:, cs].astype(BF16)
            u = u_ref[:, cs].astype(F32)
            ta = jnp.tanh(jnp.dot(u16, wa_ref[d, c], preferred_element_type=F32) + 0.5 * ba_ref[d, :, cs])
            ti = jnp.tanh(jnp.dot(u16, wi_ref[d, c], preferred_element_type=F32) + 0.5 * bi_ref[d, :, cs])
            neg_lam = -lam_ref[d, :, cs]
            softplus = jnp.maximum(neg_lam, 0.0) + jnp.log1p(jnp.exp(-jnp.abs(neg_lam)))
            k = (-0.5 * LRU_C * LOG2E) * softplus
            a = jnp.exp2(k * ta + k)
            w = 1.0 - a * a
            root = w * lax.rsqrt(jnp.maximum(w, 1e-30))
            a_sc[d, :, cs] = a
            bx_sc[d, :, cs] = root * (ti * u + u)

    def step(t, carry):
        hf, hb = carry
        rf = pl.multiple_of(t * 2 * s8, 2 * s8)
        rb = pl.multiple_of((nt - 2 - 2 * t) * s8, 2 * s8)
        hf1 = a_sc[0, pl.ds(rf, s8), :] * hf + bx_sc[0, pl.ds(rf, s8), :]
        hf2 = a_sc[0, pl.ds(rf + s8, s8), :] * hf1 + bx_sc[0, pl.ds(rf + s8, s8), :]
        yf_ref[pl.ds(rf, 2 * s8), :] = jnp.concatenate([hf1, hf2], axis=0).astype(yf_ref.dtype)
        hb1 = a_sc[1, pl.ds(rb + s8, s8), :] * hb + bx_sc[1, pl.ds(rb + s8, s8), :]
        hb2 = a_sc[1, pl.ds(rb, s8), :] * hb1 + bx_sc[1, pl.ds(rb, s8), :]
        yb_ref[pl.ds(rb, 2 * s8), :] = jnp.concatenate([hb2, hb1], axis=0).astype(yb_ref.dtype)
        return hf2, hb2

    hf, hb = lax.fori_loop(0, nt // 2, step, (h_sc[0], h_sc[1]), unroll=2)
    h_sc[0] = hf
    h_sc[1] = hb

    @pl.when(i == n - 1)
    def _():
        ht_ref[...] = h_sc[...]


def _scan_call(u2, h0, w_a, b_a, w_i, b_i, lam):
    rows_total = u2.shape[0]
    rows = SCAN_T * SUBLANES
    n = rows_total // rows
    w = D_RNN
    fwd = lambda i: (i, 0)
    bwd = lambda i: (n - 1 - i, 0)
    return pl.pallas_call(
        _scan_kernel,
        grid=(n,),
        in_specs=[
            pl.BlockSpec((rows, w), fwd),
            pl.BlockSpec((rows, w), bwd),
            _const_spec((2, SUBLANES, w)),
            _const_spec((2, N_LRU_BLOCKS, LRU_BLOCK_W, LRU_BLOCK_W)),
            _const_spec((2, 1, w)),
            _const_spec((2, N_LRU_BLOCKS, LRU_BLOCK_W, LRU_BLOCK_W)),
            _const_spec((2, 1, w)),
            _const_spec((2, 1, w)),
        ],
        out_specs=[
            pl.BlockSpec((rows, w), fwd),
            pl.BlockSpec((rows, w), bwd),
            pl.BlockSpec((2, SUBLANES, w), lambda i: (0, 0, 0)),
        ],
        out_shape=[
            jax.ShapeDtypeStruct((rows_total, w), Y_DTYPE),
            jax.ShapeDtypeStruct((rows_total, w), Y_DTYPE),
            jax.ShapeDtypeStruct((2, SUBLANES, w), F32),
        ],
        scratch_shapes=[
            pltpu.VMEM((2, rows, w), F32),
            pltpu.VMEM((2, rows, w), F32),
            pltpu.VMEM((2, SUBLANES, w), F32),
        ],
        compiler_params=_params(("arbitrary",)),
        name="lru_scan",
    )(u2, u2, h0, w_a, b_a, w_i, b_i, lam)


def _rope_tables():
    t = np.arange(SEQ)
    row = (t // GRID_W).astype(np.float64)
    col = (t % GRID_W).astype(np.float64)
    half = HEAD_DIM // 2
    inv = ROPE_BASE ** (-np.arange(0, half, 2, dtype=np.float64) / half)
    ang_r = row[:, None] * inv[None, :]
    ang_c = col[:, None] * inv[None, :]
    ang = np.concatenate([ang_r, ang_r, ang_c, ang_c], axis=-1)
    ang = np.tile(ang, (1, LANES // HEAD_DIM))
    low = (np.arange(LANES) % 32) < 16
    sin = np.sin(ang)
    tables = (np.cos(ang), np.where(low, -sin, 0.0), np.where(low, 0.0, sin))
    return tuple(jnp.asarray(a, dtype=F32) for a in tables)


def kernel(x, c, ctx, c_ctx, ada_w, ada_b, norm_g, mlp_w1, mlp_w2, attn_w_qkv, attn_w_o, attn_sink,
           lru_w_in, lru_conv_w, lru_conv_b, lru_w_a, lru_b_a, lru_w_i, lru_b_i, lru_lam, lru_w_out):
    n_lat = BATCH * SEQ
    n_ctx = BATCH * CTX_LEN

    c16 = jnp.zeros((16, D_MODEL), F32).at[:BATCH].set(c).at[BATCH].set(c_ctx)
    mods = _mod_call(c16, ada_w, ada_b).reshape(2, 16, N_MOD, D_MODEL)

    def slab_bmajor(m):
        return jnp.broadcast_to(m[:, :, None, :], (BATCH, N_MOD, SUBLANES, D_MODEL))

    def slab_ctx(m):
        return jnp.broadcast_to(m[None, :, None, :], (1, N_MOD, SUBLANES, D_MODEL))

    mod_x0 = slab_bmajor(mods[0, :BATCH])
    mod_c0 = slab_ctx(mods[0, BATCH])
    w_qkv = attn_w_qkv[0]
    w_qkv = jnp.concatenate([w_qkv[:, :D_Q] * (HEAD_DIM ** -0.5 * LOG2E), w_qkv[:, D_Q:]], axis=1).astype(BF16)
    sink2 = attn_sink[0] * LOG2E
    w_o = attn_w_o[0].astype(BF16)
    w1_all, w2_all = mlp_w1.astype(BF16), mlp_w2.astype(BF16)
    g0 = norm_g[0]
    tiles_per_batch = SEQ // TOKEN_TILE

    x2 = x.reshape(n_lat, D_MODEL)
    c2 = ctx.reshape(n_ctx, D_MODEL)
    q, k, v = _qkv_call(x2, mod_x0, g0, w_qkv, _rope_tables(), tiles_per_batch)
    qc, kc, vc = _qkv_call(c2, mod_c0, g0, w_qkv, None, n_ctx // TOKEN_TILE)
    kc3 = kc.reshape(BATCH, CTX_LEN, D_K2)
    att = _attn_call(sink2, q, k.reshape(BATCH, SEQ, D_K2), v, kc3, vc)
    att_c = _ctx_attn_call(sink2, qc, kc3, vc)
    x2 = _post_call(x2, [att], mod_x0, g0, w_o, w1_all, w2_all, 0, tiles_per_batch, lru=False)
    c2 = _post_call(c2, [att_c], mod_c0, g0, w_o, w1_all, w2_all, 0, n_ctx // TOKEN_TILE, lru=False)

    x3 = x2.reshape(BATCH, SEQ, D_MODEL)
    c3 = c2.reshape(BATCH, CTX_LEN, D_MODEL)
    mod_x1 = mods[1, :BATCH].transpose(1, 0, 2)[None]
    mod_c1 = slab_ctx(mods[1, BATCH])
    g1 = norm_g[1]
    w_in = lru_w_in[0].astype(BF16)
    conv_w = 0.5 * lru_conv_w[0]
    conv_b = 0.5 * lru_conv_b[0].reshape(1, D_RNN)
    w_a, w_i = lru_w_a[0].astype(BF16), lru_w_i[0].astype(BF16)
    b_a, b_i = lru_b_a[0].reshape(2, 1, D_RNN), lru_b_i[0].reshape(2, 1, D_RNN)
    lam = lru_lam[0].reshape(2, 1, D_RNN)
    scan = functools.partial(_scan_call, w_a=w_a, b_a=b_a, w_i=w_i, b_i=b_i, lam=lam)

    (u_c,) = _lru_in_call(c3, mod_c1, g1, w_in, conv_w, conv_b, need_gate=False)
    _, _, h_ctx = scan(u_c, jnp.zeros((2, SUBLANES, D_RNN), F32))
    gate_x, u_x = _lru_in_call(x3, mod_x1, g1, w_in, conv_w, conv_b, need_gate=True)
    yf, yb, _ = scan(u_x, h_ctx)
    return _post_call(x3, [gate_x, yf, yb], mod_x1, g1, lru_w_out[0].astype(BF16),
                      w1_all, w2_all, 1, n_lat // TOKEN_TILE, lru=True)
```

```python
import functools

import jax
import jax.numpy as jnp
import numpy as np
from jax import lax
from jax.experimental import pallas as pl
from jax.experimental.pallas import tpu as pltpu

D_MODEL = 1024
BATCH = 8
SEQ = 2048
GRID_W = 64
CTX_LEN = 256
HEAD_DIM = 64
N_HEADS = 16
N_KV_HEADS = 4
GQA_GROUP = N_HEADS // N_KV_HEADS
WINDOW = 128
BLOCK = 128
ROPE_BASE = 10000.0
D_RNN = 1280
LRU_BLOCK_W = 256
N_LRU_BLOCKS = D_RNN // LRU_BLOCK_W
CONV_W = 4
LRU_C = 8.0
D_FF = 4 * D_MODEL
N_MOD = 6
EPS = 1e-6
NEG_INF = -1e30

D_Q = N_HEADS * HEAD_DIM
D_KV = N_KV_HEADS * HEAD_DIM
D_K2 = 2 * D_KV
LANES = 128
SUBLANES = 8
N_SLAB = D_MODEL // LANES
TOKEN_TILE = 512
FF_CHUNK = 1024
ATTN_Q_PER_STEP = 16
SCAN_T = 128
HALO = 16
U_DTYPE = jnp.bfloat16
Y_DTYPE = jnp.bfloat16
LOG2E = 1.4426950408889634
VMEM_LIMIT = 60 * 1024 * 1024

F32 = jnp.float32
BF16 = jnp.bfloat16


def _rms(x, g):
    ms = jnp.mean(x * x, axis=-1, keepdims=True)
    return x * lax.rsqrt(ms + EPS) * g


def _slab(x):
    return x.reshape(x.shape[0] // SUBLANES, SUBLANES, x.shape[1])


def _modulate(h, shift8, scale8):
    out = _slab(h) * (1.0 + scale8)[None] + shift8[None]
    return out.reshape(h.shape)


def _gated_add(x, gate8, y):
    out = _slab(x) + gate8[None] * _slab(y)
    return out.reshape(x.shape)


def _const_spec(shape):
    n = len(shape)
    return pl.BlockSpec(shape, lambda *_: (0,) * n, pipeline_mode=pl.Buffered(1))


def _params(sem):
    return pltpu.CompilerParams(dimension_semantics=sem, vmem_limit_bytes=VMEM_LIMIT)


def _mod_kernel(c_ref, w_ref, b_ref, o_ref):
    s = jax.nn.silu(c_ref[...]).astype(BF16)
    o_ref[0] = jnp.dot(s, w_ref[0].astype(BF16), preferred_element_type=F32) + b_ref[0]


def _mod_call(c16, ada_w, ada_b):
    depth = ada_w.shape[0]
    nt = 1024
    return pl.pallas_call(
        _mod_kernel,
        grid=(depth, N_MOD * D_MODEL // nt),
        in_specs=[
            pl.BlockSpec((16, D_MODEL), lambda l, j: (0, 0)),
            pl.BlockSpec((1, D_MODEL, nt), lambda l, j: (l, 0, j)),
            pl.BlockSpec((1, 1, nt), lambda l, j: (l, 0, j)),
        ],
        out_specs=pl.BlockSpec((1, 16, nt), lambda l, j: (l, 0, j)),
        out_shape=jax.ShapeDtypeStruct((depth, 16, N_MOD * D_MODEL), F32),
        compiler_params=_params(("arbitrary", "arbitrary")),
        name="adaln_mod",
    )(c16, ada_w, ada_b.reshape(depth, 1, N_MOD * D_MODEL))


def _qkv_kernel(*refs, rope):
    if rope:
        x_ref, mod_ref, g_ref, w_ref, cos_ref, sa_ref, sb_ref, q_ref, k_ref, v_ref = refs
    else:
        x_ref, mod_ref, g_ref, w_ref, q_ref, k_ref, v_ref = refs
    h = _modulate(_rms(x_ref[...], g_ref[0:1, :]), mod_ref[0, 0], mod_ref[0, 1])
    y = jnp.dot(h.astype(BF16), w_ref[...], preferred_element_type=F32)
    if rope:
        cos, sa, sb = cos_ref[...], sa_ref[...], sb_ref[...]
    low = lax.broadcasted_iota(jnp.int32, (x_ref.shape[0], LANES), 1) < HEAD_DIM
    for c in range((D_Q + D_KV) // LANES):
        yc = y[:, c * LANES:(c + 1) * LANES]
        if rope:
            yc = yc * cos + pltpu.roll(yc, LANES - 16, 1) * sa + pltpu.roll(yc, 16, 1) * sb
        if c < D_Q // LANES:
            q_ref[:, c * LANES:(c + 1) * LANES] = yc.astype(BF16)
        else:
            c2 = 2 * (c - D_Q // LANES)
            swapped = pltpu.roll(yc, HEAD_DIM, 1)
            k_ref[:, c2 * LANES:(c2 + 1) * LANES] = jnp.where(low, yc, swapped).astype(BF16)
            k_ref[:, (c2 + 1) * LANES:(c2 + 2) * LANES] = jnp.where(low, swapped, yc).astype(BF16)
    for blk in range(x_ref.shape[0] // BLOCK):
        v_ref[blk] = y[blk * BLOCK:(blk + 1) * BLOCK, D_Q + D_KV:].T.astype(BF16)


def _qkv_call(x2, mod, g, w_qkv, tables, tiles_per_group):
    n = x2.shape[0]
    tm = TOKEN_TILE
    rope = tables is not None
    in_specs = [
        pl.BlockSpec((tm, D_MODEL), lambda i: (i, 0)),
        pl.BlockSpec((1, N_MOD, SUBLANES, D_MODEL), lambda i: (i // tiles_per_group, 0, 0, 0)),
        _const_spec((4, D_MODEL)),
        _const_spec((D_MODEL, D_Q + 2 * D_KV)),
    ]
    args = [x2, mod, g, w_qkv]
    if rope:
        nt = SEQ // tm
        in_specs += [pl.BlockSpec((tm, LANES), lambda i: (i % nt, 0))] * 3
        args += list(tables)
    return pl.pallas_call(
        functools.partial(_qkv_kernel, rope=rope),
        grid=(n // tm,),
        in_specs=in_specs,
        out_specs=[
            pl.BlockSpec((tm, D_Q), lambda i: (i, 0)),
            pl.BlockSpec((tm, D_K2), lambda i: (i, 0)),
            pl.BlockSpec((tm // BLOCK, D_KV, BLOCK), lambda i: (i, 0, 0)),
        ],
        out_shape=[
            jax.ShapeDtypeStruct((n, D_Q), BF16),
            jax.ShapeDtypeStruct((n, D_K2), BF16),
            jax.ShapeDtypeStruct((n // BLOCK, D_KV, BLOCK), BF16),
        ],
        compiler_params=_params(("parallel",)),
        name="qkv_rope" if rope else "qkv_ctx",
    )(*args)


VT_ROWS = HEAD_DIM + 16


def _attn_kernel(*refs, local, q_per_step):
    if local:
        sink_ref, q_ref, k_ref, v_ref, kc_ref, vc_ref, o_ref, s_sc, p_sc = refs
    else:
        sink_ref, q_ref, kc_ref, vc_ref, o_ref, s_sc, p_sc = refs
    j = pl.program_id(1)
    seq_blocks = SEQ // BLOCK if local else 0
    ctx_blocks = CTX_LEN // BLOCK
    n_keys = s_sc.shape[1]
    pad_row = lax.broadcasted_iota(jnp.int32, (VT_ROWS - HEAD_DIM, n_keys), 0)
    vt_pad = jnp.where(pad_row == 0, 1.0, 0.0).astype(BF16)

    nt = (((1,), (1,)), ((), ()))
    n_band = 3
    n_chunks = D_Q // LANES
    lane = lax.broadcasted_iota(jnp.int32, (BLOCK, LANES), 1)
    first_head = lax.broadcasted_iota(jnp.int32, (1, 2 * BLOCK), 1) < BLOCK

    def block_params(qb):
        jq = j * q_per_step + qb
        rows = pl.ds(pl.multiple_of(qb * BLOCK, BLOCK), BLOCK)
        if not local:
            return rows, None, None, None
        blk0 = jnp.clip(jq - 1, 0, seq_blocks - n_band)
        start = pl.multiple_of(blk0 * BLOCK, BLOCK)
        kpos = start + lax.broadcasted_iota(jnp.int32, (n_band * BLOCK, BLOCK), 0)
        qpos = jq * BLOCK + lax.broadcasted_iota(jnp.int32, (n_band * BLOCK, BLOCK), 1)
        bias = jnp.where(jnp.abs(kpos - qpos) <= WINDOW, 0.0, NEG_INF).astype(F32)
        return rows, blk0, start, jnp.concatenate([bias, bias], axis=1)

    def scores(c, params):
        rows, _, start, bias2 = params
        kcols = slice((c // 2) * LANES, (c // 2 + 1) * LANES)
        qc = q_ref[rows, c * LANES:(c + 1) * LANES]
        zero = jnp.zeros_like(qc)
        q2 = jnp.concatenate([jnp.where(lane < HEAD_DIM, qc, zero), jnp.where(lane < HEAD_DIM, zero, qc)], axis=0)
        s_sc[c % 2, 0:CTX_LEN] = lax.dot_general(kc_ref[0, :, kcols], q2, nt, preferred_element_type=F32)
        if local:
            s_sc[c % 2, CTX_LEN:] = lax.dot_general(k_ref[0, pl.ds(start, n_band * BLOCK), kcols], q2, nt,
                                                    preferred_element_type=F32) + bias2

    def softmax(c):
        s = s_sc[c % 2]
        sink_row = jnp.where(first_head, sink_ref[2 * c], sink_ref[2 * c + 1])
        m = jnp.maximum(jnp.max(s, axis=0, keepdims=True), sink_row)
        p_sc[c % 2] = jnp.exp2(s - m).astype(BF16)
        return jnp.exp2(sink_row - m)

    def values(c, params, sink_term):
        rows, blk0, _, _ = params
        hrows = slice((c // 2) * HEAD_DIM, (c // 2 + 1) * HEAD_DIM)
        vt = [vc_ref[i, hrows, :] for i in range(ctx_blocks)]
        if local:
            vt_band = v_ref[pl.ds(blk0, n_band), hrows, :]
            vt += [vt_band[i] for i in range(n_band)]
        vt_aug = jnp.concatenate([jnp.concatenate(vt, axis=1), vt_pad], axis=0)
        acc = jnp.dot(vt_aug, p_sc[c % 2], preferred_element_type=F32)
        out_t = acc[0:HEAD_DIM] / (acc[HEAD_DIM:HEAD_DIM + 1] + sink_term)
        both = jnp.concatenate([out_t[:, :BLOCK], out_t[:, BLOCK:]], axis=0)
        o_ref[rows, c * LANES:(c + 1) * LANES] = both.T.astype(o_ref.dtype)

    def query_block(qb, sink_term0):
        cur = block_params(qb)
        nxt = block_params(jnp.minimum(qb + 1, q_per_step - 1))
        sink_terms = {0: sink_term0}
        for c in range(n_chunks):
            if c + 2 < n_chunks:
                scores(c + 2, cur)
            else:
                scores(c + 2 - n_chunks, nxt)
            sink_terms[c + 1] = softmax((c + 1) % n_chunks)
            values(c, cur, sink_terms[c])
        return sink_terms[n_chunks]

    first = block_params(0)
    scores(0, first)
    scores(1, first)
    lax.fori_loop(0, q_per_step, query_block, softmax(0))


def _attn_scratch(n_keys):
    return [
        pltpu.VMEM((2, n_keys, 2 * BLOCK), F32),
        pltpu.VMEM((2, n_keys, 2 * BLOCK), BF16),
    ]


def _attn_call(sink2, q, k, vt, kc, vtc):
    nb = SEQ // BLOCK
    nbc = CTX_LEN // BLOCK
    qps = ATTN_Q_PER_STEP
    steps = nb // qps
    return pl.pallas_call(
        functools.partial(_attn_kernel, local=True, q_per_step=qps),
        grid=(BATCH, steps),
        in_specs=[
            pl.BlockSpec(memory_space=pltpu.SMEM),
            pl.BlockSpec((qps * BLOCK, D_Q), lambda b, j: (b * steps + j, 0)),
            pl.BlockSpec((1, SEQ, D_K2), lambda b, j: (b, 0, 0)),
            pl.BlockSpec((nb, D_KV, BLOCK), lambda b, j: (b, 0, 0)),
            pl.BlockSpec((1, CTX_LEN, D_K2), lambda b, j: (b, 0, 0)),
            pl.BlockSpec((nbc, D_KV, BLOCK), lambda b, j: (b, 0, 0)),
        ],
        out_specs=pl.BlockSpec((qps * BLOCK, D_Q), lambda b, j: (b * steps + j, 0)),
        out_shape=jax.ShapeDtypeStruct((BATCH * SEQ, D_Q), BF16),
        scratch_shapes=_attn_scratch(CTX_LEN + 3 * BLOCK),
        compiler_params=_params(("parallel", "arbitrary")),
        name="band_attn",
    )(sink2, q, k, vt, kc, vtc)


def _ctx_attn_call(sink2, qc, kc, vtc):
    nb = CTX_LEN // BLOCK
    return pl.pallas_call(
        functools.partial(_attn_kernel, local=False, q_per_step=nb),
        grid=(BATCH, 1),
        in_specs=[
            pl.BlockSpec(memory_space=pltpu.SMEM),
            pl.BlockSpec((CTX_LEN, D_Q), lambda b, j: (b, 0)),
            pl.BlockSpec((1, CTX_LEN, D_K2), lambda b, j: (b, 0, 0)),
            pl.BlockSpec((nb, D_KV, BLOCK), lambda b, j: (b, 0, 0)),
        ],
        out_specs=pl.BlockSpec((CTX_LEN, D_Q), lambda b, j: (b, 0)),
        out_shape=jax.ShapeDtypeStruct((BATCH * CTX_LEN, D_Q), BF16),
        scratch_shapes=_attn_scratch(CTX_LEN),
        compiler_params=_params(("parallel", "arbitrary")),
        name="ctx_attn",
    )(sink2, qc, kc, vtc)


def _to_time_major(src_ref, sc_ref, t0=0, nt=None):
    nt = src_ref.shape[1] if nt is None else nt
    r0 = t0 * SUBLANES
    for b in range(BATCH):
        for s in range(N_SLAB):
            sc_ref[s, pl.ds(r0 + b, nt, stride=SUBLANES), :] = src_ref[b, t0:t0 + nt, s * LANES:(s + 1) * LANES]
    return jnp.concatenate([sc_ref[s, r0:r0 + nt * SUBLANES, :] for s in range(N_SLAB)], axis=1)


def _from_time_major(val, sc_ref, dst_ref, t0, nt):
    r0 = t0 * SUBLANES
    for s in range(N_SLAB):
        sc_ref[s, r0:r0 + nt * SUBLANES, :] = val[:, s * LANES:(s + 1) * LANES]
    for b in range(BATCH):
        for s in range(N_SLAB):
            dst_ref[b, t0:t0 + nt, s * LANES:(s + 1) * LANES] = sc_ref[s, pl.ds(r0 + b, nt, stride=SUBLANES), :]


def _post_kernel(*refs, lru):
    if lru:
        (x_ref, gate_ref, yf_ref, yb_ref, mod_ref, g_ref, wf_ref, w1_ref, w2_ref, o_ref,
         x1_sc, h_sc, acc_sc, tin_sc, tout_sc) = refs
        half_rows = gate_ref.shape[0] // 2
    else:
        x_ref, a_ref, mod_ref, g_ref, wf_ref, w1_ref, w2_ref, o_ref, x1_sc, h_sc, acc_sc = refs
        half_rows = x_ref.shape[0] // 2
    half_t = half_rows // BATCH

    def head(r):
        rs = slice(r * half_rows, (r + 1) * half_rows)
        if lru:
            front = (gate_ref[rs, :].astype(F32)
                     * (yf_ref[rs, :].astype(F32) + yb_ref[rs, :].astype(F32))).astype(BF16)
            x = _to_time_major(x_ref, tin_sc, r * half_t, half_t)
        else:
            front = a_ref[rs, :]
            x = x_ref[rs, :]
        y = jnp.dot(front, wf_ref[...], preferred_element_type=F32)
        x1 = _gated_add(x, mod_ref[0, 2], _rms(y, g_ref[1:2, :]))
        x1_sc[r] = x1
        h_sc[r] = _modulate(_rms(x1, g_ref[2:3, :]), mod_ref[0, 3], mod_ref[0, 4]).astype(BF16)

    def mlp(r):
        acc = jnp.zeros((half_rows, D_MODEL), F32)
        for c in range(D_FF // FF_CHUNK):
            hid = jnp.dot(h_sc[r], w1_ref[:, c * FF_CHUNK:(c + 1) * FF_CHUNK], preferred_element_type=F32)
            hid = jnp.square(jnp.maximum(hid, 0.0)).astype(BF16)
            acc = acc + jnp.dot(hid, w2_ref[c * FF_CHUNK:(c + 1) * FF_CHUNK, :], preferred_element_type=F32)
        acc_sc[r] = acc

    def tail(r):
        out = _gated_add(x1_sc[r], mod_ref[0, 5], _rms(acc_sc[r], g_ref[3:4, :]))
        if lru:
            _from_time_major(out, tout_sc, o_ref, r * half_t, half_t)
        else:
            o_ref[r * half_rows:(r + 1) * half_rows, :] = out

    head(0)
    head(1)
    mlp(0)
    tail(0)
    mlp(1)
    tail(1)


def _post_call(x, fronts, mod, g, w_front, w1, w2, layer, tiles_per_group, lru):
    tm = TOKEN_TILE
    layer_spec = lambda shape: pl.BlockSpec((None,) + shape[1:], lambda i: (layer, 0, 0),
                                            pipeline_mode=pl.Buffered(1))
    row = lambda i: (i, 0)
    if lru:
        nt = tm // BATCH
        n = x.shape[0] * x.shape[1]
        x_spec = pl.BlockSpec((BATCH, nt, D_MODEL), lambda i: (0, i, 0))
        scratch = [pltpu.VMEM((N_SLAB, tm, LANES), F32), pltpu.VMEM((N_SLAB, tm, LANES), F32)]
    else:
        n = x.shape[0]
        x_spec = pl.BlockSpec((tm, D_MODEL), row)
        scratch = []
    scratch = [
        pltpu.VMEM((2, tm // 2, D_MODEL), F32),
        pltpu.VMEM((2, tm // 2, D_MODEL), BF16),
        pltpu.VMEM((2, tm // 2, D_MODEL), F32),
    ] + scratch
    in_specs = [x_spec]
    in_specs += [pl.BlockSpec((tm, f.shape[1]), row) for f in fronts]
    in_specs += [
        pl.BlockSpec((1, N_MOD, SUBLANES, D_MODEL), lambda i: (i // tiles_per_group, 0, 0, 0)),
        _const_spec((4, D_MODEL)),
        _const_spec(w_front.shape),
        layer_spec(w1.shape),
        layer_spec(w2.shape),
    ]
    return pl.pallas_call(
        functools.partial(_post_kernel, lru=lru),
        grid=(n // tm,),
        in_specs=in_specs,
        out_specs=x_spec,
        out_shape=jax.ShapeDtypeStruct(x.shape, F32),
        scratch_shapes=scratch,
        compiler_params=_params(("parallel",)),
        name="lru_out_mlp" if lru else "attn_out_mlp",
    )(x, *fronts, mod, g, w_front, w1, w2)


def _lru_in_kernel(*refs, need_gate):
    if need_gate:
        (x_ref, xp_ref, xn_ref, mod_ref, g_ref, w_ref, cw_ref, cb_ref, gate_ref, u_ref,
         v_sc, h_sc, t_sc, tp_sc, tn_sc) = refs
    else:
        x_ref, xp_ref, xn_ref, mod_ref, g_ref, w_ref, cw_ref, cb_ref, u_ref, v_sc, h_sc, t_sc, tp_sc, tn_sc = refs
    i = pl.program_id(0)
    n = pl.num_programs(0)
    rows = x_ref.shape[0] * x_ref.shape[1]
    half = rows // 2
    half_t = x_ref.shape[1] // 2
    s8 = SUBLANES

    def pre(x):
        return _modulate(_rms(x, g_ref[0:1, :]), mod_ref[0, 0], mod_ref[0, 1]).astype(BF16)

    h_sc[0, 0:HALO] = pre(_to_time_major(xp_ref, tp_sc)[SUBLANES * SUBLANES - HALO:])
    h_sc[0, HALO:] = pre(_to_time_major(x_ref, t_sc, 0, half_t))
    h_sc[1, 0:half] = pre(_to_time_major(x_ref, t_sc, half_t, half_t))
    h_sc[1, half:] = pre(_to_time_major(xn_ref, tn_sc)[:HALO])
    ext = half + HALO
    for r in range(2):
        v_sc[r * ext:(r + 1) * ext] = jnp.dot(h_sc[r], w_ref[:, D_RNN:], preferred_element_type=F32)
    v_sc[0:HALO] = v_sc[0:HALO] * (i > 0).astype(F32)
    v_sc[HALO + rows:HALO + rows + s8] = v_sc[HALO + rows:HALO + rows + s8] * (i < n - 1).astype(F32)
    for r in range(2):
        if need_gate:
            h_r = h_sc[0, HALO:] if r == 0 else h_sc[1, 0:half]
            gate_ref[r * half:(r + 1) * half, :] = jax.nn.gelu(
                jnp.dot(h_r, w_ref[:, :D_RNN], preferred_element_type=F32)).astype(BF16)
        base = HALO + r * half
        u_ref[r * half:(r + 1) * half, :] = (
            cb_ref[...]
            + cw_ref[0:1, :] * v_sc[base - 2 * s8:base - 2 * s8 + half]
            + cw_ref[1:2, :] * v_sc[base - s8:base - s8 + half]
            + cw_ref[2:3, :] * v_sc[base:base + half]
            + cw_ref[3:4, :] * v_sc[base + s8:base + s8 + half]).astype(u_ref.dtype)


def _lru_in_call(x3, mod, g, w_in, conv_w, conv_b, need_gate):
    t_total = x3.shape[1]
    n = BATCH * t_total
    tm = TOKEN_TILE
    nt = tm // BATCH
    row = lambda i: (i, 0)
    per_tile = nt // SUBLANES
    last = t_total // SUBLANES - 1
    halo_spec = lambda f: pl.BlockSpec((BATCH, SUBLANES, D_MODEL), f)
    out_specs = [pl.BlockSpec((tm, D_RNN), row), pl.BlockSpec((tm, D_RNN), row)]
    out_shape = [jax.ShapeDtypeStruct((n, D_RNN), BF16), jax.ShapeDtypeStruct((n, D_RNN), U_DTYPE)]
    if not need_gate:
        out_specs, out_shape = out_specs[1:], out_shape[1:]
    return pl.pallas_call(
        functools.partial(_lru_in_kernel, need_gate=need_gate),
        grid=(n // tm,),
        in_specs=[
            pl.BlockSpec((BATCH, nt, D_MODEL), lambda i: (0, i, 0)),
            halo_spec(lambda i: (0, jnp.maximum(i * per_tile - 1, 0), 0)),
            halo_spec(lambda i: (0, jnp.minimum((i + 1) * per_tile, last), 0)),
            _const_spec((1, N_MOD, SUBLANES, D_MODEL)),
            _const_spec((4, D_MODEL)),
            _const_spec((D_MODEL, 2 * D_RNN)),
            _const_spec((CONV_W, D_RNN)),
            _const_spec((1, D_RNN)),
        ],
        out_specs=out_specs,
        out_shape=out_shape,
        scratch_shapes=[
            pltpu.VMEM((tm + 2 * HALO, D_RNN), F32),
            pltpu.VMEM((2, tm // 2 + HALO, D_MODEL), BF16),
            pltpu.VMEM((N_SLAB, tm, LANES), F32),
            pltpu.VMEM((N_SLAB, SUBLANES * SUBLANES, LANES), F32),
            pltpu.VMEM((N_SLAB, SUBLANES * SUBLANES, LANES), F32),
        ],
        compiler_params=_params(("parallel",)),
        name="lru_in",
    )(x3, x3, x3, mod, g, w_in, conv_w, conv_b)


def _scan_kernel(uf_ref, ub_ref, h0_ref, wa_ref, ba_ref, wi_ref, bi_ref, lam_ref, yf_ref, yb_ref, ht_ref,
                 a_sc, bx_sc, h_sc):
    i = pl.program_id(0)
    n = pl.num_programs(0)
    rows = uf_ref.shape[0]
    nt = rows // SUBLANES
    s8 = SUBLANES

    @pl.when(i == 0)
    def _():
        h_sc[...] = h0_ref[...]

    for d, u_ref in enumerate((uf_ref, ub_ref)):
        for c in range(N_LRU_BLOCKS):
            cs = slice(c * LRU_BLOCK_W, (c + 1) * LRU_BLOCK_W)
            u16 = u_ref[:, cs].astype(BF16)
            u = u_ref[:, cs].astype(F32)
            ta = jnp.tanh(jnp.dot(u16, wa_ref[d, c], preferred_element_type=F32) + 0.5 * ba_ref[d, :, cs])
            ti = jnp.tanh(jnp.dot(u16, wi_ref[d, c], preferred_element_type=F32) + 0.5 * bi_ref[d, :, cs])
            neg_lam = -lam_ref[d, :, cs]
            softplus = jnp.maximum(neg_lam, 0.0) + jnp.log1p(jnp.exp(-jnp.abs(neg_lam)))
            k = (-0.5 * LRU_C * LOG2E) * softplus
            a = jnp.exp2(k * ta + k)
            w = 1.0 - a * a
            root = w * lax.rsqrt(jnp.maximum(w, 1e-30))
            a_sc[d, :, cs] = a
            bx_sc[d, :, cs] = root * (ti * u + u)

    def step(t, carry):
        hf, hb = carry
        rf = pl.multiple_of(t * 2 * s8, 2 * s8)
        rb = pl.multiple_of((nt - 2 - 2 * t) * s8, 2 * s8)
        hf1 = a_sc[0, pl.ds(rf, s8), :] * hf + bx_sc[0, pl.ds(rf, s8), :]
        hf2 = a_sc[0, pl.ds(rf + s8, s8), :] * hf1 + bx_sc[0, pl.ds(rf + s8, s8), :]
        yf_ref[pl.ds(rf, 2 * s8), :] = jnp.concatenate([hf1, hf2], axis=0).astype(yf_ref.dtype)
        hb1 = a_sc[1, pl.ds(rb + s8, s8), :] * hb + bx_sc[1, pl.ds(rb + s8, s8), :]
        hb2 = a_sc[1, pl.ds(rb, s8), :] * hb1 + bx_sc[1, pl.ds(rb, s8), :]
        yb_ref[pl.ds(rb, 2 * s8), :] = jnp.concatenate([hb2, hb1], axis=0).astype(yb_ref.dtype)
        return hf2, hb2

    hf, hb = lax.fori_loop(0, nt // 2, step, (h_sc[0], h_sc[1]), unroll=2)
    h_sc[0] = hf
    h_sc[1] = hb

    @pl.when(i == n - 1)
    def _():
        ht_ref[...] = h_sc[...]


def _scan_call(u2, h0, w_a, b_a, w_i, b_i, lam):
    rows_total = u2.shape[0]
    rows = SCAN_T * SUBLANES
    n = rows_total // rows
    w = D_RNN
    fwd = lambda i: (i, 0)
    bwd = lambda i: (n - 1 - i, 0)
    return pl.pallas_call(
        _scan_kernel,
        grid=(n,),
        in_specs=[
            pl.BlockSpec((rows, w), fwd),
            pl.BlockSpec((rows, w), bwd),
            _const_spec((2, SUBLANES, w)),
            _const_spec((2, N_LRU_BLOCKS, LRU_BLOCK_W, LRU_BLOCK_W)),
            _const_spec((2, 1, w)),
            _const_spec((2, N_LRU_BLOCKS, LRU_BLOCK_W, LRU_BLOCK_W)),
            _const_spec((2, 1, w)),
            _const_spec((2, 1, w)),
        ],
        out_specs=[
            pl.BlockSpec((rows, w), fwd),
            pl.BlockSpec((rows, w), bwd),
            pl.BlockSpec((2, SUBLANES, w), lambda i: (0, 0, 0)),
        ],
        out_shape=[
            jax.ShapeDtypeStruct((rows_total, w), Y_DTYPE),
            jax.ShapeDtypeStruct((rows_total, w), Y_DTYPE),
            jax.ShapeDtypeStruct((2, SUBLANES, w), F32),
        ],
        scratch_shapes=[
            pltpu.VMEM((2, rows, w), F32),
            pltpu.VMEM((2, rows, w), F32),
            pltpu.VMEM((2, SUBLANES, w), F32),
        ],
        compiler_params=_params(("arbitrary",)),
        name="lru_scan",
    )(u2, u2, h0, w_a, b_a, w_i, b_i, lam)


def _rope_tables():
    t = np.arange(SEQ)
    row = (t // GRID_W).astype(np.float64)
    col = (t % GRID_W).astype(np.float64)
    half = HEAD_DIM // 2
    inv = ROPE_BASE ** (-np.arange(0, half, 2, dtype=np.float64) / half)
    ang_r = row[:, None] * inv[None, :]
    ang_c = col[:, None] * inv[None, :]
    ang = np.concatenate([ang_r, ang_r, ang_c, ang_c], axis=-1)
    ang = np.tile(ang, (1, LANES // HEAD_DIM))
    low = (np.arange(LANES) % 32) < 16
    sin = np.sin(ang)
    tables = (np.cos(ang), np.where(low, -sin, 0.0), np.where(low, 0.0, sin))
    return tuple(jnp.asarray(a, dtype=F32) for a in tables)


def kernel(x, c, ctx, c_ctx, ada_w, ada_b, norm_g, mlp_w1, mlp_w2, attn_w_qkv, attn_w_o, attn_sink,
           lru_w_in, lru_conv_w, lru_conv_b, lru_w_a, lru_b_a, lru_w_i, lru_b_i, lru_lam, lru_w_out):
    n_lat = BATCH * SEQ
    n_ctx = BATCH * CTX_LEN

    c16 = jnp.zeros((16, D_MODEL), F32).at[:BATCH].set(c).at[BATCH].set(c_ctx)
    mods = _mod_call(c16, ada_w, ada_b).reshape(2, 16, N_MOD, D_MODEL)

    def slab_bmajor(m):
        return jnp.broadcast_to(m[:, :, None, :], (BATCH, N_MOD, SUBLANES, D_MODEL))

    def slab_ctx(m):
        return jnp.broadcast_to(m[None, :, None, :], (1, N_MOD, SUBLANES, D_MODEL))

    mod_x0 = slab_bmajor(mods[0, :BATCH])
    mod_c0 = slab_ctx(mods[0, BATCH])
    w_qkv = attn_w_qkv[0]
    w_qkv = jnp.concatenate([w_qkv[:, :D_Q] * (HEAD_DIM ** -0.5 * LOG2E), w_qkv[:, D_Q:]], axis=1).astype(BF16)
    sink2 = attn_sink[0] * LOG2E
    w_o = attn_w_o[0].astype(BF16)
    w1_all, w2_all = mlp_w1.astype(BF16), mlp_w2.astype(BF16)
    g0 = norm_g[0]
    tiles_per_batch = SEQ // TOKEN_TILE

    x2 = x.reshape(n_lat, D_MODEL)
    c2 = ctx.reshape(n_ctx, D_MODEL)
    q, k, v = _qkv_call(x2, mod_x0, g0, w_qkv, _rope_tables(), tiles_per_batch)
    qc, kc, vc = _qkv_call(c2, mod_c0, g0, w_qkv, None, n_ctx // TOKEN_TILE)
    kc3 = kc.reshape(BATCH, CTX_LEN, D_K2)
    att = _attn_call(sink2, q, k.reshape(BATCH, SEQ, D_K2), v, kc3, vc)
    att_c = _ctx_attn_call(sink2, qc, kc3, vc)
    x2 = _post_call(x2, [att], mod_x0, g0, w_o, w1_all, w2_all, 0, tiles_per_batch, lru=False)
    c2 = _post_call(c2, [att_c], mod_c0, g0, w_o, w1_all, w2_all, 0, n_ctx // TOKEN_TILE, lru=False)

    x3 = x2.reshape(BATCH, SEQ, D_MODEL)
    c3 = c2.reshape(BATCH, CTX_LEN, D_MODEL)
    mod_x1 = mods[1, :BATCH].transpose(1, 0, 2)[None]
    mod_c1 = slab_ctx(mods[1, BATCH])
    g1 = norm_g[1]
    w_in = lru_w_in[0].astype(BF16)
    conv_w = 0.5 * lru_conv_w[0]
    conv_b = 0.5 * lru_conv_b[0].reshape(1, D_RNN)
    w_a, w_i = lru_w_a[0].astype(BF16), lru_w_i[0].astype(BF16)
    b_a, b_i = lru_b_a[0].reshape(2, 1, D_RNN), lru_b_i[0].reshape(2, 1, D_RNN)
    lam = lru_lam[0].reshape(2, 1, D_RNN)
    scan = functools.partial(_scan_call, w_a=w_a, b_a=b_a, w_i=w_i, b_i=b_i, lam=lam)

    (u_c,) = _lru_in_call(c3, mod_c1, g1, w_in, conv_w, conv_b, need_gate=False)
    _, _, h_ctx = scan(u_c, jnp.zeros((2, SUBLANES, D_RNN), F32))
    gate_x, u_x = _lru_in_call(x3, mod_x1, g1, w_in, conv_w, conv_b, need_gate=True)
    yf, yb, _ = scan(u_x, h_ctx)
    return _post_call(x3, [gate_x, yf, yb], mod_x1, g1, lru_w_out[0].astype(BF16),
                      w1_all, w2_all, 1, n_lat // TOKEN_TILE, lru=True)
```

```python
import functools

import jax
import jax.numpy as jnp
import numpy as np
from jax import lax
from jax.experimental import pallas as pl
from jax.experimental.pallas import tpu as pltpu

D_MODEL = 1024
BATCH = 8
SEQ = 2048
GRID_W = 64
CTX_LEN = 256
HEAD_DIM = 64
N_HEADS = 16
N_KV_HEADS = 4
GQA_GROUP = N_HEADS // N_KV_HEADS
WINDOW = 128
BLOCK = 128
ROPE_BASE = 10000.0
D_RNN = 1280
LRU_BLOCK_W = 256
N_LRU_BLOCKS = D_RNN // LRU_BLOCK_W
CONV_W = 4
LRU_C = 8.0
D_FF = 4 * D_MODEL
N_MOD = 6
EPS = 1e-6
NEG_INF = -1e30

D_Q = N_HEADS * HEAD_DIM
D_KV = N_KV_HEADS * HEAD_DIM
D_K2 = 2 * D_KV
LANES = 128
SUBLANES = 8
N_SLAB = D_MODEL // LANES
TOKEN_TILE = 512
FF_CHUNK = 1024
ATTN_Q_PER_STEP = 16
SCAN_T = 128
HALO = 16
U_DTYPE = jnp.bfloat16
Y_DTYPE = jnp.bfloat16
LOG2E = 1.4426950408889634
VMEM_LIMIT = 60 * 1024 * 1024

F32 = jnp.float32
BF16 = jnp.bfloat16


def _rms(x, g):
    ms = jnp.mean(x * x, axis=-1, keepdims=True)
    return x * lax.rsqrt(ms + EPS) * g


def _slab(x):
    return x.reshape(x.shape[0] // SUBLANES, SUBLANES, x.shape[1])


def _modulate(h, shift8, scale8):
    out = _slab(h) * (1.0 + scale8)[None] + shift8[None]
    return out.reshape(h.shape)


def _gated_add(x, gate8, y):
    out = _slab(x) + gate8[None] * _slab(y)
    return out.reshape(x.shape)


def _const_spec(shape):
    n = len(shape)
    return pl.BlockSpec(shape, lambda *_: (0,) * n, pipeline_mode=pl.Buffered(1))


def _params(sem):
    return pltpu.CompilerParams(dimension_semantics=sem, vmem_limit_bytes=VMEM_LIMIT)


def _mod_kernel(c_ref, w_ref, b_ref, o_ref):
    s = jax.nn.silu(c_ref[...]).astype(BF16)
    o_ref[0] = jnp.dot(s, w_ref[0].astype(BF16), preferred_element_type=F32) + b_ref[0]


def _mod_call(c16, ada_w, ada_b):
    depth = ada_w.shape[0]
    nt = 1024
    return pl.pallas_call(
        _mod_kernel,
        grid=(depth, N_MOD * D_MODEL // nt),
        in_specs=[
            pl.BlockSpec((16, D_MODEL), lambda l, j: (0, 0)),
            pl.BlockSpec((1, D_MODEL, nt), lambda l, j: (l, 0, j)),
            pl.BlockSpec((1, 1, nt), lambda l, j: (l, 0, j)),
        ],
        out_specs=pl.BlockSpec((1, 16, nt), lambda l, j: (l, 0, j)),
        out_shape=jax.ShapeDtypeStruct((depth, 16, N_MOD * D_MODEL), F32),
        compiler_params=_params(("arbitrary", "arbitrary")),
        name="adaln_mod",
    )(c16, ada_w, ada_b.reshape(depth, 1, N_MOD * D_MODEL))


def _qkv_kernel(*refs, rope):
    if rope:
        x_ref, mod_ref, g_ref, w_ref, cos_ref, sa_ref, sb_ref, q_ref, k_ref, v_ref, h_sc, y_sc = refs
    else:
        x_ref, mod_ref, g_ref, w_ref, q_ref, k_ref, v_ref, h_sc, y_sc = refs
    half = x_ref.shape[0] // 2
    low = lax.broadcasted_iota(jnp.int32, (half, LANES), 1) < HEAD_DIM
    for r in range(2):
        rs = slice(r * half, (r + 1) * half)
        h_sc[r] = _modulate(_rms(x_ref[rs, :], g_ref[0:1, :]), mod_ref[0, 0], mod_ref[0, 1]).astype(BF16)
    for r in range(2):
        y_sc[r] = jnp.dot(h_sc[r], w_ref[...], preferred_element_type=F32)
    for r in range(2):
        rs = slice(r * half, (r + 1) * half)
        if rope:
            cos, sa, sb = cos_ref[rs, :], sa_ref[rs, :], sb_ref[rs, :]
        for c in range((D_Q + D_KV) // LANES):
            yc = y_sc[r, :, c * LANES:(c + 1) * LANES]
            if rope:
                yc = yc * cos + pltpu.roll(yc, LANES - 16, 1) * sa + pltpu.roll(yc, 16, 1) * sb
            if c < D_Q // LANES:
                q_ref[rs, c * LANES:(c + 1) * LANES] = yc.astype(BF16)
            else:
                c2 = 2 * (c - D_Q // LANES)
                swapped = pltpu.roll(yc, HEAD_DIM, 1)
                k_ref[rs, c2 * LANES:(c2 + 1) * LANES] = jnp.where(low, yc, swapped).astype(BF16)
                k_ref[rs, (c2 + 1) * LANES:(c2 + 2) * LANES] = jnp.where(low, swapped, yc).astype(BF16)
        for blk in range(half // BLOCK):
            v_ref[r * (half // BLOCK) + blk] = y_sc[r, blk * BLOCK:(blk + 1) * BLOCK, D_Q + D_KV:].T.astype(BF16)


def _qkv_call(x2, mod, g, w_qkv, tables, tiles_per_group):
    n = x2.shape[0]
    tm = TOKEN_TILE
    rope = tables is not None
    in_specs = [
        pl.BlockSpec((tm, D_MODEL), lambda i: (i, 0)),
        pl.BlockSpec((1, N_MOD, SUBLANES, D_MODEL), lambda i: (i // tiles_per_group, 0, 0, 0)),
        _const_spec((4, D_MODEL)),
        _const_spec((D_MODEL, D_Q + 2 * D_KV)),
    ]
    args = [x2, mod, g, w_qkv]
    if rope:
        nt = SEQ // tm
        in_specs += [pl.BlockSpec((tm, LANES), lambda i: (i % nt, 0))] * 3
        args += list(tables)
    return pl.pallas_call(
        functools.partial(_qkv_kernel, rope=rope),
        grid=(n // tm,),
        in_specs=in_specs,
        out_specs=[
            pl.BlockSpec((tm, D_Q), lambda i: (i, 0)),
            pl.BlockSpec((tm, D_K2), lambda i: (i, 0)),
            pl.BlockSpec((tm // BLOCK, D_KV, BLOCK), lambda i: (i, 0, 0)),
        ],
        out_shape=[
            jax.ShapeDtypeStruct((n, D_Q), BF16),
            jax.ShapeDtypeStruct((n, D_K2), BF16),
            jax.ShapeDtypeStruct((n // BLOCK, D_KV, BLOCK), BF16),
        ],
        scratch_shapes=[
            pltpu.VMEM((2, tm // 2, D_MODEL), BF16),
            pltpu.VMEM((2, tm // 2, D_Q + 2 * D_KV), F32),
        ],
        compiler_params=_params(("parallel",)),
        name="qkv_rope" if rope else "qkv_ctx",
    )(*args)


VT_ROWS = HEAD_DIM + 16


def _attn_kernel(*refs, local, q_per_step):
    if local:
        sink_ref, q_ref, k_ref, v_ref, kc_ref, vc_ref, o_ref, s_sc, p_sc = refs
    else:
        sink_ref, q_ref, kc_ref, vc_ref, o_ref, s_sc, p_sc = refs
    j = pl.program_id(1)
    seq_blocks = SEQ // BLOCK if local else 0
    ctx_blocks = CTX_LEN // BLOCK
    n_keys = s_sc.shape[1]
    pad_row = lax.broadcasted_iota(jnp.int32, (VT_ROWS - HEAD_DIM, n_keys), 0)
    vt_pad = jnp.where(pad_row == 0, 1.0, 0.0).astype(BF16)

    nt = (((1,), (1,)), ((), ()))
    n_band = 3
    n_chunks = D_Q // LANES
    lane = lax.broadcasted_iota(jnp.int32, (BLOCK, LANES), 1)
    first_head = lax.broadcasted_iota(jnp.int32, (1, 2 * BLOCK), 1) < BLOCK

    def block_params(qb):
        jq = j * q_per_step + qb
        rows = pl.ds(pl.multiple_of(qb * BLOCK, BLOCK), BLOCK)
        if not local:
            return rows, None, None, None, qb // ctx_blocks
        blk0 = jnp.clip(jq - 1, 0, seq_blocks - n_band)
        start = pl.multiple_of(blk0 * BLOCK, BLOCK)
        kpos = start + lax.broadcasted_iota(jnp.int32, (n_band * BLOCK, BLOCK), 0)
        qpos = jq * BLOCK + lax.broadcasted_iota(jnp.int32, (n_band * BLOCK, BLOCK), 1)
        bias = jnp.where(jnp.abs(kpos - qpos) <= WINDOW, 0.0, NEG_INF).astype(F32)
        return rows, blk0, start, jnp.concatenate([bias, bias], axis=1), 0

    def scores(c, params):
        rows, _, start, bias2, cb = params
        kcols = slice((c // 2) * LANES, (c // 2 + 1) * LANES)
        qc = q_ref[rows, c * LANES:(c + 1) * LANES]
        zero = jnp.zeros_like(qc)
        q2 = jnp.concatenate([jnp.where(lane < HEAD_DIM, qc, zero), jnp.where(lane < HEAD_DIM, zero, qc)], axis=0)
        s_sc[c % 2, 0:CTX_LEN] = lax.dot_general(kc_ref[cb, :, kcols], q2, nt, preferred_element_type=F32)
        if local:
            s_sc[c % 2, CTX_LEN:] = lax.dot_general(k_ref[0, pl.ds(start, n_band * BLOCK), kcols], q2, nt,
                                                    preferred_element_type=F32) + bias2

    def softmax(c):
        s = s_sc[c % 2]
        sink_row = jnp.where(first_head, sink_ref[2 * c], sink_ref[2 * c + 1])
        m = jnp.maximum(jnp.max(s, axis=0, keepdims=True), sink_row)
        p_sc[c % 2] = jnp.exp2(s - m).astype(BF16)
        return jnp.exp2(sink_row - m)

    def values(c, params, sink_term):
        rows, blk0, _, _, cb = params
        hrows = slice((c // 2) * HEAD_DIM, (c // 2 + 1) * HEAD_DIM)
        vt = [vc_ref[cb * ctx_blocks + i, hrows, :] for i in range(ctx_blocks)]
        if local:
            vt_band = v_ref[pl.ds(blk0, n_band), hrows, :]
            vt += [vt_band[i] for i in range(n_band)]
        vt_aug = jnp.concatenate([jnp.concatenate(vt, axis=1), vt_pad], axis=0)
        acc = jnp.dot(vt_aug, p_sc[c % 2], preferred_element_type=F32)
        out_t = acc[0:HEAD_DIM] / (acc[HEAD_DIM:HEAD_DIM + 1] + sink_term)
        both = jnp.concatenate([out_t[:, :BLOCK], out_t[:, BLOCK:]], axis=0)
        o_ref[rows, c * LANES:(c + 1) * LANES] = both.T.astype(o_ref.dtype)

    def query_block(qb, sink_term0):
        cur = block_params(qb)
        nxt = block_params(jnp.minimum(qb + 1, q_per_step - 1))
        sink_terms = {0: sink_term0}
        for c in range(n_chunks):
            if c + 2 < n_chunks:
                scores(c + 2, cur)
            else:
                scores(c + 2 - n_chunks, nxt)
            sink_terms[c + 1] = softmax((c + 1) % n_chunks)
            values(c, cur, sink_terms[c])
        return sink_terms[n_chunks]

    first = block_params(0)
    scores(0, first)
    scores(1, first)
    lax.fori_loop(0, q_per_step, query_block, softmax(0))


def _attn_scratch(n_keys):
    return [
        pltpu.VMEM((2, n_keys, 2 * BLOCK), F32),
        pltpu.VMEM((2, n_keys, 2 * BLOCK), BF16),
    ]


def _attn_call(sink2, q, k, vt, kc, vtc):
    nb = SEQ // BLOCK
    nbc = CTX_LEN // BLOCK
    qps = ATTN_Q_PER_STEP
    steps = nb // qps
    return pl.pallas_call(
        functools.partial(_attn_kernel, local=True, q_per_step=qps),
        grid=(BATCH, steps),
        in_specs=[
            pl.BlockSpec(memory_space=pltpu.SMEM),
            pl.BlockSpec((qps * BLOCK, D_Q), lambda b, j: (b * steps + j, 0)),
            pl.BlockSpec((1, SEQ, D_K2), lambda b, j: (b, 0, 0)),
            pl.BlockSpec((nb, D_KV, BLOCK), lambda b, j: (b, 0, 0)),
            pl.BlockSpec((1, CTX_LEN, D_K2), lambda b, j: (b, 0, 0)),
            pl.BlockSpec((nbc, D_KV, BLOCK), lambda b, j: (b, 0, 0)),
        ],
        out_specs=pl.BlockSpec((qps * BLOCK, D_Q), lambda b, j: (b * steps + j, 0)),
        out_shape=jax.ShapeDtypeStruct((BATCH * SEQ, D_Q), BF16),
        scratch_shapes=_attn_scratch(CTX_LEN + 3 * BLOCK),
        compiler_params=_params(("parallel", "arbitrary")),
        name="band_attn",
    )(sink2, q, k, vt, kc, vtc)


def _ctx_attn_call(sink2, qc, kc, vtc):
    nb = BATCH * CTX_LEN // BLOCK
    whole = lambda shape: pl.BlockSpec(shape, lambda b, j: (0,) * len(shape))
    return pl.pallas_call(
        functools.partial(_attn_kernel, local=False, q_per_step=nb),
        grid=(1, 1),
        in_specs=[
            pl.BlockSpec(memory_space=pltpu.SMEM),
            whole((BATCH * CTX_LEN, D_Q)),
            whole((BATCH, CTX_LEN, D_K2)),
            whole((nb, D_KV, BLOCK)),
        ],
        out_specs=whole((BATCH * CTX_LEN, D_Q)),
        out_shape=jax.ShapeDtypeStruct((BATCH * CTX_LEN, D_Q), BF16),
        scratch_shapes=_attn_scratch(CTX_LEN),
        compiler_params=_params(("arbitrary", "arbitrary")),
        name="ctx_attn",
    )(sink2, qc, kc, vtc)


def _to_time_major(src_ref, sc_ref, t0=0, nt=None):
    nt = src_ref.shape[1] if nt is None else nt
    r0 = t0 * SUBLANES
    for b in range(BATCH):
        for s in range(N_SLAB):
            sc_ref[s, pl.ds(r0 + b, nt, stride=SUBLANES), :] = src_ref[b, t0:t0 + nt, s * LANES:(s + 1) * LANES]
    return jnp.concatenate([sc_ref[s, r0:r0 + nt * SUBLANES, :] for s in range(N_SLAB)], axis=1)


def _from_time_major(val, sc_ref, dst_ref, t0, nt):
    r0 = t0 * SUBLANES
    for s in range(N_SLAB):
        sc_ref[s, r0:r0 + nt * SUBLANES, :] = val[:, s * LANES:(s + 1) * LANES]
    for b in range(BATCH):
        for s in range(N_SLAB):
            dst_ref[b, t0:t0 + nt, s * LANES:(s + 1) * LANES] = sc_ref[s, pl.ds(r0 + b, nt, stride=SUBLANES), :]


def _post_kernel(*refs, lru):
    if lru:
        (x_ref, gate_ref, yf_ref, yb_ref, mod_ref, g_ref, wf_ref, w1_ref, w2_ref, o_ref,
         x1_sc, h_sc, acc_sc, tin_sc, tout_sc) = refs
        half_rows = gate_ref.shape[0] // 2
    else:
        x_ref, a_ref, mod_ref, g_ref, wf_ref, w1_ref, w2_ref, o_ref, x1_sc, h_sc, acc_sc = refs
        half_rows = x_ref.shape[0] // 2
    half_t = half_rows // BATCH

    def head(r):
        rs = slice(r * half_rows, (r + 1) * half_rows)
        if lru:
            front = (gate_ref[rs, :].astype(F32)
                     * (yf_ref[rs, :].astype(F32) + yb_ref[rs, :].astype(F32))).astype(BF16)
            x = _to_time_major(x_ref, tin_sc, r * half_t, half_t)
        else:
            front = a_ref[rs, :]
            x = x_ref[rs, :]
        y = jnp.dot(front, wf_ref[...], preferred_element_type=F32)
        x1 = _gated_add(x, mod_ref[0, 2], _rms(y, g_ref[1:2, :]))
        x1_sc[r] = x1
        h_sc[r] = _modulate(_rms(x1, g_ref[2:3, :]), mod_ref[0, 3], mod_ref[0, 4]).astype(BF16)

    def mlp(r):
        acc = jnp.zeros((half_rows, D_MODEL), F32)
        for c in range(D_FF // FF_CHUNK):
            hid = jnp.dot(h_sc[r], w1_ref[:, c * FF_CHUNK:(c + 1) * FF_CHUNK], preferred_element_type=F32)
            hid = jnp.square(jnp.maximum(hid, 0.0)).astype(BF16)
            acc = acc + jnp.dot(hid, w2_ref[c * FF_CHUNK:(c + 1) * FF_CHUNK, :], preferred_element_type=F32)
        acc_sc[r] = acc

    def tail(r):
        out = _gated_add(x1_sc[r], mod_ref[0, 5], _rms(acc_sc[r], g_ref[3:4, :]))
        if lru:
            _from_time_major(out, tout_sc, o_ref, r * half_t, half_t)
        else:
            o_ref[r * half_rows:(r + 1) * half_rows, :] = out

    head(0)
    head(1)
    mlp(0)
    tail(0)
    mlp(1)
    tail(1)


def _post_call(x, fronts, mod, g, w_front, w1, w2, layer, tiles_per_group, lru):
    tm = TOKEN_TILE
    layer_spec = lambda shape: pl.BlockSpec((None,) + shape[1:], lambda i: (layer, 0, 0),
                                            pipeline_mode=pl.Buffered(1))
    row = lambda i: (i, 0)
    if lru:
        nt = tm // BATCH
        n = x.shape[0] * x.shape[1]
        x_spec = pl.BlockSpec((BATCH, nt, D_MODEL), lambda i: (0, i, 0))
        scratch = [pltpu.VMEM((N_SLAB, tm, LANES), F32), pltpu.VMEM((N_SLAB, tm, LANES), F32)]
    else:
        n = x.shape[0]
        x_spec = pl.BlockSpec((tm, D_MODEL), row)
        scratch = []
    scratch = [
        pltpu.VMEM((2, tm // 2, D_MODEL), F32),
        pltpu.VMEM((2, tm // 2, D_MODEL), BF16),
        pltpu.VMEM((2, tm // 2, D_MODEL), F32),
    ] + scratch
    in_specs = [x_spec]
    in_specs += [pl.BlockSpec((tm, f.shape[1]), row) for f in fronts]
    in_specs += [
        pl.BlockSpec((1, N_MOD, SUBLANES, D_MODEL), lambda i: (i // tiles_per_group, 0, 0, 0)),
        _const_spec((4, D_MODEL)),
        _const_spec(w_front.shape),
        layer_spec(w1.shape),
        layer_spec(w2.shape),
    ]
    return pl.pallas_call(
        functools.partial(_post_kernel, lru=lru),
        grid=(n // tm,),
        in_specs=in_specs,
        out_specs=x_spec,
        out_shape=jax.ShapeDtypeStruct(x.shape, F32),
        scratch_shapes=scratch,
        compiler_params=_params(("parallel",)),
        name="lru_out_mlp" if lru else "attn_out_mlp",
    )(x, *fronts, mod, g, w_front, w1, w2)


def _lru_in_kernel(*refs, need_gate):
    if need_gate:
        (x_ref, xp_ref, xn_ref, mod_ref, g_ref, w_ref, cw_ref, cb_ref, gate_ref, u_ref,
         v_sc, h_sc, t_sc, tp_sc, tn_sc) = refs
    else:
        x_ref, xp_ref, xn_ref, mod_ref, g_ref, w_ref, cw_ref, cb_ref, u_ref, v_sc, h_sc, t_sc, tp_sc, tn_sc = refs
    i = pl.program_id(0)
    n = pl.num_programs(0)
    rows = x_ref.shape[0] * x_ref.shape[1]
    half = rows // 2
    half_t = x_ref.shape[1] // 2
    s8 = SUBLANES

    def pre(x):
        return _modulate(_rms(x, g_ref[0:1, :]), mod_ref[0, 0], mod_ref[0, 1]).astype(BF16)

    h_sc[0, 0:HALO] = pre(_to_time_major(xp_ref, tp_sc)[SUBLANES * SUBLANES - HALO:])
    h_sc[0, HALO:] = pre(_to_time_major(x_ref, t_sc, 0, half_t))
    h_sc[1, 0:half] = pre(_to_time_major(x_ref, t_sc, half_t, half_t))
    h_sc[1, half:] = pre(_to_time_major(xn_ref, tn_sc)[:HALO])
    ext = half + HALO
    for r in range(2):
        v_sc[r * ext:(r + 1) * ext] = jnp.dot(h_sc[r], w_ref[:, D_RNN:], preferred_element_type=F32)
    v_sc[0:HALO] = v_sc[0:HALO] * (i > 0).astype(F32)
    v_sc[HALO + rows:HALO + rows + s8] = v_sc[HALO + rows:HALO + rows + s8] * (i < n - 1).astype(F32)
    for r in range(2):
        if need_gate:
            h_r = h_sc[0, HALO:] if r == 0 else h_sc[1, 0:half]
            gate_ref[r * half:(r + 1) * half, :] = jax.nn.gelu(
                jnp.dot(h_r, w_ref[:, :D_RNN], preferred_element_type=F32)).astype(BF16)
        base = HALO + r * half
        u_ref[r * half:(r + 1) * half, :] = (
            cb_ref[...]
            + cw_ref[0:1, :] * v_sc[base - 2 * s8:base - 2 * s8 + half]
            + cw_ref[1:2, :] * v_sc[base - s8:base - s8 + half]
            + cw_ref[2:3, :] * v_sc[base:base + half]
            + cw_ref[3:4, :] * v_sc[base + s8:base + s8 + half]).astype(u_ref.dtype)


def _lru_in_call(x3, mod, g, w_in, conv_w, conv_b, need_gate):
    t_total = x3.shape[1]
    n = BATCH * t_total
    tm = TOKEN_TILE
    nt = tm // BATCH
    row = lambda i: (i, 0)
    per_tile = nt // SUBLANES
    last = t_total // SUBLANES - 1
    halo_spec = lambda f: pl.BlockSpec((BATCH, SUBLANES, D_MODEL), f)
    out_specs = [pl.BlockSpec((tm, D_RNN), row), pl.BlockSpec((tm, D_RNN), row)]
    out_shape = [jax.ShapeDtypeStruct((n, D_RNN), BF16), jax.ShapeDtypeStruct((n, D_RNN), U_DTYPE)]
    if not need_gate:
        out_specs, out_shape = out_specs[1:], out_shape[1:]
    return pl.pallas_call(
        functools.partial(_lru_in_kernel, need_gate=need_gate),
        grid=(n // tm,),
        in_specs=[
            pl.BlockSpec((BATCH, nt, D_MODEL), lambda i: (0, i, 0)),
            halo_spec(lambda i: (0, jnp.maximum(i * per_tile - 1, 0), 0)),
            halo_spec(lambda i: (0, jnp.minimum((i + 1) * per_tile, last), 0)),
            _const_spec((1, N_MOD, SUBLANES, D_MODEL)),
            _const_spec((4, D_MODEL)),
            _const_spec((D_MODEL, 2 * D_RNN)),
            _const_spec((CONV_W, D_RNN)),
            _const_spec((1, D_RNN)),
        ],
        out_specs=out_specs,
        out_shape=out_shape,
        scratch_shapes=[
            pltpu.VMEM((tm + 2 * HALO, D_RNN), F32),
            pltpu.VMEM((2, tm // 2 + HALO, D_MODEL), BF16),
            pltpu.VMEM((N_SLAB, tm, LANES), F32),
            pltpu.VMEM((N_SLAB, SUBLANES * SUBLANES, LANES), F32),
            pltpu.VMEM((N_SLAB, SUBLANES * SUBLANES, LANES), F32),
        ],
        compiler_params=_params(("parallel",)),
        name="lru_in",
    )(x3, x3, x3, mod, g, w_in, conv_w, conv_b)


def _scan_kernel(uf_ref, ub_ref, h0_ref, wa_ref, ba_ref, wi_ref, bi_ref, lam_ref, yf_ref, yb_ref, ht_ref,
                 a_sc, bx_sc, h_sc):
    i = pl.program_id(0)
    n = pl.num_programs(0)
    rows = uf_ref.shape[0]
    nt = rows // SUBLANES
    s8 = SUBLANES

    @pl.when(i == 0)
    def _():
        h_sc[...] = h0_ref[...]

    for d, u_ref in enumerate((uf_ref, ub_ref)):
        for c in range(N_LRU_BLOCKS):
            cs = slice(c * LRU_BLOCK_W, (c + 1) * LRU_BLOCK_W)
            u16 = u_ref[:, cs].astype(BF16)
            u = u_ref[:, cs].astype(F32)
            ta = jnp.tanh(jnp.dot(u16, wa_ref[d, c], preferred_element_type=F32) + 0.5 * ba_ref[d, :, cs])
            ti = jnp.tanh(jnp.dot(u16, wi_ref[d, c], preferred_element_type=F32) + 0.5 * bi_ref[d, :, cs])
            neg_lam = -lam_ref[d, :, cs]
            softplus = jnp.maximum(neg_lam, 0.0) + jnp.log1p(jnp.exp(-jnp.abs(neg_lam)))
            k = (-0.5 * LRU_C * LOG2E) * softplus
            a = jnp.exp2(k * ta + k)
            w = 1.0 - a * a
            root = w * lax.rsqrt(jnp.maximum(w, 1e-30))
            a_sc[d, :, cs] = a
            bx_sc[d, :, cs] = root * (ti * u + u)

    def step(t, carry):
        hf, hb = carry
        rf = pl.multiple_of(t * 2 * s8, 2 * s8)
        rb = pl.multiple_of((nt - 2 - 2 * t) * s8, 2 * s8)
        hf1 = a_sc[0, pl.ds(rf, s8), :] * hf + bx_sc[0, pl.ds(rf, s8), :]
        hf2 = a_sc[0, pl.ds(rf + s8, s8), :] * hf1 + bx_sc[0, pl.ds(rf + s8, s8), :]
        yf_ref[pl.ds(rf, 2 * s8), :] = jnp.concatenate([hf1, hf2], axis=0).astype(yf_ref.dtype)
        hb1 = a_sc[1, pl.ds(rb + s8, s8), :] * hb + bx_sc[1, pl.ds(rb + s8, s8), :]
        hb2 = a_sc[1, pl.ds(rb, s8), :] * hb1 + bx_sc[1, pl.ds(rb, s8), :]
        yb_ref[pl.ds(rb, 2 * s8), :] = jnp.concatenate([hb2, hb1], axis=0).astype(yb_ref.dtype)
        return hf2, hb2

    hf, hb = lax.fori_loop(0, nt // 2, step, (h_sc[0], h_sc[1]), unroll=2)
    h_sc[0] = hf
    h_sc[1] = hb

    @pl.when(i == n - 1)
    def _():
        ht_ref[...] = h_sc[...]


def _scan_call(u2, h0, w_a, b_a, w_i, b_i, lam):
    rows_total = u2.shape[0]
    rows = SCAN_T * SUBLANES
    n = rows_total // rows
    w = D_RNN
    fwd = lambda i: (i, 0)
    bwd = lambda i: (n - 1 - i, 0)
    return pl.pallas_call(
        _scan_kernel,
        grid=(n,),
        in_specs=[
            pl.BlockSpec((rows, w), fwd),
            pl.BlockSpec((rows, w), bwd),
            _const_spec((2, SUBLANES, w)),
            _const_spec((2, N_LRU_BLOCKS, LRU_BLOCK_W, LRU_BLOCK_W)),
            _const_spec((2, 1, w)),
            _const_spec((2, N_LRU_BLOCKS, LRU_BLOCK_W, LRU_BLOCK_W)),
            _const_spec((2, 1, w)),
            _const_spec((2, 1, w)),
        ],
        out_specs=[
            pl.BlockSpec((rows, w), fwd),
            pl.BlockSpec((rows, w), bwd),
            pl.BlockSpec((2, SUBLANES, w), lambda i: (0, 0, 0)),
        ],
        out_shape=[
            jax.ShapeDtypeStruct((rows_total, w), Y_DTYPE),
            jax.ShapeDtypeStruct((rows_total, w), Y_DTYPE),
            jax.ShapeDtypeStruct((2, SUBLANES, w), F32),
        ],
        scratch_shapes=[
            pltpu.VMEM((2, rows, w), F32),
            pltpu.VMEM((2, rows, w), F32),
            pltpu.VMEM((2, SUBLANES, w), F32),
        ],
        compiler_params=_params(("arbitrary",)),
        name="lru_scan",
    )(u2, u2, h0, w_a, b_a, w_i, b_i, lam)


def _rope_tables():
    t = np.arange(SEQ)
    row = (t // GRID_W).astype(np.float64)
    col = (t % GRID_W).astype(np.float64)
    half = HEAD_DIM // 2
    inv = ROPE_BASE ** (-np.arange(0, half, 2, dtype=np.float64) / half)
    ang_r = row[:, None] * inv[None, :]
    ang_c = col[:, None] * inv[None, :]
    ang = np.concatenate([ang_r, ang_r, ang_c, ang_c], axis=-1)
    ang = np.tile(ang, (1, LANES // HEAD_DIM))
    low = (np.arange(LANES) % 32) < 16
    sin = np.sin(ang)
    tables = (np.cos(ang), np.where(low, -sin, 0.0), np.where(low, 0.0, sin))
    return tuple(jnp.asarray(a, dtype=F32) for a in tables)


def kernel(x, c, ctx, c_ctx, ada_w, ada_b, norm_g, mlp_w1, mlp_w2, attn_w_qkv, attn_w_o, attn_sink,
           lru_w_in, lru_conv_w, lru_conv_b, lru_w_a, lru_b_a, lru_w_i, lru_b_i, lru_lam, lru_w_out):
    n_lat = BATCH * SEQ
    n_ctx = BATCH * CTX_LEN

    c16 = jnp.zeros((16, D_MODEL), F32).at[:BATCH].set(c).at[BATCH].set(c_ctx)
    mods = _mod_call(c16, ada_w, ada_b).reshape(2, 16, N_MOD, D_MODEL)

    def slab_bmajor(m):
        return jnp.broadcast_to(m[:, :, None, :], (BATCH, N_MOD, SUBLANES, D_MODEL))

    def slab_ctx(m):
        return jnp.broadcast_to(m[None, :, None, :], (1, N_MOD, SUBLANES, D_MODEL))

    mod_x0 = slab_bmajor(mods[0, :BATCH])
    mod_c0 = slab_ctx(mods[0, BATCH])
    w_qkv = attn_w_qkv[0]
    w_qkv = jnp.concatenate([w_qkv[:, :D_Q] * (HEAD_DIM ** -0.5 * LOG2E), w_qkv[:, D_Q:]], axis=1).astype(BF16)
    sink2 = attn_sink[0] * LOG2E
    w_o = attn_w_o[0].astype(BF16)
    w1_all, w2_all = mlp_w1.astype(BF16), mlp_w2.astype(BF16)
    g0 = norm_g[0]
    tiles_per_batch = SEQ // TOKEN_TILE

    x2 = x.reshape(n_lat, D_MODEL)
    c2 = ctx.reshape(n_ctx, D_MODEL)
    q, k, v = _qkv_call(x2, mod_x0, g0, w_qkv, _rope_tables(), tiles_per_batch)
    qc, kc, vc = _qkv_call(c2, mod_c0, g0, w_qkv, None, n_ctx // TOKEN_TILE)
    kc3 = kc.reshape(BATCH, CTX_LEN, D_K2)
    att = _attn_call(sink2, q, k.reshape(BATCH, SEQ, D_K2), v, kc3, vc)
    att_c = _ctx_attn_call(sink2, qc, kc3, vc)
    x2 = _post_call(x2, [att], mod_x0, g0, w_o, w1_all, w2_all, 0, tiles_per_batch, lru=False)
    c2 = _post_call(c2, [att_c], mod_c0, g0, w_o, w1_all, w2_all, 0, n_ctx // TOKEN_TILE, lru=False)

    x3 = x2.reshape(BATCH, SEQ, D_MODEL)
    c3 = c2.reshape(BATCH, CTX_LEN, D_MODEL)
    mod_x1 = mods[1, :BATCH].transpose(1, 0, 2)[None]
    mod_c1 = slab_ctx(mods[1, BATCH])
    g1 = norm_g[1]
    w_in = lru_w_in[0].astype(BF16)
    conv_w = 0.5 * lru_conv_w[0]
    conv_b = 0.5 * lru_conv_b[0].reshape(1, D_RNN)
    w_a, w_i = lru_w_a[0].astype(BF16), lru_w_i[0].astype(BF16)
    b_a, b_i = lru_b_a[0].reshape(2, 1, D_RNN), lru_b_i[0].reshape(2, 1, D_RNN)
    lam = lru_lam[0].reshape(2, 1, D_RNN)
    scan = functools.partial(_scan_call, w_a=w_a, b_a=b_a, w_i=w_i, b_i=b_i, lam=lam)

    (u_c,) = _lru_in_call(c3, mod_c1, g1, w_in, conv_w, conv_b, need_gate=False)
    _, _, h_ctx = scan(u_c, jnp.zeros((2, SUBLANES, D_RNN), F32))
    gate_x, u_x = _lru_in_call(x3, mod_x1, g1, w_in, conv_w, conv_b, need_gate=True)
    yf, yb, _ = scan(u_x, h_ctx)
    return _post_call(x3, [gate_x, yf, yb], mod_x1, g1, lru_w_out[0].astype(BF16),
                      w1_all, w2_all, 1, n_lat // TOKEN_TILE, lru=True)
```

```python
import functools

import jax
import jax.numpy as jnp
import numpy as np
from jax import lax
from jax.experimental import pallas as pl
from jax.experimental.pallas import tpu as pltpu

D_MODEL = 1024
BATCH = 8
SEQ = 2048
GRID_W = 64
CTX_LEN = 256
HEAD_DIM = 64
N_HEADS = 16
N_KV_HEADS = 4
GQA_GROUP = N_HEADS // N_KV_HEADS
WINDOW = 128
BLOCK = 128
ROPE_BASE = 10000.0
D_RNN = 1280
LRU_BLOCK_W = 256
N_LRU_BLOCKS = D_RNN // LRU_BLOCK_W
CONV_W = 4
LRU_C = 8.0
D_FF = 4 * D_MODEL
N_MOD = 6
EPS = 1e-6
NEG_INF = -1e30

D_Q = N_HEADS * HEAD_DIM
D_KV = N_KV_HEADS * HEAD_DIM
D_K2 = 2 * D_KV
LANES = 128
SUBLANES = 8
N_SLAB = D_MODEL // LANES
TOKEN_TILE = 512
FF_CHUNK = 1024
ATTN_Q_PER_STEP = 16
SCAN_T = 128
HALO = 16
U_DTYPE = jnp.bfloat16
Y_DTYPE = jnp.bfloat16
LOG2E = 1.4426950408889634
VMEM_LIMIT = 60 * 1024 * 1024

F32 = jnp.float32
BF16 = jnp.bfloat16


def _rms(x, g):
    ms = jnp.mean(x * x, axis=-1, keepdims=True)
    return x * lax.rsqrt(ms + EPS) * g


def _slab(x):
    return x.reshape(x.shape[0] // SUBLANES, SUBLANES, x.shape[1])


def _modulate(h, shift8, scale8):
    out = _slab(h) * (1.0 + scale8)[None] + shift8[None]
    return out.reshape(h.shape)


def _gated_add(x, gate8, y):
    out = _slab(x) + gate8[None] * _slab(y)
    return out.reshape(x.shape)


def _const_spec(shape):
    n = len(shape)
    return pl.BlockSpec(shape, lambda *_: (0,) * n, pipeline_mode=pl.Buffered(1))


def _params(sem, flags=None):
    return pltpu.CompilerParams(dimension_semantics=sem, vmem_limit_bytes=VMEM_LIMIT, flags=flags)


def _mod_kernel(c_ref, w_ref, b_ref, o_ref):
    s = jax.nn.silu(c_ref[...]).astype(BF16)
    o_ref[0] = jnp.dot(s, w_ref[0].astype(BF16), preferred_element_type=F32) + b_ref[0]


def _mod_call(c16, ada_w, ada_b):
    depth = ada_w.shape[0]
    nt = 1024
    return pl.pallas_call(
        _mod_kernel,
        grid=(depth, N_MOD * D_MODEL // nt),
        in_specs=[
            pl.BlockSpec((16, D_MODEL), lambda l, j: (0, 0)),
            pl.BlockSpec((1, D_MODEL, nt), lambda l, j: (l, 0, j)),
            pl.BlockSpec((1, 1, nt), lambda l, j: (l, 0, j)),
        ],
        out_specs=pl.BlockSpec((1, 16, nt), lambda l, j: (l, 0, j)),
        out_shape=jax.ShapeDtypeStruct((depth, 16, N_MOD * D_MODEL), F32),
        compiler_params=_params(("arbitrary", "arbitrary")),
        name="adaln_mod",
    )(c16, ada_w, ada_b.reshape(depth, 1, N_MOD * D_MODEL))


def _qkv_kernel(*refs, rope):
    if rope:
        x_ref, mod_ref, g_ref, w_ref, cos_ref, sa_ref, sb_ref, q_ref, k_ref, v_ref, h_sc, y_sc = refs
    else:
        x_ref, mod_ref, g_ref, w_ref, q_ref, k_ref, v_ref, h_sc, y_sc = refs
    half = x_ref.shape[0] // 2
    low = lax.broadcasted_iota(jnp.int32, (half, LANES), 1) < HEAD_DIM
    for r in range(2):
        rs = slice(r * half, (r + 1) * half)
        h_sc[r] = _modulate(_rms(x_ref[rs, :], g_ref[0:1, :]), mod_ref[0, 0], mod_ref[0, 1]).astype(BF16)
    for r in range(2):
        y_sc[r] = jnp.dot(h_sc[r], w_ref[...], preferred_element_type=F32)
    for r in range(2):
        rs = slice(r * half, (r + 1) * half)
        if rope:
            cos, sa, sb = cos_ref[rs, :], sa_ref[rs, :], sb_ref[rs, :]
        for c in range((D_Q + D_KV) // LANES):
            yc = y_sc[r, :, c * LANES:(c + 1) * LANES]
            if rope:
                yc = yc * cos + pltpu.roll(yc, LANES - 16, 1) * sa + pltpu.roll(yc, 16, 1) * sb
            if c < D_Q // LANES:
                q_ref[rs, c * LANES:(c + 1) * LANES] = yc.astype(BF16)
            else:
                c2 = 2 * (c - D_Q // LANES)
                swapped = pltpu.roll(yc, HEAD_DIM, 1)
                k_ref[rs, c2 * LANES:(c2 + 1) * LANES] = jnp.where(low, yc, swapped).astype(BF16)
                k_ref[rs, (c2 + 1) * LANES:(c2 + 2) * LANES] = jnp.where(low, swapped, yc).astype(BF16)
        for blk in range(half // BLOCK):
            v_ref[r * (half // BLOCK) + blk] = y_sc[r, blk * BLOCK:(blk + 1) * BLOCK, D_Q + D_KV:].T.astype(BF16)


def _qkv_call(x2, mod, g, w_qkv, tables, tiles_per_group):
    n = x2.shape[0]
    tm = TOKEN_TILE
    rope = tables is not None
    in_specs = [
        pl.BlockSpec((tm, D_MODEL), lambda i: (i, 0)),
        pl.BlockSpec((1, N_MOD, SUBLANES, D_MODEL), lambda i: (i // tiles_per_group, 0, 0, 0)),
        _const_spec((4, D_MODEL)),
        _const_spec((D_MODEL, D_Q + 2 * D_KV)),
    ]
    args = [x2, mod, g, w_qkv]
    if rope:
        nt = SEQ // tm
        in_specs += [pl.BlockSpec((tm, LANES), lambda i: (i % nt, 0))] * 3
        args += list(tables)
    return pl.pallas_call(
        functools.partial(_qkv_kernel, rope=rope),
        grid=(n // tm,),
        in_specs=in_specs,
        out_specs=[
            pl.BlockSpec((tm, D_Q), lambda i: (i, 0)),
            pl.BlockSpec((tm, D_K2), lambda i: (i, 0)),
            pl.BlockSpec((tm // BLOCK, D_KV, BLOCK), lambda i: (i, 0, 0)),
        ],
        out_shape=[
            jax.ShapeDtypeStruct((n, D_Q), BF16),
            jax.ShapeDtypeStruct((n, D_K2), BF16),
            jax.ShapeDtypeStruct((n // BLOCK, D_KV, BLOCK), BF16),
        ],
        scratch_shapes=[
            pltpu.VMEM((2, tm // 2, D_MODEL), BF16),
            pltpu.VMEM((2, tm // 2, D_Q + 2 * D_KV), F32),
        ],
        compiler_params=_params(("parallel",)),
        name="qkv_rope" if rope else "qkv_ctx",
    )(*args)


VT_ROWS = HEAD_DIM + 16


def _attn_kernel(*refs, local, q_per_step):
    if local:
        sink_ref, q_ref, k_ref, v_ref, kc_ref, vc_ref, o_ref, s_sc, p_sc, bias_sc, m_sc = refs
    else:
        sink_ref, q_ref, kc_ref, vc_ref, o_ref, s_sc, p_sc, bias_sc, m_sc = refs
    j = pl.program_id(1)
    seq_blocks = SEQ // BLOCK if local else 0
    ctx_blocks = CTX_LEN // BLOCK
    n_keys = s_sc.shape[1]
    pad_row = lax.broadcasted_iota(jnp.int32, (VT_ROWS - HEAD_DIM, n_keys), 0)
    vt_pad = jnp.where(pad_row == 0, 1.0, 0.0).astype(BF16)

    nt = (((1,), (1,)), ((), ()))
    n_band = 3
    n_chunks = D_Q // LANES
    lane = lax.broadcasted_iota(jnp.int32, (BLOCK, LANES), 1)
    first_head = lax.broadcasted_iota(jnp.int32, (1, 2 * BLOCK), 1) < BLOCK

    def block_params(qb):
        jq = j * q_per_step + qb
        rows = pl.ds(pl.multiple_of(qb * BLOCK, BLOCK), BLOCK)
        if not local:
            return rows, None, None, None, qb // ctx_blocks
        blk0 = jnp.clip(jq - 1, 0, seq_blocks - n_band)
        return rows, blk0, pl.multiple_of(blk0 * BLOCK, BLOCK), qb % 2, 0

    def store_bias(qb):
        if local:
            jq = j * q_per_step + qb
            start = jnp.clip(jq - 1, 0, seq_blocks - n_band) * BLOCK
            kpos = start + lax.broadcasted_iota(jnp.int32, (n_band * BLOCK, BLOCK), 0)
            qpos = jq * BLOCK + lax.broadcasted_iota(jnp.int32, (n_band * BLOCK, BLOCK), 1)
            bias_sc[qb % 2] = jnp.where(jnp.abs(kpos - qpos) <= WINDOW, 0.0, NEG_INF).astype(F32)

    def scores(c, params):
        rows, _, start, slot, cb = params
        kcols = slice((c // 2) * LANES, (c // 2 + 1) * LANES)
        qc = q_ref[rows, c * LANES:(c + 1) * LANES]
        zero = jnp.zeros_like(qc)
        q2 = jnp.concatenate([jnp.where(lane < HEAD_DIM, qc, zero), jnp.where(lane < HEAD_DIM, zero, qc)], axis=0)
        s_ctx = lax.dot_general(kc_ref[cb, :, kcols], q2, nt, preferred_element_type=F32)
        s_sc[c % 2, 0:CTX_LEN] = s_ctx
        m = jnp.max(s_ctx, axis=0, keepdims=True)
        if local:
            s_band = lax.dot_general(k_ref[0, pl.ds(start, n_band * BLOCK), kcols], q2, nt,
                                     preferred_element_type=F32)
            halves = [s_band[:, hd * BLOCK:(hd + 1) * BLOCK] + bias_sc[slot] for hd in range(2)]
            for hd in range(2):
                s_sc[c % 2, CTX_LEN:, hd * BLOCK:(hd + 1) * BLOCK] = halves[hd]
            m = jnp.maximum(m, jnp.concatenate([jnp.max(hv, axis=0, keepdims=True) for hv in halves], axis=1))
        m_sc[c % 2] = m

    def softmax(c):
        sink_row = jnp.where(first_head, sink_ref[2 * c], sink_ref[2 * c + 1])
        m = jnp.maximum(m_sc[c % 2], sink_row)
        p_sc[c % 2] = jnp.exp2(s_sc[c % 2] - m).astype(BF16)
        return jnp.exp2(sink_row - m)

    def values(c, params, sink_term):
        rows, blk0, _, _, cb = params
        hrows = slice((c // 2) * HEAD_DIM, (c // 2 + 1) * HEAD_DIM)
        vt = [vc_ref[cb * ctx_blocks + i, hrows, :] for i in range(ctx_blocks)]
        if local:
            vt_band = v_ref[pl.ds(blk0, n_band), hrows, :]
            vt += [vt_band[i] for i in range(n_band)]
        vt_aug = jnp.concatenate([jnp.concatenate(vt, axis=1), vt_pad], axis=0)
        acc = jnp.dot(vt_aug, p_sc[c % 2], preferred_element_type=F32)
        out_t = acc[0:HEAD_DIM] / (acc[HEAD_DIM:HEAD_DIM + 1] + sink_term)
        both = jnp.concatenate([out_t[:, :BLOCK], out_t[:, BLOCK:]], axis=0)
        o_ref[rows, c * LANES:(c + 1) * LANES] = both.T.astype(o_ref.dtype)

    def query_block(qb, sink_term0):
        cur = block_params(qb)
        nxt = block_params(jnp.minimum(qb + 1, q_per_step - 1))
        store_bias(qb + 1)
        sink_terms = {0: sink_term0}
        for c in range(n_chunks):
            if c + 2 < n_chunks:
                scores(c + 2, cur)
            else:
                scores(c + 2 - n_chunks, nxt)
            sink_terms[c + 1] = softmax((c + 1) % n_chunks)
            values(c, cur, sink_terms[c])
        return sink_terms[n_chunks]

    first = block_params(0)
    store_bias(0)
    scores(0, first)
    scores(1, first)
    lax.fori_loop(0, q_per_step, query_block, softmax(0))


def _attn_scratch(n_keys):
    return [
        pltpu.VMEM((2, n_keys, 2 * BLOCK), F32),
        pltpu.VMEM((2, n_keys, 2 * BLOCK), BF16),
        pltpu.VMEM((2, 3 * BLOCK, BLOCK), F32),
        pltpu.VMEM((2, 1, 2 * BLOCK), F32),
    ]


def _attn_call(sink2, q, k, vt, kc, vtc):
    nb = SEQ // BLOCK
    nbc = CTX_LEN // BLOCK
    qps = ATTN_Q_PER_STEP
    steps = nb // qps
    return pl.pallas_call(
        functools.partial(_attn_kernel, local=True, q_per_step=qps),
        grid=(BATCH, steps),
        in_specs=[
            pl.BlockSpec(memory_space=pltpu.SMEM),
            pl.BlockSpec((qps * BLOCK, D_Q), lambda b, j: (b * steps + j, 0)),
            pl.BlockSpec((1, SEQ, D_K2), lambda b, j: (b, 0, 0)),
            pl.BlockSpec((nb, D_KV, BLOCK), lambda b, j: (b, 0, 0)),
            pl.BlockSpec((1, CTX_LEN, D_K2), lambda b, j: (b, 0, 0)),
            pl.BlockSpec((nbc, D_KV, BLOCK), lambda b, j: (b, 0, 0)),
        ],
        out_specs=pl.BlockSpec((qps * BLOCK, D_Q), lambda b, j: (b * steps + j, 0)),
        out_shape=jax.ShapeDtypeStruct((BATCH * SEQ, D_Q), BF16),
        scratch_shapes=_attn_scratch(CTX_LEN + 3 * BLOCK),
        compiler_params=_params(("parallel", "arbitrary")),
        name="band_attn",
    )(sink2, q, k, vt, kc, vtc)


def _ctx_attn_call(sink2, qc, kc, vtc):
    nb = BATCH * CTX_LEN // BLOCK
    whole = lambda shape: pl.BlockSpec(shape, lambda b, j: (0,) * len(shape))
    return pl.pallas_call(
        functools.partial(_attn_kernel, local=False, q_per_step=nb),
        grid=(1, 1),
        in_specs=[
            pl.BlockSpec(memory_space=pltpu.SMEM),
            whole((BATCH * CTX_LEN, D_Q)),
            whole((BATCH, CTX_LEN, D_K2)),
            whole((nb, D_KV, BLOCK)),
        ],
        out_specs=whole((BATCH * CTX_LEN, D_Q)),
        out_shape=jax.ShapeDtypeStruct((BATCH * CTX_LEN, D_Q), BF16),
        scratch_shapes=_attn_scratch(CTX_LEN),
        compiler_params=_params(("arbitrary", "arbitrary")),
        name="ctx_attn",
    )(sink2, qc, kc, vtc)


def _to_time_major(src_ref, sc_ref, t0=0, nt=None):
    nt = src_ref.shape[1] if nt is None else nt
    r0 = t0 * SUBLANES
    for b in range(BATCH):
        for s in range(N_SLAB):
            sc_ref[s, pl.ds(r0 + b, nt, stride=SUBLANES), :] = src_ref[b, t0:t0 + nt, s * LANES:(s + 1) * LANES]
    return jnp.concatenate([sc_ref[s, r0:r0 + nt * SUBLANES, :] for s in range(N_SLAB)], axis=1)


def _from_time_major(val, sc_ref, dst_ref, t0, nt):
    r0 = t0 * SUBLANES
    for s in range(N_SLAB):
        sc_ref[s, r0:r0 + nt * SUBLANES, :] = val[:, s * LANES:(s + 1) * LANES]
    for b in range(BATCH):
        for s in range(N_SLAB):
            dst_ref[b, t0:t0 + nt, s * LANES:(s + 1) * LANES] = sc_ref[s, pl.ds(r0 + b, nt, stride=SUBLANES), :]


def _post_kernel(*refs, lru):
    if lru:
        (x_ref, gate_ref, yf_ref, yb_ref, mod_ref, g_ref, wf_ref, w1_ref, w2_ref, o_ref,
         x1_sc, h_sc, acc_sc, tin_sc, tout_sc) = refs
        half_rows = gate_ref.shape[0] // 2
    else:
        x_ref, a_ref, mod_ref, g_ref, wf_ref, w1_ref, w2_ref, o_ref, x1_sc, h_sc, acc_sc = refs
        half_rows = x_ref.shape[0] // 2
    half_t = half_rows // BATCH

    def head(r):
        rs = slice(r * half_rows, (r + 1) * half_rows)
        if lru:
            front = (gate_ref[rs, :].astype(F32)
                     * (yf_ref[rs, :].astype(F32) + yb_ref[rs, :].astype(F32))).astype(BF16)
            x = _to_time_major(x_ref, tin_sc, r * half_t, half_t)
        else:
            front = a_ref[rs, :]
            x = x_ref[rs, :]
        y = jnp.dot(front, wf_ref[...], preferred_element_type=F32)
        x1 = _gated_add(x, mod_ref[0, 2], _rms(y, g_ref[1:2, :]))
        x1_sc[r] = x1
        h_sc[r] = _modulate(_rms(x1, g_ref[2:3, :]), mod_ref[0, 3], mod_ref[0, 4]).astype(BF16)

    def mlp(r):
        acc = jnp.zeros((half_rows, D_MODEL), F32)
        for c in range(D_FF // FF_CHUNK):
            hid = jnp.dot(h_sc[r], w1_ref[:, c * FF_CHUNK:(c + 1) * FF_CHUNK], preferred_element_type=F32)
            hid = jnp.square(jnp.maximum(hid, 0.0)).astype(BF16)
            acc = acc + jnp.dot(hid, w2_ref[c * FF_CHUNK:(c + 1) * FF_CHUNK, :], preferred_element_type=F32)
        acc_sc[r] = acc

    def tail(r):
        out = _gated_add(x1_sc[r], mod_ref[0, 5], _rms(acc_sc[r], g_ref[3:4, :]))
        if lru:
            _from_time_major(out, tout_sc, o_ref, r * half_t, half_t)
        else:
            o_ref[r * half_rows:(r + 1) * half_rows, :] = out

    head(0)
    head(1)
    mlp(0)
    tail(0)
    mlp(1)
    tail(1)


def _post_call(x, fronts, mod, g, w_front, w1, w2, layer, tiles_per_group, lru):
    tm = TOKEN_TILE
    layer_spec = lambda shape: pl.BlockSpec((None,) + shape[1:], lambda i: (layer, 0, 0),
                                            pipeline_mode=pl.Buffered(1))
    row = lambda i: (i, 0)
    if lru:
        nt = tm // BATCH
        n = x.shape[0] * x.shape[1]
        x_spec = pl.BlockSpec((BATCH, nt, D_MODEL), lambda i: (0, i, 0))
        scratch = [pltpu.VMEM((N_SLAB, tm, LANES), F32), pltpu.VMEM((N_SLAB, tm, LANES), F32)]
    else:
        n = x.shape[0]
        x_spec = pl.BlockSpec((tm, D_MODEL), row)
        scratch = []
    scratch = [
        pltpu.VMEM((2, tm // 2, D_MODEL), F32),
        pltpu.VMEM((2, tm // 2, D_MODEL), BF16),
        pltpu.VMEM((2, tm // 2, D_MODEL), F32),
    ] + scratch
    in_specs = [x_spec]
    in_specs += [pl.BlockSpec((tm, f.shape[1]), row) for f in fronts]
    in_specs += [
        pl.BlockSpec((1, N_MOD, SUBLANES, D_MODEL), lambda i: (i // tiles_per_group, 0, 0, 0)),
        _const_spec((4, D_MODEL)),
        _const_spec(w_front.shape),
        layer_spec(w1.shape),
        layer_spec(w2.shape),
    ]
    return pl.pallas_call(
        functools.partial(_post_kernel, lru=lru),
        grid=(n // tm,),
        in_specs=in_specs,
        out_specs=x_spec,
        out_shape=jax.ShapeDtypeStruct(x.shape, F32),
        scratch_shapes=scratch,
        compiler_params=_params(("parallel",)),
        name="lru_out_mlp" if lru else "attn_out_mlp",
    )(x, *fronts, mod, g, w_front, w1, w2)


def _lru_in_kernel(*refs, need_gate):
    if need_gate:
        (x_ref, xp_ref, xn_ref, mod_ref, g_ref, w_ref, cw_ref, cb_ref, gate_ref, u_ref,
         v_sc, h_sc, t_sc, tp_sc, tn_sc) = refs
    else:
        x_ref, xp_ref, xn_ref, mod_ref, g_ref, w_ref, cw_ref, cb_ref, u_ref, v_sc, h_sc, t_sc, tp_sc, tn_sc = refs
    i = pl.program_id(0)
    n = pl.num_programs(0)
    rows = x_ref.shape[0] * x_ref.shape[1]
    half = rows // 2
    half_t = x_ref.shape[1] // 2
    s8 = SUBLANES

    def pre(x):
        return _modulate(_rms(x, g_ref[0:1, :]), mod_ref[0, 0], mod_ref[0, 1]).astype(BF16)

    h_sc[0, 0:HALO] = pre(_to_time_major(xp_ref, tp_sc)[SUBLANES * SUBLANES - HALO:])
    h_sc[0, HALO:] = pre(_to_time_major(x_ref, t_sc, 0, half_t))
    h_sc[1, 0:half] = pre(_to_time_major(x_ref, t_sc, half_t, half_t))
    h_sc[1, half:] = pre(_to_time_major(xn_ref, tn_sc)[:HALO])
    ext = half + HALO
    for r in range(2):
        v_sc[r * ext:(r + 1) * ext] = jnp.dot(h_sc[r], w_ref[:, D_RNN:], preferred_element_type=F32)
    v_sc[0:HALO] = v_sc[0:HALO] * (i > 0).astype(F32)
    v_sc[HALO + rows:HALO + rows + s8] = v_sc[HALO + rows:HALO + rows + s8] * (i < n - 1).astype(F32)
    for r in range(2):
        if need_gate:
            h_r = h_sc[0, HALO:] if r == 0 else h_sc[1, 0:half]
            gate_ref[r * half:(r + 1) * half, :] = jax.nn.gelu(
                jnp.dot(h_r, w_ref[:, :D_RNN], preferred_element_type=F32)).astype(BF16)
        base = HALO + r * half
        u_ref[r * half:(r + 1) * half, :] = (
            cb_ref[...]
            + cw_ref[0:1, :] * v_sc[base - 2 * s8:base - 2 * s8 + half]
            + cw_ref[1:2, :] * v_sc[base - s8:base - s8 + half]
            + cw_ref[2:3, :] * v_sc[base:base + half]
            + cw_ref[3:4, :] * v_sc[base + s8:base + s8 + half]).astype(u_ref.dtype)


def _lru_in_call(x3, mod, g, w_in, conv_w, conv_b, need_gate):
    t_total = x3.shape[1]
    n = BATCH * t_total
    tm = TOKEN_TILE
    nt = tm // BATCH
    row = lambda i: (i, 0)
    per_tile = nt // SUBLANES
    last = t_total // SUBLANES - 1
    halo_spec = lambda f: pl.BlockSpec((BATCH, SUBLANES, D_MODEL), f)
    out_specs = [pl.BlockSpec((tm, D_RNN), row), pl.BlockSpec((tm, D_RNN), row)]
    out_shape = [jax.ShapeDtypeStruct((n, D_RNN), BF16), jax.ShapeDtypeStruct((n, D_RNN), U_DTYPE)]
    if not need_gate:
        out_specs, out_shape = out_specs[1:], out_shape[1:]
    return pl.pallas_call(
        functools.partial(_lru_in_kernel, need_gate=need_gate),
        grid=(n // tm,),
        in_specs=[
            pl.BlockSpec((BATCH, nt, D_MODEL), lambda i: (0, i, 0)),
            halo_spec(lambda i: (0, jnp.maximum(i * per_tile - 1, 0), 0)),
            halo_spec(lambda i: (0, jnp.minimum((i + 1) * per_tile, last), 0)),
            _const_spec((1, N_MOD, SUBLANES, D_MODEL)),
            _const_spec((4, D_MODEL)),
            _const_spec((D_MODEL, 2 * D_RNN)),
            _const_spec((CONV_W, D_RNN)),
            _const_spec((1, D_RNN)),
        ],
        out_specs=out_specs,
        out_shape=out_shape,
        scratch_shapes=[
            pltpu.VMEM((tm + 2 * HALO, D_RNN), F32),
            pltpu.VMEM((2, tm // 2 + HALO, D_MODEL), BF16),
            pltpu.VMEM((N_SLAB, tm, LANES), F32),
            pltpu.VMEM((N_SLAB, SUBLANES * SUBLANES, LANES), F32),
            pltpu.VMEM((N_SLAB, SUBLANES * SUBLANES, LANES), F32),
        ],
        compiler_params=_params(("parallel",)),
        name="lru_in",
    )(x3, x3, x3, mod, g, w_in, conv_w, conv_b)


def _scan_kernel(uf_ref, ub_ref, h0_ref, wa_ref, ba_ref, wi_ref, bi_ref, lam_ref, yf_ref, yb_ref, ht_ref,
                 a_sc, bx_sc, h_sc):
    i = pl.program_id(0)
    n = pl.num_programs(0)
    rows = uf_ref.shape[0]
    nt = rows // SUBLANES
    s8 = SUBLANES

    @pl.when(i == 0)
    def _():
        h_sc[...] = h0_ref[...]

    for d, u_ref in enumerate((uf_ref, ub_ref)):
        for c in range(N_LRU_BLOCKS):
            cs = slice(c * LRU_BLOCK_W, (c + 1) * LRU_BLOCK_W)
            u16 = u_ref[:, cs].astype(BF16)
            u = u_ref[:, cs].astype(F32)
            ta = jnp.tanh(jnp.dot(u16, wa_ref[d, c], preferred_element_type=F32) + 0.5 * ba_ref[d, :, cs])
            ti = jnp.tanh(jnp.dot(u16, wi_ref[d, c], preferred_element_type=F32) + 0.5 * bi_ref[d, :, cs])
            neg_lam = -lam_ref[d, :, cs]
            softplus = jnp.maximum(neg_lam, 0.0) + jnp.log1p(jnp.exp(-jnp.abs(neg_lam)))
            k = (-0.5 * LRU_C * LOG2E) * softplus
            a = jnp.exp2(k * ta + k)
            w = 1.0 - a * a
            root = w * lax.rsqrt(jnp.maximum(w, 1e-30))
            a_sc[d, :, cs] = a
            bx_sc[d, :, cs] = root * (ti * u + u)

    def step(t, carry):
        hf, hb = carry
        rf = pl.multiple_of(t * 2 * s8, 2 * s8)
        rb = pl.multiple_of((nt - 2 - 2 * t) * s8, 2 * s8)
        hf1 = a_sc[0, pl.ds(rf, s8), :] * hf + bx_sc[0, pl.ds(rf, s8), :]
        hf2 = a_sc[0, pl.ds(rf + s8, s8), :] * hf1 + bx_sc[0, pl.ds(rf + s8, s8), :]
        yf_ref[pl.ds(rf, 2 * s8), :] = jnp.concatenate([hf1, hf2], axis=0).astype(yf_ref.dtype)
        hb1 = a_sc[1, pl.ds(rb + s8, s8), :] * hb + bx_sc[1, pl.ds(rb + s8, s8), :]
        hb2 = a_sc[1, pl.ds(rb, s8), :] * hb1 + bx_sc[1, pl.ds(rb, s8), :]
        yb_ref[pl.ds(rb, 2 * s8), :] = jnp.concatenate([hb2, hb1], axis=0).astype(yb_ref.dtype)
        return hf2, hb2

    hf, hb = lax.fori_loop(0, nt // 2, step, (h_sc[0], h_sc[1]), unroll=2)
    h_sc[0] = hf
    h_sc[1] = hb

    @pl.when(i == n - 1)
    def _():
        ht_ref[...] = h_sc[...]


def _scan_call(u2, h0, w_a, b_a, w_i, b_i, lam):
    rows_total = u2.shape[0]
    rows = SCAN_T * SUBLANES
    n = rows_total // rows
    w = D_RNN
    fwd = lambda i: (i, 0)
    bwd = lambda i: (n - 1 - i, 0)
    return pl.pallas_call(
        _scan_kernel,
        grid=(n,),
        in_specs=[
            pl.BlockSpec((rows, w), fwd),
            pl.BlockSpec((rows, w), bwd),
            _const_spec((2, SUBLANES, w)),
            _const_spec((2, N_LRU_BLOCKS, LRU_BLOCK_W, LRU_BLOCK_W)),
            _const_spec((2, 1, w)),
            _const_spec((2, N_LRU_BLOCKS, LRU_BLOCK_W, LRU_BLOCK_W)),
            _const_spec((2, 1, w)),
            _const_spec((2, 1, w)),
        ],
        out_specs=[
            pl.BlockSpec((rows, w), fwd),
            pl.BlockSpec((rows, w), bwd),
            pl.BlockSpec((2, SUBLANES, w), lambda i: (0, 0, 0)),
        ],
        out_shape=[
            jax.ShapeDtypeStruct((rows_total, w), Y_DTYPE),
            jax.ShapeDtypeStruct((rows_total, w), Y_DTYPE),
            jax.ShapeDtypeStruct((2, SUBLANES, w), F32),
        ],
        scratch_shapes=[
            pltpu.VMEM((2, rows, w), F32),
            pltpu.VMEM((2, rows, w), F32),
            pltpu.VMEM((2, SUBLANES, w), F32),
        ],
        compiler_params=_params(("arbitrary",)),
        name="lru_scan",
    )(u2, u2, h0, w_a, b_a, w_i, b_i, lam)


def _rope_tables():
    t = np.arange(SEQ)
    row = (t // GRID_W).astype(np.float64)
    col = (t % GRID_W).astype(np.float64)
    half = HEAD_DIM // 2
    inv = ROPE_BASE ** (-np.arange(0, half, 2, dtype=np.float64) / half)
    ang_r = row[:, None] * inv[None, :]
    ang_c = col[:, None] * inv[None, :]
    ang = np.concatenate([ang_r, ang_r, ang_c, ang_c], axis=-1)
    ang = np.tile(ang, (1, LANES // HEAD_DIM))
    low = (np.arange(LANES) % 32) < 16
    sin = np.sin(ang)
    tables = (np.cos(ang), np.where(low, -sin, 0.0), np.where(low, 0.0, sin))
    return tuple(jnp.asarray(a, dtype=F32) for a in tables)


def kernel(x, c, ctx, c_ctx, ada_w, ada_b, norm_g, mlp_w1, mlp_w2, attn_w_qkv, attn_w_o, attn_sink,
           lru_w_in, lru_conv_w, lru_conv_b, lru_w_a, lru_b_a, lru_w_i, lru_b_i, lru_lam, lru_w_out):
    n_lat = BATCH * SEQ
    n_ctx = BATCH * CTX_LEN

    c16 = jnp.zeros((16, D_MODEL), F32).at[:BATCH].set(c).at[BATCH].set(c_ctx)
    mods = _mod_call(c16, ada_w, ada_b).reshape(2, 16, N_MOD, D_MODEL)

    def slab_bmajor(m):
        return jnp.broadcast_to(m[:, :, None, :], (BATCH, N_MOD, SUBLANES, D_MODEL))

    def slab_ctx(m):
        return jnp.broadcast_to(m[None, :, None, :], (1, N_MOD, SUBLANES, D_MODEL))

    mod_x0 = slab_bmajor(mods[0, :BATCH])
    mod_c0 = slab_ctx(mods[0, BATCH])
    w_qkv = attn_w_qkv[0]
    w_qkv = jnp.concatenate([w_qkv[:, :D_Q] * (HEAD_DIM ** -0.5 * LOG2E), w_qkv[:, D_Q:]], axis=1).astype(BF16)
    sink2 = attn_sink[0] * LOG2E
    w_o = attn_w_o[0].astype(BF16)
    w1_all, w2_all = mlp_w1.astype(BF16), mlp_w2.astype(BF16)
    g0 = norm_g[0]
    tiles_per_batch = SEQ // TOKEN_TILE

    x2 = x.reshape(n_lat, D_MODEL)
    c2 = ctx.reshape(n_ctx, D_MODEL)
    q, k, v = _qkv_call(x2, mod_x0, g0, w_qkv, _rope_tables(), tiles_per_batch)
    qc, kc, vc = _qkv_call(c2, mod_c0, g0, w_qkv, None, n_ctx // TOKEN_TILE)
    kc3 = kc.reshape(BATCH, CTX_LEN, D_K2)
    att = _attn_call(sink2, q, k.reshape(BATCH, SEQ, D_K2), v, kc3, vc)
    att_c = _ctx_attn_call(sink2, qc, kc3, vc)
    x2 = _post_call(x2, [att], mod_x0, g0, w_o, w1_all, w2_all, 0, tiles_per_batch, lru=False)
    c2 = _post_call(c2, [att_c], mod_c0, g0, w_o, w1_all, w2_all, 0, n_ctx // TOKEN_TILE, lru=False)

    x3 = x2.reshape(BATCH, SEQ, D_MODEL)
    c3 = c2.reshape(BATCH, CTX_LEN, D_MODEL)
    mod_x1 = mods[1, :BATCH].transpose(1, 0, 2)[None]
    mod_c1 = slab_ctx(mods[1, BATCH])
    g1 = norm_g[1]
    w_in = lru_w_in[0].astype(BF16)
    conv_w = 0.5 * lru_conv_w[0]
    conv_b = 0.5 * lru_conv_b[0].reshape(1, D_RNN)
    w_a, w_i = lru_w_a[0].astype(BF16), lru_w_i[0].astype(BF16)
    b_a, b_i = lru_b_a[0].reshape(2, 1, D_RNN), lru_b_i[0].reshape(2, 1, D_RNN)
    lam = lru_lam[0].reshape(2, 1, D_RNN)
    scan = functools.partial(_scan_call, w_a=w_a, b_a=b_a, w_i=w_i, b_i=b_i, lam=lam)

    (u_c,) = _lru_in_call(c3, mod_c1, g1, w_in, conv_w, conv_b, need_gate=False)
    _, _, h_ctx = scan(u_c, jnp.zeros((2, SUBLANES, D_RNN), F32))
    gate_x, u_x = _lru_in_call(x3, mod_x1, g1, w_in, conv_w, conv_b, need_gate=True)
    yf, yb, _ = scan(u_x, h_ctx)
    return _post_call(x3, [gate_x, yf, yb], mod_x1, g1, lru_w_out[0].astype(BF16),
                      w1_all, w2_all, 1, n_lat // TOKEN_TILE, lru=True)
```

```python
import functools

import jax
import jax.numpy as jnp
import numpy as np
from jax import lax
from jax.experimental import pallas as pl
from jax.experimental.pallas import tpu as pltpu

D_MODEL = 1024
BATCH = 8
SEQ = 2048
GRID_W = 64
CTX_LEN = 256
HEAD_DIM = 64
N_HEADS = 16
N_KV_HEADS = 4
GQA_GROUP = N_HEADS // N_KV_HEADS
WINDOW = 128
BLOCK = 128
ROPE_BASE = 10000.0
D_RNN = 1280
LRU_BLOCK_W = 256
N_LRU_BLOCKS = D_RNN // LRU_BLOCK_W
CONV_W = 4
LRU_C = 8.0
D_FF = 4 * D_MODEL
N_MOD = 6
EPS = 1e-6
NEG_INF = -1e30

D_Q = N_HEADS * HEAD_DIM
D_KV = N_KV_HEADS * HEAD_DIM
D_K2 = 2 * D_KV
LANES = 128
SUBLANES = 8
N_SLAB = D_MODEL // LANES
TOKEN_TILE = 512
FF_CHUNK = 1024
ATTN_Q_PER_STEP = 16
SCAN_T = 128
HALO = 16
U_DTYPE = jnp.bfloat16
Y_DTYPE = jnp.bfloat16
LOG2E = 1.4426950408889634
VMEM_LIMIT = 60 * 1024 * 1024

F32 = jnp.float32
BF16 = jnp.bfloat16


def _rms(x, g):
    ms = jnp.mean(x * x, axis=-1, keepdims=True)
    return x * lax.rsqrt(ms + EPS) * g


def _slab(x):
    return x.reshape(x.shape[0] // SUBLANES, SUBLANES, x.shape[1])


def _modulate(h, shift8, scale8):
    out = _slab(h) * (1.0 + scale8)[None] + shift8[None]
    return out.reshape(h.shape)


def _gated_add(x, gate8, y):
    out = _slab(x) + gate8[None] * _slab(y)
    return out.reshape(x.shape)


def _const_spec(shape):
    n = len(shape)
    return pl.BlockSpec(shape, lambda *_: (0,) * n, pipeline_mode=pl.Buffered(1))


def _params(sem, flags=None):
    return pltpu.CompilerParams(dimension_semantics=sem, vmem_limit_bytes=VMEM_LIMIT, flags=flags)


def _mod_kernel(c_ref, w_ref, b_ref, o_ref):
    s = jax.nn.silu(c_ref[...]).astype(BF16)
    o_ref[0] = jnp.dot(s, w_ref[0].astype(BF16), preferred_element_type=F32) + b_ref[0]


def _mod_call(c16, ada_w, ada_b):
    depth = ada_w.shape[0]
    nt = 1024
    return pl.pallas_call(
        _mod_kernel,
        grid=(depth, N_MOD * D_MODEL // nt),
        in_specs=[
            pl.BlockSpec((16, D_MODEL), lambda l, j: (0, 0)),
            pl.BlockSpec((1, D_MODEL, nt), lambda l, j: (l, 0, j)),
            pl.BlockSpec((1, 1, nt), lambda l, j: (l, 0, j)),
        ],
        out_specs=pl.BlockSpec((1, 16, nt), lambda l, j: (l, 0, j)),
        out_shape=jax.ShapeDtypeStruct((depth, 16, N_MOD * D_MODEL), F32),
        compiler_params=_params(("arbitrary", "arbitrary")),
        name="adaln_mod",
    )(c16, ada_w, ada_b.reshape(depth, 1, N_MOD * D_MODEL))


def _qkv_kernel(*refs, rope):
    if rope:
        x_ref, mod_ref, g_ref, w_ref, cos_ref, sa_ref, sb_ref, q_ref, k_ref, v_ref, h_sc, y_sc = refs
    else:
        x_ref, mod_ref, g_ref, w_ref, q_ref, k_ref, v_ref, h_sc, y_sc = refs
    half = x_ref.shape[0] // 2
    low = lax.broadcasted_iota(jnp.int32, (half, LANES), 1) < HEAD_DIM
    for r in range(2):
        rs = slice(r * half, (r + 1) * half)
        h_sc[r] = _modulate(_rms(x_ref[rs, :], g_ref[0:1, :]), mod_ref[0, 0], mod_ref[0, 1]).astype(BF16)
    for r in range(2):
        y_sc[r] = jnp.dot(h_sc[r], w_ref[...], preferred_element_type=F32)
    for r in range(2):
        rs = slice(r * half, (r + 1) * half)
        if rope:
            cos, sa, sb = cos_ref[rs, :], sa_ref[rs, :], sb_ref[rs, :]
        for c in range((D_Q + D_KV) // LANES):
            yc = y_sc[r, :, c * LANES:(c + 1) * LANES]
            if rope:
                yc = yc * cos + pltpu.roll(yc, LANES - 16, 1) * sa + pltpu.roll(yc, 16, 1) * sb
            if c < D_Q // LANES:
                q_ref[rs, c * LANES:(c + 1) * LANES] = yc.astype(BF16)
            else:
                c2 = 2 * (c - D_Q // LANES)
                swapped = pltpu.roll(yc, HEAD_DIM, 1)
                k_ref[rs, c2 * LANES:(c2 + 1) * LANES] = jnp.where(low, yc, swapped).astype(BF16)
                k_ref[rs, (c2 + 1) * LANES:(c2 + 2) * LANES] = jnp.where(low, swapped, yc).astype(BF16)
        for blk in range(half // BLOCK):
            v_ref[r * (half // BLOCK) + blk] = y_sc[r, blk * BLOCK:(blk + 1) * BLOCK, D_Q + D_KV:].T.astype(BF16)


def _qkv_call(x2, mod, g, w_qkv, tables, tiles_per_group):
    n = x2.shape[0]
    tm = TOKEN_TILE
    rope = tables is not None
    in_specs = [
        pl.BlockSpec((tm, D_MODEL), lambda i: (i, 0)),
        pl.BlockSpec((1, N_MOD, SUBLANES, D_MODEL), lambda i: (i // tiles_per_group, 0, 0, 0)),
        _const_spec((4, D_MODEL)),
        _const_spec((D_MODEL, D_Q + 2 * D_KV)),
    ]
    args = [x2, mod, g, w_qkv]
    if rope:
        nt = SEQ // tm
        in_specs += [pl.BlockSpec((tm, LANES), lambda i: (i % nt, 0))] * 3
        args += list(tables)
    return pl.pallas_call(
        functools.partial(_qkv_kernel, rope=rope),
        grid=(n // tm,),
        in_specs=in_specs,
        out_specs=[
            pl.BlockSpec((tm, D_Q), lambda i: (i, 0)),
            pl.BlockSpec((tm, D_K2), lambda i: (i, 0)),
            pl.BlockSpec((tm // BLOCK, D_KV, BLOCK), lambda i: (i, 0, 0)),
        ],
        out_shape=[
            jax.ShapeDtypeStruct((n, D_Q), BF16),
            jax.ShapeDtypeStruct((n, D_K2), BF16),
            jax.ShapeDtypeStruct((n // BLOCK, D_KV, BLOCK), BF16),
        ],
        scratch_shapes=[
            pltpu.VMEM((2, tm // 2, D_MODEL), BF16),
            pltpu.VMEM((2, tm // 2, D_Q + 2 * D_KV), F32),
        ],
        compiler_params=_params(("parallel",)),
        name="qkv_rope" if rope else "qkv_ctx",
    )(*args)


VT_ROWS = HEAD_DIM + 16


def _attn_kernel(*refs, local, q_per_step):
    if local:
        sink_ref, q_ref, k_ref, v_ref, kc_ref, vc_ref, o_ref, s_sc, p_sc, bias_sc = refs
    else:
        sink_ref, q_ref, kc_ref, vc_ref, o_ref, s_sc, p_sc, bias_sc = refs
    j = pl.program_id(1)
    seq_blocks = SEQ // BLOCK if local else 0
    ctx_blocks = CTX_LEN // BLOCK
    n_keys = s_sc.shape[1]
    pad_row = lax.broadcasted_iota(jnp.int32, (VT_ROWS - HEAD_DIM, n_keys), 0)
    vt_pad = jnp.where(pad_row == 0, 1.0, 0.0).astype(BF16)

    nt = (((1,), (1,)), ((), ()))
    n_band = 3
    n_chunks = D_Q // LANES
    lane = lax.broadcasted_iota(jnp.int32, (BLOCK, LANES), 1)
    first_head = lax.broadcasted_iota(jnp.int32, (1, 2 * BLOCK), 1) < BLOCK

    def block_params(qb):
        jq = j * q_per_step + qb
        rows = pl.ds(pl.multiple_of(qb * BLOCK, BLOCK), BLOCK)
        if not local:
            return rows, None, None, None, qb // ctx_blocks
        blk0 = jnp.clip(jq - 1, 0, seq_blocks - n_band)
        return rows, blk0, pl.multiple_of(blk0 * BLOCK, BLOCK), qb % 2, 0

    def store_bias(qb):
        if local:
            jq = j * q_per_step + qb
            start = jnp.clip(jq - 1, 0, seq_blocks - n_band) * BLOCK
            kpos = start + lax.broadcasted_iota(jnp.int32, (n_band * BLOCK, BLOCK), 0)
            qpos = jq * BLOCK + lax.broadcasted_iota(jnp.int32, (n_band * BLOCK, BLOCK), 1)
            bias_sc[qb % 2] = jnp.where(jnp.abs(kpos - qpos) <= WINDOW, 0.0, NEG_INF).astype(F32)

    def scores(c, params):
        rows, _, start, slot, cb = params
        kcols = slice((c // 2) * LANES, (c // 2 + 1) * LANES)
        qc = q_ref[rows, c * LANES:(c + 1) * LANES]
        zero = jnp.zeros_like(qc)
        q2 = jnp.concatenate([jnp.where(lane < HEAD_DIM, qc, zero), jnp.where(lane < HEAD_DIM, zero, qc)], axis=0)
        s_sc[c % 2, 0:CTX_LEN] = lax.dot_general(kc_ref[cb, :, kcols], q2, nt, preferred_element_type=F32)
        if local:
            s_band = lax.dot_general(k_ref[0, pl.ds(start, n_band * BLOCK), kcols], q2, nt,
                                     preferred_element_type=F32)
            for hd in range(2):
                s_sc[c % 2, CTX_LEN:, hd * BLOCK:(hd + 1) * BLOCK] = (
                    s_band[:, hd * BLOCK:(hd + 1) * BLOCK] + bias_sc[slot])

    def softmax(c):
        s = s_sc[c % 2]
        sink_row = jnp.where(first_head, sink_ref[2 * c], sink_ref[2 * c + 1])
        m = jnp.maximum(jnp.max(s, axis=0, keepdims=True), sink_row)
        p_sc[c % 2] = jnp.exp2(s - m).astype(BF16)
        return jnp.exp2(sink_row - m)

    def values(c, params, sink_term):
        rows, blk0, _, _, cb = params
        hrows = slice((c // 2) * HEAD_DIM, (c // 2 + 1) * HEAD_DIM)
        vt = [vc_ref[cb * ctx_blocks + i, hrows, :] for i in range(ctx_blocks)]
        if local:
            vt_band = v_ref[pl.ds(blk0, n_band), hrows, :]
            vt += [vt_band[i] for i in range(n_band)]
        vt_aug = jnp.concatenate([jnp.concatenate(vt, axis=1), vt_pad], axis=0)
        acc = jnp.dot(vt_aug, p_sc[c % 2], preferred_element_type=F32)
        out_t = acc[0:HEAD_DIM] / (acc[HEAD_DIM:HEAD_DIM + 1] + sink_term)
        both = jnp.concatenate([out_t[:, :BLOCK], out_t[:, BLOCK:]], axis=0)
        o_ref[rows, c * LANES:(c + 1) * LANES] = both.T.astype(o_ref.dtype)

    def query_block(qb, sink_term0):
        cur = block_params(qb)
        nxt = block_params(jnp.minimum(qb + 1, q_per_step - 1))
        store_bias(qb + 1)
        sink_terms = {0: sink_term0}
        for c in range(n_chunks):
            if c + 2 < n_chunks:
                scores(c + 2, cur)
            else:
                scores(c + 2 - n_chunks, nxt)
            sink_terms[c + 1] = softmax((c + 1) % n_chunks)
            values(c, cur, sink_terms[c])
        return sink_terms[n_chunks]

    first = block_params(0)
    store_bias(0)
    scores(0, first)
    scores(1, first)
    lax.fori_loop(0, q_per_step, query_block, softmax(0))


def _attn_scratch(n_keys):
    return [
        pltpu.VMEM((2, n_keys, 2 * BLOCK), F32),
        pltpu.VMEM((2, n_keys, 2 * BLOCK), BF16),
        pltpu.VMEM((2, 3 * BLOCK, BLOCK), F32),
    ]


def _attn_call(sink2, q, k, vt, kc, vtc):
    nb = SEQ // BLOCK
    nbc = CTX_LEN // BLOCK
    qps = ATTN_Q_PER_STEP
    steps = nb // qps
    return pl.pallas_call(
        functools.partial(_attn_kernel, local=True, q_per_step=qps),
        grid=(BATCH, steps),
        in_specs=[
            pl.BlockSpec(memory_space=pltpu.SMEM),
            pl.BlockSpec((qps * BLOCK, D_Q), lambda b, j: (b * steps + j, 0)),
            pl.BlockSpec((1, SEQ, D_K2), lambda b, j: (b, 0, 0)),
            pl.BlockSpec((nb, D_KV, BLOCK), lambda b, j: (b, 0, 0)),
            pl.BlockSpec((1, CTX_LEN, D_K2), lambda b, j: (b, 0, 0)),
            pl.BlockSpec((nbc, D_KV, BLOCK), lambda b, j: (b, 0, 0)),
        ],
        out_specs=pl.BlockSpec((qps * BLOCK, D_Q), lambda b, j: (b * steps + j, 0)),
        out_shape=jax.ShapeDtypeStruct((BATCH * SEQ, D_Q), BF16),
        scratch_shapes=_attn_scratch(CTX_LEN + 3 * BLOCK),
        compiler_params=_params(("parallel", "arbitrary")),
        name="band_attn",
    )(sink2, q, k, vt, kc, vtc)


def _ctx_attn_call(sink2, qc, kc, vtc):
    nb = BATCH * CTX_LEN // BLOCK
    whole = lambda shape: pl.BlockSpec(shape, lambda b, j: (0,) * len(shape))
    return pl.pallas_call(
        functools.partial(_attn_kernel, local=False, q_per_step=nb),
        grid=(1, 1),
        in_specs=[
            pl.BlockSpec(memory_space=pltpu.SMEM),
            whole((BATCH * CTX_LEN, D_Q)),
            whole((BATCH, CTX_LEN, D_K2)),
            whole((nb, D_KV, BLOCK)),
        ],
        out_specs=whole((BATCH * CTX_LEN, D_Q)),
        out_shape=jax.ShapeDtypeStruct((BATCH * CTX_LEN, D_Q), BF16),
        scratch_shapes=_attn_scratch(CTX_LEN),
        compiler_params=_params(("arbitrary", "arbitrary")),
        name="ctx_attn",
    )(sink2, qc, kc, vtc)


def _to_time_major(src_ref, sc_ref, t0=0, nt=None):
    nt = src_ref.shape[1] if nt is None else nt
    r0 = t0 * SUBLANES
    for b in range(BATCH):
        for s in range(N_SLAB):
            sc_ref[s, pl.ds(r0 + b, nt, stride=SUBLANES), :] = src_ref[b, t0:t0 + nt, s * LANES:(s + 1) * LANES]
    return jnp.concatenate([sc_ref[s, r0:r0 + nt * SUBLANES, :] for s in range(N_SLAB)], axis=1)


def _from_time_major(val, sc_ref, dst_ref, t0, nt):
    r0 = t0 * SUBLANES
    for s in range(N_SLAB):
        sc_ref[s, r0:r0 + nt * SUBLANES, :] = val[:, s * LANES:(s + 1) * LANES]
    for b in range(BATCH):
        for s in range(N_SLAB):
            dst_ref[b, t0:t0 + nt, s * LANES:(s + 1) * LANES] = sc_ref[s, pl.ds(r0 + b, nt, stride=SUBLANES), :]


def _post_kernel(*refs, lru):
    if lru:
        (x_ref, gate_ref, yf_ref, yb_ref, mod_ref, g_ref, wf_ref, w1_ref, w2_ref, o_ref,
         x1_sc, h_sc, acc_sc, tin_sc, tout_sc) = refs
        half_rows = gate_ref.shape[0] // 2
    else:
        x_ref, a_ref, mod_ref, g_ref, wf_ref, w1_ref, w2_ref, o_ref, x1_sc, h_sc, acc_sc = refs
        half_rows = x_ref.shape[0] // 2
    half_t = half_rows // BATCH

    def head(r):
        rs = slice(r * half_rows, (r + 1) * half_rows)
        if lru:
            front = (gate_ref[rs, :].astype(F32)
                     * (yf_ref[rs, :].astype(F32) + yb_ref[rs, :].astype(F32))).astype(BF16)
            x = _to_time_major(x_ref, tin_sc, r * half_t, half_t)
        else:
            front = a_ref[rs, :]
            x = x_ref[rs, :]
        y = jnp.dot(front, wf_ref[...], preferred_element_type=F32)
        x1 = _gated_add(x, mod_ref[0, 2], _rms(y, g_ref[1:2, :]))
        x1_sc[r] = x1
        h_sc[r] = _modulate(_rms(x1, g_ref[2:3, :]), mod_ref[0, 3], mod_ref[0, 4]).astype(BF16)

    def mlp(r):
        acc = jnp.zeros((half_rows, D_MODEL), F32)
        for c in range(D_FF // FF_CHUNK):
            hid = jnp.dot(h_sc[r], w1_ref[:, c * FF_CHUNK:(c + 1) * FF_CHUNK], preferred_element_type=F32)
            hid = jnp.square(jnp.maximum(hid, 0.0)).astype(BF16)
            acc = acc + jnp.dot(hid, w2_ref[c * FF_CHUNK:(c + 1) * FF_CHUNK, :], preferred_element_type=F32)
        acc_sc[r] = acc

    def tail(r):
        out = _gated_add(x1_sc[r], mod_ref[0, 5], _rms(acc_sc[r], g_ref[3:4, :]))
        if lru:
            _from_time_major(out, tout_sc, o_ref, r * half_t, half_t)
        else:
            o_ref[r * half_rows:(r + 1) * half_rows, :] = out

    head(0)
    head(1)
    mlp(0)
    tail(0)
    mlp(1)
    tail(1)


def _post_call(x, fronts, mod, g, w_front, w1, w2, layer, tiles_per_group, lru):
    tm = TOKEN_TILE
    layer_spec = lambda shape: pl.BlockSpec((None,) + shape[1:], lambda i: (layer, 0, 0),
                                            pipeline_mode=pl.Buffered(1))
    row = lambda i: (i, 0)
    if lru:
        nt = tm // BATCH
        n = x.shape[0] * x.shape[1]
        x_spec = pl.BlockSpec((BATCH, nt, D_MODEL), lambda i: (0, i, 0))
        scratch = [pltpu.VMEM((N_SLAB, tm, LANES), F32), pltpu.VMEM((N_SLAB, tm, LANES), F32)]
    else:
        n = x.shape[0]
        x_spec = pl.BlockSpec((tm, D_MODEL), row)
        scratch = []
    scratch = [
        pltpu.VMEM((2, tm // 2, D_MODEL), F32),
        pltpu.VMEM((2, tm // 2, D_MODEL), BF16),
        pltpu.VMEM((2, tm // 2, D_MODEL), F32),
    ] + scratch
    in_specs = [x_spec]
    in_specs += [pl.BlockSpec((tm, f.shape[1]), row) for f in fronts]
    in_specs += [
        pl.BlockSpec((1, N_MOD, SUBLANES, D_MODEL), lambda i: (i // tiles_per_group, 0, 0, 0)),
        _const_spec((4, D_MODEL)),
        _const_spec(w_front.shape),
        layer_spec(w1.shape),
        layer_spec(w2.shape),
    ]
    return pl.pallas_call(
        functools.partial(_post_kernel, lru=lru),
        grid=(n // tm,),
        in_specs=in_specs,
        out_specs=x_spec,
        out_shape=jax.ShapeDtypeStruct(x.shape, F32),
        scratch_shapes=scratch,
        compiler_params=_params(("parallel",)),
        name="lru_out_mlp" if lru else "attn_out_mlp",
    )(x, *fronts, mod, g, w_front, w1, w2)


def _lru_in_kernel(*refs, need_gate):
    if need_gate:
        (x_ref, xp_ref, xn_ref, mod_ref, g_ref, w_ref, cw_ref, cb_ref, gate_ref, u_ref,
         v_sc, h_sc, t_sc, tp_sc, tn_sc) = refs
    else:
        x_ref, xp_ref, xn_ref, mod_ref, g_ref, w_ref, cw_ref, cb_ref, u_ref, v_sc, h_sc, t_sc, tp_sc, tn_sc = refs
    i = pl.program_id(0)
    n = pl.num_programs(0)
    rows = x_ref.shape[0] * x_ref.shape[1]
    half = rows // 2
    half_t = x_ref.shape[1] // 2
    s8 = SUBLANES

    def pre(x):
        return _modulate(_rms(x, g_ref[0:1, :]), mod_ref[0, 0], mod_ref[0, 1]).astype(BF16)

    h_sc[0, 0:HALO] = pre(_to_time_major(xp_ref, tp_sc)[SUBLANES * SUBLANES - HALO:])
    h_sc[0, HALO:] = pre(_to_time_major(x_ref, t_sc, 0, half_t))
    h_sc[1, 0:half] = pre(_to_time_major(x_ref, t_sc, half_t, half_t))
    h_sc[1, half:] = pre(_to_time_major(xn_ref, tn_sc)[:HALO])
    ext = half + HALO
    for r in range(2):
        v_sc[r * ext:(r + 1) * ext] = jnp.dot(h_sc[r], w_ref[:, D_RNN:], preferred_element_type=F32)
    v_sc[0:HALO] = v_sc[0:HALO] * (i > 0).astype(F32)
    v_sc[HALO + rows:HALO + rows + s8] = v_sc[HALO + rows:HALO + rows + s8] * (i < n - 1).astype(F32)
    for r in range(2):
        if need_gate:
            h_r = h_sc[0, HALO:] if r == 0 else h_sc[1, 0:half]
            gate_ref[r * half:(r + 1) * half, :] = jax.nn.gelu(
                jnp.dot(h_r, w_ref[:, :D_RNN], preferred_element_type=F32)).astype(BF16)
        base = HALO + r * half
        u_ref[r * half:(r + 1) * half, :] = (
            cb_ref[...]
            + cw_ref[0:1, :] * v_sc[base - 2 * s8:base - 2 * s8 + half]
            + cw_ref[1:2, :] * v_sc[base - s8:base - s8 + half]
            + cw_ref[2:3, :] * v_sc[base:base + half]
            + cw_ref[3:4, :] * v_sc[base + s8:base + s8 + half]).astype(u_ref.dtype)


def _lru_in_call(x3, mod, g, w_in, conv_w, conv_b, need_gate):
    t_total = x3.shape[1]
    n = BATCH * t_total
    tm = TOKEN_TILE
    nt = tm // BATCH
    row = lambda i: (i, 0)
    per_tile = nt // SUBLANES
    last = t_total // SUBLANES - 1
    halo_spec = lambda f: pl.BlockSpec((BATCH, SUBLANES, D_MODEL), f)
    out_specs = [pl.BlockSpec((tm, D_RNN), row), pl.BlockSpec((tm, D_RNN), row)]
    out_shape = [jax.ShapeDtypeStruct((n, D_RNN), BF16), jax.ShapeDtypeStruct((n, D_RNN), U_DTYPE)]
    if not need_gate:
        out_specs, out_shape = out_specs[1:], out_shape[1:]
    return pl.pallas_call(
        functools.partial(_lru_in_kernel, need_gate=need_gate),
        grid=(n // tm,),
        in_specs=[
            pl.BlockSpec((BATCH, nt, D_MODEL), lambda i: (0, i, 0)),
            halo_spec(lambda i: (0, jnp.maximum(i * per_tile - 1, 0), 0)),
            halo_spec(lambda i: (0, jnp.minimum((i + 1) * per_tile, last), 0)),
            _const_spec((1, N_MOD, SUBLANES, D_MODEL)),
            _const_spec((4, D_MODEL)),
            _const_spec((D_MODEL, 2 * D_RNN)),
            _const_spec((CONV_W, D_RNN)),
            _const_spec((1, D_RNN)),
        ],
        out_specs=out_specs,
        out_shape=out_shape,
        scratch_shapes=[
            pltpu.VMEM((tm + 2 * HALO, D_RNN), F32),
            pltpu.VMEM((2, tm // 2 + HALO, D_MODEL), BF16),
            pltpu.VMEM((N_SLAB, tm, LANES), F32),
            pltpu.VMEM((N_SLAB, SUBLANES * SUBLANES, LANES), F32),
            pltpu.VMEM((N_SLAB, SUBLANES * SUBLANES, LANES), F32),
        ],
        compiler_params=_params(("parallel",)),
        name="lru_in",
    )(x3, x3, x3, mod, g, w_in, conv_w, conv_b)


def _scan_kernel(uf_ref, ub_ref, h0_ref, wa_ref, ba_ref, wi_ref, bi_ref, lam_ref, yf_ref, yb_ref, ht_ref,
                 a_sc, bx_sc, h_sc):
    i = pl.program_id(0)
    n = pl.num_programs(0)
    rows = uf_ref.shape[0]
    nt = rows // SUBLANES
    s8 = SUBLANES

    @pl.when(i == 0)
    def _():
        h_sc[...] = h0_ref[...]

    for d, u_ref in enumerate((uf_ref, ub_ref)):
        for c in range(N_LRU_BLOCKS):
            cs = slice(c * LRU_BLOCK_W, (c + 1) * LRU_BLOCK_W)
            u16 = u_ref[:, cs].astype(BF16)
            u = u_ref[:, cs].astype(F32)
            ta = jnp.tanh(jnp.dot(u16, wa_ref[d, c], preferred_element_type=F32) + 0.5 * ba_ref[d, :, cs])
            ti = jnp.tanh(jnp.dot(u16, wi_ref[d, c], preferred_element_type=F32) + 0.5 * bi_ref[d, :, cs])
            neg_lam = -lam_ref[d, :, cs]
            softplus = jnp.maximum(neg_lam, 0.0) + jnp.log1p(jnp.exp(-jnp.abs(neg_lam)))
            k = (-0.5 * LRU_C * LOG2E) * softplus
            a = jnp.exp2(k * ta + k)
            w = 1.0 - a * a
            root = w * lax.rsqrt(jnp.maximum(w, 1e-30))
            a_sc[d, :, cs] = a
            bx_sc[d, :, cs] = root * (ti * u + u)

    def step(t, carry):
        hf, hb = carry
        rf = pl.multiple_of(t * 2 * s8, 2 * s8)
        rb = pl.multiple_of((nt - 2 - 2 * t) * s8, 2 * s8)
        hf1 = a_sc[0, pl.ds(rf, s8), :] * hf + bx_sc[0, pl.ds(rf, s8), :]
        hf2 = a_sc[0, pl.ds(rf + s8, s8), :] * hf1 + bx_sc[0, pl.ds(rf + s8, s8), :]
        yf_ref[pl.ds(rf, 2 * s8), :] = jnp.concatenate([hf1, hf2], axis=0).astype(yf_ref.dtype)
        hb1 = a_sc[1, pl.ds(rb + s8, s8), :] * hb + bx_sc[1, pl.ds(rb + s8, s8), :]
        hb2 = a_sc[1, pl.ds(rb, s8), :] * hb1 + bx_sc[1, pl.ds(rb, s8), :]
        yb_ref[pl.ds(rb, 2 * s8), :] = jnp.concatenate([hb2, hb1], axis=0).astype(yb_ref.dtype)
        return hf2, hb2

    hf, hb = lax.fori_loop(0, nt // 2, step, (h_sc[0], h_sc[1]), unroll=2)
    h_sc[0] = hf
    h_sc[1] = hb

    @pl.when(i == n - 1)
    def _():
        ht_ref[...] = h_sc[...]


def _scan_call(u2, h0, w_a, b_a, w_i, b_i, lam):
    rows_total = u2.shape[0]
    rows = SCAN_T * SUBLANES
    n = rows_total // rows
    w = D_RNN
    fwd = lambda i: (i, 0)
    bwd = lambda i: (n - 1 - i, 0)
    return pl.pallas_call(
        _scan_kernel,
        grid=(n,),
        in_specs=[
            pl.BlockSpec((rows, w), fwd),
            pl.BlockSpec((rows, w), bwd),
            _const_spec((2, SUBLANES, w)),
            _const_spec((2, N_LRU_BLOCKS, LRU_BLOCK_W, LRU_BLOCK_W)),
            _const_spec((2, 1, w)),
            _const_spec((2, N_LRU_BLOCKS, LRU_BLOCK_W, LRU_BLOCK_W)),
            _const_spec((2, 1, w)),
            _const_spec((2, 1, w)),
        ],
        out_specs=[
            pl.BlockSpec((rows, w), fwd),
            pl.BlockSpec((rows, w), bwd),
            pl.BlockSpec((2, SUBLANES, w), lambda i: (0, 0, 0)),
        ],
        out_shape=[
            jax.ShapeDtypeStruct((rows_total, w), Y_DTYPE),
            jax.ShapeDtypeStruct((rows_total, w), Y_DTYPE),
            jax.ShapeDtypeStruct((2, SUBLANES, w), F32),
        ],
        scratch_shapes=[
            pltpu.VMEM((2, rows, w), F32),
            pltpu.VMEM((2, rows, w), F32),
            pltpu.VMEM((2, SUBLANES, w), F32),
        ],
        compiler_params=_params(("arbitrary",)),
        name="lru_scan",
    )(u2, u2, h0, w_a, b_a, w_i, b_i, lam)


def _rope_tables():
    t = np.arange(SEQ)
    row = (t // GRID_W).astype(np.float64)
    col = (t % GRID_W).astype(np.float64)
    half = HEAD_DIM // 2
    inv = ROPE_BASE ** (-np.arange(0, half, 2, dtype=np.float64) / half)
    ang_r = row[:, None] * inv[None, :]
    ang_c = col[:, None] * inv[None, :]
    ang = np.concatenate([ang_r, ang_r, ang_c, ang_c], axis=-1)
    ang = np.tile(ang, (1, LANES // HEAD_DIM))
    low = (np.arange(LANES) % 32) < 16
    sin = np.sin(ang)
    tables = (np.cos(ang), np.where(low, -sin, 0.0), np.where(low, 0.0, sin))
    return tuple(jnp.asarray(a, dtype=F32) for a in tables)


def kernel(x, c, ctx, c_ctx, ada_w, ada_b, norm_g, mlp_w1, mlp_w2, attn_w_qkv, attn_w_o, attn_sink,
           lru_w_in, lru_conv_w, lru_conv_b, lru_w_a, lru_b_a, lru_w_i, lru_b_i, lru_lam, lru_w_out):
    n_lat = BATCH * SEQ
    n_ctx = BATCH * CTX_LEN

    c16 = jnp.zeros((16, D_MODEL), F32).at[:BATCH].set(c).at[BATCH].set(c_ctx)
    mods = _mod_call(c16, ada_w, ada_b).reshape(2, 16, N_MOD, D_MODEL)

    def slab_bmajor(m):
        return jnp.broadcast_to(m[:, :, None, :], (BATCH, N_MOD, SUBLANES, D_MODEL))

    def slab_ctx(m):
        return jnp.broadcast_to(m[None, :, None, :], (1, N_MOD, SUBLANES, D_MODEL))

    mod_x0 = slab_bmajor(mods[0, :BATCH])
    mod_c0 = slab_ctx(mods[0, BATCH])
    w_qkv = attn_w_qkv[0]
    w_qkv = jnp.concatenate([w_qkv[:, :D_Q] * (HEAD_DIM ** -0.5 * LOG2E), w_qkv[:, D_Q:]], axis=1).astype(BF16)
    sink2 = attn_sink[0] * LOG2E
    w_o = attn_w_o[0].astype(BF16)
    w1_all, w2_all = mlp_w1.astype(BF16), mlp_w2.astype(BF16)
    g0 = norm_g[0]
    tiles_per_batch = SEQ // TOKEN_TILE

    x2 = x.reshape(n_lat, D_MODEL)
    c2 = ctx.reshape(n_ctx, D_MODEL)
    q, k, v = _qkv_call(x2, mod_x0, g0, w_qkv, _rope_tables(), tiles_per_batch)
    qc, kc, vc = _qkv_call(c2, mod_c0, g0, w_qkv, None, n_ctx // TOKEN_TILE)
    kc3 = kc.reshape(BATCH, CTX_LEN, D_K2)
    att = _attn_call(sink2, q, k.reshape(BATCH, SEQ, D_K2), v, kc3, vc)
    att_c = _ctx_attn_call(sink2, qc, kc3, vc)
    x2 = _post_call(x2, [att], mod_x0, g0, w_o, w1_all, w2_all, 0, tiles_per_batch, lru=False)
    c2 = _post_call(c2, [att_c], mod_c0, g0, w_o, w1_all, w2_all, 0, n_ctx // TOKEN_TILE, lru=False)

    x3 = x2.reshape(BATCH, SEQ, D_MODEL)
    c3 = c2.reshape(BATCH, CTX_LEN, D_MODEL)
    mod_x1 = mods[1, :BATCH].transpose(1, 0, 2)[None]
    mod_c1 = slab_ctx(mods[1, BATCH])
    g1 = norm_g[1]
    w_in = lru_w_in[0].astype(BF16)
    conv_w = 0.5 * lru_conv_w[0]
    conv_b = 0.5 * lru_conv_b[0].reshape(1, D_RNN)
    w_a, w_i = lru_w_a[0].astype(BF16), lru_w_i[0].astype(BF16)
    b_a, b_i = lru_b_a[0].reshape(2, 1, D_RNN), lru_b_i[0].reshape(2, 1, D_RNN)
    lam = lru_lam[0].reshape(2, 1, D_RNN)
    scan = functools.partial(_scan_call, w_a=w_a, b_a=b_a, w_i=w_i, b_i=b_i, lam=lam)

    (u_c,) = _lru_in_call(c3, mod_c1, g1, w_in, conv_w, conv_b, need_gate=False)
    _, _, h_ctx = scan(u_c, jnp.zeros((2, SUBLANES, D_RNN), F32))
    gate_x, u_x = _lru_in_call(x3, mod_x1, g1, w_in, conv_w, conv_b, need_gate=True)
    yf, yb, _ = scan(u_x, h_ctx)
    return _post_call(x3, [gate_x, yf, yb], mod_x1, g1, lru_w_out[0].astype(BF16),
                      w1_all, w2_all, 1, n_lat // TOKEN_TILE, lru=True)
```

```python
import functools

import jax
import jax.numpy as jnp
import numpy as np
from jax import lax
from jax.experimental import pallas as pl
from jax.experimental.pallas import tpu as pltpu

D_MODEL = 1024
BATCH = 8
SEQ = 2048
GRID_W = 64
CTX_LEN = 256
HEAD_DIM = 64
N_HEADS = 16
N_KV_HEADS = 4
GQA_GROUP = N_HEADS // N_KV_HEADS
WINDOW = 128
BLOCK = 128
ROPE_BASE = 10000.0
D_RNN = 1280
LRU_BLOCK_W = 256
N_LRU_BLOCKS = D_RNN // LRU_BLOCK_W
CONV_W = 4
LRU_C = 8.0
D_FF = 4 * D_MODEL
N_MOD = 6
EPS = 1e-6
NEG_INF = -1e30

D_Q = N_HEADS * HEAD_DIM
D_KV = N_KV_HEADS * HEAD_DIM
D_K2 = 2 * D_KV
LANES = 128
SUBLANES = 8
N_SLAB = D_MODEL // LANES
TOKEN_TILE = 512
FF_CHUNK = 1024
ATTN_Q_PER_STEP = 16
SCAN_T = 128
HALO = 16
U_DTYPE = jnp.bfloat16
Y_DTYPE = jnp.bfloat16
LOG2E = 1.4426950408889634
VMEM_LIMIT = 60 * 1024 * 1024

F32 = jnp.float32
BF16 = jnp.bfloat16


def _rms(x, g):
    ms = jnp.mean(x * x, axis=-1, keepdims=True)
    return x * lax.rsqrt(ms + EPS) * g


def _slab(x):
    return x.reshape(x.shape[0] // SUBLANES, SUBLANES, x.shape[1])


def _modulate(h, shift8, scale8):
    out = _slab(h) * (1.0 + scale8)[None] + shift8[None]
    return out.reshape(h.shape)


def _gated_add(x, gate8, y):
    out = _slab(x) + gate8[None] * _slab(y)
    return out.reshape(x.shape)


def _const_spec(shape):
    n = len(shape)
    return pl.BlockSpec(shape, lambda *_: (0,) * n, pipeline_mode=pl.Buffered(1))


def _params(sem, flags=None):
    return pltpu.CompilerParams(dimension_semantics=sem, vmem_limit_bytes=VMEM_LIMIT, flags=flags)


def _mod_kernel(c_ref, w_ref, b_ref, o_ref):
    s = jax.nn.silu(c_ref[...]).astype(BF16)
    o_ref[0] = jnp.dot(s, w_ref[0].astype(BF16), preferred_element_type=F32) + b_ref[0]


def _mod_call(c16, ada_w, ada_b):
    depth = ada_w.shape[0]
    nt = 1024
    return pl.pallas_call(
        _mod_kernel,
        grid=(depth, N_MOD * D_MODEL // nt),
        in_specs=[
            pl.BlockSpec((16, D_MODEL), lambda l, j: (0, 0)),
            pl.BlockSpec((1, D_MODEL, nt), lambda l, j: (l, 0, j)),
            pl.BlockSpec((1, 1, nt), lambda l, j: (l, 0, j)),
        ],
        out_specs=pl.BlockSpec((1, 16, nt), lambda l, j: (l, 0, j)),
        out_shape=jax.ShapeDtypeStruct((depth, 16, N_MOD * D_MODEL), F32),
        compiler_params=_params(("arbitrary", "arbitrary")),
        name="adaln_mod",
    )(c16, ada_w, ada_b.reshape(depth, 1, N_MOD * D_MODEL))


def _qkv_kernel(*refs, rope):
    if rope:
        x_ref, mod_ref, g_ref, w_ref, cos_ref, sa_ref, sb_ref, q_ref, k_ref, v_ref, h_sc, y_sc = refs
    else:
        x_ref, mod_ref, g_ref, w_ref, q_ref, k_ref, v_ref, h_sc, y_sc = refs
    half = x_ref.shape[0] // 2
    low = lax.broadcasted_iota(jnp.int32, (half, LANES), 1) < HEAD_DIM
    for r in range(2):
        rs = slice(r * half, (r + 1) * half)
        h_sc[r] = _modulate(_rms(x_ref[rs, :], g_ref[0:1, :]), mod_ref[0, 0], mod_ref[0, 1]).astype(BF16)
    for r in range(2):
        y_sc[r] = jnp.dot(h_sc[r], w_ref[...], preferred_element_type=F32)
    for r in range(2):
        rs = slice(r * half, (r + 1) * half)
        if rope:
            cos, sa, sb = cos_ref[rs, :], sa_ref[rs, :], sb_ref[rs, :]
        for c in range((D_Q + D_KV) // LANES):
            yc = y_sc[r, :, c * LANES:(c + 1) * LANES]
            if rope:
                yc = yc * cos + pltpu.roll(yc, LANES - 16, 1) * sa + pltpu.roll(yc, 16, 1) * sb
            if c < D_Q // LANES:
                q_ref[rs, c * LANES:(c + 1) * LANES] = yc.astype(BF16)
            else:
                c2 = 2 * (c - D_Q // LANES)
                swapped = pltpu.roll(yc, HEAD_DIM, 1)
                k_ref[rs, c2 * LANES:(c2 + 1) * LANES] = jnp.where(low, yc, swapped).astype(BF16)
                k_ref[rs, (c2 + 1) * LANES:(c2 + 2) * LANES] = jnp.where(low, swapped, yc).astype(BF16)
        for blk in range(half // BLOCK):
            v_ref[r * (half // BLOCK) + blk] = y_sc[r, blk * BLOCK:(blk + 1) * BLOCK, D_Q + D_KV:].T.astype(BF16)


def _qkv_call(x2, mod, g, w_qkv, tables, tiles_per_group):
    n = x2.shape[0]
    tm = TOKEN_TILE
    rope = tables is not None
    in_specs = [
        pl.BlockSpec((tm, D_MODEL), lambda i: (i, 0)),
        pl.BlockSpec((1, N_MOD, SUBLANES, D_MODEL), lambda i: (i // tiles_per_group, 0, 0, 0)),
        _const_spec((4, D_MODEL)),
        _const_spec((D_MODEL, D_Q + 2 * D_KV)),
    ]
    args = [x2, mod, g, w_qkv]
    if rope:
        nt = SEQ // tm
        in_specs += [pl.BlockSpec((tm, LANES), lambda i: (i % nt, 0))] * 3
        args += list(tables)
    return pl.pallas_call(
        functools.partial(_qkv_kernel, rope=rope),
        grid=(n // tm,),
        in_specs=in_specs,
        out_specs=[
            pl.BlockSpec((tm, D_Q), lambda i: (i, 0)),
            pl.BlockSpec((tm, D_K2), lambda i: (i, 0)),
            pl.BlockSpec((tm // BLOCK, D_KV, BLOCK), lambda i: (i, 0, 0)),
        ],
        out_shape=[
            jax.ShapeDtypeStruct((n, D_Q), BF16),
            jax.ShapeDtypeStruct((n, D_K2), BF16),
            jax.ShapeDtypeStruct((n // BLOCK, D_KV, BLOCK), BF16),
        ],
        scratch_shapes=[
            pltpu.VMEM((2, tm // 2, D_MODEL), BF16),
            pltpu.VMEM((2, tm // 2, D_Q + 2 * D_KV), F32),
        ],
        compiler_params=_params(("parallel",)),
        name="qkv_rope" if rope else "qkv_ctx",
    )(*args)


VT_ROWS = HEAD_DIM + 16


def _attn_kernel(*refs, local, q_per_step):
    if local:
        sink_ref, q_ref, k_ref, v_ref, kc_ref, vc_ref, o_ref, s_sc, p_sc, bias_sc = refs
    else:
        sink_ref, q_ref, kc_ref, vc_ref, o_ref, s_sc, p_sc, bias_sc = refs
    j = pl.program_id(1)
    seq_blocks = SEQ // BLOCK if local else 0
    ctx_blocks = CTX_LEN // BLOCK
    n_keys = s_sc.shape[1]
    pad_row = lax.broadcasted_iota(jnp.int32, (VT_ROWS - HEAD_DIM, n_keys), 0)
    vt_pad = jnp.where(pad_row == 0, 1.0, 0.0).astype(BF16)

    nt = (((1,), (1,)), ((), ()))
    n_band = 3
    n_chunks = D_Q // LANES
    lane = lax.broadcasted_iota(jnp.int32, (BLOCK, LANES), 1)
    first_head = lax.broadcasted_iota(jnp.int32, (1, 2 * BLOCK), 1) < BLOCK

    def block_params(qb):
        jq = j * q_per_step + qb
        rows = pl.ds(pl.multiple_of(qb * BLOCK, BLOCK), BLOCK)
        if not local:
            return rows, None, None, None, qb // ctx_blocks
        blk0 = jnp.clip(jq - 1, 0, seq_blocks - n_band)
        return rows, blk0, pl.multiple_of(blk0 * BLOCK, BLOCK), qb % 2, 0

    def store_bias(qb):
        if local:
            jq = j * q_per_step + qb
            start = jnp.clip(jq - 1, 0, seq_blocks - n_band) * BLOCK
            kpos = start + lax.broadcasted_iota(jnp.int32, (n_band * BLOCK, BLOCK), 0)
            qpos = jq * BLOCK + lax.broadcasted_iota(jnp.int32, (n_band * BLOCK, BLOCK), 1)
            bias_sc[qb % 2] = jnp.where(jnp.abs(kpos - qpos) <= WINDOW, 0.0, NEG_INF).astype(F32)

    def scores(c, params):
        rows, _, start, slot, cb = params
        kcols = slice((c // 2) * LANES, (c // 2 + 1) * LANES)
        qc = q_ref[rows, c * LANES:(c + 1) * LANES]
        zero = jnp.zeros_like(qc)
        q2 = jnp.concatenate([jnp.where(lane < HEAD_DIM, qc, zero), jnp.where(lane < HEAD_DIM, zero, qc)], axis=0)
        s_sc[c % 2, 0:CTX_LEN] = lax.dot_general(kc_ref[cb, :, kcols], q2, nt, preferred_element_type=F32)
        if local:
            s_band = lax.dot_general(k_ref[0, pl.ds(start, n_band * BLOCK), kcols], q2, nt,
                                     preferred_element_type=F32)
            for hd in range(2):
                s_sc[c % 2, CTX_LEN:, hd * BLOCK:(hd + 1) * BLOCK] = (
                    s_band[:, hd * BLOCK:(hd + 1) * BLOCK] + bias_sc[slot])

    def softmax(c):
        s = s_sc[c % 2]
        sink_row = jnp.where(first_head, sink_ref[2 * c], sink_ref[2 * c + 1])
        m = jnp.maximum(jnp.max(s, axis=0, keepdims=True), sink_row)
        p_sc[c % 2] = jnp.exp2(s - m).astype(BF16)
        return jnp.exp2(sink_row - m)

    def values(c, params, sink_term):
        rows, blk0, _, _, cb = params
        hrows = slice((c // 2) * HEAD_DIM, (c // 2 + 1) * HEAD_DIM)
        vt = [vc_ref[cb * ctx_blocks + i, hrows, :] for i in range(ctx_blocks)]
        if local:
            vt_band = v_ref[pl.ds(blk0, n_band), hrows, :]
            vt += [vt_band[i] for i in range(n_band)]
        vt_aug = jnp.concatenate([jnp.concatenate(vt, axis=1), vt_pad], axis=0)
        acc = jnp.dot(vt_aug, p_sc[c % 2], preferred_element_type=F32)
        out_t = acc[0:HEAD_DIM] / (acc[HEAD_DIM:HEAD_DIM + 1] + sink_term)
        both = jnp.concatenate([out_t[:, :BLOCK], out_t[:, BLOCK:]], axis=0)
        o_ref[rows, c * LANES:(c + 1) * LANES] = both.T.astype(o_ref.dtype)

    def query_block(qb, sink_term0):
        cur = block_params(qb)
        nxt = block_params(jnp.minimum(qb + 1, q_per_step - 1))
        store_bias(qb + 1)
        sink_terms = {0: sink_term0}
        for c in range(n_chunks):
            if c + 2 < n_chunks:
                scores(c + 2, cur)
            else:
                scores(c + 2 - n_chunks, nxt)
            sink_terms[c + 1] = softmax((c + 1) % n_chunks)
            values(c, cur, sink_terms[c])
        return sink_terms[n_chunks]

    first = block_params(0)
    store_bias(0)
    scores(0, first)
    scores(1, first)
    lax.fori_loop(0, q_per_step, query_block, softmax(0))


def _attn_scratch(n_keys):
    return [
        pltpu.VMEM((2, n_keys, 2 * BLOCK), F32),
        pltpu.VMEM((2, n_keys, 2 * BLOCK), BF16),
        pltpu.VMEM((2, 3 * BLOCK, BLOCK), F32),
    ]


def _attn_call(sink2, q, k, vt, kc, vtc):
    nb = SEQ // BLOCK
    nbc = CTX_LEN // BLOCK
    qps = ATTN_Q_PER_STEP
    steps = nb // qps
    return pl.pallas_call(
        functools.partial(_attn_kernel, local=True, q_per_step=qps),
        grid=(BATCH, steps),
        in_specs=[
            pl.BlockSpec(memory_space=pltpu.SMEM),
            pl.BlockSpec((qps * BLOCK, D_Q), lambda b, j: (b * steps + j, 0)),
            pl.BlockSpec((1, SEQ, D_K2), lambda b, j: (b, 0, 0)),
            pl.BlockSpec((nb, D_KV, BLOCK), lambda b, j: (b, 0, 0)),
            pl.BlockSpec((1, CTX_LEN, D_K2), lambda b, j: (b, 0, 0)),
            pl.BlockSpec((nbc, D_KV, BLOCK), lambda b, j: (b, 0, 0)),
        ],
        out_specs=pl.BlockSpec((qps * BLOCK, D_Q), lambda b, j: (b * steps + j, 0)),
        out_shape=jax.ShapeDtypeStruct((BATCH * SEQ, D_Q), BF16),
        scratch_shapes=_attn_scratch(CTX_LEN + 3 * BLOCK),
        compiler_params=_params(("parallel", "arbitrary")),
        name="band_attn",
    )(sink2, q, k, vt, kc, vtc)


def _ctx_attn_call(sink2, qc, kc, vtc):
    nb = BATCH * CTX_LEN // BLOCK
    whole = lambda shape: pl.BlockSpec(shape, lambda b, j: (0,) * len(shape))
    return pl.pallas_call(
        functools.partial(_attn_kernel, local=False, q_per_step=nb),
        grid=(1, 1),
        in_specs=[
            pl.BlockSpec(memory_space=pltpu.SMEM),
            whole((BATCH * CTX_LEN, D_Q)),
            whole((BATCH, CTX_LEN, D_K2)),
            whole((nb, D_KV, BLOCK)),
        ],
        out_specs=whole((BATCH * CTX_LEN, D_Q)),
        out_shape=jax.ShapeDtypeStruct((BATCH * CTX_LEN, D_Q), BF16),
        scratch_shapes=_attn_scratch(CTX_LEN),
        compiler_params=_params(("arbitrary", "arbitrary")),
        name="ctx_attn",
    )(sink2, qc, kc, vtc)


def _to_time_major(src_ref, sc_ref, t0=0, nt=None):
    nt = src_ref.shape[1] if nt is None else nt
    r0 = t0 * SUBLANES
    for b in range(BATCH):
        for s in range(N_SLAB):
            sc_ref[s, pl.ds(r0 + b, nt, stride=SUBLANES), :] = src_ref[b, t0:t0 + nt, s * LANES:(s + 1) * LANES]
    return jnp.concatenate([sc_ref[s, r0:r0 + nt * SUBLANES, :] for s in range(N_SLAB)], axis=1)


def _from_time_major(val, sc_ref, dst_ref, t0, nt):
    r0 = t0 * SUBLANES
    for s in range(N_SLAB):
        sc_ref[s, r0:r0 + nt * SUBLANES, :] = val[:, s * LANES:(s + 1) * LANES]
    for b in range(BATCH):
        for s in range(N_SLAB):
            dst_ref[b, t0:t0 + nt, s * LANES:(s + 1) * LANES] = sc_ref[s, pl.ds(r0 + b, nt, stride=SUBLANES), :]


def _post_kernel(*refs, lru):
    if lru:
        (x_ref, gate_ref, yf_ref, yb_ref, mod_ref, g_ref, wf_ref, w1_ref, w2_ref, o_ref,
         x1_sc, h_sc, acc_sc, tout_sc) = refs
    else:
        x_ref, a_ref, mod_ref, g_ref, wf_ref, w1_ref, w2_ref, o_ref, x1_sc, h_sc, acc_sc = refs
    half_rows = x_ref.shape[0] // 2
    half_t = half_rows // BATCH

    def head(r):
        rs = slice(r * half_rows, (r + 1) * half_rows)
        if lru:
            front = (gate_ref[rs, :].astype(F32)
                     * (yf_ref[rs, :].astype(F32) + yb_ref[rs, :].astype(F32))).astype(BF16)
        else:
            front = a_ref[rs, :]
        y = jnp.dot(front, wf_ref[...], preferred_element_type=F32)
        x1 = _gated_add(x_ref[rs, :], mod_ref[0, 2], _rms(y, g_ref[1:2, :]))
        x1_sc[r] = x1
        h_sc[r] = _modulate(_rms(x1, g_ref[2:3, :]), mod_ref[0, 3], mod_ref[0, 4]).astype(BF16)

    def mlp(r):
        acc = jnp.zeros((half_rows, D_MODEL), F32)
        for c in range(D_FF // FF_CHUNK):
            hid = jnp.dot(h_sc[r], w1_ref[:, c * FF_CHUNK:(c + 1) * FF_CHUNK], preferred_element_type=F32)
            hid = jnp.square(jnp.maximum(hid, 0.0)).astype(BF16)
            acc = acc + jnp.dot(hid, w2_ref[c * FF_CHUNK:(c + 1) * FF_CHUNK, :], preferred_element_type=F32)
        acc_sc[r] = acc

    def tail(r):
        out = _gated_add(x1_sc[r], mod_ref[0, 5], _rms(acc_sc[r], g_ref[3:4, :]))
        if lru:
            _from_time_major(out, tout_sc, o_ref, r * half_t, half_t)
        else:
            o_ref[r * half_rows:(r + 1) * half_rows, :] = out

    head(0)
    head(1)
    mlp(0)
    tail(0)
    mlp(1)
    tail(1)


def _post_call(x, fronts, mod, g, w_front, w1, w2, layer, tiles_per_group, lru):
    tm = TOKEN_TILE
    layer_spec = lambda shape: pl.BlockSpec((None,) + shape[1:], lambda i: (layer, 0, 0),
                                            pipeline_mode=pl.Buffered(1))
    row = lambda i: (i, 0)
    n = x.shape[0]
    x_spec = pl.BlockSpec((tm, D_MODEL), row)
    if lru:
        out_spec = pl.BlockSpec((BATCH, tm // BATCH, D_MODEL), lambda i: (0, i, 0))
        out_shape = jax.ShapeDtypeStruct((BATCH, n // BATCH, D_MODEL), F32)
        scratch = [pltpu.VMEM((N_SLAB, tm, LANES), F32)]
    else:
        out_spec, out_shape = x_spec, jax.ShapeDtypeStruct(x.shape, F32)
        scratch = []
    scratch = [
        pltpu.VMEM((2, tm // 2, D_MODEL), F32),
        pltpu.VMEM((2, tm // 2, D_MODEL), BF16),
        pltpu.VMEM((2, tm // 2, D_MODEL), F32),
    ] + scratch
    in_specs = [x_spec]
    in_specs += [pl.BlockSpec((tm, f.shape[1]), row) for f in fronts]
    in_specs += [
        pl.BlockSpec((1, N_MOD, SUBLANES, D_MODEL), lambda i: (i // tiles_per_group, 0, 0, 0)),
        _const_spec((4, D_MODEL)),
        _const_spec(w_front.shape),
        layer_spec(w1.shape),
        layer_spec(w2.shape),
    ]
    return pl.pallas_call(
        functools.partial(_post_kernel, lru=lru),
        grid=(n // tm,),
        in_specs=in_specs,
        out_specs=out_spec,
        out_shape=out_shape,
        scratch_shapes=scratch,
        compiler_params=_params(("parallel",)),
        name="lru_out_mlp" if lru else "attn_out_mlp",
    )(x, *fronts, mod, g, w_front, w1, w2)


def _lru_in_kernel(*refs, need_gate):
    if need_gate:
        (x_ref, xp_ref, xn_ref, mod_ref, g_ref, w_ref, cw_ref, cb_ref, gate_ref, u_ref, xt_ref,
         v_sc, h_sc, t_sc, tp_sc, tn_sc) = refs
    else:
        x_ref, xp_ref, xn_ref, mod_ref, g_ref, w_ref, cw_ref, cb_ref, u_ref, v_sc, h_sc, t_sc, tp_sc, tn_sc = refs
    i = pl.program_id(0)
    n = pl.num_programs(0)
    rows = x_ref.shape[0] * x_ref.shape[1]
    half = rows // 2
    half_t = x_ref.shape[1] // 2
    s8 = SUBLANES

    def pre(x):
        return _modulate(_rms(x, g_ref[0:1, :]), mod_ref[0, 0], mod_ref[0, 1]).astype(BF16)

    h_sc[0, 0:HALO] = pre(_to_time_major(xp_ref, tp_sc)[SUBLANES * SUBLANES - HALO:])
    x_halves = [_to_time_major(x_ref, t_sc, r * half_t, half_t) for r in range(2)]
    if need_gate:
        for r in range(2):
            xt_ref[r * half:(r + 1) * half, :] = x_halves[r]
    h_sc[0, HALO:] = pre(x_halves[0])
    h_sc[1, 0:half] = pre(x_halves[1])
    h_sc[1, half:] = pre(_to_time_major(xn_ref, tn_sc)[:HALO])
    ext = half + HALO
    for r in range(2):
        v_sc[r * ext:(r + 1) * ext] = jnp.dot(h_sc[r], w_ref[:, D_RNN:], preferred_element_type=F32)
    v_sc[0:HALO] = v_sc[0:HALO] * (i > 0).astype(F32)
    v_sc[HALO + rows:HALO + rows + s8] = v_sc[HALO + rows:HALO + rows + s8] * (i < n - 1).astype(F32)
    for r in range(2):
        if need_gate:
            h_r = h_sc[0, HALO:] if r == 0 else h_sc[1, 0:half]
            gate_ref[r * half:(r + 1) * half, :] = jax.nn.gelu(
                jnp.dot(h_r, w_ref[:, :D_RNN], preferred_element_type=F32)).astype(BF16)
        base = HALO + r * half
        u_ref[r * half:(r + 1) * half, :] = (
            cb_ref[...]
            + cw_ref[0:1, :] * v_sc[base - 2 * s8:base - 2 * s8 + half]
            + cw_ref[1:2, :] * v_sc[base - s8:base - s8 + half]
            + cw_ref[2:3, :] * v_sc[base:base + half]
            + cw_ref[3:4, :] * v_sc[base + s8:base + s8 + half]).astype(u_ref.dtype)


def _lru_in_call(x3, mod, g, w_in, conv_w, conv_b, need_gate):
    t_total = x3.shape[1]
    n = BATCH * t_total
    tm = TOKEN_TILE
    nt = tm // BATCH
    row = lambda i: (i, 0)
    per_tile = nt // SUBLANES
    last = t_total // SUBLANES - 1
    halo_spec = lambda f: pl.BlockSpec((BATCH, SUBLANES, D_MODEL), f)
    out_specs = [pl.BlockSpec((tm, D_RNN), row), pl.BlockSpec((tm, D_RNN), row), pl.BlockSpec((tm, D_MODEL), row)]
    out_shape = [jax.ShapeDtypeStruct((n, D_RNN), BF16), jax.ShapeDtypeStruct((n, D_RNN), U_DTYPE),
                 jax.ShapeDtypeStruct((n, D_MODEL), F32)]
    if not need_gate:
        out_specs, out_shape = out_specs[1:2], out_shape[1:2]
    return pl.pallas_call(
        functools.partial(_lru_in_kernel, need_gate=need_gate),
        grid=(n // tm,),
        in_specs=[
            pl.BlockSpec((BATCH, nt, D_MODEL), lambda i: (0, i, 0)),
            halo_spec(lambda i: (0, jnp.maximum(i * per_tile - 1, 0), 0)),
            halo_spec(lambda i: (0, jnp.minimum((i + 1) * per_tile, last), 0)),
            _const_spec((1, N_MOD, SUBLANES, D_MODEL)),
            _const_spec((4, D_MODEL)),
            _const_spec((D_MODEL, 2 * D_RNN)),
            _const_spec((CONV_W, D_RNN)),
            _const_spec((1, D_RNN)),
        ],
        out_specs=out_specs,
        out_shape=out_shape,
        scratch_shapes=[
            pltpu.VMEM((tm + 2 * HALO, D_RNN), F32),
            pltpu.VMEM((2, tm // 2 + HALO, D_MODEL), BF16),
            pltpu.VMEM((N_SLAB, tm, LANES), F32),
            pltpu.VMEM((N_SLAB, SUBLANES * SUBLANES, LANES), F32),
            pltpu.VMEM((N_SLAB, SUBLANES * SUBLANES, LANES), F32),
        ],
        compiler_params=_params(("parallel",)),
        name="lru_in",
    )(x3, x3, x3, mod, g, w_in, conv_w, conv_b)


def _scan_kernel(uf_ref, ub_ref, h0_ref, wa_ref, ba_ref, wi_ref, bi_ref, lam_ref, yf_ref, yb_ref, ht_ref,
                 a_sc, bx_sc, h_sc):
    i = pl.program_id(0)
    n = pl.num_programs(0)
    rows = uf_ref.shape[0]
    nt = rows // SUBLANES
    s8 = SUBLANES

    @pl.when(i == 0)
    def _():
        h_sc[...] = h0_ref[...]

    for d, u_ref in enumerate((uf_ref, ub_ref)):
        for c in range(N_LRU_BLOCKS):
            cs = slice(c * LRU_BLOCK_W, (c + 1) * LRU_BLOCK_W)
            u16 = u_ref[:, cs].astype(BF16)
            u = u_ref[:, cs].astype(F32)
            ta = jnp.tanh(jnp.dot(u16, wa_ref[d, c], preferred_element_type=F32) + 0.5 * ba_ref[d, :, cs])
            ti = jnp.tanh(jnp.dot(u16, wi_ref[d, c], preferred_element_type=F32) + 0.5 * bi_ref[d, :, cs])
            neg_lam = -lam_ref[d, :, cs]
            softplus = jnp.maximum(neg_lam, 0.0) + jnp.log1p(jnp.exp(-jnp.abs(neg_lam)))
            k = (-0.5 * LRU_C * LOG2E) * softplus
            a = jnp.exp2(k * ta + k)
            w = 1.0 - a * a
            root = w * lax.rsqrt(jnp.maximum(w, 1e-30))
            a_sc[d, :, cs] = a
            bx_sc[d, :, cs] = root * (ti * u + u)

    def step(t, carry):
        hf, hb = carry
        rf = pl.multiple_of(t * 2 * s8, 2 * s8)
        rb = pl.multiple_of((nt - 2 - 2 * t) * s8, 2 * s8)
        hf1 = a_sc[0, pl.ds(rf, s8), :] * hf + bx_sc[0, pl.ds(rf, s8), :]
        hf2 = a_sc[0, pl.ds(rf + s8, s8), :] * hf1 + bx_sc[0, pl.ds(rf + s8, s8), :]
        yf_ref[pl.ds(rf, 2 * s8), :] = jnp.concatenate([hf1, hf2], axis=0).astype(yf_ref.dtype)
        hb1 = a_sc[1, pl.ds(rb + s8, s8), :] * hb + bx_sc[1, pl.ds(rb + s8, s8), :]
        hb2 = a_sc[1, pl.ds(rb, s8), :] * hb1 + bx_sc[1, pl.ds(rb, s8), :]
        yb_ref[pl.ds(rb, 2 * s8), :] = jnp.concatenate([hb2, hb1], axis=0).astype(yb_ref.dtype)
        return hf2, hb2

    hf, hb = lax.fori_loop(0, nt // 2, step, (h_sc[0], h_sc[1]), unroll=2)
    h_sc[0] = hf
    h_sc[1] = hb

    @pl.when(i == n - 1)
    def _():
        ht_ref[...] = h_sc[...]


def _scan_call(u2, h0, w_a, b_a, w_i, b_i, lam):
    rows_total = u2.shape[0]
    rows = SCAN_T * SUBLANES
    n = rows_total // rows
    w = D_RNN
    fwd = lambda i: (i, 0)
    bwd = lambda i: (n - 1 - i, 0)
    return pl.pallas_call(
        _scan_kernel,
        grid=(n,),
        in_specs=[
            pl.BlockSpec((rows, w), fwd),
            pl.BlockSpec((rows, w), bwd),
            _const_spec((2, SUBLANES, w)),
            _const_spec((2, N_LRU_BLOCKS, LRU_BLOCK_W, LRU_BLOCK_W)),
            _const_spec((2, 1, w)),
            _const_spec((2, N_LRU_BLOCKS, LRU_BLOCK_W, LRU_BLOCK_W)),
            _const_spec((2, 1, w)),
            _const_spec((2, 1, w)),
        ],
        out_specs=[
            pl.BlockSpec((rows, w), fwd),
            pl.BlockSpec((rows, w), bwd),
            pl.BlockSpec((2, SUBLANES, w), lambda i: (0, 0, 0)),
        ],
        out_shape=[
            jax.ShapeDtypeStruct((rows_total, w), Y_DTYPE),
            jax.ShapeDtypeStruct((rows_total, w), Y_DTYPE),
            jax.ShapeDtypeStruct((2, SUBLANES, w), F32),
        ],
        scratch_shapes=[
            pltpu.VMEM((2, rows, w), F32),
            pltpu.VMEM((2, rows, w), F32),
            pltpu.VMEM((2, SUBLANES, w), F32),
        ],
        compiler_params=_params(("arbitrary",)),
        name="lru_scan",
    )(u2, u2, h0, w_a, b_a, w_i, b_i, lam)


def _rope_tables():
    t = np.arange(SEQ)
    row = (t // GRID_W).astype(np.float64)
    col = (t % GRID_W).astype(np.float64)
    half = HEAD_DIM // 2
    inv = ROPE_BASE ** (-np.arange(0, half, 2, dtype=np.float64) / half)
    ang_r = row[:, None] * inv[None, :]
    ang_c = col[:, None] * inv[None, :]
    ang = np.concatenate([ang_r, ang_r, ang_c, ang_c], axis=-1)
    ang = np.tile(ang, (1, LANES // HEAD_DIM))
    low = (np.arange(LANES) % 32) < 16
    sin = np.sin(ang)
    tables = (np.cos(ang), np.where(low, -sin, 0.0), np.where(low, 0.0, sin))
    return tuple(jnp.asarray(a, dtype=F32) for a in tables)


def kernel(x, c, ctx, c_ctx, ada_w, ada_b, norm_g, mlp_w1, mlp_w2, attn_w_qkv, attn_w_o, attn_sink,
           lru_w_in, lru_conv_w, lru_conv_b, lru_w_a, lru_b_a, lru_w_i, lru_b_i, lru_lam, lru_w_out):
    n_lat = BATCH * SEQ
    n_ctx = BATCH * CTX_LEN

    c16 = jnp.zeros((16, D_MODEL), F32).at[:BATCH].set(c).at[BATCH].set(c_ctx)
    mods = _mod_call(c16, ada_w, ada_b).reshape(2, 16, N_MOD, D_MODEL)

    def slab_bmajor(m):
        return jnp.broadcast_to(m[:, :, None, :], (BATCH, N_MOD, SUBLANES, D_MODEL))

    def slab_ctx(m):
        return jnp.broadcast_to(m[None, :, None, :], (1, N_MOD, SUBLANES, D_MODEL))

    mod_x0 = slab_bmajor(mods[0, :BATCH])
    mod_c0 = slab_ctx(mods[0, BATCH])
    w_qkv = attn_w_qkv[0]
    w_qkv = jnp.concatenate([w_qkv[:, :D_Q] * (HEAD_DIM ** -0.5 * LOG2E), w_qkv[:, D_Q:]], axis=1).astype(BF16)
    sink2 = attn_sink[0] * LOG2E
    w_o = attn_w_o[0].astype(BF16)
    w1_all, w2_all = mlp_w1.astype(BF16), mlp_w2.astype(BF16)
    g0 = norm_g[0]
    tiles_per_batch = SEQ // TOKEN_TILE

    x2 = x.reshape(n_lat, D_MODEL)
    c2 = ctx.reshape(n_ctx, D_MODEL)
    q, k, v = _qkv_call(x2, mod_x0, g0, w_qkv, _rope_tables(), tiles_per_batch)
    qc, kc, vc = _qkv_call(c2, mod_c0, g0, w_qkv, None, n_ctx // TOKEN_TILE)
    kc3 = kc.reshape(BATCH, CTX_LEN, D_K2)
    att = _attn_call(sink2, q, k.reshape(BATCH, SEQ, D_K2), v, kc3, vc)
    att_c = _ctx_attn_call(sink2, qc, kc3, vc)
    x2 = _post_call(x2, [att], mod_x0, g0, w_o, w1_all, w2_all, 0, tiles_per_batch, lru=False)
    c2 = _post_call(c2, [att_c], mod_c0, g0, w_o, w1_all, w2_all, 0, n_ctx // TOKEN_TILE, lru=False)

    x3 = x2.reshape(BATCH, SEQ, D_MODEL)
    c3 = c2.reshape(BATCH, CTX_LEN, D_MODEL)
    mod_x1 = mods[1, :BATCH].transpose(1, 0, 2)[None]
    mod_c1 = slab_ctx(mods[1, BATCH])
    g1 = norm_g[1]
    w_in = lru_w_in[0].astype(BF16)
    conv_w = 0.5 * lru_conv_w[0]
    conv_b = 0.5 * lru_conv_b[0].reshape(1, D_RNN)
    w_a, w_i = lru_w_a[0].astype(BF16), lru_w_i[0].astype(BF16)
    b_a, b_i = lru_b_a[0].reshape(2, 1, D_RNN), lru_b_i[0].reshape(2, 1, D_RNN)
    lam = lru_lam[0].reshape(2, 1, D_RNN)
    scan = functools.partial(_scan_call, w_a=w_a, b_a=b_a, w_i=w_i, b_i=b_i, lam=lam)

    (u_c,) = _lru_in_call(c3, mod_c1, g1, w_in, conv_w, conv_b, need_gate=False)
    _, _, h_ctx = scan(u_c, jnp.zeros((2, SUBLANES, D_RNN), F32))
    gate_x, u_x, x_t = _lru_in_call(x3, mod_x1, g1, w_in, conv_w, conv_b, need_gate=True)
    yf, yb, _ = scan(u_x, h_ctx)
    return _post_call(x_t, [gate_x, yf, yb], mod_x1, g1, lru_w_out[0].astype(BF16),
                      w1_all, w2_all, 1, n_lat // TOKEN_TILE, lru=True)
```

```python
import functools

import jax
import jax.numpy as jnp
import numpy as np
from jax import lax
from jax.experimental import pallas as pl
from jax.experimental.pallas import tpu as pltpu

D_MODEL = 1024
BATCH = 8
SEQ = 2048
GRID_W = 64
CTX_LEN = 256
HEAD_DIM = 64
N_HEADS = 16
N_KV_HEADS = 4
GQA_GROUP = N_HEADS // N_KV_HEADS
WINDOW = 128
BLOCK = 128
ROPE_BASE = 10000.0
D_RNN = 1280
LRU_BLOCK_W = 256
N_LRU_BLOCKS = D_RNN // LRU_BLOCK_W
CONV_W = 4
LRU_C = 8.0
D_FF = 4 * D_MODEL
N_MOD = 6
EPS = 1e-6
NEG_INF = -1e30

D_Q = N_HEADS * HEAD_DIM
D_KV = N_KV_HEADS * HEAD_DIM
D_K2 = 2 * D_KV
LANES = 128
SUBLANES = 8
N_SLAB = D_MODEL // LANES
TOKEN_TILE = 512
MLP_TILE = 1024
FF_CHUNK = 1024
ATTN_Q_PER_STEP = 16
SCAN_T = 128
HALO = 16
U_DTYPE = jnp.bfloat16
Y_DTYPE = jnp.bfloat16
LOG2E = 1.4426950408889634
VMEM_LIMIT = 60 * 1024 * 1024

F32 = jnp.float32
BF16 = jnp.bfloat16


def _rms(x, g):
    ms = jnp.mean(x * x, axis=-1, keepdims=True)
    return x * lax.rsqrt(ms + EPS) * g


def _slab(x):
    return x.reshape(x.shape[0] // SUBLANES, SUBLANES, x.shape[1])


def _modulate(h, shift8, scale8):
    out = _slab(h) * (1.0 + scale8)[None] + shift8[None]
    return out.reshape(h.shape)


def _gated_add(x, gate8, y):
    out = _slab(x) + gate8[None] * _slab(y)
    return out.reshape(x.shape)


def _const_spec(shape):
    n = len(shape)
    return pl.BlockSpec(shape, lambda *_: (0,) * n, pipeline_mode=pl.Buffered(1))


def _params(sem, flags=None):
    return pltpu.CompilerParams(dimension_semantics=sem, vmem_limit_bytes=VMEM_LIMIT, flags=flags)


def _mod_kernel(c_ref, w_ref, b_ref, o_ref):
    s = jax.nn.silu(c_ref[...]).astype(BF16)
    o_ref[0] = jnp.dot(s, w_ref[0].astype(BF16), preferred_element_type=F32) + b_ref[0]


def _mod_call(c16, ada_w, ada_b):
    depth = ada_w.shape[0]
    nt = 1024
    return pl.pallas_call(
        _mod_kernel,
        grid=(depth, N_MOD * D_MODEL // nt),
        in_specs=[
            pl.BlockSpec((16, D_MODEL), lambda l, j: (0, 0)),
            pl.BlockSpec((1, D_MODEL, nt), lambda l, j: (l, 0, j)),
            pl.BlockSpec((1, 1, nt), lambda l, j: (l, 0, j)),
        ],
        out_specs=pl.BlockSpec((1, 16, nt), lambda l, j: (l, 0, j)),
        out_shape=jax.ShapeDtypeStruct((depth, 16, N_MOD * D_MODEL), F32),
        compiler_params=_params(("arbitrary", "arbitrary")),
        name="adaln_mod",
    )(c16, ada_w, ada_b.reshape(depth, 1, N_MOD * D_MODEL))


def _qkv_kernel(*refs, rope):
    if rope:
        x_ref, mod_ref, g_ref, w_ref, cos_ref, sa_ref, sb_ref, q_ref, k_ref, v_ref, h_sc, y_sc = refs
    else:
        x_ref, mod_ref, g_ref, w_ref, q_ref, k_ref, v_ref, h_sc, y_sc = refs
    half = x_ref.shape[0] // 2
    low = lax.broadcasted_iota(jnp.int32, (half, LANES), 1) < HEAD_DIM
    for r in range(2):
        rs = slice(r * half, (r + 1) * half)
        h_sc[r] = _modulate(_rms(x_ref[rs, :], g_ref[0:1, :]), mod_ref[0, 0], mod_ref[0, 1]).astype(BF16)
    for r in range(2):
        y_sc[r] = jnp.dot(h_sc[r], w_ref[...], preferred_element_type=F32)
    for r in range(2):
        rs = slice(r * half, (r + 1) * half)
        if rope:
            cos, sa, sb = cos_ref[rs, :], sa_ref[rs, :], sb_ref[rs, :]
        for c in range((D_Q + D_KV) // LANES):
            yc = y_sc[r, :, c * LANES:(c + 1) * LANES]
            if rope:
                yc = yc * cos + pltpu.roll(yc, LANES - 16, 1) * sa + pltpu.roll(yc, 16, 1) * sb
            if c < D_Q // LANES:
                q_ref[rs, c * LANES:(c + 1) * LANES] = yc.astype(BF16)
            else:
                c2 = 2 * (c - D_Q // LANES)
                swapped = pltpu.roll(yc, HEAD_DIM, 1)
                k_ref[rs, c2 * LANES:(c2 + 1) * LANES] = jnp.where(low, yc, swapped).astype(BF16)
                k_ref[rs, (c2 + 1) * LANES:(c2 + 2) * LANES] = jnp.where(low, swapped, yc).astype(BF16)
        for blk in range(half // BLOCK):
            v_ref[r * (half // BLOCK) + blk] = y_sc[r, blk * BLOCK:(blk + 1) * BLOCK, D_Q + D_KV:].T.astype(BF16)


def _qkv_call(x2, mod, g, w_qkv, tables, tiles_per_group):
    n = x2.shape[0]
    tm = TOKEN_TILE
    rope = tables is not None
    in_specs = [
        pl.BlockSpec((tm, D_MODEL), lambda i: (i, 0)),
        pl.BlockSpec((1, N_MOD, SUBLANES, D_MODEL), lambda i: (i // tiles_per_group, 0, 0, 0)),
        _const_spec((4, D_MODEL)),
        _const_spec((D_MODEL, D_Q + 2 * D_KV)),
    ]
    args = [x2, mod, g, w_qkv]
    if rope:
        nt = SEQ // tm
        in_specs += [pl.BlockSpec((tm, LANES), lambda i: (i % nt, 0))] * 3
        args += list(tables)
    return pl.pallas_call(
        functools.partial(_qkv_kernel, rope=rope),
        grid=(n // tm,),
        in_specs=in_specs,
        out_specs=[
            pl.BlockSpec((tm, D_Q), lambda i: (i, 0)),
            pl.BlockSpec((tm, D_K2), lambda i: (i, 0)),
            pl.BlockSpec((tm // BLOCK, D_KV, BLOCK), lambda i: (i, 0, 0)),
        ],
        out_shape=[
            jax.ShapeDtypeStruct((n, D_Q), BF16),
            jax.ShapeDtypeStruct((n, D_K2), BF16),
            jax.ShapeDtypeStruct((n // BLOCK, D_KV, BLOCK), BF16),
        ],
        scratch_shapes=[
            pltpu.VMEM((2, tm // 2, D_MODEL), BF16),
            pltpu.VMEM((2, tm // 2, D_Q + 2 * D_KV), F32),
        ],
        compiler_params=_params(("parallel",)),
        name="qkv_rope" if rope else "qkv_ctx",
    )(*args)


VT_ROWS = HEAD_DIM + 16


def _attn_kernel(*refs, local, q_per_step):
    if local:
        sink_ref, q_ref, k_ref, v_ref, kc_ref, vc_ref, o_ref, s_sc, p_sc, bias_sc = refs
    else:
        sink_ref, q_ref, kc_ref, vc_ref, o_ref, s_sc, p_sc, bias_sc = refs
    j = pl.program_id(1)
    seq_blocks = SEQ // BLOCK if local else 0
    ctx_blocks = CTX_LEN // BLOCK
    n_keys = s_sc.shape[1]
    pad_row = lax.broadcasted_iota(jnp.int32, (VT_ROWS - HEAD_DIM, n_keys), 0)
    vt_pad = jnp.where(pad_row == 0, 1.0, 0.0).astype(BF16)

    nt = (((1,), (1,)), ((), ()))
    n_band = 3
    n_chunks = D_Q // LANES
    lane = lax.broadcasted_iota(jnp.int32, (BLOCK, LANES), 1)
    first_head = lax.broadcasted_iota(jnp.int32, (1, 2 * BLOCK), 1) < BLOCK

    def block_params(qb):
        jq = j * q_per_step + qb
        rows = pl.ds(pl.multiple_of(qb * BLOCK, BLOCK), BLOCK)
        if not local:
            return rows, None, None, None, qb // ctx_blocks
        blk0 = jnp.clip(jq - 1, 0, seq_blocks - n_band)
        return rows, blk0, pl.multiple_of(blk0 * BLOCK, BLOCK), qb % 2, 0

    def store_bias(qb):
        if local:
            jq = j * q_per_step + qb
            start = jnp.clip(jq - 1, 0, seq_blocks - n_band) * BLOCK
            kpos = start + lax.broadcasted_iota(jnp.int32, (n_band * BLOCK, BLOCK), 0)
            qpos = jq * BLOCK + lax.broadcasted_iota(jnp.int32, (n_band * BLOCK, BLOCK), 1)
            bias_sc[qb % 2] = jnp.where(jnp.abs(kpos - qpos) <= WINDOW, 0.0, NEG_INF).astype(F32)

    def scores(c, params):
        rows, _, start, slot, cb = params
        kcols = slice((c // 2) * LANES, (c // 2 + 1) * LANES)
        qc = q_ref[rows, c * LANES:(c + 1) * LANES]
        zero = jnp.zeros_like(qc)
        q2 = jnp.concatenate([jnp.where(lane < HEAD_DIM, qc, zero), jnp.where(lane < HEAD_DIM, zero, qc)], axis=0)
        s_sc[c % 2, 0:CTX_LEN] = lax.dot_general(kc_ref[cb, :, kcols], q2, nt, preferred_element_type=F32)
        if local:
            s_band = lax.dot_general(k_ref[0, pl.ds(start, n_band * BLOCK), kcols], q2, nt,
                                     preferred_element_type=F32)
            for hd in range(2):
                s_sc[c % 2, CTX_LEN:, hd * BLOCK:(hd + 1) * BLOCK] = (
                    s_band[:, hd * BLOCK:(hd + 1) * BLOCK] + bias_sc[slot])

    def softmax(c):
        s = s_sc[c % 2]
        sink_row = jnp.where(first_head, sink_ref[2 * c], sink_ref[2 * c + 1])
        m = jnp.maximum(jnp.max(s, axis=0, keepdims=True), sink_row)
        p_sc[c % 2] = jnp.exp2(s - m).astype(BF16)
        return jnp.exp2(sink_row - m)

    def values(c, params, sink_term):
        rows, blk0, _, _, cb = params
        hrows = slice((c // 2) * HEAD_DIM, (c // 2 + 1) * HEAD_DIM)
        vt = [vc_ref[cb * ctx_blocks + i, hrows, :] for i in range(ctx_blocks)]
        if local:
            vt_band = v_ref[pl.ds(blk0, n_band), hrows, :]
            vt += [vt_band[i] for i in range(n_band)]
        vt_aug = jnp.concatenate([jnp.concatenate(vt, axis=1), vt_pad], axis=0)
        acc = jnp.dot(vt_aug, p_sc[c % 2], preferred_element_type=F32)
        out_t = acc[0:HEAD_DIM] / (acc[HEAD_DIM:HEAD_DIM + 1] + sink_term)
        both = jnp.concatenate([out_t[:, :BLOCK], out_t[:, BLOCK:]], axis=0)
        o_ref[rows, c * LANES:(c + 1) * LANES] = both.T.astype(o_ref.dtype)

    def query_block(qb, sink_term0):
        cur = block_params(qb)
        nxt = block_params(jnp.minimum(qb + 1, q_per_step - 1))
        store_bias(qb + 1)
        sink_terms = {0: sink_term0}
        for c in range(n_chunks):
            if c + 2 < n_chunks:
                scores(c + 2, cur)
            else:
                scores(c + 2 - n_chunks, nxt)
            sink_terms[c + 1] = softmax((c + 1) % n_chunks)
            values(c, cur, sink_terms[c])
        return sink_terms[n_chunks]

    first = block_params(0)
    store_bias(0)
    scores(0, first)
    scores(1, first)
    lax.fori_loop(0, q_per_step, query_block, softmax(0))


def _attn_scratch(n_keys):
    return [
        pltpu.VMEM((2, n_keys, 2 * BLOCK), F32),
        pltpu.VMEM((2, n_keys, 2 * BLOCK), BF16),
        pltpu.VMEM((2, 3 * BLOCK, BLOCK), F32),
    ]


def _attn_call(sink2, q, k, vt, kc, vtc):
    nb = SEQ // BLOCK
    nbc = CTX_LEN // BLOCK
    qps = ATTN_Q_PER_STEP
    steps = nb // qps
    return pl.pallas_call(
        functools.partial(_attn_kernel, local=True, q_per_step=qps),
        grid=(BATCH, steps),
        in_specs=[
            pl.BlockSpec(memory_space=pltpu.SMEM),
            pl.BlockSpec((qps * BLOCK, D_Q), lambda b, j: (b * steps + j, 0)),
            pl.BlockSpec((1, SEQ, D_K2), lambda b, j: (b, 0, 0)),
            pl.BlockSpec((nb, D_KV, BLOCK), lambda b, j: (b, 0, 0)),
            pl.BlockSpec((1, CTX_LEN, D_K2), lambda b, j: (b, 0, 0)),
            pl.BlockSpec((nbc, D_KV, BLOCK), lambda b, j: (b, 0, 0)),
        ],
        out_specs=pl.BlockSpec((qps * BLOCK, D_Q), lambda b, j: (b * steps + j, 0)),
        out_shape=jax.ShapeDtypeStruct((BATCH * SEQ, D_Q), BF16),
        scratch_shapes=_attn_scratch(CTX_LEN + 3 * BLOCK),
        compiler_params=_params(("parallel", "arbitrary")),
        name="band_attn",
    )(sink2, q, k, vt, kc, vtc)


def _ctx_attn_call(sink2, qc, kc, vtc):
    nb = BATCH * CTX_LEN // BLOCK
    whole = lambda shape: pl.BlockSpec(shape, lambda b, j: (0,) * len(shape))
    return pl.pallas_call(
        functools.partial(_attn_kernel, local=False, q_per_step=nb),
        grid=(1, 1),
        in_specs=[
            pl.BlockSpec(memory_space=pltpu.SMEM),
            whole((BATCH * CTX_LEN, D_Q)),
            whole((BATCH, CTX_LEN, D_K2)),
            whole((nb, D_KV, BLOCK)),
        ],
        out_specs=whole((BATCH * CTX_LEN, D_Q)),
        out_shape=jax.ShapeDtypeStruct((BATCH * CTX_LEN, D_Q), BF16),
        scratch_shapes=_attn_scratch(CTX_LEN),
        compiler_params=_params(("arbitrary", "arbitrary")),
        name="ctx_attn",
    )(sink2, qc, kc, vtc)


def _to_time_major(src_ref, sc_ref, t0=0, nt=None):
    nt = src_ref.shape[1] if nt is None else nt
    r0 = t0 * SUBLANES
    for b in range(BATCH):
        for s in range(N_SLAB):
            sc_ref[s, pl.ds(r0 + b, nt, stride=SUBLANES), :] = src_ref[b, t0:t0 + nt, s * LANES:(s + 1) * LANES]
    return jnp.concatenate([sc_ref[s, r0:r0 + nt * SUBLANES, :] for s in range(N_SLAB)], axis=1)


def _from_time_major(val, sc_ref, dst_ref, t0, nt):
    r0 = t0 * SUBLANES
    for s in range(N_SLAB):
        sc_ref[s, r0:r0 + nt * SUBLANES, :] = val[:, s * LANES:(s + 1) * LANES]
    for b in range(BATCH):
        for s in range(N_SLAB):
            dst_ref[b, t0:t0 + nt, s * LANES:(s + 1) * LANES] = sc_ref[s, pl.ds(r0 + b, nt, stride=SUBLANES), :]


def _post_kernel(*refs, lru):
    if lru:
        (x_ref, gate_ref, yf_ref, yb_ref, mod_ref, g_ref, wf_ref, w1_ref, w2_ref, o_ref,
         x1_sc, h_sc, acc_sc, tout_sc) = refs
    else:
        x_ref, a_ref, mod_ref, g_ref, wf_ref, w1_ref, w2_ref, o_ref, x1_sc, h_sc, acc_sc = refs
    half_rows = x_ref.shape[0] // 2
    half_t = half_rows // BATCH

    def head(r):
        rs = slice(r * half_rows, (r + 1) * half_rows)
        if lru:
            front = (gate_ref[rs, :].astype(F32)
                     * (yf_ref[rs, :].astype(F32) + yb_ref[rs, :].astype(F32))).astype(BF16)
        else:
            front = a_ref[rs, :]
        y = jnp.dot(front, wf_ref[...], preferred_element_type=F32)
        x1 = _gated_add(x_ref[rs, :], mod_ref[0, 2], _rms(y, g_ref[1:2, :]))
        x1_sc[r] = x1
        h_sc[r] = _modulate(_rms(x1, g_ref[2:3, :]), mod_ref[0, 3], mod_ref[0, 4]).astype(BF16)

    def mlp(r):
        acc = jnp.zeros((half_rows, D_MODEL), F32)
        for c in range(D_FF // FF_CHUNK):
            hid = jnp.dot(h_sc[r], w1_ref[:, c * FF_CHUNK:(c + 1) * FF_CHUNK], preferred_element_type=F32)
            hid = jnp.square(jnp.maximum(hid, 0.0)).astype(BF16)
            acc = acc + jnp.dot(hid, w2_ref[c * FF_CHUNK:(c + 1) * FF_CHUNK, :], preferred_element_type=F32)
        acc_sc[r] = acc

    def tail(r):
        out = _gated_add(x1_sc[r], mod_ref[0, 5], _rms(acc_sc[r], g_ref[3:4, :]))
        if lru:
            _from_time_major(out, tout_sc, o_ref, r * half_t, half_t)
        else:
            o_ref[r * half_rows:(r + 1) * half_rows, :] = out

    head(0)
    head(1)
    mlp(0)
    tail(0)
    mlp(1)
    tail(1)


def _post_call(x, fronts, mod, g, w_front, w1, w2, layer, tiles_per_group, lru):
    tm = TOKEN_TILE if lru else MLP_TILE
    layer_spec = lambda shape: pl.BlockSpec((None,) + shape[1:], lambda i: (layer, 0, 0),
                                            pipeline_mode=pl.Buffered(1))
    row = lambda i: (i, 0)
    n = x.shape[0]
    x_spec = pl.BlockSpec((tm, D_MODEL), row)
    if lru:
        out_spec = pl.BlockSpec((BATCH, tm // BATCH, D_MODEL), lambda i: (0, i, 0))
        out_shape = jax.ShapeDtypeStruct((BATCH, n // BATCH, D_MODEL), F32)
        scratch = [pltpu.VMEM((N_SLAB, tm, LANES), F32)]
    else:
        out_spec, out_shape = x_spec, jax.ShapeDtypeStruct(x.shape, F32)
        scratch = []
    scratch = [
        pltpu.VMEM((2, tm // 2, D_MODEL), F32),
        pltpu.VMEM((2, tm // 2, D_MODEL), BF16),
        pltpu.VMEM((2, tm // 2, D_MODEL), F32),
    ] + scratch
    in_specs = [x_spec]
    in_specs += [pl.BlockSpec((tm, f.shape[1]), row) for f in fronts]
    in_specs += [
        pl.BlockSpec((1, N_MOD, SUBLANES, D_MODEL), lambda i: (i // tiles_per_group, 0, 0, 0)),
        _const_spec((4, D_MODEL)),
        _const_spec(w_front.shape),
        layer_spec(w1.shape),
        layer_spec(w2.shape),
    ]
    return pl.pallas_call(
        functools.partial(_post_kernel, lru=lru),
        grid=(n // tm,),
        in_specs=in_specs,
        out_specs=out_spec,
        out_shape=out_shape,
        scratch_shapes=scratch,
        compiler_params=_params(("parallel",)),
        name="lru_out_mlp" if lru else "attn_out_mlp",
    )(x, *fronts, mod, g, w_front, w1, w2)


def _lru_in_kernel(*refs, need_gate):
    if need_gate:
        (x_ref, xp_ref, xn_ref, mod_ref, g_ref, w_ref, cw_ref, cb_ref, gate_ref, u_ref, xt_ref,
         v_sc, h_sc, t_sc, tp_sc, tn_sc) = refs
    else:
        x_ref, xp_ref, xn_ref, mod_ref, g_ref, w_ref, cw_ref, cb_ref, u_ref, v_sc, h_sc, t_sc, tp_sc, tn_sc = refs
    i = pl.program_id(0)
    n = pl.num_programs(0)
    rows = x_ref.shape[0] * x_ref.shape[1]
    half = rows // 2
    half_t = x_ref.shape[1] // 2
    s8 = SUBLANES

    def pre(x):
        return _modulate(_rms(x, g_ref[0:1, :]), mod_ref[0, 0], mod_ref[0, 1]).astype(BF16)

    h_sc[0, 0:HALO] = pre(_to_time_major(xp_ref, tp_sc)[SUBLANES * SUBLANES - HALO:])
    x_halves = [_to_time_major(x_ref, t_sc, r * half_t, half_t) for r in range(2)]
    if need_gate:
        for r in range(2):
            xt_ref[r * half:(r + 1) * half, :] = x_halves[r]
    h_sc[0, HALO:] = pre(x_halves[0])
    h_sc[1, 0:half] = pre(x_halves[1])
    h_sc[1, half:] = pre(_to_time_major(xn_ref, tn_sc)[:HALO])
    ext = half + HALO
    for r in range(2):
        v_sc[r * ext:(r + 1) * ext] = jnp.dot(h_sc[r], w_ref[:, D_RNN:], preferred_element_type=F32)
    v_sc[0:HALO] = v_sc[0:HALO] * (i > 0).astype(F32)
    v_sc[HALO + rows:HALO + rows + s8] = v_sc[HALO + rows:HALO + rows + s8] * (i < n - 1).astype(F32)
    for r in range(2):
        if need_gate:
            h_r = h_sc[0, HALO:] if r == 0 else h_sc[1, 0:half]
            gate_ref[r * half:(r + 1) * half, :] = jax.nn.gelu(
                jnp.dot(h_r, w_ref[:, :D_RNN], preferred_element_type=F32)).astype(BF16)
        base = HALO + r * half
        u_ref[r * half:(r + 1) * half, :] = (
            cb_ref[...]
            + cw_ref[0:1, :] * v_sc[base - 2 * s8:base - 2 * s8 + half]
            + cw_ref[1:2, :] * v_sc[base - s8:base - s8 + half]
            + cw_ref[2:3, :] * v_sc[base:base + half]
            + cw_ref[3:4, :] * v_sc[base + s8:base + s8 + half]).astype(u_ref.dtype)


def _lru_in_call(x3, mod, g, w_in, conv_w, conv_b, need_gate):
    t_total = x3.shape[1]
    n = BATCH * t_total
    tm = TOKEN_TILE
    nt = tm // BATCH
    row = lambda i: (i, 0)
    per_tile = nt // SUBLANES
    last = t_total // SUBLANES - 1
    halo_spec = lambda f: pl.BlockSpec((BATCH, SUBLANES, D_MODEL), f)
    out_specs = [pl.BlockSpec((tm, D_RNN), row), pl.BlockSpec((tm, D_RNN), row), pl.BlockSpec((tm, D_MODEL), row)]
    out_shape = [jax.ShapeDtypeStruct((n, D_RNN), BF16), jax.ShapeDtypeStruct((n, D_RNN), U_DTYPE),
                 jax.ShapeDtypeStruct((n, D_MODEL), F32)]
    if not need_gate:
        out_specs, out_shape = out_specs[1:2], out_shape[1:2]
    return pl.pallas_call(
        functools.partial(_lru_in_kernel, need_gate=need_gate),
        grid=(n // tm,),
        in_specs=[
            pl.BlockSpec((BATCH, nt, D_MODEL), lambda i: (0, i, 0)),
            halo_spec(lambda i: (0, jnp.maximum(i * per_tile - 1, 0), 0)),
            halo_spec(lambda i: (0, jnp.minimum((i + 1) * per_tile, last), 0)),
            _const_spec((1, N_MOD, SUBLANES, D_MODEL)),
            _const_spec((4, D_MODEL)),
            _const_spec((D_MODEL, 2 * D_RNN)),
            _const_spec((CONV_W, D_RNN)),
            _const_spec((1, D_RNN)),
        ],
        out_specs=out_specs,
        out_shape=out_shape,
        scratch_shapes=[
            pltpu.VMEM((tm + 2 * HALO, D_RNN), F32),
            pltpu.VMEM((2, tm // 2 + HALO, D_MODEL), BF16),
            pltpu.VMEM((N_SLAB, tm, LANES), F32),
            pltpu.VMEM((N_SLAB, SUBLANES * SUBLANES, LANES), F32),
            pltpu.VMEM((N_SLAB, SUBLANES * SUBLANES, LANES), F32),
        ],
        compiler_params=_params(("parallel",)),
        name="lru_in",
    )(x3, x3, x3, mod, g, w_in, conv_w, conv_b)


def _scan_kernel(uf_ref, ub_ref, h0_ref, wa_ref, ba_ref, wi_ref, bi_ref, lam_ref, yf_ref, yb_ref, ht_ref,
                 a_sc, bx_sc, h_sc):
    i = pl.program_id(0)
    n = pl.num_programs(0)
    rows = uf_ref.shape[0]
    nt = rows // SUBLANES
    s8 = SUBLANES

    @pl.when(i == 0)
    def _():
        h_sc[...] = h0_ref[...]

    for d, u_ref in enumerate((uf_ref, ub_ref)):
        for c in range(N_LRU_BLOCKS):
            cs = slice(c * LRU_BLOCK_W, (c + 1) * LRU_BLOCK_W)
            u16 = u_ref[:, cs].astype(BF16)
            u = u_ref[:, cs].astype(F32)
            ta = jnp.tanh(jnp.dot(u16, wa_ref[d, c], preferred_element_type=F32) + 0.5 * ba_ref[d, :, cs])
            ti = jnp.tanh(jnp.dot(u16, wi_ref[d, c], preferred_element_type=F32) + 0.5 * bi_ref[d, :, cs])
            neg_lam = -lam_ref[d, :, cs]
            softplus = jnp.maximum(neg_lam, 0.0) + jnp.log1p(jnp.exp(-jnp.abs(neg_lam)))
            k = (-0.5 * LRU_C * LOG2E) * softplus
            a = jnp.exp2(k * ta + k)
            w = 1.0 - a * a
            root = w * lax.rsqrt(jnp.maximum(w, 1e-30))
            a_sc[d, :, cs] = a
            bx_sc[d, :, cs] = root * (ti * u + u)

    def step(t, carry):
        hf, hb = carry
        rf = pl.multiple_of(t * 2 * s8, 2 * s8)
        rb = pl.multiple_of((nt - 2 - 2 * t) * s8, 2 * s8)
        hf1 = a_sc[0, pl.ds(rf, s8), :] * hf + bx_sc[0, pl.ds(rf, s8), :]
        hf2 = a_sc[0, pl.ds(rf + s8, s8), :] * hf1 + bx_sc[0, pl.ds(rf + s8, s8), :]
        yf_ref[pl.ds(rf, 2 * s8), :] = jnp.concatenate([hf1, hf2], axis=0).astype(yf_ref.dtype)
        hb1 = a_sc[1, pl.ds(rb + s8, s8), :] * hb + bx_sc[1, pl.ds(rb + s8, s8), :]
        hb2 = a_sc[1, pl.ds(rb, s8), :] * hb1 + bx_sc[1, pl.ds(rb, s8), :]
        yb_ref[pl.ds(rb, 2 * s8), :] = jnp.concatenate([hb2, hb1], axis=0).astype(yb_ref.dtype)
        return hf2, hb2

    hf, hb = lax.fori_loop(0, nt // 2, step, (h_sc[0], h_sc[1]), unroll=2)
    h_sc[0] = hf
    h_sc[1] = hb

    @pl.when(i == n - 1)
    def _():
        ht_ref[...] = h_sc[...]


def _scan_call(u2, h0, w_a, b_a, w_i, b_i, lam):
    rows_total = u2.shape[0]
    rows = SCAN_T * SUBLANES
    n = rows_total // rows
    w = D_RNN
    fwd = lambda i: (i, 0)
    bwd = lambda i: (n - 1 - i, 0)
    return pl.pallas_call(
        _scan_kernel,
        grid=(n,),
        in_specs=[
            pl.BlockSpec((rows, w), fwd),
            pl.BlockSpec((rows, w), bwd),
            _const_spec((2, SUBLANES, w)),
            _const_spec((2, N_LRU_BLOCKS, LRU_BLOCK_W, LRU_BLOCK_W)),
            _const_spec((2, 1, w)),
            _const_spec((2, N_LRU_BLOCKS, LRU_BLOCK_W, LRU_BLOCK_W)),
            _const_spec((2, 1, w)),
            _const_spec((2, 1, w)),
        ],
        out_specs=[
            pl.BlockSpec((rows, w), fwd),
            pl.BlockSpec((rows, w), bwd),
            pl.BlockSpec((2, SUBLANES, w), lambda i: (0, 0, 0)),
        ],
        out_shape=[
            jax.ShapeDtypeStruct((rows_total, w), Y_DTYPE),
            jax.ShapeDtypeStruct((rows_total, w), Y_DTYPE),
            jax.ShapeDtypeStruct((2, SUBLANES, w), F32),
        ],
        scratch_shapes=[
            pltpu.VMEM((2, rows, w), F32),
            pltpu.VMEM((2, rows, w), F32),
            pltpu.VMEM((2, SUBLANES, w), F32),
        ],
        compiler_params=_params(("arbitrary",)),
        name="lru_scan",
    )(u2, u2, h0, w_a, b_a, w_i, b_i, lam)


def _rope_tables():
    t = np.arange(SEQ)
    row = (t // GRID_W).astype(np.float64)
    col = (t % GRID_W).astype(np.float64)
    half = HEAD_DIM // 2
    inv = ROPE_BASE ** (-np.arange(0, half, 2, dtype=np.float64) / half)
    ang_r = row[:, None] * inv[None, :]
    ang_c = col[:, None] * inv[None, :]
    ang = np.concatenate([ang_r, ang_r, ang_c, ang_c], axis=-1)
    ang = np.tile(ang, (1, LANES // HEAD_DIM))
    low = (np.arange(LANES) % 32) < 16
    sin = np.sin(ang)
    tables = (np.cos(ang), np.where(low, -sin, 0.0), np.where(low, 0.0, sin))
    return tuple(jnp.asarray(a, dtype=F32) for a in tables)


def kernel(x, c, ctx, c_ctx, ada_w, ada_b, norm_g, mlp_w1, mlp_w2, attn_w_qkv, attn_w_o, attn_sink,
           lru_w_in, lru_conv_w, lru_conv_b, lru_w_a, lru_b_a, lru_w_i, lru_b_i, lru_lam, lru_w_out):
    n_lat = BATCH * SEQ
    n_ctx = BATCH * CTX_LEN

    c16 = jnp.zeros((16, D_MODEL), F32).at[:BATCH].set(c).at[BATCH].set(c_ctx)
    mods = _mod_call(c16, ada_w, ada_b).reshape(2, 16, N_MOD, D_MODEL)

    def slab_bmajor(m):
        return jnp.broadcast_to(m[:, :, None, :], (BATCH, N_MOD, SUBLANES, D_MODEL))

    def slab_ctx(m):
        return jnp.broadcast_to(m[None, :, None, :], (1, N_MOD, SUBLANES, D_MODEL))

    mod_x0 = slab_bmajor(mods[0, :BATCH])
    mod_c0 = slab_ctx(mods[0, BATCH])
    w_qkv = attn_w_qkv[0]
    w_qkv = jnp.concatenate([w_qkv[:, :D_Q] * (HEAD_DIM ** -0.5 * LOG2E), w_qkv[:, D_Q:]], axis=1).astype(BF16)
    sink2 = attn_sink[0] * LOG2E
    w_o = attn_w_o[0].astype(BF16)
    w1_all, w2_all = mlp_w1.astype(BF16), mlp_w2.astype(BF16)
    g0 = norm_g[0]
    tiles_per_batch = SEQ // TOKEN_TILE

    x2 = x.reshape(n_lat, D_MODEL)
    c2 = ctx.reshape(n_ctx, D_MODEL)
    q, k, v = _qkv_call(x2, mod_x0, g0, w_qkv, _rope_tables(), tiles_per_batch)
    qc, kc, vc = _qkv_call(c2, mod_c0, g0, w_qkv, None, n_ctx // TOKEN_TILE)
    kc3 = kc.reshape(BATCH, CTX_LEN, D_K2)
    att = _attn_call(sink2, q, k.reshape(BATCH, SEQ, D_K2), v, kc3, vc)
    att_c = _ctx_attn_call(sink2, qc, kc3, vc)
    x2 = _post_call(x2, [att], mod_x0, g0, w_o, w1_all, w2_all, 0, SEQ // MLP_TILE, lru=False)
    c2 = _post_call(c2, [att_c], mod_c0, g0, w_o, w1_all, w2_all, 0, n_ctx // MLP_TILE, lru=False)

    x3 = x2.reshape(BATCH, SEQ, D_MODEL)
    c3 = c2.reshape(BATCH, CTX_LEN, D_MODEL)
    mod_x1 = mods[1, :BATCH].transpose(1, 0, 2)[None]
    mod_c1 = slab_ctx(mods[1, BATCH])
    g1 = norm_g[1]
    w_in = lru_w_in[0].astype(BF16)
    conv_w = 0.5 * lru_conv_w[0]
    conv_b = 0.5 * lru_conv_b[0].reshape(1, D_RNN)
    w_a, w_i = lru_w_a[0].astype(BF16), lru_w_i[0].astype(BF16)
    b_a, b_i = lru_b_a[0].reshape(2, 1, D_RNN), lru_b_i[0].reshape(2, 1, D_RNN)
    lam = lru_lam[0].reshape(2, 1, D_RNN)
    scan = functools.partial(_scan_call, w_a=w_a, b_a=b_a, w_i=w_i, b_i=b_i, lam=lam)

    (u_c,) = _lru_in_call(c3, mod_c1, g1, w_in, conv_w, conv_b, need_gate=False)
    _, _, h_ctx = scan(u_c, jnp.zeros((2, SUBLANES, D_RNN), F32))
    gate_x, u_x, x_t = _lru_in_call(x3, mod_x1, g1, w_in, conv_w, conv_b, need_gate=True)
    yf, yb, _ = scan(u_x, h_ctx)
    return _post_call(x_t, [gate_x, yf, yb], mod_x1, g1, lru_w_out[0].astype(BF16),
                      w1_all, w2_all, 1, n_lat // TOKEN_TILE, lru=True)
```

```python
import functools

import jax
import jax.numpy as jnp
import numpy as np
from jax import lax
from jax.experimental import pallas as pl
from jax.experimental.pallas import tpu as pltpu

D_MODEL = 1024
BATCH = 8
SEQ = 2048
GRID_W = 64
CTX_LEN = 256
HEAD_DIM = 64
N_HEADS = 16
N_KV_HEADS = 4
GQA_GROUP = N_HEADS // N_KV_HEADS
WINDOW = 128
BLOCK = 128
ROPE_BASE = 10000.0
D_RNN = 1280
LRU_BLOCK_W = 256
N_LRU_BLOCKS = D_RNN // LRU_BLOCK_W
CONV_W = 4
LRU_C = 8.0
D_FF = 4 * D_MODEL
N_MOD = 6
EPS = 1e-6
NEG_INF = -1e30

D_Q = N_HEADS * HEAD_DIM
D_KV = N_KV_HEADS * HEAD_DIM
D_K2 = 2 * D_KV
LANES = 128
SUBLANES = 8
N_SLAB = D_MODEL // LANES
TOKEN_TILE = 512
MLP_TILE = 1024
LRU_IN_TILE = 1024
FF_CHUNK = 1024
ATTN_Q_PER_STEP = 16
SCAN_T = 128
HALO = 16
U_DTYPE = jnp.bfloat16
Y_DTYPE = jnp.bfloat16
LOG2E = 1.4426950408889634
VMEM_LIMIT = 60 * 1024 * 1024

F32 = jnp.float32
BF16 = jnp.bfloat16


def _rms(x, g):
    ms = jnp.mean(x * x, axis=-1, keepdims=True)
    return x * lax.rsqrt(ms + EPS) * g


def _slab(x):
    return x.reshape(x.shape[0] // SUBLANES, SUBLANES, x.shape[1])


def _modulate(h, shift8, scale8):
    out = _slab(h) * (1.0 + scale8)[None] + shift8[None]
    return out.reshape(h.shape)


def _gated_add(x, gate8, y):
    out = _slab(x) + gate8[None] * _slab(y)
    return out.reshape(x.shape)


def _const_spec(shape):
    n = len(shape)
    return pl.BlockSpec(shape, lambda *_: (0,) * n, pipeline_mode=pl.Buffered(1))


def _params(sem, flags=None):
    return pltpu.CompilerParams(dimension_semantics=sem, vmem_limit_bytes=VMEM_LIMIT, flags=flags)


def _mod_kernel(c_ref, w_ref, b_ref, o_ref):
    s = jax.nn.silu(c_ref[...]).astype(BF16)
    o_ref[0] = jnp.dot(s, w_ref[0].astype(BF16), preferred_element_type=F32) + b_ref[0]


def _mod_call(c16, ada_w, ada_b):
    depth = ada_w.shape[0]
    nt = 2048
    return pl.pallas_call(
        _mod_kernel,
        grid=(depth, N_MOD * D_MODEL // nt),
        in_specs=[
            pl.BlockSpec((16, D_MODEL), lambda l, j: (0, 0)),
            pl.BlockSpec((1, D_MODEL, nt), lambda l, j: (l, 0, j)),
            pl.BlockSpec((1, 1, nt), lambda l, j: (l, 0, j)),
        ],
        out_specs=pl.BlockSpec((1, 16, nt), lambda l, j: (l, 0, j)),
        out_shape=jax.ShapeDtypeStruct((depth, 16, N_MOD * D_MODEL), F32),
        compiler_params=_params(("arbitrary", "arbitrary")),
        name="adaln_mod",
    )(c16, ada_w, ada_b.reshape(depth, 1, N_MOD * D_MODEL))


def _qkv_kernel(*refs, rope):
    if rope:
        x_ref, mod_ref, g_ref, w_ref, cos_ref, sa_ref, sb_ref, q_ref, k_ref, v_ref, h_sc, y_sc = refs
    else:
        x_ref, mod_ref, g_ref, w_ref, q_ref, k_ref, v_ref, h_sc, y_sc = refs
    half = x_ref.shape[0] // 2
    low = lax.broadcasted_iota(jnp.int32, (half, LANES), 1) < HEAD_DIM
    for r in range(2):
        rs = slice(r * half, (r + 1) * half)
        h_sc[r] = _modulate(_rms(x_ref[rs, :], g_ref[0:1, :]), mod_ref[0, 0], mod_ref[0, 1]).astype(BF16)
    for r in range(2):
        y_sc[r] = jnp.dot(h_sc[r], w_ref[...], preferred_element_type=F32)
    for r in range(2):
        rs = slice(r * half, (r + 1) * half)
        if rope:
            cos, sa, sb = cos_ref[rs, :], sa_ref[rs, :], sb_ref[rs, :]
        for c in range((D_Q + D_KV) // LANES):
            yc = y_sc[r, :, c * LANES:(c + 1) * LANES]
            if rope:
                yc = yc * cos + pltpu.roll(yc, LANES - 16, 1) * sa + pltpu.roll(yc, 16, 1) * sb
            if c < D_Q // LANES:
                q_ref[rs, c * LANES:(c + 1) * LANES] = yc.astype(BF16)
            else:
                c2 = 2 * (c - D_Q // LANES)
                swapped = pltpu.roll(yc, HEAD_DIM, 1)
                k_ref[rs, c2 * LANES:(c2 + 1) * LANES] = jnp.where(low, yc, swapped).astype(BF16)
                k_ref[rs, (c2 + 1) * LANES:(c2 + 2) * LANES] = jnp.where(low, swapped, yc).astype(BF16)
        for blk in range(half // BLOCK):
            v_ref[r * (half // BLOCK) + blk] = y_sc[r, blk * BLOCK:(blk + 1) * BLOCK, D_Q + D_KV:].T.astype(BF16)


def _qkv_call(x2, mod, g, w_qkv, tables, tiles_per_group):
    n = x2.shape[0]
    tm = TOKEN_TILE
    rope = tables is not None
    in_specs = [
        pl.BlockSpec((tm, D_MODEL), lambda i: (i, 0)),
        pl.BlockSpec((1, N_MOD, SUBLANES, D_MODEL), lambda i: (i // tiles_per_group, 0, 0, 0)),
        _const_spec((4, D_MODEL)),
        _const_spec((D_MODEL, D_Q + 2 * D_KV)),
    ]
    args = [x2, mod, g, w_qkv]
    if rope:
        nt = SEQ // tm
        in_specs += [pl.BlockSpec((tm, LANES), lambda i: (i % nt, 0))] * 3
        args += list(tables)
    return pl.pallas_call(
        functools.partial(_qkv_kernel, rope=rope),
        grid=(n // tm,),
        in_specs=in_specs,
        out_specs=[
            pl.BlockSpec((tm, D_Q), lambda i: (i, 0)),
            pl.BlockSpec((tm, D_K2), lambda i: (i, 0)),
            pl.BlockSpec((tm // BLOCK, D_KV, BLOCK), lambda i: (i, 0, 0)),
        ],
        out_shape=[
            jax.ShapeDtypeStruct((n, D_Q), BF16),
            jax.ShapeDtypeStruct((n, D_K2), BF16),
            jax.ShapeDtypeStruct((n // BLOCK, D_KV, BLOCK), BF16),
        ],
        scratch_shapes=[
            pltpu.VMEM((2, tm // 2, D_MODEL), BF16),
            pltpu.VMEM((2, tm // 2, D_Q + 2 * D_KV), F32),
        ],
        compiler_params=_params(("parallel",)),
        name="qkv_rope" if rope else "qkv_ctx",
    )(*args)


VT_ROWS = HEAD_DIM + 16


def _attn_kernel(*refs, local, q_per_step):
    if local:
        sink_ref, q_ref, k_ref, v_ref, kc_ref, vc_ref, o_ref, s_sc, p_sc, bias_sc = refs
    else:
        sink_ref, q_ref, kc_ref, vc_ref, o_ref, s_sc, p_sc, bias_sc = refs
    j = pl.program_id(1)
    seq_blocks = SEQ // BLOCK if local else 0
    ctx_blocks = CTX_LEN // BLOCK
    n_keys = s_sc.shape[1]
    pad_row = lax.broadcasted_iota(jnp.int32, (VT_ROWS - HEAD_DIM, n_keys), 0)
    vt_pad = jnp.where(pad_row == 0, 1.0, 0.0).astype(BF16)

    nt = (((1,), (1,)), ((), ()))
    n_band = 3
    n_chunks = D_Q // LANES
    lane = lax.broadcasted_iota(jnp.int32, (BLOCK, LANES), 1)
    first_head = lax.broadcasted_iota(jnp.int32, (1, 2 * BLOCK), 1) < BLOCK

    def block_params(qb):
        jq = j * q_per_step + qb
        rows = pl.ds(pl.multiple_of(qb * BLOCK, BLOCK), BLOCK)
        if not local:
            return rows, None, None, None, qb // ctx_blocks
        blk0 = jnp.clip(jq - 1, 0, seq_blocks - n_band)
        return rows, blk0, pl.multiple_of(blk0 * BLOCK, BLOCK), qb % 2, 0

    def store_bias(qb):
        if local:
            jq = j * q_per_step + qb
            start = jnp.clip(jq - 1, 0, seq_blocks - n_band) * BLOCK
            kpos = start + lax.broadcasted_iota(jnp.int32, (n_band * BLOCK, BLOCK), 0)
            qpos = jq * BLOCK + lax.broadcasted_iota(jnp.int32, (n_band * BLOCK, BLOCK), 1)
            bias_sc[qb % 2] = jnp.where(jnp.abs(kpos - qpos) <= WINDOW, 0.0, NEG_INF).astype(F32)

    def scores(c, params):
        rows, _, start, slot, cb = params
        kcols = slice((c // 2) * LANES, (c // 2 + 1) * LANES)
        qc = q_ref[rows, c * LANES:(c + 1) * LANES]
        zero = jnp.zeros_like(qc)
        q2 = jnp.concatenate([jnp.where(lane < HEAD_DIM, qc, zero), jnp.where(lane < HEAD_DIM, zero, qc)], axis=0)
        s_sc[c % 2, 0:CTX_LEN] = lax.dot_general(kc_ref[cb, :, kcols], q2, nt, preferred_element_type=F32)
        if local:
            s_band = lax.dot_general(k_ref[0, pl.ds(start, n_band * BLOCK), kcols], q2, nt,
                                     preferred_element_type=F32)
            for hd in range(2):
                s_sc[c % 2, CTX_LEN:, hd * BLOCK:(hd + 1) * BLOCK] = (
                    s_band[:, hd * BLOCK:(hd + 1) * BLOCK] + bias_sc[slot])

    def softmax(c):
        s = s_sc[c % 2]
        sink_row = jnp.where(first_head, sink_ref[2 * c], sink_ref[2 * c + 1])
        m = jnp.maximum(jnp.max(s, axis=0, keepdims=True), sink_row)
        p_sc[c % 2] = jnp.exp2(s - m).astype(BF16)
        return jnp.exp2(sink_row - m)

    def values(c, params, sink_term):
        rows, blk0, _, _, cb = params
        hrows = slice((c // 2) * HEAD_DIM, (c // 2 + 1) * HEAD_DIM)
        vt = [vc_ref[cb * ctx_blocks + i, hrows, :] for i in range(ctx_blocks)]
        if local:
            vt_band = v_ref[pl.ds(blk0, n_band), hrows, :]
            vt += [vt_band[i] for i in range(n_band)]
        vt_aug = jnp.concatenate([jnp.concatenate(vt, axis=1), vt_pad], axis=0)
        acc = jnp.dot(vt_aug, p_sc[c % 2], preferred_element_type=F32)
        out_t = acc[0:HEAD_DIM] / (acc[HEAD_DIM:HEAD_DIM + 1] + sink_term)
        both = jnp.concatenate([out_t[:, :BLOCK], out_t[:, BLOCK:]], axis=0)
        o_ref[rows, c * LANES:(c + 1) * LANES] = both.T.astype(o_ref.dtype)

    def query_block(qb, sink_term0):
        cur = block_params(qb)
        nxt = block_params(jnp.minimum(qb + 1, q_per_step - 1))
        store_bias(qb + 1)
        sink_terms = {0: sink_term0}
        for c in range(n_chunks):
            if c + 2 < n_chunks:
                scores(c + 2, cur)
            else:
                scores(c + 2 - n_chunks, nxt)
            sink_terms[c + 1] = softmax((c + 1) % n_chunks)
            values(c, cur, sink_terms[c])
        return sink_terms[n_chunks]

    first = block_params(0)
    store_bias(0)
    scores(0, first)
    scores(1, first)
    lax.fori_loop(0, q_per_step, query_block, softmax(0))


def _attn_scratch(n_keys):
    return [
        pltpu.VMEM((2, n_keys, 2 * BLOCK), F32),
        pltpu.VMEM((2, n_keys, 2 * BLOCK), BF16),
        pltpu.VMEM((2, 3 * BLOCK, BLOCK), F32),
    ]


def _attn_call(sink2, q, k, vt, kc, vtc):
    nb = SEQ // BLOCK
    nbc = CTX_LEN // BLOCK
    qps = ATTN_Q_PER_STEP
    steps = nb // qps
    return pl.pallas_call(
        functools.partial(_attn_kernel, local=True, q_per_step=qps),
        grid=(BATCH, steps),
        in_specs=[
            pl.BlockSpec(memory_space=pltpu.SMEM),
            pl.BlockSpec((qps * BLOCK, D_Q), lambda b, j: (b * steps + j, 0)),
            pl.BlockSpec((1, SEQ, D_K2), lambda b, j: (b, 0, 0)),
            pl.BlockSpec((nb, D_KV, BLOCK), lambda b, j: (b, 0, 0)),
            pl.BlockSpec((1, CTX_LEN, D_K2), lambda b, j: (b, 0, 0)),
            pl.BlockSpec((nbc, D_KV, BLOCK), lambda b, j: (b, 0, 0)),
        ],
        out_specs=pl.BlockSpec((qps * BLOCK, D_Q), lambda b, j: (b * steps + j, 0)),
        out_shape=jax.ShapeDtypeStruct((BATCH * SEQ, D_Q), BF16),
        scratch_shapes=_attn_scratch(CTX_LEN + 3 * BLOCK),
        compiler_params=_params(("parallel", "arbitrary")),
        name="band_attn",
    )(sink2, q, k, vt, kc, vtc)


def _ctx_attn_call(sink2, qc, kc, vtc):
    nb = BATCH * CTX_LEN // BLOCK
    whole = lambda shape: pl.BlockSpec(shape, lambda b, j: (0,) * len(shape))
    return pl.pallas_call(
        functools.partial(_attn_kernel, local=False, q_per_step=nb),
        grid=(1, 1),
        in_specs=[
            pl.BlockSpec(memory_space=pltpu.SMEM),
            whole((BATCH * CTX_LEN, D_Q)),
            whole((BATCH, CTX_LEN, D_K2)),
            whole((nb, D_KV, BLOCK)),
        ],
        out_specs=whole((BATCH * CTX_LEN, D_Q)),
        out_shape=jax.ShapeDtypeStruct((BATCH * CTX_LEN, D_Q), BF16),
        scratch_shapes=_attn_scratch(CTX_LEN),
        compiler_params=_params(("arbitrary", "arbitrary")),
        name="ctx_attn",
    )(sink2, qc, kc, vtc)


def _to_time_major(src_ref, sc_ref, t0=0, nt=None):
    nt = src_ref.shape[1] if nt is None else nt
    r0 = t0 * SUBLANES
    for b in range(BATCH):
        for s in range(N_SLAB):
            sc_ref[s, pl.ds(r0 + b, nt, stride=SUBLANES), :] = src_ref[b, t0:t0 + nt, s * LANES:(s + 1) * LANES]
    return jnp.concatenate([sc_ref[s, r0:r0 + nt * SUBLANES, :] for s in range(N_SLAB)], axis=1)


def _from_time_major(val, sc_ref, dst_ref, t0, nt):
    r0 = t0 * SUBLANES
    for s in range(N_SLAB):
        sc_ref[s, r0:r0 + nt * SUBLANES, :] = val[:, s * LANES:(s + 1) * LANES]
    for b in range(BATCH):
        for s in range(N_SLAB):
            dst_ref[b, t0:t0 + nt, s * LANES:(s + 1) * LANES] = sc_ref[s, pl.ds(r0 + b, nt, stride=SUBLANES), :]


def _post_kernel(*refs, lru):
    if lru:
        (x_ref, gate_ref, yf_ref, yb_ref, mod_ref, g_ref, wf_ref, w1_ref, w2_ref, o_ref,
         x1_sc, h_sc, acc_sc, tout_sc) = refs
    else:
        x_ref, a_ref, mod_ref, g_ref, wf_ref, w1_ref, w2_ref, o_ref, x1_sc, h_sc, acc_sc = refs
    half_rows = x_ref.shape[0] // 2
    half_t = half_rows // BATCH

    def head(r):
        rs = slice(r * half_rows, (r + 1) * half_rows)
        if lru:
            front = (gate_ref[rs, :].astype(F32)
                     * (yf_ref[rs, :].astype(F32) + yb_ref[rs, :].astype(F32))).astype(BF16)
        else:
            front = a_ref[rs, :]
        y = jnp.dot(front, wf_ref[...], preferred_element_type=F32)
        x1 = _gated_add(x_ref[rs, :], mod_ref[0, 2], _rms(y, g_ref[1:2, :]))
        x1_sc[r] = x1
        h_sc[r] = _modulate(_rms(x1, g_ref[2:3, :]), mod_ref[0, 3], mod_ref[0, 4]).astype(BF16)

    def mlp(r):
        acc = jnp.zeros((half_rows, D_MODEL), F32)
        for c in range(D_FF // FF_CHUNK):
            hid = jnp.dot(h_sc[r], w1_ref[:, c * FF_CHUNK:(c + 1) * FF_CHUNK], preferred_element_type=F32)
            hid = jnp.square(jnp.maximum(hid, 0.0)).astype(BF16)
            acc = acc + jnp.dot(hid, w2_ref[c * FF_CHUNK:(c + 1) * FF_CHUNK, :], preferred_element_type=F32)
        acc_sc[r] = acc

    def tail(r):
        out = _gated_add(x1_sc[r], mod_ref[0, 5], _rms(acc_sc[r], g_ref[3:4, :]))
        if lru:
            _from_time_major(out, tout_sc, o_ref, r * half_t, half_t)
        else:
            o_ref[r * half_rows:(r + 1) * half_rows, :] = out

    head(0)
    head(1)
    mlp(0)
    tail(0)
    mlp(1)
    tail(1)


def _post_call(x, fronts, mod, g, w_front, w1, w2, layer, tiles_per_group, lru):
    tm = TOKEN_TILE if lru else MLP_TILE
    layer_spec = lambda shape: pl.BlockSpec((None,) + shape[1:], lambda i: (layer, 0, 0),
                                            pipeline_mode=pl.Buffered(1))
    row = lambda i: (i, 0)
    n = x.shape[0]
    x_spec = pl.BlockSpec((tm, D_MODEL), row)
    if lru:
        out_spec = pl.BlockSpec((BATCH, tm // BATCH, D_MODEL), lambda i: (0, i, 0))
        out_shape = jax.ShapeDtypeStruct((BATCH, n // BATCH, D_MODEL), F32)
        scratch = [pltpu.VMEM((N_SLAB, tm, LANES), F32)]
    else:
        out_spec, out_shape = x_spec, jax.ShapeDtypeStruct(x.shape, F32)
        scratch = []
    scratch = [
        pltpu.VMEM((2, tm // 2, D_MODEL), F32),
        pltpu.VMEM((2, tm // 2, D_MODEL), BF16),
        pltpu.VMEM((2, tm // 2, D_MODEL), F32),
    ] + scratch
    in_specs = [x_spec]
    in_specs += [pl.BlockSpec((tm, f.shape[1]), row) for f in fronts]
    in_specs += [
        pl.BlockSpec((1, N_MOD, SUBLANES, D_MODEL), lambda i: (i // tiles_per_group, 0, 0, 0)),
        _const_spec((4, D_MODEL)),
        _const_spec(w_front.shape),
        layer_spec(w1.shape),
        layer_spec(w2.shape),
    ]
    return pl.pallas_call(
        functools.partial(_post_kernel, lru=lru),
        grid=(n // tm,),
        in_specs=in_specs,
        out_specs=out_spec,
        out_shape=out_shape,
        scratch_shapes=scratch,
        compiler_params=_params(("parallel",)),
        name="lru_out_mlp" if lru else "attn_out_mlp",
    )(x, *fronts, mod, g, w_front, w1, w2)


def _lru_in_kernel(*refs, need_gate):
    if need_gate:
        (x_ref, xp_ref, xn_ref, mod_ref, g_ref, w_ref, cw_ref, cb_ref, gate_ref, u_ref, xt_ref,
         v_sc, h_sc, t_sc, tp_sc, tn_sc) = refs
    else:
        x_ref, xp_ref, xn_ref, mod_ref, g_ref, w_ref, cw_ref, cb_ref, u_ref, v_sc, h_sc, t_sc, tp_sc, tn_sc = refs
    i = pl.program_id(0)
    n = pl.num_programs(0)
    rows = x_ref.shape[0] * x_ref.shape[1]
    half = rows // 2
    half_t = x_ref.shape[1] // 2
    s8 = SUBLANES

    def pre(x):
        return _modulate(_rms(x, g_ref[0:1, :]), mod_ref[0, 0], mod_ref[0, 1]).astype(BF16)

    h_sc[0, 0:HALO] = pre(_to_time_major(xp_ref, tp_sc)[SUBLANES * SUBLANES - HALO:])
    x_halves = [_to_time_major(x_ref, t_sc, r * half_t, half_t) for r in range(2)]
    if need_gate:
        for r in range(2):
            xt_ref[r * half:(r + 1) * half, :] = x_halves[r]
    h_sc[0, HALO:] = pre(x_halves[0])
    h_sc[1, 0:half] = pre(x_halves[1])
    h_sc[1, half:] = pre(_to_time_major(xn_ref, tn_sc)[:HALO])
    ext = half + HALO
    for r in range(2):
        v_sc[r * ext:(r + 1) * ext] = jnp.dot(h_sc[r], w_ref[:, D_RNN:], preferred_element_type=F32)
    v_sc[0:HALO] = v_sc[0:HALO] * (i > 0).astype(F32)
    v_sc[HALO + rows:HALO + rows + s8] = v_sc[HALO + rows:HALO + rows + s8] * (i < n - 1).astype(F32)
    for r in range(2):
        if need_gate:
            h_r = h_sc[0, HALO:] if r == 0 else h_sc[1, 0:half]
            gate_ref[r * half:(r + 1) * half, :] = jax.nn.gelu(
                jnp.dot(h_r, w_ref[:, :D_RNN], preferred_element_type=F32)).astype(BF16)
        base = HALO + r * half
        u_ref[r * half:(r + 1) * half, :] = (
            cb_ref[...]
            + cw_ref[0:1, :] * v_sc[base - 2 * s8:base - 2 * s8 + half]
            + cw_ref[1:2, :] * v_sc[base - s8:base - s8 + half]
            + cw_ref[2:3, :] * v_sc[base:base + half]
            + cw_ref[3:4, :] * v_sc[base + s8:base + s8 + half]).astype(u_ref.dtype)


def _lru_in_call(x3, mod, g, w_in, conv_w, conv_b, need_gate):
    t_total = x3.shape[1]
    n = BATCH * t_total
    tm = LRU_IN_TILE
    nt = tm // BATCH
    row = lambda i: (i, 0)
    per_tile = nt // SUBLANES
    last = t_total // SUBLANES - 1
    halo_spec = lambda f: pl.BlockSpec((BATCH, SUBLANES, D_MODEL), f)
    out_specs = [pl.BlockSpec((tm, D_RNN), row), pl.BlockSpec((tm, D_RNN), row), pl.BlockSpec((tm, D_MODEL), row)]
    out_shape = [jax.ShapeDtypeStruct((n, D_RNN), BF16), jax.ShapeDtypeStruct((n, D_RNN), U_DTYPE),
                 jax.ShapeDtypeStruct((n, D_MODEL), F32)]
    if not need_gate:
        out_specs, out_shape = out_specs[1:2], out_shape[1:2]
    return pl.pallas_call(
        functools.partial(_lru_in_kernel, need_gate=need_gate),
        grid=(n // tm,),
        in_specs=[
            pl.BlockSpec((BATCH, nt, D_MODEL), lambda i: (0, i, 0)),
            halo_spec(lambda i: (0, jnp.maximum(i * per_tile - 1, 0), 0)),
            halo_spec(lambda i: (0, jnp.minimum((i + 1) * per_tile, last), 0)),
            _const_spec((1, N_MOD, SUBLANES, D_MODEL)),
            _const_spec((4, D_MODEL)),
            _const_spec((D_MODEL, 2 * D_RNN)),
            _const_spec((CONV_W, D_RNN)),
            _const_spec((1, D_RNN)),
        ],
        out_specs=out_specs,
        out_shape=out_shape,
        scratch_shapes=[
            pltpu.VMEM((tm + 2 * HALO, D_RNN), F32),
            pltpu.VMEM((2, tm // 2 + HALO, D_MODEL), BF16),
            pltpu.VMEM((N_SLAB, tm, LANES), F32),
            pltpu.VMEM((N_SLAB, SUBLANES * SUBLANES, LANES), F32),
            pltpu.VMEM((N_SLAB, SUBLANES * SUBLANES, LANES), F32),
        ],
        compiler_params=_params(("parallel",)),
        name="lru_in",
    )(x3, x3, x3, mod, g, w_in, conv_w, conv_b)


def _scan_kernel(uf_ref, ub_ref, h0_ref, wa_ref, ba_ref, wi_ref, bi_ref, lam_ref, yf_ref, yb_ref, ht_ref,
                 a_sc, bx_sc, h_sc):
    i = pl.program_id(0)
    n = pl.num_programs(0)
    rows = uf_ref.shape[0]
    nt = rows // SUBLANES
    s8 = SUBLANES

    @pl.when(i == 0)
    def _():
        h_sc[...] = h0_ref[...]

    for d, u_ref in enumerate((uf_ref, ub_ref)):
        for c in range(N_LRU_BLOCKS):
            cs = slice(c * LRU_BLOCK_W, (c + 1) * LRU_BLOCK_W)
            u16 = u_ref[:, cs].astype(BF16)
            u = u_ref[:, cs].astype(F32)
            ta = jnp.tanh(jnp.dot(u16, wa_ref[d, c], preferred_element_type=F32) + 0.5 * ba_ref[d, :, cs])
            ti = jnp.tanh(jnp.dot(u16, wi_ref[d, c], preferred_element_type=F32) + 0.5 * bi_ref[d, :, cs])
            neg_lam = -lam_ref[d, :, cs]
            softplus = jnp.maximum(neg_lam, 0.0) + jnp.log1p(jnp.exp(-jnp.abs(neg_lam)))
            k = (-0.5 * LRU_C * LOG2E) * softplus
            a = jnp.exp2(k * ta + k)
            w = 1.0 - a * a
            root = w * lax.rsqrt(jnp.maximum(w, 1e-30))
            a_sc[d, :, cs] = a
            bx_sc[d, :, cs] = root * (ti * u + u)

    def step(t, carry):
        hf, hb = carry
        rf = pl.multiple_of(t * 2 * s8, 2 * s8)
        rb = pl.multiple_of((nt - 2 - 2 * t) * s8, 2 * s8)
        hf1 = a_sc[0, pl.ds(rf, s8), :] * hf + bx_sc[0, pl.ds(rf, s8), :]
        hf2 = a_sc[0, pl.ds(rf + s8, s8), :] * hf1 + bx_sc[0, pl.ds(rf + s8, s8), :]
        yf_ref[pl.ds(rf, 2 * s8), :] = jnp.concatenate([hf1, hf2], axis=0).astype(yf_ref.dtype)
        hb1 = a_sc[1, pl.ds(rb + s8, s8), :] * hb + bx_sc[1, pl.ds(rb + s8, s8), :]
        hb2 = a_sc[1, pl.ds(rb, s8), :] * hb1 + bx_sc[1, pl.ds(rb, s8), :]
        yb_ref[pl.ds(rb, 2 * s8), :] = jnp.concatenate([hb2, hb1], axis=0).astype(yb_ref.dtype)
        return hf2, hb2

    hf, hb = lax.fori_loop(0, nt // 2, step, (h_sc[0], h_sc[1]), unroll=2)
    h_sc[0] = hf
    h_sc[1] = hb

    @pl.when(i == n - 1)
    def _():
        ht_ref[...] = h_sc[...]


def _scan_call(u2, h0, w_a, b_a, w_i, b_i, lam):
    rows_total = u2.shape[0]
    rows = SCAN_T * SUBLANES
    n = rows_total // rows
    w = D_RNN
    fwd = lambda i: (i, 0)
    bwd = lambda i: (n - 1 - i, 0)
    return pl.pallas_call(
        _scan_kernel,
        grid=(n,),
        in_specs=[
            pl.BlockSpec((rows, w), fwd),
            pl.BlockSpec((rows, w), bwd),
            _const_spec((2, SUBLANES, w)),
            _const_spec((2, N_LRU_BLOCKS, LRU_BLOCK_W, LRU_BLOCK_W)),
            _const_spec((2, 1, w)),
            _const_spec((2, N_LRU_BLOCKS, LRU_BLOCK_W, LRU_BLOCK_W)),
            _const_spec((2, 1, w)),
            _const_spec((2, 1, w)),
        ],
        out_specs=[
            pl.BlockSpec((rows, w), fwd),
            pl.BlockSpec((rows, w), bwd),
            pl.BlockSpec((2, SUBLANES, w), lambda i: (0, 0, 0)),
        ],
        out_shape=[
            jax.ShapeDtypeStruct((rows_total, w), Y_DTYPE),
            jax.ShapeDtypeStruct((rows_total, w), Y_DTYPE),
            jax.ShapeDtypeStruct((2, SUBLANES, w), F32),
        ],
        scratch_shapes=[
            pltpu.VMEM((2, rows, w), F32),
            pltpu.VMEM((2, rows, w), F32),
            pltpu.VMEM((2, SUBLANES, w), F32),
        ],
        compiler_params=_params(("arbitrary",)),
        name="lru_scan",
    )(u2, u2, h0, w_a, b_a, w_i, b_i, lam)


def _rope_tables():
    t = np.arange(SEQ)
    row = (t // GRID_W).astype(np.float64)
    col = (t % GRID_W).astype(np.float64)
    half = HEAD_DIM // 2
    inv = ROPE_BASE ** (-np.arange(0, half, 2, dtype=np.float64) / half)
    ang_r = row[:, None] * inv[None, :]
    ang_c = col[:, None] * inv[None, :]
    ang = np.concatenate([ang_r, ang_r, ang_c, ang_c], axis=-1)
    ang = np.tile(ang, (1, LANES // HEAD_DIM))
    low = (np.arange(LANES) % 32) < 16
    sin = np.sin(ang)
    tables = (np.cos(ang), np.where(low, -sin, 0.0), np.where(low, 0.0, sin))
    return tuple(jnp.asarray(a, dtype=F32) for a in tables)


def kernel(x, c, ctx, c_ctx, ada_w, ada_b, norm_g, mlp_w1, mlp_w2, attn_w_qkv, attn_w_o, attn_sink,
           lru_w_in, lru_conv_w, lru_conv_b, lru_w_a, lru_b_a, lru_w_i, lru_b_i, lru_lam, lru_w_out):
    n_lat = BATCH * SEQ
    n_ctx = BATCH * CTX_LEN

    c16 = jnp.zeros((16, D_MODEL), F32).at[:BATCH].set(c).at[BATCH].set(c_ctx)
    mods = _mod_call(c16, ada_w, ada_b).reshape(2, 16, N_MOD, D_MODEL)

    def slab_bmajor(m):
        return jnp.broadcast_to(m[:, :, None, :], (BATCH, N_MOD, SUBLANES, D_MODEL))

    def slab_ctx(m):
        return jnp.broadcast_to(m[None, :, None, :], (1, N_MOD, SUBLANES, D_MODEL))

    mod_x0 = slab_bmajor(mods[0, :BATCH])
    mod_c0 = slab_ctx(mods[0, BATCH])
    w_qkv = attn_w_qkv[0]
    w_qkv = jnp.concatenate([w_qkv[:, :D_Q] * (HEAD_DIM ** -0.5 * LOG2E), w_qkv[:, D_Q:]], axis=1).astype(BF16)
    sink2 = attn_sink[0] * LOG2E
    w_o = attn_w_o[0].astype(BF16)
    w1_all, w2_all = mlp_w1.astype(BF16), mlp_w2.astype(BF16)
    g0 = norm_g[0]
    tiles_per_batch = SEQ // TOKEN_TILE

    x2 = x.reshape(n_lat, D_MODEL)
    c2 = ctx.reshape(n_ctx, D_MODEL)
    q, k, v = _qkv_call(x2, mod_x0, g0, w_qkv, _rope_tables(), tiles_per_batch)
    qc, kc, vc = _qkv_call(c2, mod_c0, g0, w_qkv, None, n_ctx // TOKEN_TILE)
    kc3 = kc.reshape(BATCH, CTX_LEN, D_K2)
    att = _attn_call(sink2, q, k.reshape(BATCH, SEQ, D_K2), v, kc3, vc)
    att_c = _ctx_attn_call(sink2, qc, kc3, vc)
    x2 = _post_call(x2, [att], mod_x0, g0, w_o, w1_all, w2_all, 0, SEQ // MLP_TILE, lru=False)
    c2 = _post_call(c2, [att_c], mod_c0, g0, w_o, w1_all, w2_all, 0, n_ctx // MLP_TILE, lru=False)

    x3 = x2.reshape(BATCH, SEQ, D_MODEL)
    c3 = c2.reshape(BATCH, CTX_LEN, D_MODEL)
    mod_x1 = mods[1, :BATCH].transpose(1, 0, 2)[None]
    mod_c1 = slab_ctx(mods[1, BATCH])
    g1 = norm_g[1]
    w_in = lru_w_in[0].astype(BF16)
    conv_w = 0.5 * lru_conv_w[0]
    conv_b = 0.5 * lru_conv_b[0].reshape(1, D_RNN)
    w_a, w_i = lru_w_a[0].astype(BF16), lru_w_i[0].astype(BF16)
    b_a, b_i = lru_b_a[0].reshape(2, 1, D_RNN), lru_b_i[0].reshape(2, 1, D_RNN)
    lam = lru_lam[0].reshape(2, 1, D_RNN)
    scan = functools.partial(_scan_call, w_a=w_a, b_a=b_a, w_i=w_i, b_i=b_i, lam=lam)

    (u_c,) = _lru_in_call(c3, mod_c1, g1, w_in, conv_w, conv_b, need_gate=False)
    _, _, h_ctx = scan(u_c, jnp.zeros((2, SUBLANES, D_RNN), F32))
    gate_x, u_x, x_t = _lru_in_call(x3, mod_x1, g1, w_in, conv_w, conv_b, need_gate=True)
    yf, yb, _ = scan(u_x, h_ctx)
    return _post_call(x_t, [gate_x, yf, yb], mod_x1, g1, lru_w_out[0].astype(BF16),
                      w1_all, w2_all, 1, n_lat // TOKEN_TILE, lru=True)
```

```python
import functools

import jax
import jax.numpy as jnp
import numpy as np
from jax import lax
from jax.experimental import pallas as pl
from jax.experimental.pallas import tpu as pltpu

D_MODEL = 1024
BATCH = 8
SEQ = 2048
GRID_W = 64
CTX_LEN = 256
HEAD_DIM = 64
N_HEADS = 16
N_KV_HEADS = 4
GQA_GROUP = N_HEADS // N_KV_HEADS
WINDOW = 128
BLOCK = 128
ROPE_BASE = 10000.0
D_RNN = 1280
LRU_BLOCK_W = 256
N_LRU_BLOCKS = D_RNN // LRU_BLOCK_W
CONV_W = 4
LRU_C = 8.0
D_FF = 4 * D_MODEL
N_MOD = 6
EPS = 1e-6
NEG_INF = -1e30

D_Q = N_HEADS * HEAD_DIM
D_KV = N_KV_HEADS * HEAD_DIM
D_K2 = 2 * D_KV
LANES = 128
SUBLANES = 8
N_SLAB = D_MODEL // LANES
TOKEN_TILE = 512
MLP_TILE = 1024
LRU_IN_TILE = 512
FF_CHUNK = 1024
ATTN_Q_PER_STEP = 16
SCAN_T = 128
HALO = 16
U_DTYPE = jnp.bfloat16
Y_DTYPE = jnp.bfloat16
LOG2E = 1.4426950408889634
VMEM_LIMIT = 60 * 1024 * 1024

F32 = jnp.float32
BF16 = jnp.bfloat16


def _rms(x, g):
    ms = jnp.mean(x * x, axis=-1, keepdims=True)
    return x * lax.rsqrt(ms + EPS) * g


def _slab(x):
    return x.reshape(x.shape[0] // SUBLANES, SUBLANES, x.shape[1])


def _modulate(h, shift8, scale8):
    out = _slab(h) * (1.0 + scale8)[None] + shift8[None]
    return out.reshape(h.shape)


def _gated_add(x, gate8, y):
    out = _slab(x) + gate8[None] * _slab(y)
    return out.reshape(x.shape)


def _const_spec(shape):
    n = len(shape)
    return pl.BlockSpec(shape, lambda *_: (0,) * n, pipeline_mode=pl.Buffered(1))


def _params(sem):
    return pltpu.CompilerParams(dimension_semantics=sem, vmem_limit_bytes=VMEM_LIMIT)


def _mod_kernel(c_ref, w_ref, b_ref, o_ref):
    s = jax.nn.silu(c_ref[...]).astype(BF16)
    o_ref[0] = jnp.dot(s, w_ref[0].astype(BF16), preferred_element_type=F32) + b_ref[0]


def _mod_call(c16, ada_w, ada_b):
    depth = ada_w.shape[0]
    nt = 2048
    return pl.pallas_call(
        _mod_kernel,
        grid=(depth, N_MOD * D_MODEL // nt),
        in_specs=[
            pl.BlockSpec((16, D_MODEL), lambda l, j: (0, 0)),
            pl.BlockSpec((1, D_MODEL, nt), lambda l, j: (l, 0, j)),
            pl.BlockSpec((1, 1, nt), lambda l, j: (l, 0, j)),
        ],
        out_specs=pl.BlockSpec((1, 16, nt), lambda l, j: (l, 0, j)),
        out_shape=jax.ShapeDtypeStruct((depth, 16, N_MOD * D_MODEL), F32),
        compiler_params=_params(("arbitrary", "arbitrary")),
        name="adaln_mod",
    )(c16, ada_w, ada_b.reshape(depth, 1, N_MOD * D_MODEL))


def _qkv_kernel(*refs, rope):
    if rope:
        x_ref, mod_ref, g_ref, w_ref, cos_ref, sa_ref, sb_ref, q_ref, k_ref, v_ref, h_sc, y_sc = refs
    else:
        x_ref, mod_ref, g_ref, w_ref, q_ref, k_ref, v_ref, h_sc, y_sc = refs
    half = x_ref.shape[0] // 2
    low = lax.broadcasted_iota(jnp.int32, (half, LANES), 1) < HEAD_DIM
    for r in range(2):
        rs = slice(r * half, (r + 1) * half)
        h_sc[r] = _modulate(_rms(x_ref[rs, :], g_ref[0:1, :]), mod_ref[0, 0], mod_ref[0, 1]).astype(BF16)
    for r in range(2):
        y_sc[r] = jnp.dot(h_sc[r], w_ref[...], preferred_element_type=F32)
    for r in range(2):
        rs = slice(r * half, (r + 1) * half)
        if rope:
            cos, sa, sb = cos_ref[rs, :], sa_ref[rs, :], sb_ref[rs, :]
        for c in range((D_Q + D_KV) // LANES):
            yc = y_sc[r, :, c * LANES:(c + 1) * LANES]
            if rope:
                yc = yc * cos + pltpu.roll(yc, LANES - 16, 1) * sa + pltpu.roll(yc, 16, 1) * sb
            if c < D_Q // LANES:
                q_ref[rs, c * LANES:(c + 1) * LANES] = yc.astype(BF16)
            else:
                c2 = 2 * (c - D_Q // LANES)
                swapped = pltpu.roll(yc, HEAD_DIM, 1)
                k_ref[rs, c2 * LANES:(c2 + 1) * LANES] = jnp.where(low, yc, swapped).astype(BF16)
                k_ref[rs, (c2 + 1) * LANES:(c2 + 2) * LANES] = jnp.where(low, swapped, yc).astype(BF16)
        for blk in range(half // BLOCK):
            v_ref[r * (half // BLOCK) + blk] = y_sc[r, blk * BLOCK:(blk + 1) * BLOCK, D_Q + D_KV:].T.astype(BF16)


def _qkv_call(x2, mod, g, w_qkv, tables, tiles_per_group):
    n = x2.shape[0]
    tm = TOKEN_TILE
    rope = tables is not None
    in_specs = [
        pl.BlockSpec((tm, D_MODEL), lambda i: (i, 0)),
        pl.BlockSpec((1, N_MOD, SUBLANES, D_MODEL), lambda i: (i // tiles_per_group, 0, 0, 0)),
        _const_spec((4, D_MODEL)),
        _const_spec((D_MODEL, D_Q + 2 * D_KV)),
    ]
    args = [x2, mod, g, w_qkv]
    if rope:
        nt = SEQ // tm
        in_specs += [pl.BlockSpec((tm, LANES), lambda i: (i % nt, 0))] * 3
        args += list(tables)
    return pl.pallas_call(
        functools.partial(_qkv_kernel, rope=rope),
        grid=(n // tm,),
        in_specs=in_specs,
        out_specs=[
            pl.BlockSpec((tm, D_Q), lambda i: (i, 0)),
            pl.BlockSpec((tm, D_K2), lambda i: (i, 0)),
            pl.BlockSpec((tm // BLOCK, D_KV, BLOCK), lambda i: (i, 0, 0)),
        ],
        out_shape=[
            jax.ShapeDtypeStruct((n, D_Q), BF16),
            jax.ShapeDtypeStruct((n, D_K2), BF16),
            jax.ShapeDtypeStruct((n // BLOCK, D_KV, BLOCK), BF16),
        ],
        scratch_shapes=[
            pltpu.VMEM((2, tm // 2, D_MODEL), BF16),
            pltpu.VMEM((2, tm // 2, D_Q + 2 * D_KV), F32),
        ],
        compiler_params=_params(("parallel",)),
        name="qkv_rope" if rope else "qkv_ctx",
    )(*args)


VT_ROWS = HEAD_DIM + 16


def _attn_kernel(*refs, local, q_per_step):
    if local:
        sink_ref, q_ref, k_ref, v_ref, kc_ref, vc_ref, o_ref, s_sc, p_sc, bias_sc = refs
    else:
        sink_ref, q_ref, kc_ref, vc_ref, o_ref, s_sc, p_sc, bias_sc = refs
    j = pl.program_id(1)
    seq_blocks = SEQ // BLOCK if local else 0
    ctx_blocks = CTX_LEN // BLOCK
    n_keys = s_sc.shape[1]
    pad_row = lax.broadcasted_iota(jnp.int32, (VT_ROWS - HEAD_DIM, n_keys), 0)
    vt_pad = jnp.where(pad_row == 0, 1.0, 0.0).astype(BF16)

    nt = (((1,), (1,)), ((), ()))
    n_band = 3
    n_chunks = D_Q // LANES
    lane = lax.broadcasted_iota(jnp.int32, (BLOCK, LANES), 1)
    first_head = lax.broadcasted_iota(jnp.int32, (1, 2 * BLOCK), 1) < BLOCK

    def block_params(qb):
        jq = j * q_per_step + qb
        rows = pl.ds(pl.multiple_of(qb * BLOCK, BLOCK), BLOCK)
        if not local:
            return rows, None, None, None, qb // ctx_blocks
        blk0 = jnp.clip(jq - 1, 0, seq_blocks - n_band)
        return rows, blk0, pl.multiple_of(blk0 * BLOCK, BLOCK), qb % 2, 0

    def store_bias(qb):
        if local:
            jq = j * q_per_step + qb
            start = jnp.clip(jq - 1, 0, seq_blocks - n_band) * BLOCK
            kpos = start + lax.broadcasted_iota(jnp.int32, (n_band * BLOCK, BLOCK), 0)
            qpos = jq * BLOCK + lax.broadcasted_iota(jnp.int32, (n_band * BLOCK, BLOCK), 1)
            bias_sc[qb % 2] = jnp.where(jnp.abs(kpos - qpos) <= WINDOW, 0.0, NEG_INF).astype(F32)

    def scores(c, params):
        rows, _, start, slot, cb = params
        kcols = slice((c // 2) * LANES, (c // 2 + 1) * LANES)
        qc = q_ref[rows, c * LANES:(c + 1) * LANES]
        zero = jnp.zeros_like(qc)
        q2 = jnp.concatenate([jnp.where(lane < HEAD_DIM, qc, zero), jnp.where(lane < HEAD_DIM, zero, qc)], axis=0)
        s_sc[c % 2, 0:CTX_LEN] = lax.dot_general(kc_ref[cb, :, kcols], q2, nt, preferred_element_type=F32)
        if local:
            s_band = lax.dot_general(k_ref[0, pl.ds(start, n_band * BLOCK), kcols], q2, nt,
                                     preferred_element_type=F32)
            for hd in range(2):
                s_sc[c % 2, CTX_LEN:, hd * BLOCK:(hd + 1) * BLOCK] = (
                    s_band[:, hd * BLOCK:(hd + 1) * BLOCK] + bias_sc[slot])

    def softmax(c):
        s = s_sc[c % 2]
        sink_row = jnp.where(first_head, sink_ref[2 * c], sink_ref[2 * c + 1])
        m = jnp.maximum(jnp.max(s, axis=0, keepdims=True), sink_row)
        p_sc[c % 2] = jnp.exp2(s - m).astype(BF16)
        return jnp.exp2(sink_row - m)

    def values(c, params, sink_term):
        rows, blk0, _, _, cb = params
        hrows = slice((c // 2) * HEAD_DIM, (c // 2 + 1) * HEAD_DIM)
        vt = [vc_ref[cb * ctx_blocks + i, hrows, :] for i in range(ctx_blocks)]
        if local:
            vt_band = v_ref[pl.ds(blk0, n_band), hrows, :]
            vt += [vt_band[i] for i in range(n_band)]
        vt_aug = jnp.concatenate([jnp.concatenate(vt, axis=1), vt_pad], axis=0)
        acc = jnp.dot(vt_aug, p_sc[c % 2], preferred_element_type=F32)
        out_t = acc[0:HEAD_DIM] * (1.0 / (acc[HEAD_DIM:HEAD_DIM + 1] + sink_term))
        both = jnp.concatenate([out_t[:, :BLOCK], out_t[:, BLOCK:]], axis=0)
        o_ref[rows, c * LANES:(c + 1) * LANES] = both.T.astype(o_ref.dtype)

    def query_block(qb, sink_term0):
        cur = block_params(qb)
        nxt = block_params(jnp.minimum(qb + 1, q_per_step - 1))
        store_bias(qb + 1)
        sink_terms = {0: sink_term0}
        for c in range(n_chunks):
            if c + 2 < n_chunks:
                scores(c + 2, cur)
            else:
                scores(c + 2 - n_chunks, nxt)
            sink_terms[c + 1] = softmax((c + 1) % n_chunks)
            values(c, cur, sink_terms[c])
        return sink_terms[n_chunks]

    first = block_params(0)
    store_bias(0)
    scores(0, first)
    scores(1, first)
    lax.fori_loop(0, q_per_step, query_block, softmax(0))


def _attn_scratch(n_keys):
    return [
        pltpu.VMEM((2, n_keys, 2 * BLOCK), F32),
        pltpu.VMEM((2, n_keys, 2 * BLOCK), BF16),
        pltpu.VMEM((2, 3 * BLOCK, BLOCK), F32),
    ]


def _attn_call(sink2, q, k, vt, kc, vtc):
    nb = SEQ // BLOCK
    nbc = CTX_LEN // BLOCK
    qps = ATTN_Q_PER_STEP
    steps = nb // qps
    return pl.pallas_call(
        functools.partial(_attn_kernel, local=True, q_per_step=qps),
        grid=(BATCH, steps),
        in_specs=[
            pl.BlockSpec(memory_space=pltpu.SMEM),
            pl.BlockSpec((qps * BLOCK, D_Q), lambda b, j: (b * steps + j, 0)),
            pl.BlockSpec((1, SEQ, D_K2), lambda b, j: (b, 0, 0)),
            pl.BlockSpec((nb, D_KV, BLOCK), lambda b, j: (b, 0, 0)),
            pl.BlockSpec((1, CTX_LEN, D_K2), lambda b, j: (b, 0, 0)),
            pl.BlockSpec((nbc, D_KV, BLOCK), lambda b, j: (b, 0, 0)),
        ],
        out_specs=pl.BlockSpec((qps * BLOCK, D_Q), lambda b, j: (b * steps + j, 0)),
        out_shape=jax.ShapeDtypeStruct((BATCH * SEQ, D_Q), BF16),
        scratch_shapes=_attn_scratch(CTX_LEN + 3 * BLOCK),
        compiler_params=_params(("parallel", "arbitrary")),
        name="band_attn",
    )(sink2, q, k, vt, kc, vtc)


def _ctx_attn_call(sink2, qc, kc, vtc):
    nb = BATCH * CTX_LEN // BLOCK
    whole = lambda shape: pl.BlockSpec(shape, lambda b, j: (0,) * len(shape))
    return pl.pallas_call(
        functools.partial(_attn_kernel, local=False, q_per_step=nb),
        grid=(1, 1),
        in_specs=[
            pl.BlockSpec(memory_space=pltpu.SMEM),
            whole((BATCH * CTX_LEN, D_Q)),
            whole((BATCH, CTX_LEN, D_K2)),
            whole((nb, D_KV, BLOCK)),
        ],
        out_specs=whole((BATCH * CTX_LEN, D_Q)),
        out_shape=jax.ShapeDtypeStruct((BATCH * CTX_LEN, D_Q), BF16),
        scratch_shapes=_attn_scratch(CTX_LEN),
        compiler_params=_params(("arbitrary", "arbitrary")),
        name="ctx_attn",
    )(sink2, qc, kc, vtc)


def _to_time_major(src_ref, sc_ref, t0=0, nt=None):
    nt = src_ref.shape[1] if nt is None else nt
    r0 = t0 * SUBLANES
    for b in range(BATCH):
        for s in range(N_SLAB):
            sc_ref[s, pl.ds(r0 + b, nt, stride=SUBLANES), :] = src_ref[b, t0:t0 + nt, s * LANES:(s + 1) * LANES]
    return jnp.concatenate([sc_ref[s, r0:r0 + nt * SUBLANES, :] for s in range(N_SLAB)], axis=1)


def _from_time_major(val, sc_ref, dst_ref, t0, nt):
    r0 = t0 * SUBLANES
    for s in range(N_SLAB):
        sc_ref[s, r0:r0 + nt * SUBLANES, :] = val[:, s * LANES:(s + 1) * LANES]
    for b in range(BATCH):
        for s in range(N_SLAB):
            dst_ref[b, t0:t0 + nt, s * LANES:(s + 1) * LANES] = sc_ref[s, pl.ds(r0 + b, nt, stride=SUBLANES), :]


def _post_kernel(*refs, lru):
    if lru:
        (x_ref, gate_ref, yf_ref, yb_ref, mod_ref, g_ref, wf_ref, w1_ref, w2_ref, o_ref,
         x1_sc, h_sc, acc_sc, tout_sc) = refs
    else:
        x_ref, a_ref, mod_ref, g_ref, wf_ref, w1_ref, w2_ref, o_ref, x1_sc, h_sc, acc_sc = refs
    half_rows = x_ref.shape[0] // 2
    half_t = half_rows // BATCH

    def head(r):
        rs = slice(r * half_rows, (r + 1) * half_rows)
        if lru:
            front = (gate_ref[rs, :].astype(F32)
                     * (yf_ref[rs, :].astype(F32) + yb_ref[rs, :].astype(F32))).astype(BF16)
        else:
            front = a_ref[rs, :]
        y = jnp.dot(front, wf_ref[...], preferred_element_type=F32)
        x1 = _gated_add(x_ref[rs, :], mod_ref[0, 2], _rms(y, g_ref[1:2, :]))
        x1_sc[r] = x1
        h_sc[r] = _modulate(_rms(x1, g_ref[2:3, :]), mod_ref[0, 3], mod_ref[0, 4]).astype(BF16)

    def mlp(r):
        acc = jnp.zeros((half_rows, D_MODEL), F32)
        for c in range(D_FF // FF_CHUNK):
            hid = jnp.dot(h_sc[r], w1_ref[:, c * FF_CHUNK:(c + 1) * FF_CHUNK], preferred_element_type=F32)
            hid = jnp.square(jnp.maximum(hid, 0.0)).astype(BF16)
            acc = acc + jnp.dot(hid, w2_ref[c * FF_CHUNK:(c + 1) * FF_CHUNK, :], preferred_element_type=F32)
        acc_sc[r] = acc

    def tail(r):
        out = _gated_add(x1_sc[r], mod_ref[0, 5], _rms(acc_sc[r], g_ref[3:4, :]))
        if lru:
            _from_time_major(out, tout_sc, o_ref, r * half_t, half_t)
        else:
            o_ref[r * half_rows:(r + 1) * half_rows, :] = out

    head(0)
    head(1)
    mlp(0)
    tail(0)
    mlp(1)
    tail(1)


def _post_call(x, fronts, mod, g, w_front, w1, w2, layer, tiles_per_group, lru):
    tm = TOKEN_TILE if lru else MLP_TILE
    layer_spec = lambda shape: pl.BlockSpec((None,) + shape[1:], lambda i: (layer, 0, 0),
                                            pipeline_mode=pl.Buffered(1))
    row = lambda i: (i, 0)
    n = x.shape[0]
    x_spec = pl.BlockSpec((tm, D_MODEL), row)
    if lru:
        out_spec = pl.BlockSpec((BATCH, tm // BATCH, D_MODEL), lambda i: (0, i, 0))
        out_shape = jax.ShapeDtypeStruct((BATCH, n // BATCH, D_MODEL), F32)
        scratch = [pltpu.VMEM((N_SLAB, tm, LANES), F32)]
    else:
        out_spec, out_shape = x_spec, jax.ShapeDtypeStruct(x.shape, F32)
        scratch = []
    scratch = [
        pltpu.VMEM((2, tm // 2, D_MODEL), F32),
        pltpu.VMEM((2, tm // 2, D_MODEL), BF16),
        pltpu.VMEM((2, tm // 2, D_MODEL), F32),
    ] + scratch
    in_specs = [x_spec]
    in_specs += [pl.BlockSpec((tm, f.shape[1]), row) for f in fronts]
    in_specs += [
        pl.BlockSpec((1, N_MOD, SUBLANES, D_MODEL), lambda i: (i // tiles_per_group, 0, 0, 0)),
        _const_spec((4, D_MODEL)),
        _const_spec(w_front.shape),
        layer_spec(w1.shape),
        layer_spec(w2.shape),
    ]
    return pl.pallas_call(
        functools.partial(_post_kernel, lru=lru),
        grid=(n // tm,),
        in_specs=in_specs,
        out_specs=out_spec,
        out_shape=out_shape,
        scratch_shapes=scratch,
        compiler_params=_params(("parallel",)),
        name="lru_out_mlp" if lru else "attn_out_mlp",
    )(x, *fronts, mod, g, w_front, w1, w2)


def _lru_in_kernel(*refs, need_gate):
    if need_gate:
        (x_ref, xp_ref, xn_ref, mod_ref, g_ref, w_ref, cw_ref, cb_ref, gate_ref, u_ref, xt_ref,
         v_sc, h_sc, t_sc, tp_sc, tn_sc) = refs
    else:
        x_ref, xp_ref, xn_ref, mod_ref, g_ref, w_ref, cw_ref, cb_ref, u_ref, v_sc, h_sc, t_sc, tp_sc, tn_sc = refs
    i = pl.program_id(0)
    n = pl.num_programs(0)
    rows = x_ref.shape[0] * x_ref.shape[1]
    half = rows // 2
    half_t = x_ref.shape[1] // 2
    s8 = SUBLANES

    def pre(x):
        return _modulate(_rms(x, g_ref[0:1, :]), mod_ref[0, 0], mod_ref[0, 1]).astype(BF16)

    h_sc[0, 0:HALO] = pre(_to_time_major(xp_ref, tp_sc)[SUBLANES * SUBLANES - HALO:])
    x_halves = [_to_time_major(x_ref, t_sc, r * half_t, half_t) for r in range(2)]
    if need_gate:
        for r in range(2):
            xt_ref[r * half:(r + 1) * half, :] = x_halves[r]
    h_sc[0, HALO:] = pre(x_halves[0])
    h_sc[1, 0:half] = pre(x_halves[1])
    h_sc[1, half:] = pre(_to_time_major(xn_ref, tn_sc)[:HALO])
    ext = half + HALO
    for r in range(2):
        v_sc[r * ext:(r + 1) * ext] = jnp.dot(h_sc[r], w_ref[:, D_RNN:], preferred_element_type=F32)
    v_sc[0:HALO] = v_sc[0:HALO] * (i > 0).astype(F32)
    v_sc[HALO + rows:HALO + rows + s8] = v_sc[HALO + rows:HALO + rows + s8] * (i < n - 1).astype(F32)
    for r in range(2):
        if need_gate:
            h_r = h_sc[0, HALO:] if r == 0 else h_sc[1, 0:half]
            gate_ref[r * half:(r + 1) * half, :] = jax.nn.gelu(
                jnp.dot(h_r, w_ref[:, :D_RNN], preferred_element_type=F32)).astype(BF16)
        base = HALO + r * half
        u_ref[r * half:(r + 1) * half, :] = (
            cb_ref[...]
            + cw_ref[0:1, :] * v_sc[base - 2 * s8:base - 2 * s8 + half]
            + cw_ref[1:2, :] * v_sc[base - s8:base - s8 + half]
            + cw_ref[2:3, :] * v_sc[base:base + half]
            + cw_ref[3:4, :] * v_sc[base + s8:base + s8 + half]).astype(u_ref.dtype)


def _lru_in_call(x3, mod, g, w_in, conv_w, conv_b, need_gate):
    t_total = x3.shape[1]
    n = BATCH * t_total
    tm = LRU_IN_TILE
    nt = tm // BATCH
    row = lambda i: (i, 0)
    per_tile = nt // SUBLANES
    last = t_total // SUBLANES - 1
    halo_spec = lambda f: pl.BlockSpec((BATCH, SUBLANES, D_MODEL), f)
    out_specs = [pl.BlockSpec((tm, D_RNN), row), pl.BlockSpec((tm, D_RNN), row), pl.BlockSpec((tm, D_MODEL), row)]
    out_shape = [jax.ShapeDtypeStruct((n, D_RNN), BF16), jax.ShapeDtypeStruct((n, D_RNN), U_DTYPE),
                 jax.ShapeDtypeStruct((n, D_MODEL), F32)]
    if not need_gate:
        out_specs, out_shape = out_specs[1:2], out_shape[1:2]
    return pl.pallas_call(
        functools.partial(_lru_in_kernel, need_gate=need_gate),
        grid=(n // tm,),
        in_specs=[
            pl.BlockSpec((BATCH, nt, D_MODEL), lambda i: (0, i, 0)),
            halo_spec(lambda i: (0, jnp.maximum(i * per_tile - 1, 0), 0)),
            halo_spec(lambda i: (0, jnp.minimum((i + 1) * per_tile, last), 0)),
            _const_spec((1, N_MOD, SUBLANES, D_MODEL)),
            _const_spec((4, D_MODEL)),
            _const_spec((D_MODEL, 2 * D_RNN)),
            _const_spec((CONV_W, D_RNN)),
            _const_spec((1, D_RNN)),
        ],
        out_specs=out_specs,
        out_shape=out_shape,
        scratch_shapes=[
            pltpu.VMEM((tm + 2 * HALO, D_RNN), F32),
            pltpu.VMEM((2, tm // 2 + HALO, D_MODEL), BF16),
            pltpu.VMEM((N_SLAB, tm, LANES), F32),
            pltpu.VMEM((N_SLAB, SUBLANES * SUBLANES, LANES), F32),
            pltpu.VMEM((N_SLAB, SUBLANES * SUBLANES, LANES), F32),
        ],
        compiler_params=_params(("parallel",)),
        name="lru_in",
    )(x3, x3, x3, mod, g, w_in, conv_w, conv_b)


def _scan_kernel(uf_ref, ub_ref, h0_ref, wa_ref, ba_ref, wi_ref, bi_ref, lam_ref, yf_ref, yb_ref, ht_ref,
                 a_sc, bx_sc, h_sc):
    i = pl.program_id(0)
    n = pl.num_programs(0)
    rows = uf_ref.shape[0]
    nt = rows // SUBLANES
    s8 = SUBLANES

    @pl.when(i == 0)
    def _():
        h_sc[...] = h0_ref[...]

    for d, u_ref in enumerate((uf_ref, ub_ref)):
        for c in range(N_LRU_BLOCKS):
            cs = slice(c * LRU_BLOCK_W, (c + 1) * LRU_BLOCK_W)
            u16 = u_ref[:, cs].astype(BF16)
            u = u_ref[:, cs].astype(F32)
            ta = jnp.tanh(jnp.dot(u16, wa_ref[d, c], preferred_element_type=F32) + 0.5 * ba_ref[d, :, cs])
            ti = jnp.tanh(jnp.dot(u16, wi_ref[d, c], preferred_element_type=F32) + 0.5 * bi_ref[d, :, cs])
            neg_lam = -lam_ref[d, :, cs]
            softplus = jnp.maximum(neg_lam, 0.0) + jnp.log1p(jnp.exp(-jnp.abs(neg_lam)))
            k = (-0.5 * LRU_C * LOG2E) * softplus
            a = jnp.exp2(k * ta + k)
            w = 1.0 - a * a
            root = w * lax.rsqrt(jnp.maximum(w, 1e-30))
            a_sc[d, :, cs] = a
            bx_sc[d, :, cs] = root * (ti * u + u)

    def step(t, carry):
        hf, hb = carry
        rf = pl.multiple_of(t * 2 * s8, 2 * s8)
        rb = pl.multiple_of((nt - 2 - 2 * t) * s8, 2 * s8)
        hf1 = a_sc[0, pl.ds(rf, s8), :] * hf + bx_sc[0, pl.ds(rf, s8), :]
        hf2 = a_sc[0, pl.ds(rf + s8, s8), :] * hf1 + bx_sc[0, pl.ds(rf + s8, s8), :]
        yf_ref[pl.ds(rf, 2 * s8), :] = jnp.concatenate([hf1, hf2], axis=0).astype(yf_ref.dtype)
        hb1 = a_sc[1, pl.ds(rb + s8, s8), :] * hb + bx_sc[1, pl.ds(rb + s8, s8), :]
        hb2 = a_sc[1, pl.ds(rb, s8), :] * hb1 + bx_sc[1, pl.ds(rb, s8), :]
        yb_ref[pl.ds(rb, 2 * s8), :] = jnp.concatenate([hb2, hb1], axis=0).astype(yb_ref.dtype)
        return hf2, hb2

    hf, hb = lax.fori_loop(0, nt // 2, step, (h_sc[0], h_sc[1]), unroll=2)
    h_sc[0] = hf
    h_sc[1] = hb

    @pl.when(i == n - 1)
    def _():
        ht_ref[...] = h_sc[...]


def _scan_call(u2, h0, w_a, b_a, w_i, b_i, lam):
    rows_total = u2.shape[0]
    rows = SCAN_T * SUBLANES
    n = rows_total // rows
    w = D_RNN
    fwd = lambda i: (i, 0)
    bwd = lambda i: (n - 1 - i, 0)
    return pl.pallas_call(
        _scan_kernel,
        grid=(n,),
        in_specs=[
            pl.BlockSpec((rows, w), fwd),
            pl.BlockSpec((rows, w), bwd),
            _const_spec((2, SUBLANES, w)),
            _const_spec((2, N_LRU_BLOCKS, LRU_BLOCK_W, LRU_BLOCK_W)),
            _const_spec((2, 1, w)),
            _const_spec((2, N_LRU_BLOCKS, LRU_BLOCK_W, LRU_BLOCK_W)),
            _const_spec((2, 1, w)),
            _const_spec((2, 1, w)),
        ],
        out_specs=[
            pl.BlockSpec((rows, w), fwd),
            pl.BlockSpec((rows, w), bwd),
            pl.BlockSpec((2, SUBLANES, w), lambda i: (0, 0, 0)),
        ],
        out_shape=[
            jax.ShapeDtypeStruct((rows_total, w), Y_DTYPE),
            jax.ShapeDtypeStruct((rows_total, w), Y_DTYPE),
            jax.ShapeDtypeStruct((2, SUBLANES, w), F32),
        ],
        scratch_shapes=[
            pltpu.VMEM((2, rows, w), F32),
            pltpu.VMEM((2, rows, w), F32),
            pltpu.VMEM((2, SUBLANES, w), F32),
        ],
        compiler_params=_params(("arbitrary",)),
        name="lru_scan",
    )(u2, u2, h0, w_a, b_a, w_i, b_i, lam)


def _rope_tables():
    t = np.arange(SEQ)
    row = (t // GRID_W).astype(np.float64)
    col = (t % GRID_W).astype(np.float64)
    half = HEAD_DIM // 2
    inv = ROPE_BASE ** (-np.arange(0, half, 2, dtype=np.float64) / half)
    ang_r = row[:, None] * inv[None, :]
    ang_c = col[:, None] * inv[None, :]
    ang = np.concatenate([ang_r, ang_r, ang_c, ang_c], axis=-1)
    ang = np.tile(ang, (1, LANES // HEAD_DIM))
    low = (np.arange(LANES) % 32) < 16
    sin = np.sin(ang)
    tables = (np.cos(ang), np.where(low, -sin, 0.0), np.where(low, 0.0, sin))
    return tuple(jnp.asarray(a, dtype=F32) for a in tables)


def kernel(x, c, ctx, c_ctx, ada_w, ada_b, norm_g, mlp_w1, mlp_w2, attn_w_qkv, attn_w_o, attn_sink,
           lru_w_in, lru_conv_w, lru_conv_b, lru_w_a, lru_b_a, lru_w_i, lru_b_i, lru_lam, lru_w_out):
    n_lat = BATCH * SEQ
    n_ctx = BATCH * CTX_LEN

    c16 = jnp.zeros((16, D_MODEL), F32).at[:BATCH].set(c).at[BATCH].set(c_ctx)
    mods = _mod_call(c16, ada_w, ada_b).reshape(2, 16, N_MOD, D_MODEL)

    def slab_bmajor(m):
        return jnp.broadcast_to(m[:, :, None, :], (BATCH, N_MOD, SUBLANES, D_MODEL))

    def slab_ctx(m):
        return jnp.broadcast_to(m[None, :, None, :], (1, N_MOD, SUBLANES, D_MODEL))

    mod_x0 = slab_bmajor(mods[0, :BATCH])
    mod_c0 = slab_ctx(mods[0, BATCH])
    w_qkv = attn_w_qkv[0]
    w_qkv = jnp.concatenate([w_qkv[:, :D_Q] * (HEAD_DIM ** -0.5 * LOG2E), w_qkv[:, D_Q:]], axis=1).astype(BF16)
    sink2 = attn_sink[0] * LOG2E
    w_o = attn_w_o[0].astype(BF16)
    w1_all, w2_all = mlp_w1.astype(BF16), mlp_w2.astype(BF16)
    g0 = norm_g[0]
    tiles_per_batch = SEQ // TOKEN_TILE

    x2 = x.reshape(n_lat, D_MODEL)
    c2 = ctx.reshape(n_ctx, D_MODEL)
    q, k, v = _qkv_call(x2, mod_x0, g0, w_qkv, _rope_tables(), tiles_per_batch)
    qc, kc, vc = _qkv_call(c2, mod_c0, g0, w_qkv, None, n_ctx // TOKEN_TILE)
    kc3 = kc.reshape(BATCH, CTX_LEN, D_K2)
    att = _attn_call(sink2, q, k.reshape(BATCH, SEQ, D_K2), v, kc3, vc)
    att_c = _ctx_attn_call(sink2, qc, kc3, vc)
    x2 = _post_call(x2, [att], mod_x0, g0, w_o, w1_all, w2_all, 0, SEQ // MLP_TILE, lru=False)
    c2 = _post_call(c2, [att_c], mod_c0, g0, w_o, w1_all, w2_all, 0, n_ctx // MLP_TILE, lru=False)

    x3 = x2.reshape(BATCH, SEQ, D_MODEL)
    c3 = c2.reshape(BATCH, CTX_LEN, D_MODEL)
    mod_x1 = mods[1, :BATCH].transpose(1, 0, 2)[None]
    mod_c1 = slab_ctx(mods[1, BATCH])
    g1 = norm_g[1]
    w_in = lru_w_in[0].astype(BF16)
    conv_w = 0.5 * lru_conv_w[0]
    conv_b = 0.5 * lru_conv_b[0].reshape(1, D_RNN)
    w_a, w_i = lru_w_a[0].astype(BF16), lru_w_i[0].astype(BF16)
    b_a, b_i = lru_b_a[0].reshape(2, 1, D_RNN), lru_b_i[0].reshape(2, 1, D_RNN)
    lam = lru_lam[0].reshape(2, 1, D_RNN)
    scan = functools.partial(_scan_call, w_a=w_a, b_a=b_a, w_i=w_i, b_i=b_i, lam=lam)

    (u_c,) = _lru_in_call(c3, mod_c1, g1, w_in, conv_w, conv_b, need_gate=False)
    _, _, h_ctx = scan(u_c, jnp.zeros((2, SUBLANES, D_RNN), F32))
    gate_x, u_x, x_t = _lru_in_call(x3, mod_x1, g1, w_in, conv_w, conv_b, need_gate=True)
    yf, yb, _ = scan(u_x, h_ctx)
    return _post_call(x_t, [gate_x, yf, yb], mod_x1, g1, lru_w_out[0].astype(BF16),
                      w1_all, w2_all, 1, n_lat // TOKEN_TILE, lru=True)
```

```python
import functools

import jax
import jax.numpy as jnp
import numpy as np
from jax import lax
from jax.experimental import pallas as pl
from jax.experimental.pallas import tpu as pltpu

D_MODEL = 1024
BATCH = 8
SEQ = 2048
GRID_W = 64
CTX_LEN = 256
HEAD_DIM = 64
N_HEADS = 16
N_KV_HEADS = 4
GQA_GROUP = N_HEADS // N_KV_HEADS
WINDOW = 128
BLOCK = 128
ROPE_BASE = 10000.0
D_RNN = 1280
LRU_BLOCK_W = 256
N_LRU_BLOCKS = D_RNN // LRU_BLOCK_W
CONV_W = 4
LRU_C = 8.0
D_FF = 4 * D_MODEL
N_MOD = 6
EPS = 1e-6
NEG_INF = -1e30

D_Q = N_HEADS * HEAD_DIM
D_KV = N_KV_HEADS * HEAD_DIM
D_K2 = 2 * D_KV
LANES = 128
SUBLANES = 8
N_SLAB = D_MODEL // LANES
TOKEN_TILE = 512
MLP_TILE = 1024
LRU_IN_TILE = 512
FF_CHUNK = 1024
ATTN_Q_PER_STEP = 16
SCAN_T = 128
HALO = 16
U_DTYPE = jnp.bfloat16
Y_DTYPE = jnp.bfloat16
LOG2E = 1.4426950408889634
VMEM_LIMIT = 60 * 1024 * 1024

F32 = jnp.float32
BF16 = jnp.bfloat16


def _rms(x, g):
    ms = jnp.mean(x * x, axis=-1, keepdims=True)
    return x * lax.rsqrt(ms + EPS) * g


def _slab(x):
    return x.reshape(x.shape[0] // SUBLANES, SUBLANES, x.shape[1])


def _modulate(h, shift8, scale8):
    out = _slab(h) * (1.0 + scale8)[None] + shift8[None]
    return out.reshape(h.shape)


def _gated_add(x, gate8, y):
    out = _slab(x) + gate8[None] * _slab(y)
    return out.reshape(x.shape)


def _const_spec(shape):
    n = len(shape)
    return pl.BlockSpec(shape, lambda *_: (0,) * n, pipeline_mode=pl.Buffered(1))


def _params(sem):
    return pltpu.CompilerParams(dimension_semantics=sem, vmem_limit_bytes=VMEM_LIMIT)


def _mod_kernel(c_ref, w_ref, b_ref, o_ref):
    s = jax.nn.silu(c_ref[...]).astype(BF16)
    o_ref[0] = jnp.dot(s, w_ref[0].astype(BF16), preferred_element_type=F32) + b_ref[0]


def _mod_call(c16, ada_w, ada_b):
    depth = ada_w.shape[0]
    nt = 2048
    return pl.pallas_call(
        _mod_kernel,
        grid=(depth, N_MOD * D_MODEL // nt),
        in_specs=[
            pl.BlockSpec((16, D_MODEL), lambda l, j: (0, 0)),
            pl.BlockSpec((1, D_MODEL, nt), lambda l, j: (l, 0, j)),
            pl.BlockSpec((1, 1, nt), lambda l, j: (l, 0, j)),
        ],
        out_specs=pl.BlockSpec((1, 16, nt), lambda l, j: (l, 0, j)),
        out_shape=jax.ShapeDtypeStruct((depth, 16, N_MOD * D_MODEL), F32),
        compiler_params=_params(("arbitrary", "arbitrary")),
        name="adaln_mod",
    )(c16, ada_w, ada_b.reshape(depth, 1, N_MOD * D_MODEL))


def _qkv_kernel(*refs, rope):
    if rope:
        x_ref, mod_ref, g_ref, w_ref, cos_ref, sa_ref, sb_ref, q_ref, k_ref, v_ref, h_sc, y_sc = refs
    else:
        x_ref, mod_ref, g_ref, w_ref, q_ref, k_ref, v_ref, h_sc, y_sc = refs
    half = x_ref.shape[0] // 2
    low = lax.broadcasted_iota(jnp.int32, (half, LANES), 1) < HEAD_DIM
    for r in range(2):
        rs = slice(r * half, (r + 1) * half)
        h_sc[r] = _modulate(_rms(x_ref[rs, :], g_ref[0:1, :]), mod_ref[0, 0], mod_ref[0, 1]).astype(BF16)
    for r in range(2):
        y_sc[r] = jnp.dot(h_sc[r], w_ref[...], preferred_element_type=F32)
    for r in range(2):
        rs = slice(r * half, (r + 1) * half)
        if rope:
            cos, sa, sb = cos_ref[rs, :], sa_ref[rs, :], sb_ref[rs, :]
        for c in range((D_Q + D_KV) // LANES):
            yc = y_sc[r, :, c * LANES:(c + 1) * LANES]
            if rope:
                yc = yc * cos + pltpu.roll(yc, LANES - 16, 1) * sa + pltpu.roll(yc, 16, 1) * sb
            if c < D_Q // LANES:
                q_ref[rs, c * LANES:(c + 1) * LANES] = yc.astype(BF16)
            else:
                c2 = 2 * (c - D_Q // LANES)
                swapped = pltpu.roll(yc, HEAD_DIM, 1)
                k_ref[rs, c2 * LANES:(c2 + 1) * LANES] = jnp.where(low, yc, swapped).astype(BF16)
                k_ref[rs, (c2 + 1) * LANES:(c2 + 2) * LANES] = jnp.where(low, swapped, yc).astype(BF16)
        for blk in range(half // BLOCK):
            v_ref[r * (half // BLOCK) + blk] = y_sc[r, blk * BLOCK:(blk + 1) * BLOCK, D_Q + D_KV:].T.astype(BF16)


def _qkv_call(x2, mod, g, w_qkv, tables, tiles_per_group):
    n = x2.shape[0]
    tm = TOKEN_TILE
    rope = tables is not None
    in_specs = [
        pl.BlockSpec((tm, D_MODEL), lambda i: (i, 0)),
        pl.BlockSpec((1, N_MOD, SUBLANES, D_MODEL), lambda i: (i // tiles_per_group, 0, 0, 0)),
        _const_spec((4, D_MODEL)),
        _const_spec((D_MODEL, D_Q + 2 * D_KV)),
    ]
    args = [x2, mod, g, w_qkv]
    if rope:
        nt = SEQ // tm
        in_specs += [pl.BlockSpec((tm, LANES), lambda i: (i % nt, 0))] * 3
        args += list(tables)
    return pl.pallas_call(
        functools.partial(_qkv_kernel, rope=rope),
        grid=(n // tm,),
        in_specs=in_specs,
        out_specs=[
            pl.BlockSpec((tm, D_Q), lambda i: (i, 0)),
            pl.BlockSpec((tm, D_K2), lambda i: (i, 0)),
            pl.BlockSpec((tm // BLOCK, D_KV, BLOCK), lambda i: (i, 0, 0)),
        ],
        out_shape=[
            jax.ShapeDtypeStruct((n, D_Q), BF16),
            jax.ShapeDtypeStruct((n, D_K2), BF16),
            jax.ShapeDtypeStruct((n // BLOCK, D_KV, BLOCK), BF16),
        ],
        scratch_shapes=[
            pltpu.VMEM((2, tm // 2, D_MODEL), BF16),
            pltpu.VMEM((2, tm // 2, D_Q + 2 * D_KV), F32),
        ],
        compiler_params=_params(("parallel",)),
        name="qkv_rope" if rope else "qkv_ctx",
    )(*args)


VT_ROWS = HEAD_DIM + 16


def _attn_kernel(*refs, local, q_per_step):
    if local:
        sink_ref, q_ref, k_ref, v_ref, kc_ref, vc_ref, o_ref, s_sc, p_sc, bias_sc = refs
    else:
        sink_ref, q_ref, kc_ref, vc_ref, o_ref, s_sc, p_sc, bias_sc = refs
    j = pl.program_id(1)
    seq_blocks = SEQ // BLOCK if local else 0
    ctx_blocks = CTX_LEN // BLOCK
    n_keys = s_sc.shape[1]
    pad_row = lax.broadcasted_iota(jnp.int32, (VT_ROWS - HEAD_DIM, n_keys), 0)
    vt_pad = jnp.where(pad_row == 0, 1.0, 0.0).astype(BF16)

    nt = (((1,), (1,)), ((), ()))
    n_band = 3
    n_chunks = D_Q // LANES
    lane = lax.broadcasted_iota(jnp.int32, (BLOCK, LANES), 1)
    first_head = lax.broadcasted_iota(jnp.int32, (1, 2 * BLOCK), 1) < BLOCK

    def block_params(qb):
        jq = j * q_per_step + qb
        rows = pl.ds(pl.multiple_of(qb * BLOCK, BLOCK), BLOCK)
        if not local:
            return rows, None, None, None, qb // ctx_blocks
        blk0 = jnp.clip(jq - 1, 0, seq_blocks - n_band)
        return rows, blk0, pl.multiple_of(blk0 * BLOCK, BLOCK), qb % 2, 0

    def store_bias(qb):
        if local:
            jq = j * q_per_step + qb
            start = jnp.clip(jq - 1, 0, seq_blocks - n_band) * BLOCK
            kpos = start + lax.broadcasted_iota(jnp.int32, (n_band * BLOCK, BLOCK), 0)
            qpos = jq * BLOCK + lax.broadcasted_iota(jnp.int32, (n_band * BLOCK, BLOCK), 1)
            bias_sc[qb % 2] = jnp.where(jnp.abs(kpos - qpos) <= WINDOW, 0.0, NEG_INF).astype(F32)

    def scores(c, params):
        rows, _, start, slot, cb = params
        kcols = slice((c // 2) * LANES, (c // 2 + 1) * LANES)
        qc = q_ref[rows, c * LANES:(c + 1) * LANES]
        zero = jnp.zeros_like(qc)
        q2 = jnp.concatenate([jnp.where(lane < HEAD_DIM, qc, zero), jnp.where(lane < HEAD_DIM, zero, qc)], axis=0)
        s_sc[c % 2, 0:CTX_LEN] = lax.dot_general(kc_ref[cb, :, kcols], q2, nt, preferred_element_type=F32)
        if local:
            s_band = lax.dot_general(k_ref[0, pl.ds(start, n_band * BLOCK), kcols], q2, nt,
                                     preferred_element_type=F32)
            for hd in range(2):
                s_sc[c % 2, CTX_LEN:, hd * BLOCK:(hd + 1) * BLOCK] = (
                    s_band[:, hd * BLOCK:(hd + 1) * BLOCK] + bias_sc[slot])

    def softmax(c):
        s = s_sc[c % 2]
        sink_row = jnp.where(first_head, sink_ref[2 * c], sink_ref[2 * c + 1])
        m = jnp.maximum(jnp.max(s, axis=0, keepdims=True), sink_row)
        p_sc[c % 2] = jnp.exp2(s - m).astype(BF16)
        return jnp.exp2(sink_row - m)

    def values(c, params, sink_term):
        rows, blk0, _, _, cb = params
        hrows = slice((c // 2) * HEAD_DIM, (c // 2 + 1) * HEAD_DIM)
        vt = [vc_ref[cb * ctx_blocks + i, hrows, :] for i in range(ctx_blocks)]
        if local:
            vt_band = v_ref[pl.ds(blk0, n_band), hrows, :]
            vt += [vt_band[i] for i in range(n_band)]
        vt_aug = jnp.concatenate([jnp.concatenate(vt, axis=1), vt_pad], axis=0)
        acc = jnp.dot(vt_aug, p_sc[c % 2], preferred_element_type=F32)
        out_t = acc[0:HEAD_DIM] * (1.0 / (acc[HEAD_DIM:HEAD_DIM + 1] + sink_term))
        both = jnp.concatenate([out_t[:, :BLOCK], out_t[:, BLOCK:]], axis=0)
        o_ref[rows, c * LANES:(c + 1) * LANES] = both.T.astype(o_ref.dtype)

    def query_block(qb, sink_term0):
        cur = block_params(qb)
        nxt = block_params(jnp.minimum(qb + 1, q_per_step - 1))
        store_bias(qb + 1)
        sink_terms = {0: sink_term0}
        for c in range(n_chunks):
            if c + 2 < n_chunks:
                scores(c + 2, cur)
            else:
                scores(c + 2 - n_chunks, nxt)
            sink_terms[c + 1] = softmax((c + 1) % n_chunks)
            values(c, cur, sink_terms[c])
        return sink_terms[n_chunks]

    first = block_params(0)
    store_bias(0)
    scores(0, first)
    scores(1, first)
    lax.fori_loop(0, q_per_step, query_block, softmax(0))


def _attn_scratch(n_keys):
    return [
        pltpu.VMEM((2, n_keys, 2 * BLOCK), F32),
        pltpu.VMEM((2, n_keys, 2 * BLOCK), BF16),
        pltpu.VMEM((2, 3 * BLOCK, BLOCK), F32),
    ]


def _attn_call(sink2, q, k, vt, kc, vtc):
    nb = SEQ // BLOCK
    nbc = CTX_LEN // BLOCK
    qps = ATTN_Q_PER_STEP
    steps = nb // qps
    return pl.pallas_call(
        functools.partial(_attn_kernel, local=True, q_per_step=qps),
        grid=(BATCH, steps),
        in_specs=[
            pl.BlockSpec(memory_space=pltpu.SMEM),
            pl.BlockSpec((qps * BLOCK, D_Q), lambda b, j: (b * steps + j, 0)),
            pl.BlockSpec((1, SEQ, D_K2), lambda b, j: (b, 0, 0)),
            pl.BlockSpec((nb, D_KV, BLOCK), lambda b, j: (b, 0, 0)),
            pl.BlockSpec((1, CTX_LEN, D_K2), lambda b, j: (b, 0, 0)),
            pl.BlockSpec((nbc, D_KV, BLOCK), lambda b, j: (b, 0, 0)),
        ],
        out_specs=pl.BlockSpec((qps * BLOCK, D_Q), lambda b, j: (b * steps + j, 0)),
        out_shape=jax.ShapeDtypeStruct((BATCH * SEQ, D_Q), BF16),
        scratch_shapes=_attn_scratch(CTX_LEN + 3 * BLOCK),
        compiler_params=_params(("parallel", "arbitrary")),
        name="band_attn",
    )(sink2, q, k, vt, kc, vtc)


def _ctx_attn_call(sink2, qc, kc, vtc):
    nb = BATCH * CTX_LEN // BLOCK
    whole = lambda shape: pl.BlockSpec(shape, lambda b, j: (0,) * len(shape))
    return pl.pallas_call(
        functools.partial(_attn_kernel, local=False, q_per_step=nb),
        grid=(1, 1),
        in_specs=[
            pl.BlockSpec(memory_space=pltpu.SMEM),
            whole((BATCH * CTX_LEN, D_Q)),
            whole((BATCH, CTX_LEN, D_K2)),
            whole((nb, D_KV, BLOCK)),
        ],
        out_specs=whole((BATCH * CTX_LEN, D_Q)),
        out_shape=jax.ShapeDtypeStruct((BATCH * CTX_LEN, D_Q), BF16),
        scratch_shapes=_attn_scratch(CTX_LEN),
        compiler_params=_params(("arbitrary", "arbitrary")),
        name="ctx_attn",
    )(sink2, qc, kc, vtc)


def _to_time_major(src_ref, sc_ref, t0=0, nt=None):
    nt = src_ref.shape[1] if nt is None else nt
    r0 = t0 * SUBLANES
    for b in range(BATCH):
        for s in range(N_SLAB):
            sc_ref[s, pl.ds(r0 + b, nt, stride=SUBLANES), :] = src_ref[b, t0:t0 + nt, s * LANES:(s + 1) * LANES]
    return jnp.concatenate([sc_ref[s, r0:r0 + nt * SUBLANES, :] for s in range(N_SLAB)], axis=1)


def _from_time_major(val, sc_ref, dst_ref, t0, nt):
    r0 = t0 * SUBLANES
    for s in range(N_SLAB):
        sc_ref[s, r0:r0 + nt * SUBLANES, :] = val[:, s * LANES:(s + 1) * LANES]
    for b in range(BATCH):
        for s in range(N_SLAB):
            dst_ref[b, t0:t0 + nt, s * LANES:(s + 1) * LANES] = sc_ref[s, pl.ds(r0 + b, nt, stride=SUBLANES), :]


def _post_kernel(*refs, lru):
    if lru:
        (x_ref, gate_ref, yf_ref, yb_ref, mod_ref, g_ref, wf_ref, w1_ref, w2_ref, o_ref,
         x1_sc, h_sc, acc_sc, tout_sc) = refs
    else:
        x_ref, a_ref, mod_ref, g_ref, wf_ref, w1_ref, w2_ref, o_ref, x1_sc, h_sc, acc_sc = refs
    half_rows = x_ref.shape[0] // 2
    half_t = half_rows // BATCH

    def head(r):
        rs = slice(r * half_rows, (r + 1) * half_rows)
        if lru:
            front = (gate_ref[rs, :].astype(F32)
                     * (yf_ref[rs, :].astype(F32) + yb_ref[rs, :].astype(F32))).astype(BF16)
        else:
            front = a_ref[rs, :]
        y = jnp.dot(front, wf_ref[...], preferred_element_type=F32)
        x1 = _gated_add(x_ref[rs, :], mod_ref[0, 2], _rms(y, g_ref[1:2, :]))
        x1_sc[r] = x1
        h_sc[r] = _modulate(_rms(x1, g_ref[2:3, :]), mod_ref[0, 3], mod_ref[0, 4]).astype(BF16)

    def mlp(r):
        acc = jnp.zeros((half_rows, D_MODEL), F32)
        for c in range(D_FF // FF_CHUNK):
            hid = jnp.dot(h_sc[r], w1_ref[:, c * FF_CHUNK:(c + 1) * FF_CHUNK], preferred_element_type=F32)
            hid = jnp.square(jnp.maximum(hid, 0.0)).astype(BF16)
            acc = acc + jnp.dot(hid, w2_ref[c * FF_CHUNK:(c + 1) * FF_CHUNK, :], preferred_element_type=F32)
        acc_sc[r] = acc

    def tail(r):
        out = _gated_add(x1_sc[r], mod_ref[0, 5], _rms(acc_sc[r], g_ref[3:4, :]))
        if lru:
            _from_time_major(out, tout_sc, o_ref, r * half_t, half_t)
        else:
            o_ref[r * half_rows:(r + 1) * half_rows, :] = out

    head(0)
    head(1)
    mlp(0)
    tail(0)
    mlp(1)
    tail(1)


def _post_call(x, fronts, mod, g, w_front, w1, w2, layer, rows_per_group, lru):
    tm = MLP_TILE if (not lru and x.shape[0] >= 8 * MLP_TILE) else TOKEN_TILE
    tiles_per_group = rows_per_group // tm
    layer_spec = lambda shape: pl.BlockSpec((None,) + shape[1:], lambda i: (layer, 0, 0),
                                            pipeline_mode=pl.Buffered(1))
    row = lambda i: (i, 0)
    n = x.shape[0]
    x_spec = pl.BlockSpec((tm, D_MODEL), row)
    if lru:
        out_spec = pl.BlockSpec((BATCH, tm // BATCH, D_MODEL), lambda i: (0, i, 0))
        out_shape = jax.ShapeDtypeStruct((BATCH, n // BATCH, D_MODEL), F32)
        scratch = [pltpu.VMEM((N_SLAB, tm, LANES), F32)]
    else:
        out_spec, out_shape = x_spec, jax.ShapeDtypeStruct(x.shape, F32)
        scratch = []
    scratch = [
        pltpu.VMEM((2, tm // 2, D_MODEL), F32),
        pltpu.VMEM((2, tm // 2, D_MODEL), BF16),
        pltpu.VMEM((2, tm // 2, D_MODEL), F32),
    ] + scratch
    in_specs = [x_spec]
    in_specs += [pl.BlockSpec((tm, f.shape[1]), row) for f in fronts]
    in_specs += [
        pl.BlockSpec((1, N_MOD, SUBLANES, D_MODEL), lambda i: (i // tiles_per_group, 0, 0, 0)),
        _const_spec((4, D_MODEL)),
        _const_spec(w_front.shape),
        layer_spec(w1.shape),
        layer_spec(w2.shape),
    ]
    return pl.pallas_call(
        functools.partial(_post_kernel, lru=lru),
        grid=(n // tm,),
        in_specs=in_specs,
        out_specs=out_spec,
        out_shape=out_shape,
        scratch_shapes=scratch,
        compiler_params=_params(("parallel",)),
        name="lru_out_mlp" if lru else "attn_out_mlp",
    )(x, *fronts, mod, g, w_front, w1, w2)


def _lru_in_kernel(*refs, need_gate):
    if need_gate:
        (x_ref, xp_ref, xn_ref, mod_ref, g_ref, w_ref, cw_ref, cb_ref, gate_ref, u_ref, xt_ref,
         v_sc, h_sc, t_sc, tp_sc, tn_sc) = refs
    else:
        x_ref, xp_ref, xn_ref, mod_ref, g_ref, w_ref, cw_ref, cb_ref, u_ref, v_sc, h_sc, t_sc, tp_sc, tn_sc = refs
    i = pl.program_id(0)
    n = pl.num_programs(0)
    rows = x_ref.shape[0] * x_ref.shape[1]
    half = rows // 2
    half_t = x_ref.shape[1] // 2
    s8 = SUBLANES

    def pre(x):
        return _modulate(_rms(x, g_ref[0:1, :]), mod_ref[0, 0], mod_ref[0, 1]).astype(BF16)

    h_sc[0, 0:HALO] = pre(_to_time_major(xp_ref, tp_sc)[SUBLANES * SUBLANES - HALO:])
    x_halves = [_to_time_major(x_ref, t_sc, r * half_t, half_t) for r in range(2)]
    if need_gate:
        for r in range(2):
            xt_ref[r * half:(r + 1) * half, :] = x_halves[r]
    h_sc[0, HALO:] = pre(x_halves[0])
    h_sc[1, 0:half] = pre(x_halves[1])
    h_sc[1, half:] = pre(_to_time_major(xn_ref, tn_sc)[:HALO])
    ext = half + HALO
    for r in range(2):
        v_sc[r * ext:(r + 1) * ext] = jnp.dot(h_sc[r], w_ref[:, D_RNN:], preferred_element_type=F32)
    v_sc[0:HALO] = v_sc[0:HALO] * (i > 0).astype(F32)
    v_sc[HALO + rows:HALO + rows + s8] = v_sc[HALO + rows:HALO + rows + s8] * (i < n - 1).astype(F32)
    for r in range(2):
        if need_gate:
            h_r = h_sc[0, HALO:] if r == 0 else h_sc[1, 0:half]
            gate_ref[r * half:(r + 1) * half, :] = jax.nn.gelu(
                jnp.dot(h_r, w_ref[:, :D_RNN], preferred_element_type=F32)).astype(BF16)
        base = HALO + r * half
        u_ref[r * half:(r + 1) * half, :] = (
            cb_ref[...]
            + cw_ref[0:1, :] * v_sc[base - 2 * s8:base - 2 * s8 + half]
            + cw_ref[1:2, :] * v_sc[base - s8:base - s8 + half]
            + cw_ref[2:3, :] * v_sc[base:base + half]
            + cw_ref[3:4, :] * v_sc[base + s8:base + s8 + half]).astype(u_ref.dtype)


def _lru_in_call(x3, mod, g, w_in, conv_w, conv_b, need_gate):
    t_total = x3.shape[1]
    n = BATCH * t_total
    tm = LRU_IN_TILE
    nt = tm // BATCH
    row = lambda i: (i, 0)
    per_tile = nt // SUBLANES
    last = t_total // SUBLANES - 1
    halo_spec = lambda f: pl.BlockSpec((BATCH, SUBLANES, D_MODEL), f)
    out_specs = [pl.BlockSpec((tm, D_RNN), row), pl.BlockSpec((tm, D_RNN), row), pl.BlockSpec((tm, D_MODEL), row)]
    out_shape = [jax.ShapeDtypeStruct((n, D_RNN), BF16), jax.ShapeDtypeStruct((n, D_RNN), U_DTYPE),
                 jax.ShapeDtypeStruct((n, D_MODEL), F32)]
    if not need_gate:
        out_specs, out_shape = out_specs[1:2], out_shape[1:2]
    return pl.pallas_call(
        functools.partial(_lru_in_kernel, need_gate=need_gate),
        grid=(n // tm,),
        in_specs=[
            pl.BlockSpec((BATCH, nt, D_MODEL), lambda i: (0, i, 0)),
            halo_spec(lambda i: (0, jnp.maximum(i * per_tile - 1, 0), 0)),
            halo_spec(lambda i: (0, jnp.minimum((i + 1) * per_tile, last), 0)),
            _const_spec((1, N_MOD, SUBLANES, D_MODEL)),
            _const_spec((4, D_MODEL)),
            _const_spec((D_MODEL, 2 * D_RNN)),
            _const_spec((CONV_W, D_RNN)),
            _const_spec((1, D_RNN)),
        ],
        out_specs=out_specs,
        out_shape=out_shape,
        scratch_shapes=[
            pltpu.VMEM((tm + 2 * HALO, D_RNN), F32),
            pltpu.VMEM((2, tm // 2 + HALO, D_MODEL), BF16),
            pltpu.VMEM((N_SLAB, tm, LANES), F32),
            pltpu.VMEM((N_SLAB, SUBLANES * SUBLANES, LANES), F32),
            pltpu.VMEM((N_SLAB, SUBLANES * SUBLANES, LANES), F32),
        ],
        compiler_params=_params(("parallel",)),
        name="lru_in",
    )(x3, x3, x3, mod, g, w_in, conv_w, conv_b)


def _scan_kernel(uf_ref, ub_ref, h0_ref, wa_ref, ba_ref, wi_ref, bi_ref, lam_ref, yf_ref, yb_ref, ht_ref,
                 a_sc, bx_sc, h_sc):
    i = pl.program_id(0)
    n = pl.num_programs(0)
    rows = uf_ref.shape[0]
    nt = rows // SUBLANES
    s8 = SUBLANES

    @pl.when(i == 0)
    def _():
        h_sc[...] = h0_ref[...]

    for d, u_ref in enumerate((uf_ref, ub_ref)):
        for c in range(N_LRU_BLOCKS):
            cs = slice(c * LRU_BLOCK_W, (c + 1) * LRU_BLOCK_W)
            u16 = u_ref[:, cs].astype(BF16)
            u = u_ref[:, cs].astype(F32)
            ta = jnp.tanh(jnp.dot(u16, wa_ref[d, c], preferred_element_type=F32) + 0.5 * ba_ref[d, :, cs])
            ti = jnp.tanh(jnp.dot(u16, wi_ref[d, c], preferred_element_type=F32) + 0.5 * bi_ref[d, :, cs])
            neg_lam = -lam_ref[d, :, cs]
            softplus = jnp.maximum(neg_lam, 0.0) + jnp.log1p(jnp.exp(-jnp.abs(neg_lam)))
            k = (-0.5 * LRU_C * LOG2E) * softplus
            a = jnp.exp2(k * ta + k)
            w = 1.0 - a * a
            root = w * lax.rsqrt(jnp.maximum(w, 1e-30))
            a_sc[d, :, cs] = a
            bx_sc[d, :, cs] = root * (ti * u + u)

    def step(t, carry):
        hf, hb = carry
        rf = pl.multiple_of(t * 2 * s8, 2 * s8)
        rb = pl.multiple_of((nt - 2 - 2 * t) * s8, 2 * s8)
        hf1 = a_sc[0, pl.ds(rf, s8), :] * hf + bx_sc[0, pl.ds(rf, s8), :]
        hf2 = a_sc[0, pl.ds(rf + s8, s8), :] * hf1 + bx_sc[0, pl.ds(rf + s8, s8), :]
        yf_ref[pl.ds(rf, 2 * s8), :] = jnp.concatenate([hf1, hf2], axis=0).astype(yf_ref.dtype)
        hb1 = a_sc[1, pl.ds(rb + s8, s8), :] * hb + bx_sc[1, pl.ds(rb + s8, s8), :]
        hb2 = a_sc[1, pl.ds(rb, s8), :] * hb1 + bx_sc[1, pl.ds(rb, s8), :]
        yb_ref[pl.ds(rb, 2 * s8), :] = jnp.concatenate([hb2, hb1], axis=0).astype(yb_ref.dtype)
        return hf2, hb2

    hf, hb = lax.fori_loop(0, nt // 2, step, (h_sc[0], h_sc[1]), unroll=2)
    h_sc[0] = hf
    h_sc[1] = hb

    @pl.when(i == n - 1)
    def _():
        ht_ref[...] = h_sc[...]


def _scan_call(u2, h0, w_a, b_a, w_i, b_i, lam):
    rows_total = u2.shape[0]
    rows = SCAN_T * SUBLANES
    n = rows_total // rows
    w = D_RNN
    fwd = lambda i: (i, 0)
    bwd = lambda i: (n - 1 - i, 0)
    return pl.pallas_call(
        _scan_kernel,
        grid=(n,),
        in_specs=[
            pl.BlockSpec((rows, w), fwd),
            pl.BlockSpec((rows, w), bwd),
            _const_spec((2, SUBLANES, w)),
            _const_spec((2, N_LRU_BLOCKS, LRU_BLOCK_W, LRU_BLOCK_W)),
            _const_spec((2, 1, w)),
            _const_spec((2, N_LRU_BLOCKS, LRU_BLOCK_W, LRU_BLOCK_W)),
            _const_spec((2, 1, w)),
            _const_spec((2, 1, w)),
        ],
        out_specs=[
            pl.BlockSpec((rows, w), fwd),
            pl.BlockSpec((rows, w), bwd),
            pl.BlockSpec((2, SUBLANES, w), lambda i: (0, 0, 0)),
        ],
        out_shape=[
            jax.ShapeDtypeStruct((rows_total, w), Y_DTYPE),
            jax.ShapeDtypeStruct((rows_total, w), Y_DTYPE),
            jax.ShapeDtypeStruct((2, SUBLANES, w), F32),
        ],
        scratch_shapes=[
            pltpu.VMEM((2, rows, w), F32),
            pltpu.VMEM((2, rows, w), F32),
            pltpu.VMEM((2, SUBLANES, w), F32),
        ],
        compiler_params=_params(("arbitrary",)),
        name="lru_scan",
    )(u2, u2, h0, w_a, b_a, w_i, b_i, lam)


def _rope_tables():
    t = np.arange(SEQ)
    row = (t // GRID_W).astype(np.float64)
    col = (t % GRID_W).astype(np.float64)
    half = HEAD_DIM // 2
    inv = ROPE_BASE ** (-np.arange(0, half, 2, dtype=np.float64) / half)
    ang_r = row[:, None] * inv[None, :]
    ang_c = col[:, None] * inv[None, :]
    ang = np.concatenate([ang_r, ang_r, ang_c, ang_c], axis=-1)
    ang = np.tile(ang, (1, LANES // HEAD_DIM))
    low = (np.arange(LANES) % 32) < 16
    sin = np.sin(ang)
    tables = (np.cos(ang), np.where(low, -sin, 0.0), np.where(low, 0.0, sin))
    return tuple(jnp.asarray(a, dtype=F32) for a in tables)


def kernel(x, c, ctx, c_ctx, ada_w, ada_b, norm_g, mlp_w1, mlp_w2, attn_w_qkv, attn_w_o, attn_sink,
           lru_w_in, lru_conv_w, lru_conv_b, lru_w_a, lru_b_a, lru_w_i, lru_b_i, lru_lam, lru_w_out):
    n_lat = BATCH * SEQ
    n_ctx = BATCH * CTX_LEN

    c16 = jnp.zeros((16, D_MODEL), F32).at[:BATCH].set(c).at[BATCH].set(c_ctx)
    mods = _mod_call(c16, ada_w, ada_b).reshape(2, 16, N_MOD, D_MODEL)

    def slab_bmajor(m):
        return jnp.broadcast_to(m[:, :, None, :], (BATCH, N_MOD, SUBLANES, D_MODEL))

    def slab_ctx(m):
        return jnp.broadcast_to(m[None, :, None, :], (1, N_MOD, SUBLANES, D_MODEL))

    mod_x0 = slab_bmajor(mods[0, :BATCH])
    mod_c0 = slab_ctx(mods[0, BATCH])
    w_qkv = attn_w_qkv[0]
    w_qkv = jnp.concatenate([w_qkv[:, :D_Q] * (HEAD_DIM ** -0.5 * LOG2E), w_qkv[:, D_Q:]], axis=1).astype(BF16)
    sink2 = attn_sink[0] * LOG2E
    w_o = attn_w_o[0].astype(BF16)
    w1_all, w2_all = mlp_w1.astype(BF16), mlp_w2.astype(BF16)
    g0 = norm_g[0]
    tiles_per_batch = SEQ // TOKEN_TILE

    x2 = x.reshape(n_lat, D_MODEL)
    c2 = ctx.reshape(n_ctx, D_MODEL)
    q, k, v = _qkv_call(x2, mod_x0, g0, w_qkv, _rope_tables(), tiles_per_batch)
    qc, kc, vc = _qkv_call(c2, mod_c0, g0, w_qkv, None, n_ctx // TOKEN_TILE)
    kc3 = kc.reshape(BATCH, CTX_LEN, D_K2)
    att = _attn_call(sink2, q, k.reshape(BATCH, SEQ, D_K2), v, kc3, vc)
    att_c = _ctx_attn_call(sink2, qc, kc3, vc)
    x2 = _post_call(x2, [att], mod_x0, g0, w_o, w1_all, w2_all, 0, SEQ, lru=False)
    c2 = _post_call(c2, [att_c], mod_c0, g0, w_o, w1_all, w2_all, 0, n_ctx, lru=False)

    x3 = x2.reshape(BATCH, SEQ, D_MODEL)
    c3 = c2.reshape(BATCH, CTX_LEN, D_MODEL)
    mod_x1 = mods[1, :BATCH].transpose(1, 0, 2)[None]
    mod_c1 = slab_ctx(mods[1, BATCH])
    g1 = norm_g[1]
    w_in = lru_w_in[0].astype(BF16)
    conv_w = 0.5 * lru_conv_w[0]
    conv_b = 0.5 * lru_conv_b[0].reshape(1, D_RNN)
    w_a, w_i = lru_w_a[0].astype(BF16), lru_w_i[0].astype(BF16)
    b_a, b_i = lru_b_a[0].reshape(2, 1, D_RNN), lru_b_i[0].reshape(2, 1, D_RNN)
    lam = lru_lam[0].reshape(2, 1, D_RNN)
    scan = functools.partial(_scan_call, w_a=w_a, b_a=b_a, w_i=w_i, b_i=b_i, lam=lam)

    (u_c,) = _lru_in_call(c3, mod_c1, g1, w_in, conv_w, conv_b, need_gate=False)
    _, _, h_ctx = scan(u_c, jnp.zeros((2, SUBLANES, D_RNN), F32))
    gate_x, u_x, x_t = _lru_in_call(x3, mod_x1, g1, w_in, conv_w, conv_b, need_gate=True)
    yf, yb, _ = scan(u_x, h_ctx)
    return _post_call(x_t, [gate_x, yf, yb], mod_x1, g1, lru_w_out[0].astype(BF16),
                      w1_all, w2_all, 1, n_lat, lru=True)
```

```python
import functools

import jax
import jax.numpy as jnp
import numpy as np
from jax import lax
from jax.experimental import pallas as pl
from jax.experimental.pallas import tpu as pltpu

D_MODEL = 1024
BATCH = 8
SEQ = 2048
GRID_W = 64
CTX_LEN = 256
HEAD_DIM = 64
N_HEADS = 16
N_KV_HEADS = 4
GQA_GROUP = N_HEADS // N_KV_HEADS
WINDOW = 128
BLOCK = 128
ROPE_BASE = 10000.0
D_RNN = 1280
LRU_BLOCK_W = 256
N_LRU_BLOCKS = D_RNN // LRU_BLOCK_W
CONV_W = 4
LRU_C = 8.0
D_FF = 4 * D_MODEL
N_MOD = 6
EPS = 1e-6
NEG_INF = -1e30

D_Q = N_HEADS * HEAD_DIM
D_KV = N_KV_HEADS * HEAD_DIM
D_K2 = 2 * D_KV
LANES = 128
SUBLANES = 8
N_SLAB = D_MODEL // LANES
TOKEN_TILE = 512
MLP_TILE = 1024
LRU_IN_TILE = 512
FF_CHUNK = 1024
ATTN_Q_PER_STEP = 16
SCAN_T = 128
HALO = 16
U_DTYPE = jnp.float32
Y_DTYPE = jnp.bfloat16
LOG2E = 1.4426950408889634
VMEM_LIMIT = 60 * 1024 * 1024

F32 = jnp.float32
BF16 = jnp.bfloat16


def _rms(x, g):
    ms = jnp.mean(x * x, axis=-1, keepdims=True)
    return x * lax.rsqrt(ms + EPS) * g


def _slab(x):
    return x.reshape(x.shape[0] // SUBLANES, SUBLANES, x.shape[1])


def _modulate(h, shift8, scale8):
    out = _slab(h) * (1.0 + scale8)[None] + shift8[None]
    return out.reshape(h.shape)


def _gated_add(x, gate8, y):
    out = _slab(x) + gate8[None] * _slab(y)
    return out.reshape(x.shape)


def _const_spec(shape):
    n = len(shape)
    return pl.BlockSpec(shape, lambda *_: (0,) * n, pipeline_mode=pl.Buffered(1))


def _params(sem):
    return pltpu.CompilerParams(dimension_semantics=sem, vmem_limit_bytes=VMEM_LIMIT)


def _mod_kernel(c_ref, w_ref, b_ref, o_ref):
    s = jax.nn.silu(c_ref[...]).astype(BF16)
    o_ref[0] = jnp.dot(s, w_ref[0].astype(BF16), preferred_element_type=F32) + b_ref[0]


def _mod_call(c16, ada_w, ada_b):
    depth = ada_w.shape[0]
    nt = 2048
    return pl.pallas_call(
        _mod_kernel,
        grid=(depth, N_MOD * D_MODEL // nt),
        in_specs=[
            pl.BlockSpec((16, D_MODEL), lambda l, j: (0, 0)),
            pl.BlockSpec((1, D_MODEL, nt), lambda l, j: (l, 0, j)),
            pl.BlockSpec((1, 1, nt), lambda l, j: (l, 0, j)),
        ],
        out_specs=pl.BlockSpec((1, 16, nt), lambda l, j: (l, 0, j)),
        out_shape=jax.ShapeDtypeStruct((depth, 16, N_MOD * D_MODEL), F32),
        compiler_params=_params(("arbitrary", "arbitrary")),
        name="adaln_mod",
    )(c16, ada_w, ada_b.reshape(depth, 1, N_MOD * D_MODEL))


def _qkv_kernel(*refs, rope):
    if rope:
        x_ref, mod_ref, g_ref, w_ref, cos_ref, sa_ref, sb_ref, q_ref, k_ref, v_ref, h_sc, y_sc = refs
    else:
        x_ref, mod_ref, g_ref, w_ref, q_ref, k_ref, v_ref, h_sc, y_sc = refs
    half = x_ref.shape[0] // 2
    low = lax.broadcasted_iota(jnp.int32, (half, LANES), 1) < HEAD_DIM
    for r in range(2):
        rs = slice(r * half, (r + 1) * half)
        h_sc[r] = _modulate(_rms(x_ref[rs, :], g_ref[0:1, :]), mod_ref[0, 0], mod_ref[0, 1]).astype(BF16)
    for r in range(2):
        y_sc[r] = jnp.dot(h_sc[r], w_ref[...], preferred_element_type=F32)
    for r in range(2):
        rs = slice(r * half, (r + 1) * half)
        if rope:
            cos, sa, sb = cos_ref[rs, :], sa_ref[rs, :], sb_ref[rs, :]
        for c in range((D_Q + D_KV) // LANES):
            yc = y_sc[r, :, c * LANES:(c + 1) * LANES]
            if rope:
                yc = yc * cos + pltpu.roll(yc, LANES - 16, 1) * sa + pltpu.roll(yc, 16, 1) * sb
            if c < D_Q // LANES:
                q_ref[rs, c * LANES:(c + 1) * LANES] = yc.astype(BF16)
            else:
                c2 = 2 * (c - D_Q // LANES)
                swapped = pltpu.roll(yc, HEAD_DIM, 1)
                k_ref[rs, c2 * LANES:(c2 + 1) * LANES] = jnp.where(low, yc, swapped).astype(BF16)
                k_ref[rs, (c2 + 1) * LANES:(c2 + 2) * LANES] = jnp.where(low, swapped, yc).astype(BF16)
        for blk in range(half // BLOCK):
            v_ref[r * (half // BLOCK) + blk] = y_sc[r, blk * BLOCK:(blk + 1) * BLOCK, D_Q + D_KV:].T.astype(BF16)


def _qkv_call(x2, mod, g, w_qkv, tables, tiles_per_group):
    n = x2.shape[0]
    tm = TOKEN_TILE
    rope = tables is not None
    in_specs = [
        pl.BlockSpec((tm, D_MODEL), lambda i: (i, 0)),
        pl.BlockSpec((1, N_MOD, SUBLANES, D_MODEL), lambda i: (i // tiles_per_group, 0, 0, 0)),
        _const_spec((4, D_MODEL)),
        _const_spec((D_MODEL, D_Q + 2 * D_KV)),
    ]
    args = [x2, mod, g, w_qkv]
    if rope:
        nt = SEQ // tm
        in_specs += [pl.BlockSpec((tm, LANES), lambda i: (i % nt, 0))] * 3
        args += list(tables)
    return pl.pallas_call(
        functools.partial(_qkv_kernel, rope=rope),
        grid=(n // tm,),
        in_specs=in_specs,
        out_specs=[
            pl.BlockSpec((tm, D_Q), lambda i: (i, 0)),
            pl.BlockSpec((tm, D_K2), lambda i: (i, 0)),
            pl.BlockSpec((tm // BLOCK, D_KV, BLOCK), lambda i: (i, 0, 0)),
        ],
        out_shape=[
            jax.ShapeDtypeStruct((n, D_Q), BF16),
            jax.ShapeDtypeStruct((n, D_K2), BF16),
            jax.ShapeDtypeStruct((n // BLOCK, D_KV, BLOCK), BF16),
        ],
        scratch_shapes=[
            pltpu.VMEM((2, tm // 2, D_MODEL), BF16),
            pltpu.VMEM((2, tm // 2, D_Q + 2 * D_KV), F32),
        ],
        compiler_params=_params(("parallel",)),
        name="qkv_rope" if rope else "qkv_ctx",
    )(*args)


VT_ROWS = HEAD_DIM + 16


def _attn_kernel(*refs, local, q_per_step):
    if local:
        sink_ref, q_ref, k_ref, v_ref, kc_ref, vc_ref, o_ref, s_sc, p_sc, bias_sc = refs
    else:
        sink_ref, q_ref, kc_ref, vc_ref, o_ref, s_sc, p_sc, bias_sc = refs
    j = pl.program_id(1)
    seq_blocks = SEQ // BLOCK if local else 0
    ctx_blocks = CTX_LEN // BLOCK
    n_keys = s_sc.shape[1]
    pad_row = lax.broadcasted_iota(jnp.int32, (VT_ROWS - HEAD_DIM, n_keys), 0)
    vt_pad = jnp.where(pad_row == 0, 1.0, 0.0).astype(BF16)

    nt = (((1,), (1,)), ((), ()))
    n_band = 3
    n_chunks = D_Q // LANES
    lane = lax.broadcasted_iota(jnp.int32, (BLOCK, LANES), 1)
    first_head = lax.broadcasted_iota(jnp.int32, (1, 2 * BLOCK), 1) < BLOCK

    def block_params(qb):
        jq = j * q_per_step + qb
        rows = pl.ds(pl.multiple_of(qb * BLOCK, BLOCK), BLOCK)
        if not local:
            return rows, None, None, None, qb // ctx_blocks
        blk0 = jnp.clip(jq - 1, 0, seq_blocks - n_band)
        return rows, blk0, pl.multiple_of(blk0 * BLOCK, BLOCK), qb % 2, 0

    def store_bias(qb):
        if local:
            jq = j * q_per_step + qb
            start = jnp.clip(jq - 1, 0, seq_blocks - n_band) * BLOCK
            kpos = start + lax.broadcasted_iota(jnp.int32, (n_band * BLOCK, BLOCK), 0)
            qpos = jq * BLOCK + lax.broadcasted_iota(jnp.int32, (n_band * BLOCK, BLOCK), 1)
            bias_sc[qb % 2] = jnp.where(jnp.abs(kpos - qpos) <= WINDOW, 0.0, NEG_INF).astype(F32)

    def scores(c, params):
        rows, _, start, slot, cb = params
        kcols = slice((c // 2) * LANES, (c // 2 + 1) * LANES)
        qc = q_ref[rows, c * LANES:(c + 1) * LANES]
        zero = jnp.zeros_like(qc)
        q2 = jnp.concatenate([jnp.where(lane < HEAD_DIM, qc, zero), jnp.where(lane < HEAD_DIM, zero, qc)], axis=0)
        s_sc[c % 2, 0:CTX_LEN] = lax.dot_general(kc_ref[cb, :, kcols], q2, nt, preferred_element_type=F32)
        if local:
            s_band = lax.dot_general(k_ref[0, pl.ds(start, n_band * BLOCK), kcols], q2, nt,
                                     preferred_element_type=F32)
            for hd in range(2):
                s_sc[c % 2, CTX_LEN:, hd * BLOCK:(hd + 1) * BLOCK] = (
                    s_band[:, hd * BLOCK:(hd + 1) * BLOCK] + bias_sc[slot])

    def softmax(c):
        s = s_sc[c % 2]
        sink_row = jnp.where(first_head, sink_ref[2 * c], sink_ref[2 * c + 1])
        m = jnp.maximum(jnp.max(s, axis=0, keepdims=True), sink_row)
        p_sc[c % 2] = jnp.exp2(s - m).astype(BF16)
        return jnp.exp2(sink_row - m)

    def values(c, params, sink_term):
        rows, blk0, _, _, cb = params
        hrows = slice((c // 2) * HEAD_DIM, (c // 2 + 1) * HEAD_DIM)
        vt = [vc_ref[cb * ctx_blocks + i, hrows, :] for i in range(ctx_blocks)]
        if local:
            vt_band = v_ref[pl.ds(blk0, n_band), hrows, :]
            vt += [vt_band[i] for i in range(n_band)]
        vt_aug = jnp.concatenate([jnp.concatenate(vt, axis=1), vt_pad], axis=0)
        acc = jnp.dot(vt_aug, p_sc[c % 2], preferred_element_type=F32)
        out_t = acc[0:HEAD_DIM] * (1.0 / (acc[HEAD_DIM:HEAD_DIM + 1] + sink_term))
        both = jnp.concatenate([out_t[:, :BLOCK], out_t[:, BLOCK:]], axis=0)
        o_ref[rows, c * LANES:(c + 1) * LANES] = both.T.astype(o_ref.dtype)

    def query_block(qb, sink_term0):
        cur = block_params(qb)
        nxt = block_params(jnp.minimum(qb + 1, q_per_step - 1))
        store_bias(qb + 1)
        sink_terms = {0: sink_term0}
        for c in range(n_chunks):
            if c + 2 < n_chunks:
                scores(c + 2, cur)
            else:
                scores(c + 2 - n_chunks, nxt)
            sink_terms[c + 1] = softmax((c + 1) % n_chunks)
            values(c, cur, sink_terms[c])
        return sink_terms[n_chunks]

    first = block_params(0)
    store_bias(0)
    scores(0, first)
    scores(1, first)
    lax.fori_loop(0, q_per_step, query_block, softmax(0))


def _attn_scratch(n_keys):
    return [
        pltpu.VMEM((2, n_keys, 2 * BLOCK), F32),
        pltpu.VMEM((2, n_keys, 2 * BLOCK), BF16),
        pltpu.VMEM((2, 3 * BLOCK, BLOCK), F32),
    ]


def _attn_call(sink2, q, k, vt, kc, vtc):
    nb = SEQ // BLOCK
    nbc = CTX_LEN // BLOCK
    qps = ATTN_Q_PER_STEP
    steps = nb // qps
    return pl.pallas_call(
        functools.partial(_attn_kernel, local=True, q_per_step=qps),
        grid=(BATCH, steps),
        in_specs=[
            pl.BlockSpec(memory_space=pltpu.SMEM),
            pl.BlockSpec((qps * BLOCK, D_Q), lambda b, j: (b * steps + j, 0)),
            pl.BlockSpec((1, SEQ, D_K2), lambda b, j: (b, 0, 0)),
            pl.BlockSpec((nb, D_KV, BLOCK), lambda b, j: (b, 0, 0)),
            pl.BlockSpec((1, CTX_LEN, D_K2), lambda b, j: (b, 0, 0)),
            pl.BlockSpec((nbc, D_KV, BLOCK), lambda b, j: (b, 0, 0)),
        ],
        out_specs=pl.BlockSpec((qps * BLOCK, D_Q), lambda b, j: (b * steps + j, 0)),
        out_shape=jax.ShapeDtypeStruct((BATCH * SEQ, D_Q), BF16),
        scratch_shapes=_attn_scratch(CTX_LEN + 3 * BLOCK),
        compiler_params=_params(("parallel", "arbitrary")),
        name="band_attn",
    )(sink2, q, k, vt, kc, vtc)


def _ctx_attn_call(sink2, qc, kc, vtc):
    nb = BATCH * CTX_LEN // BLOCK
    whole = lambda shape: pl.BlockSpec(shape, lambda b, j: (0,) * len(shape))
    return pl.pallas_call(
        functools.partial(_attn_kernel, local=False, q_per_step=nb),
        grid=(1, 1),
        in_specs=[
            pl.BlockSpec(memory_space=pltpu.SMEM),
            whole((BATCH * CTX_LEN, D_Q)),
            whole((BATCH, CTX_LEN, D_K2)),
            whole((nb, D_KV, BLOCK)),
        ],
        out_specs=whole((BATCH * CTX_LEN, D_Q)),
        out_shape=jax.ShapeDtypeStruct((BATCH * CTX_LEN, D_Q), BF16),
        scratch_shapes=_attn_scratch(CTX_LEN),
        compiler_params=_params(("arbitrary", "arbitrary")),
        name="ctx_attn",
    )(sink2, qc, kc, vtc)


def _to_time_major(src_ref, sc_ref, t0=0, nt=None):
    nt = src_ref.shape[1] if nt is None else nt
    r0 = t0 * SUBLANES
    for b in range(BATCH):
        for s in range(N_SLAB):
            sc_ref[s, pl.ds(r0 + b, nt, stride=SUBLANES), :] = src_ref[b, t0:t0 + nt, s * LANES:(s + 1) * LANES]
    return jnp.concatenate([sc_ref[s, r0:r0 + nt * SUBLANES, :] for s in range(N_SLAB)], axis=1)


def _from_time_major(val, sc_ref, dst_ref, t0, nt):
    r0 = t0 * SUBLANES
    for s in range(N_SLAB):
        sc_ref[s, r0:r0 + nt * SUBLANES, :] = val[:, s * LANES:(s + 1) * LANES]
    for b in range(BATCH):
        for s in range(N_SLAB):
            dst_ref[b, t0:t0 + nt, s * LANES:(s + 1) * LANES] = sc_ref[s, pl.ds(r0 + b, nt, stride=SUBLANES), :]


def _post_kernel(*refs, lru):
    if lru:
        (x_ref, gate_ref, yf_ref, yb_ref, mod_ref, g_ref, wf_ref, w1_ref, w2_ref, o_ref,
         x1_sc, h_sc, acc_sc, tout_sc) = refs
    else:
        x_ref, a_ref, mod_ref, g_ref, wf_ref, w1_ref, w2_ref, o_ref, x1_sc, h_sc, acc_sc = refs
    half_rows = x_ref.shape[0] // 2
    half_t = half_rows // BATCH

    def head(r):
        rs = slice(r * half_rows, (r + 1) * half_rows)
        if lru:
            front = (gate_ref[rs, :].astype(F32)
                     * (yf_ref[rs, :].astype(F32) + yb_ref[rs, :].astype(F32))).astype(BF16)
        else:
            front = a_ref[rs, :]
        y = jnp.dot(front, wf_ref[...], preferred_element_type=F32)
        x1 = _gated_add(x_ref[rs, :], mod_ref[0, 2], _rms(y, g_ref[1:2, :]))
        x1_sc[r] = x1
        h_sc[r] = _modulate(_rms(x1, g_ref[2:3, :]), mod_ref[0, 3], mod_ref[0, 4]).astype(BF16)

    def mlp(r):
        acc = jnp.zeros((half_rows, D_MODEL), F32)
        for c in range(D_FF // FF_CHUNK):
            hid = jnp.dot(h_sc[r], w1_ref[:, c * FF_CHUNK:(c + 1) * FF_CHUNK], preferred_element_type=F32)
            hid = jnp.square(jnp.maximum(hid, 0.0)).astype(BF16)
            acc = acc + jnp.dot(hid, w2_ref[c * FF_CHUNK:(c + 1) * FF_CHUNK, :], preferred_element_type=F32)
        acc_sc[r] = acc

    def tail(r):
        out = _gated_add(x1_sc[r], mod_ref[0, 5], _rms(acc_sc[r], g_ref[3:4, :]))
        if lru:
            _from_time_major(out, tout_sc, o_ref, r * half_t, half_t)
        else:
            o_ref[r * half_rows:(r + 1) * half_rows, :] = out

    head(0)
    head(1)
    mlp(0)
    tail(0)
    mlp(1)
    tail(1)


def _post_call(x, fronts, mod, g, w_front, w1, w2, layer, rows_per_group, lru):
    tm = MLP_TILE if (not lru and x.shape[0] >= 8 * MLP_TILE) else TOKEN_TILE
    tiles_per_group = rows_per_group // tm
    layer_spec = lambda shape: pl.BlockSpec((None,) + shape[1:], lambda i: (layer, 0, 0),
                                            pipeline_mode=pl.Buffered(1))
    row = lambda i: (i, 0)
    n = x.shape[0]
    x_spec = pl.BlockSpec((tm, D_MODEL), row)
    if lru:
        out_spec = pl.BlockSpec((BATCH, tm // BATCH, D_MODEL), lambda i: (0, i, 0))
        out_shape = jax.ShapeDtypeStruct((BATCH, n // BATCH, D_MODEL), F32)
        scratch = [pltpu.VMEM((N_SLAB, tm, LANES), F32)]
    else:
        out_spec, out_shape = x_spec, jax.ShapeDtypeStruct(x.shape, F32)
        scratch = []
    scratch = [
        pltpu.VMEM((2, tm // 2, D_MODEL), F32),
        pltpu.VMEM((2, tm // 2, D_MODEL), BF16),
        pltpu.VMEM((2, tm // 2, D_MODEL), F32),
    ] + scratch
    in_specs = [x_spec]
    in_specs += [pl.BlockSpec((tm, f.shape[1]), row) for f in fronts]
    in_specs += [
        pl.BlockSpec((1, N_MOD, SUBLANES, D_MODEL), lambda i: (i // tiles_per_group, 0, 0, 0)),
        _const_spec((4, D_MODEL)),
        _const_spec(w_front.shape),
        layer_spec(w1.shape),
        layer_spec(w2.shape),
    ]
    return pl.pallas_call(
        functools.partial(_post_kernel, lru=lru),
        grid=(n // tm,),
        in_specs=in_specs,
        out_specs=out_spec,
        out_shape=out_shape,
        scratch_shapes=scratch,
        compiler_params=_params(("parallel",)),
        name="lru_out_mlp" if lru else "attn_out_mlp",
    )(x, *fronts, mod, g, w_front, w1, w2)


def _lru_in_kernel(*refs, need_gate):
    if need_gate:
        (x_ref, xp_ref, xn_ref, mod_ref, g_ref, w_ref, cw_ref, cb_ref, gate_ref, u_ref, xt_ref,
         v_sc, h_sc, t_sc, tp_sc, tn_sc) = refs
    else:
        x_ref, xp_ref, xn_ref, mod_ref, g_ref, w_ref, cw_ref, cb_ref, u_ref, v_sc, h_sc, t_sc, tp_sc, tn_sc = refs
    i = pl.program_id(0)
    n = pl.num_programs(0)
    rows = x_ref.shape[0] * x_ref.shape[1]
    half = rows // 2
    half_t = x_ref.shape[1] // 2
    s8 = SUBLANES

    def pre(x):
        return _modulate(_rms(x, g_ref[0:1, :]), mod_ref[0, 0], mod_ref[0, 1]).astype(BF16)

    h_sc[0, 0:HALO] = pre(_to_time_major(xp_ref, tp_sc)[SUBLANES * SUBLANES - HALO:])
    x_halves = [_to_time_major(x_ref, t_sc, r * half_t, half_t) for r in range(2)]
    if need_gate:
        for r in range(2):
            xt_ref[r * half:(r + 1) * half, :] = x_halves[r]
    h_sc[0, HALO:] = pre(x_halves[0])
    h_sc[1, 0:half] = pre(x_halves[1])
    h_sc[1, half:] = pre(_to_time_major(xn_ref, tn_sc)[:HALO])
    ext = half + HALO
    for r in range(2):
        v_sc[r * ext:(r + 1) * ext] = jnp.dot(h_sc[r], w_ref[:, D_RNN:], preferred_element_type=F32)
    v_sc[0:HALO] = v_sc[0:HALO] * (i > 0).astype(F32)
    v_sc[HALO + rows:HALO + rows + s8] = v_sc[HALO + rows:HALO + rows + s8] * (i < n - 1).astype(F32)
    for r in range(2):
        if need_gate:
            h_r = h_sc[0, HALO:] if r == 0 else h_sc[1, 0:half]
            gate_ref[r * half:(r + 1) * half, :] = jax.nn.gelu(
                jnp.dot(h_r, w_ref[:, :D_RNN], preferred_element_type=F32)).astype(BF16)
        base = HALO + r * half
        u_ref[r * half:(r + 1) * half, :] = (
            cb_ref[...]
            + cw_ref[0:1, :] * v_sc[base - 2 * s8:base - 2 * s8 + half]
            + cw_ref[1:2, :] * v_sc[base - s8:base - s8 + half]
            + cw_ref[2:3, :] * v_sc[base:base + half]
            + cw_ref[3:4, :] * v_sc[base + s8:base + s8 + half]).astype(u_ref.dtype)


def _lru_in_call(x3, mod, g, w_in, conv_w, conv_b, need_gate):
    t_total = x3.shape[1]
    n = BATCH * t_total
    tm = LRU_IN_TILE
    nt = tm // BATCH
    row = lambda i: (i, 0)
    per_tile = nt // SUBLANES
    last = t_total // SUBLANES - 1
    halo_spec = lambda f: pl.BlockSpec((BATCH, SUBLANES, D_MODEL), f)
    out_specs = [pl.BlockSpec((tm, D_RNN), row), pl.BlockSpec((tm, D_RNN), row), pl.BlockSpec((tm, D_MODEL), row)]
    out_shape = [jax.ShapeDtypeStruct((n, D_RNN), BF16), jax.ShapeDtypeStruct((n, D_RNN), U_DTYPE),
                 jax.ShapeDtypeStruct((n, D_MODEL), F32)]
    if not need_gate:
        out_specs, out_shape = out_specs[1:2], out_shape[1:2]
    return pl.pallas_call(
        functools.partial(_lru_in_kernel, need_gate=need_gate),
        grid=(n // tm,),
        in_specs=[
            pl.BlockSpec((BATCH, nt, D_MODEL), lambda i: (0, i, 0)),
            halo_spec(lambda i: (0, jnp.maximum(i * per_tile - 1, 0), 0)),
            halo_spec(lambda i: (0, jnp.minimum((i + 1) * per_tile, last), 0)),
            _const_spec((1, N_MOD, SUBLANES, D_MODEL)),
            _const_spec((4, D_MODEL)),
            _const_spec((D_MODEL, 2 * D_RNN)),
            _const_spec((CONV_W, D_RNN)),
            _const_spec((1, D_RNN)),
        ],
        out_specs=out_specs,
        out_shape=out_shape,
        scratch_shapes=[
            pltpu.VMEM((tm + 2 * HALO, D_RNN), F32),
            pltpu.VMEM((2, tm // 2 + HALO, D_MODEL), BF16),
            pltpu.VMEM((N_SLAB, tm, LANES), F32),
            pltpu.VMEM((N_SLAB, SUBLANES * SUBLANES, LANES), F32),
            pltpu.VMEM((N_SLAB, SUBLANES * SUBLANES, LANES), F32),
        ],
        compiler_params=_params(("parallel",)),
        name="lru_in",
    )(x3, x3, x3, mod, g, w_in, conv_w, conv_b)


def _scan_kernel(uf_ref, ub_ref, h0_ref, wa_ref, ba_ref, wi_ref, bi_ref, lam_ref, yf_ref, yb_ref, ht_ref,
                 a_sc, bx_sc, h_sc):
    i = pl.program_id(0)
    n = pl.num_programs(0)
    rows = uf_ref.shape[0]
    nt = rows // SUBLANES
    s8 = SUBLANES

    @pl.when(i == 0)
    def _():
        h_sc[...] = h0_ref[...]

    for d, u_ref in enumerate((uf_ref, ub_ref)):
        for c in range(N_LRU_BLOCKS):
            cs = slice(c * LRU_BLOCK_W, (c + 1) * LRU_BLOCK_W)
            u16 = u_ref[:, cs].astype(BF16)
            u = u_ref[:, cs].astype(F32)
            ta = jnp.tanh(jnp.dot(u16, wa_ref[d, c], preferred_element_type=F32) + 0.5 * ba_ref[d, :, cs])
            ti = jnp.tanh(jnp.dot(u16, wi_ref[d, c], preferred_element_type=F32) + 0.5 * bi_ref[d, :, cs])
            neg_lam = -lam_ref[d, :, cs]
            softplus = jnp.maximum(neg_lam, 0.0) + jnp.log1p(jnp.exp(-jnp.abs(neg_lam)))
            k = (-0.5 * LRU_C * LOG2E) * softplus
            a = jnp.exp2(k * ta + k)
            w = 1.0 - a * a
            root = w * lax.rsqrt(jnp.maximum(w, 1e-30))
            a_sc[d, :, cs] = a
            bx_sc[d, :, cs] = root * (ti * u + u)

    def step(t, carry):
        hf, hb = carry
        rf = pl.multiple_of(t * 2 * s8, 2 * s8)
        rb = pl.multiple_of((nt - 2 - 2 * t) * s8, 2 * s8)
        hf1 = a_sc[0, pl.ds(rf, s8), :] * hf + bx_sc[0, pl.ds(rf, s8), :]
        hf2 = a_sc[0, pl.ds(rf + s8, s8), :] * hf1 + bx_sc[0, pl.ds(rf + s8, s8), :]
        yf_ref[pl.ds(rf, 2 * s8), :] = jnp.concatenate([hf1, hf2], axis=0).astype(yf_ref.dtype)
        hb1 = a_sc[1, pl.ds(rb + s8, s8), :] * hb + bx_sc[1, pl.ds(rb + s8, s8), :]
        hb2 = a_sc[1, pl.ds(rb, s8), :] * hb1 + bx_sc[1, pl.ds(rb, s8), :]
        yb_ref[pl.ds(rb, 2 * s8), :] = jnp.concatenate([hb2, hb1], axis=0).astype(yb_ref.dtype)
        return hf2, hb2

    hf, hb = lax.fori_loop(0, nt // 2, step, (h_sc[0], h_sc[1]), unroll=2)
    h_sc[0] = hf
    h_sc[1] = hb

    @pl.when(i == n - 1)
    def _():
        ht_ref[...] = h_sc[...]


def _scan_call(u2, h0, w_a, b_a, w_i, b_i, lam):
    rows_total = u2.shape[0]
    rows = SCAN_T * SUBLANES
    n = rows_total // rows
    w = D_RNN
    fwd = lambda i: (i, 0)
    bwd = lambda i: (n - 1 - i, 0)
    return pl.pallas_call(
        _scan_kernel,
        grid=(n,),
        in_specs=[
            pl.BlockSpec((rows, w), fwd),
            pl.BlockSpec((rows, w), bwd),
            _const_spec((2, SUBLANES, w)),
            _const_spec((2, N_LRU_BLOCKS, LRU_BLOCK_W, LRU_BLOCK_W)),
            _const_spec((2, 1, w)),
            _const_spec((2, N_LRU_BLOCKS, LRU_BLOCK_W, LRU_BLOCK_W)),
            _const_spec((2, 1, w)),
            _const_spec((2, 1, w)),
        ],
        out_specs=[
            pl.BlockSpec((rows, w), fwd),
            pl.BlockSpec((rows, w), bwd),
            pl.BlockSpec((2, SUBLANES, w), lambda i: (0, 0, 0)),
        ],
        out_shape=[
            jax.ShapeDtypeStruct((rows_total, w), Y_DTYPE),
            jax.ShapeDtypeStruct((rows_total, w), Y_DTYPE),
            jax.ShapeDtypeStruct((2, SUBLANES, w), F32),
        ],
        scratch_shapes=[
            pltpu.VMEM((2, rows, w), F32),
            pltpu.VMEM((2, rows, w), F32),
            pltpu.VMEM((2, SUBLANES, w), F32),
        ],
        compiler_params=_params(("arbitrary",)),
        name="lru_scan",
    )(u2, u2, h0, w_a, b_a, w_i, b_i, lam)


def _rope_tables():
    t = np.arange(SEQ)
    row = (t // GRID_W).astype(np.float64)
    col = (t % GRID_W).astype(np.float64)
    half = HEAD_DIM // 2
    inv = ROPE_BASE ** (-np.arange(0, half, 2, dtype=np.float64) / half)
    ang_r = row[:, None] * inv[None, :]
    ang_c = col[:, None] * inv[None, :]
    ang = np.concatenate([ang_r, ang_r, ang_c, ang_c], axis=-1)
    ang = np.tile(ang, (1, LANES // HEAD_DIM))
    low = (np.arange(LANES) % 32) < 16
    sin = np.sin(ang)
    tables = (np.cos(ang), np.where(low, -sin, 0.0), np.where(low, 0.0, sin))
    return tuple(jnp.asarray(a, dtype=F32) for a in tables)


def kernel(x, c, ctx, c_ctx, ada_w, ada_b, norm_g, mlp_w1, mlp_w2, attn_w_qkv, attn_w_o, attn_sink,
           lru_w_in, lru_conv_w, lru_conv_b, lru_w_a, lru_b_a, lru_w_i, lru_b_i, lru_lam, lru_w_out):
    n_lat = BATCH * SEQ
    n_ctx = BATCH * CTX_LEN

    c16 = jnp.zeros((16, D_MODEL), F32).at[:BATCH].set(c).at[BATCH].set(c_ctx)
    mods = _mod_call(c16, ada_w, ada_b).reshape(2, 16, N_MOD, D_MODEL)

    def slab_bmajor(m):
        return jnp.broadcast_to(m[:, :, None, :], (BATCH, N_MOD, SUBLANES, D_MODEL))

    def slab_ctx(m):
        return jnp.broadcast_to(m[None, :, None, :], (1, N_MOD, SUBLANES, D_MODEL))

    mod_x0 = slab_bmajor(mods[0, :BATCH])
    mod_c0 = slab_ctx(mods[0, BATCH])
    w_qkv = attn_w_qkv[0]
    w_qkv = jnp.concatenate([w_qkv[:, :D_Q] * (HEAD_DIM ** -0.5 * LOG2E), w_qkv[:, D_Q:]], axis=1).astype(BF16)
    sink2 = attn_sink[0] * LOG2E
    w_o = attn_w_o[0].astype(BF16)
    w1_all, w2_all = mlp_w1.astype(BF16), mlp_w2.astype(BF16)
    g0 = norm_g[0]
    tiles_per_batch = SEQ // TOKEN_TILE

    x2 = x.reshape(n_lat, D_MODEL)
    c2 = ctx.reshape(n_ctx, D_MODEL)
    q, k, v = _qkv_call(x2, mod_x0, g0, w_qkv, _rope_tables(), tiles_per_batch)
    qc, kc, vc = _qkv_call(c2, mod_c0, g0, w_qkv, None, n_ctx // TOKEN_TILE)
    kc3 = kc.reshape(BATCH, CTX_LEN, D_K2)
    att = _attn_call(sink2, q, k.reshape(BATCH, SEQ, D_K2), v, kc3, vc)
    att_c = _ctx_attn_call(sink2, qc, kc3, vc)
    x2 = _post_call(x2, [att], mod_x0, g0, w_o, w1_all, w2_all, 0, SEQ, lru=False)
    c2 = _post_call(c2, [att_c], mod_c0, g0, w_o, w1_all, w2_all, 0, n_ctx, lru=False)

    x3 = x2.reshape(BATCH, SEQ, D_MODEL)
    c3 = c2.reshape(BATCH, CTX_LEN, D_MODEL)
    mod_x1 = mods[1, :BATCH].transpose(1, 0, 2)[None]
    mod_c1 = slab_ctx(mods[1, BATCH])
    g1 = norm_g[1]
    w_in = lru_w_in[0].astype(BF16)
    conv_w = 0.5 * lru_conv_w[0]
    conv_b = 0.5 * lru_conv_b[0].reshape(1, D_RNN)
    w_a, w_i = lru_w_a[0].astype(BF16), lru_w_i[0].astype(BF16)
    b_a, b_i = lru_b_a[0].reshape(2, 1, D_RNN), lru_b_i[0].reshape(2, 1, D_RNN)
    lam = lru_lam[0].reshape(2, 1, D_RNN)
    scan = functools.partial(_scan_call, w_a=w_a, b_a=b_a, w_i=w_i, b_i=b_i, lam=lam)

    (u_c,) = _lru_in_call(c3, mod_c1, g1, w_in, conv_w, conv_b, need_gate=False)
    _, _, h_ctx = scan(u_c, jnp.zeros((2, SUBLANES, D_RNN), F32))
    gate_x, u_x, x_t = _lru_in_call(x3, mod_x1, g1, w_in, conv_w, conv_b, need_gate=True)
    yf, yb, _ = scan(u_x, h_ctx)
    return _post_call(x_t, [gate_x, yf, yb], mod_x1, g1, lru_w_out[0].astype(BF16),
                      w1_all, w2_all, 1, n_lat, lru=True)
```

```python
import functools

import jax
import jax.numpy as jnp
import numpy as np
from jax import lax
from jax.experimental import pallas as pl
from jax.experimental.pallas import tpu as pltpu

D_MODEL = 1024
BATCH = 8
SEQ = 2048
GRID_W = 64
CTX_LEN = 256
HEAD_DIM = 64
N_HEADS = 16
N_KV_HEADS = 4
GQA_GROUP = N_HEADS // N_KV_HEADS
WINDOW = 128
BLOCK = 128
ROPE_BASE = 10000.0
D_RNN = 1280
LRU_BLOCK_W = 256
N_LRU_BLOCKS = D_RNN // LRU_BLOCK_W
CONV_W = 4
LRU_C = 8.0
D_FF = 4 * D_MODEL
N_MOD = 6
EPS = 1e-6
NEG_INF = -1e30

D_Q = N_HEADS * HEAD_DIM
D_KV = N_KV_HEADS * HEAD_DIM
D_K2 = 2 * D_KV
LANES = 128
SUBLANES = 8
N_SLAB = D_MODEL // LANES
TOKEN_TILE = 512
QKV_TILE = 1024
MLP_TILE = 1024
LRU_IN_TILE = 512
FF_CHUNK = 1024
ATTN_Q_PER_STEP = 16
SCAN_T = 128
HALO = 16
U_DTYPE = jnp.float32
Y_DTYPE = jnp.bfloat16
LOG2E = 1.4426950408889634
VMEM_LIMIT = 60 * 1024 * 1024

F32 = jnp.float32
BF16 = jnp.bfloat16


def _rms(x, g):
    ms = jnp.mean(x * x, axis=-1, keepdims=True)
    return x * lax.rsqrt(ms + EPS) * g


def _slab(x):
    return x.reshape(x.shape[0] // SUBLANES, SUBLANES, x.shape[1])


def _modulate(h, shift8, scale8):
    out = _slab(h) * (1.0 + scale8)[None] + shift8[None]
    return out.reshape(h.shape)


def _gated_add(x, gate8, y):
    out = _slab(x) + gate8[None] * _slab(y)
    return out.reshape(x.shape)


def _const_spec(shape):
    n = len(shape)
    return pl.BlockSpec(shape, lambda *_: (0,) * n, pipeline_mode=pl.Buffered(1))


def _params(sem):
    return pltpu.CompilerParams(dimension_semantics=sem, vmem_limit_bytes=VMEM_LIMIT)


def _mod_kernel(c_ref, w_ref, b_ref, o_ref):
    s = jax.nn.silu(c_ref[...]).astype(BF16)
    o_ref[0] = jnp.dot(s, w_ref[0].astype(BF16), preferred_element_type=F32) + b_ref[0]


def _mod_call(c16, ada_w, ada_b):
    depth = ada_w.shape[0]
    nt = 2048
    return pl.pallas_call(
        _mod_kernel,
        grid=(depth, N_MOD * D_MODEL // nt),
        in_specs=[
            pl.BlockSpec((16, D_MODEL), lambda l, j: (0, 0)),
            pl.BlockSpec((1, D_MODEL, nt), lambda l, j: (l, 0, j)),
            pl.BlockSpec((1, 1, nt), lambda l, j: (l, 0, j)),
        ],
        out_specs=pl.BlockSpec((1, 16, nt), lambda l, j: (l, 0, j)),
        out_shape=jax.ShapeDtypeStruct((depth, 16, N_MOD * D_MODEL), F32),
        compiler_params=_params(("arbitrary", "arbitrary")),
        name="adaln_mod",
    )(c16, ada_w, ada_b.reshape(depth, 1, N_MOD * D_MODEL))


def _qkv_kernel(*refs, rope):
    if rope:
        x_ref, mod_ref, g_ref, w_ref, cos_ref, sa_ref, sb_ref, q_ref, k_ref, v_ref, h_sc, y_sc = refs
    else:
        x_ref, mod_ref, g_ref, w_ref, q_ref, k_ref, v_ref, h_sc, y_sc = refs
    half = x_ref.shape[0] // 2
    low = lax.broadcasted_iota(jnp.int32, (half, LANES), 1) < HEAD_DIM
    for r in range(2):
        rs = slice(r * half, (r + 1) * half)
        h_sc[r] = _modulate(_rms(x_ref[rs, :], g_ref[0:1, :]), mod_ref[0, 0], mod_ref[0, 1]).astype(BF16)
    for r in range(2):
        y_sc[r] = jnp.dot(h_sc[r], w_ref[...], preferred_element_type=F32)
    for r in range(2):
        rs = slice(r * half, (r + 1) * half)
        if rope:
            cos, sa, sb = cos_ref[rs, :], sa_ref[rs, :], sb_ref[rs, :]
        for c in range((D_Q + D_KV) // LANES):
            yc = y_sc[r, :, c * LANES:(c + 1) * LANES]
            if rope:
                yc = yc * cos + pltpu.roll(yc, LANES - 16, 1) * sa + pltpu.roll(yc, 16, 1) * sb
            if c < D_Q // LANES:
                q_ref[rs, c * LANES:(c + 1) * LANES] = yc.astype(BF16)
            else:
                c2 = 2 * (c - D_Q // LANES)
                swapped = pltpu.roll(yc, HEAD_DIM, 1)
                k_ref[rs, c2 * LANES:(c2 + 1) * LANES] = jnp.where(low, yc, swapped).astype(BF16)
                k_ref[rs, (c2 + 1) * LANES:(c2 + 2) * LANES] = jnp.where(low, swapped, yc).astype(BF16)
        for blk in range(half // BLOCK):
            v_ref[r * (half // BLOCK) + blk] = y_sc[r, blk * BLOCK:(blk + 1) * BLOCK, D_Q + D_KV:].T.astype(BF16)


def _qkv_call(x2, mod, g, w_qkv, tables, rows_per_group):
    n = x2.shape[0]
    rope = tables is not None
    tm = QKV_TILE if rope else TOKEN_TILE
    tiles_per_group = rows_per_group // tm
    in_specs = [
        pl.BlockSpec((tm, D_MODEL), lambda i: (i, 0)),
        pl.BlockSpec((1, N_MOD, SUBLANES, D_MODEL), lambda i: (i // tiles_per_group, 0, 0, 0)),
        _const_spec((4, D_MODEL)),
        _const_spec((D_MODEL, D_Q + 2 * D_KV)),
    ]
    args = [x2, mod, g, w_qkv]
    if rope:
        nt = SEQ // tm
        in_specs += [pl.BlockSpec((tm, LANES), lambda i: (i % nt, 0))] * 3
        args += list(tables)
    return pl.pallas_call(
        functools.partial(_qkv_kernel, rope=rope),
        grid=(n // tm,),
        in_specs=in_specs,
        out_specs=[
            pl.BlockSpec((tm, D_Q), lambda i: (i, 0)),
            pl.BlockSpec((tm, D_K2), lambda i: (i, 0)),
            pl.BlockSpec((tm // BLOCK, D_KV, BLOCK), lambda i: (i, 0, 0)),
        ],
        out_shape=[
            jax.ShapeDtypeStruct((n, D_Q), BF16),
            jax.ShapeDtypeStruct((n, D_K2), BF16),
            jax.ShapeDtypeStruct((n // BLOCK, D_KV, BLOCK), BF16),
        ],
        scratch_shapes=[
            pltpu.VMEM((2, tm // 2, D_MODEL), BF16),
            pltpu.VMEM((2, tm // 2, D_Q + 2 * D_KV), F32),
        ],
        compiler_params=_params(("parallel",)),
        name="qkv_rope" if rope else "qkv_ctx",
    )(*args)


VT_ROWS = HEAD_DIM + 16


def _attn_kernel(*refs, local, q_per_step):
    if local:
        sink_ref, q_ref, k_ref, v_ref, kc_ref, vc_ref, o_ref, s_sc, p_sc, bias_sc = refs
    else:
        sink_ref, q_ref, kc_ref, vc_ref, o_ref, s_sc, p_sc, bias_sc = refs
    j = pl.program_id(1)
    seq_blocks = SEQ // BLOCK if local else 0
    ctx_blocks = CTX_LEN // BLOCK
    n_keys = s_sc.shape[1]
    pad_row = lax.broadcasted_iota(jnp.int32, (VT_ROWS - HEAD_DIM, n_keys), 0)
    vt_pad = jnp.where(pad_row == 0, 1.0, 0.0).astype(BF16)

    nt = (((1,), (1,)), ((), ()))
    n_band = 3
    n_chunks = D_Q // LANES
    lane = lax.broadcasted_iota(jnp.int32, (BLOCK, LANES), 1)
    first_head = lax.broadcasted_iota(jnp.int32, (1, 2 * BLOCK), 1) < BLOCK

    def block_params(qb):
        jq = j * q_per_step + qb
        rows = pl.ds(pl.multiple_of(qb * BLOCK, BLOCK), BLOCK)
        if not local:
            return rows, None, None, None, qb // ctx_blocks
        blk0 = jnp.clip(jq - 1, 0, seq_blocks - n_band)
        return rows, blk0, pl.multiple_of(blk0 * BLOCK, BLOCK), qb % 2, 0

    def store_bias(qb):
        if local:
            jq = j * q_per_step + qb
            start = jnp.clip(jq - 1, 0, seq_blocks - n_band) * BLOCK
            kpos = start + lax.broadcasted_iota(jnp.int32, (n_band * BLOCK, BLOCK), 0)
            qpos = jq * BLOCK + lax.broadcasted_iota(jnp.int32, (n_band * BLOCK, BLOCK), 1)
            bias_sc[qb % 2] = jnp.where(jnp.abs(kpos - qpos) <= WINDOW, 0.0, NEG_INF).astype(F32)

    def scores(c, params):
        rows, _, start, slot, cb = params
        kcols = slice((c // 2) * LANES, (c // 2 + 1) * LANES)
        qc = q_ref[rows, c * LANES:(c + 1) * LANES]
        zero = jnp.zeros_like(qc)
        q2 = jnp.concatenate([jnp.where(lane < HEAD_DIM, qc, zero), jnp.where(lane < HEAD_DIM, zero, qc)], axis=0)
        s_sc[c % 2, 0:CTX_LEN] = lax.dot_general(kc_ref[cb, :, kcols], q2, nt, preferred_element_type=F32)
        if local:
            s_band = lax.dot_general(k_ref[0, pl.ds(start, n_band * BLOCK), kcols], q2, nt,
                                     preferred_element_type=F32)
            for hd in range(2):
                s_sc[c % 2, CTX_LEN:, hd * BLOCK:(hd + 1) * BLOCK] = (
                    s_band[:, hd * BLOCK:(hd + 1) * BLOCK] + bias_sc[slot])

    def softmax(c):
        s = s_sc[c % 2]
        sink_row = jnp.where(first_head, sink_ref[2 * c], sink_ref[2 * c + 1])
        m = jnp.maximum(jnp.max(s, axis=0, keepdims=True), sink_row)
        p_sc[c % 2] = jnp.exp2(s - m).astype(BF16)
        return jnp.exp2(sink_row - m)

    def values(c, params, sink_term):
        rows, blk0, _, _, cb = params
        hrows = slice((c // 2) * HEAD_DIM, (c // 2 + 1) * HEAD_DIM)
        vt = [vc_ref[cb * ctx_blocks + i, hrows, :] for i in range(ctx_blocks)]
        if local:
            vt_band = v_ref[pl.ds(blk0, n_band), hrows, :]
            vt += [vt_band[i] for i in range(n_band)]
        vt_aug = jnp.concatenate([jnp.concatenate(vt, axis=1), vt_pad], axis=0)
        acc = jnp.dot(vt_aug, p_sc[c % 2], preferred_element_type=F32)
        out_t = acc[0:HEAD_DIM] * (1.0 / (acc[HEAD_DIM:HEAD_DIM + 1] + sink_term))
        both = jnp.concatenate([out_t[:, :BLOCK], out_t[:, BLOCK:]], axis=0)
        o_ref[rows, c * LANES:(c + 1) * LANES] = both.T.astype(o_ref.dtype)

    def query_block(qb, sink_term0):
        cur = block_params(qb)
        nxt = block_params(jnp.minimum(qb + 1, q_per_step - 1))
        store_bias(qb + 1)
        sink_terms = {0: sink_term0}
        for c in range(n_chunks):
            if c + 2 < n_chunks:
                scores(c + 2, cur)
            else:
                scores(c + 2 - n_chunks, nxt)
            sink_terms[c + 1] = softmax((c + 1) % n_chunks)
            values(c, cur, sink_terms[c])
        return sink_terms[n_chunks]

    first = block_params(0)
    store_bias(0)
    scores(0, first)
    scores(1, first)
    lax.fori_loop(0, q_per_step, query_block, softmax(0))


def _attn_scratch(n_keys):
    return [
        pltpu.VMEM((2, n_keys, 2 * BLOCK), F32),
        pltpu.VMEM((2, n_keys, 2 * BLOCK), BF16),
        pltpu.VMEM((2, 3 * BLOCK, BLOCK), F32),
    ]


def _attn_call(sink2, q, k, vt, kc, vtc):
    nb = SEQ // BLOCK
    nbc = CTX_LEN // BLOCK
    qps = ATTN_Q_PER_STEP
    steps = nb // qps
    return pl.pallas_call(
        functools.partial(_attn_kernel, local=True, q_per_step=qps),
        grid=(BATCH, steps),
        in_specs=[
            pl.BlockSpec(memory_space=pltpu.SMEM),
            pl.BlockSpec((qps * BLOCK, D_Q), lambda b, j: (b * steps + j, 0)),
            pl.BlockSpec((1, SEQ, D_K2), lambda b, j: (b, 0, 0)),
            pl.BlockSpec((nb, D_KV, BLOCK), lambda b, j: (b, 0, 0)),
            pl.BlockSpec((1, CTX_LEN, D_K2), lambda b, j: (b, 0, 0)),
            pl.BlockSpec((nbc, D_KV, BLOCK), lambda b, j: (b, 0, 0)),
        ],
        out_specs=pl.BlockSpec((qps * BLOCK, D_Q), lambda b, j: (b * steps + j, 0)),
        out_shape=jax.ShapeDtypeStruct((BATCH * SEQ, D_Q), BF16),
        scratch_shapes=_attn_scratch(CTX_LEN + 3 * BLOCK),
        compiler_params=_params(("parallel", "arbitrary")),
        name="band_attn",
    )(sink2, q, k, vt, kc, vtc)


def _ctx_attn_call(sink2, qc, kc, vtc):
    nb = BATCH * CTX_LEN // BLOCK
    whole = lambda shape: pl.BlockSpec(shape, lambda b, j: (0,) * len(shape))
    return pl.pallas_call(
        functools.partial(_attn_kernel, local=False, q_per_step=nb),
        grid=(1, 1),
        in_specs=[
            pl.BlockSpec(memory_space=pltpu.SMEM),
            whole((BATCH * CTX_LEN, D_Q)),
            whole((BATCH, CTX_LEN, D_K2)),
            whole((nb, D_KV, BLOCK)),
        ],
        out_specs=whole((BATCH * CTX_LEN, D_Q)),
        out_shape=jax.ShapeDtypeStruct((BATCH * CTX_LEN, D_Q), BF16),
        scratch_shapes=_attn_scratch(CTX_LEN),
        compiler_params=_params(("arbitrary", "arbitrary")),
        name="ctx_attn",
    )(sink2, qc, kc, vtc)


def _to_time_major(src_ref, sc_ref, t0=0, nt=None):
    nt = src_ref.shape[1] if nt is None else nt
    r0 = t0 * SUBLANES
    for b in range(BATCH):
        for s in range(N_SLAB):
            sc_ref[s, pl.ds(r0 + b, nt, stride=SUBLANES), :] = src_ref[b, t0:t0 + nt, s * LANES:(s + 1) * LANES]
    return jnp.concatenate([sc_ref[s, r0:r0 + nt * SUBLANES, :] for s in range(N_SLAB)], axis=1)


def _from_time_major(val, sc_ref, dst_ref, t0, nt):
    r0 = t0 * SUBLANES
    for s in range(N_SLAB):
        sc_ref[s, r0:r0 + nt * SUBLANES, :] = val[:, s * LANES:(s + 1) * LANES]
    for b in range(BATCH):
        for s in range(N_SLAB):
            dst_ref[b, t0:t0 + nt, s * LANES:(s + 1) * LANES] = sc_ref[s, pl.ds(r0 + b, nt, stride=SUBLANES), :]


def _post_kernel(*refs, lru):
    if lru:
        (x_ref, gate_ref, yf_ref, yb_ref, mod_ref, g_ref, wf_ref, w1_ref, w2_ref, o_ref,
         x1_sc, h_sc, acc_sc, tout_sc) = refs
    else:
        x_ref, a_ref, mod_ref, g_ref, wf_ref, w1_ref, w2_ref, o_ref, x1_sc, h_sc, acc_sc = refs
    half_rows = x_ref.shape[0] // 2
    half_t = half_rows // BATCH

    def head(r):
        rs = slice(r * half_rows, (r + 1) * half_rows)
        if lru:
            front = (gate_ref[rs, :].astype(F32)
                     * (yf_ref[rs, :].astype(F32) + yb_ref[rs, :].astype(F32))).astype(BF16)
        else:
            front = a_ref[rs, :]
        y = jnp.dot(front, wf_ref[...], preferred_element_type=F32)
        x1 = _gated_add(x_ref[rs, :], mod_ref[0, 2], _rms(y, g_ref[1:2, :]))
        x1_sc[r] = x1
        h_sc[r] = _modulate(_rms(x1, g_ref[2:3, :]), mod_ref[0, 3], mod_ref[0, 4]).astype(BF16)

    def mlp(r):
        acc = jnp.zeros((half_rows, D_MODEL), F32)
        for c in range(D_FF // FF_CHUNK):
            hid = jnp.dot(h_sc[r], w1_ref[:, c * FF_CHUNK:(c + 1) * FF_CHUNK], preferred_element_type=F32)
            hid = jnp.square(jnp.maximum(hid, 0.0)).astype(BF16)
            acc = acc + jnp.dot(hid, w2_ref[c * FF_CHUNK:(c + 1) * FF_CHUNK, :], preferred_element_type=F32)
        acc_sc[r] = acc

    def tail(r):
        out = _gated_add(x1_sc[r], mod_ref[0, 5], _rms(acc_sc[r], g_ref[3:4, :]))
        if lru:
            _from_time_major(out, tout_sc, o_ref, r * half_t, half_t)
        else:
            o_ref[r * half_rows:(r + 1) * half_rows, :] = out

    head(0)
    head(1)
    mlp(0)
    tail(0)
    mlp(1)
    tail(1)


def _post_call(x, fronts, mod, g, w_front, w1, w2, layer, rows_per_group, lru):
    tm = MLP_TILE if (not lru and x.shape[0] >= 8 * MLP_TILE) else TOKEN_TILE
    tiles_per_group = rows_per_group // tm
    layer_spec = lambda shape: pl.BlockSpec((None,) + shape[1:], lambda i: (layer, 0, 0),
                                            pipeline_mode=pl.Buffered(1))
    row = lambda i: (i, 0)
    n = x.shape[0]
    x_spec = pl.BlockSpec((tm, D_MODEL), row)
    if lru:
        out_spec = pl.BlockSpec((BATCH, tm // BATCH, D_MODEL), lambda i: (0, i, 0))
        out_shape = jax.ShapeDtypeStruct((BATCH, n // BATCH, D_MODEL), F32)
        scratch = [pltpu.VMEM((N_SLAB, tm, LANES), F32)]
    else:
        out_spec, out_shape = x_spec, jax.ShapeDtypeStruct(x.shape, F32)
        scratch = []
    scratch = [
        pltpu.VMEM((2, tm // 2, D_MODEL), F32),
        pltpu.VMEM((2, tm // 2, D_MODEL), BF16),
        pltpu.VMEM((2, tm // 2, D_MODEL), F32),
    ] + scratch
    in_specs = [x_spec]
    in_specs += [pl.BlockSpec((tm, f.shape[1]), row) for f in fronts]
    in_specs += [
        pl.BlockSpec((1, N_MOD, SUBLANES, D_MODEL), lambda i: (i // tiles_per_group, 0, 0, 0)),
        _const_spec((4, D_MODEL)),
        _const_spec(w_front.shape),
        layer_spec(w1.shape),
        layer_spec(w2.shape),
    ]
    return pl.pallas_call(
        functools.partial(_post_kernel, lru=lru),
        grid=(n // tm,),
        in_specs=in_specs,
        out_specs=out_spec,
        out_shape=out_shape,
        scratch_shapes=scratch,
        compiler_params=_params(("parallel",)),
        name="lru_out_mlp" if lru else "attn_out_mlp",
    )(x, *fronts, mod, g, w_front, w1, w2)


def _lru_in_kernel(*refs, need_gate):
    if need_gate:
        (x_ref, xp_ref, xn_ref, mod_ref, g_ref, w_ref, cw_ref, cb_ref, gate_ref, u_ref, xt_ref,
         v_sc, h_sc, t_sc, tp_sc, tn_sc) = refs
    else:
        x_ref, xp_ref, xn_ref, mod_ref, g_ref, w_ref, cw_ref, cb_ref, u_ref, v_sc, h_sc, t_sc, tp_sc, tn_sc = refs
    i = pl.program_id(0)
    n = pl.num_programs(0)
    rows = x_ref.shape[0] * x_ref.shape[1]
    half = rows // 2
    half_t = x_ref.shape[1] // 2
    s8 = SUBLANES

    def pre(x):
        return _modulate(_rms(x, g_ref[0:1, :]), mod_ref[0, 0], mod_ref[0, 1]).astype(BF16)

    h_sc[0, 0:HALO] = pre(_to_time_major(xp_ref, tp_sc)[SUBLANES * SUBLANES - HALO:])
    x_halves = [_to_time_major(x_ref, t_sc, r * half_t, half_t) for r in range(2)]
    if need_gate:
        for r in range(2):
            xt_ref[r * half:(r + 1) * half, :] = x_halves[r]
    h_sc[0, HALO:] = pre(x_halves[0])
    h_sc[1, 0:half] = pre(x_halves[1])
    h_sc[1, half:] = pre(_to_time_major(xn_ref, tn_sc)[:HALO])
    ext = half + HALO
    for r in range(2):
        v_sc[r * ext:(r + 1) * ext] = jnp.dot(h_sc[r], w_ref[:, D_RNN:], preferred_element_type=F32)
    v_sc[0:HALO] = v_sc[0:HALO] * (i > 0).astype(F32)
    v_sc[HALO + rows:HALO + rows + s8] = v_sc[HALO + rows:HALO + rows + s8] * (i < n - 1).astype(F32)
    for r in range(2):
        if need_gate:
            h_r = h_sc[0, HALO:] if r == 0 else h_sc[1, 0:half]
            gate_ref[r * half:(r + 1) * half, :] = jax.nn.gelu(
                jnp.dot(h_r, w_ref[:, :D_RNN], preferred_element_type=F32)).astype(BF16)
        base = HALO + r * half
        u_ref[r * half:(r + 1) * half, :] = (
            cb_ref[...]
            + cw_ref[0:1, :] * v_sc[base - 2 * s8:base - 2 * s8 + half]
            + cw_ref[1:2, :] * v_sc[base - s8:base - s8 + half]
            + cw_ref[2:3, :] * v_sc[base:base + half]
            + cw_ref[3:4, :] * v_sc[base + s8:base + s8 + half]).astype(u_ref.dtype)


def _lru_in_call(x3, mod, g, w_in, conv_w, conv_b, need_gate):
    t_total = x3.shape[1]
    n = BATCH * t_total
    tm = LRU_IN_TILE
    nt = tm // BATCH
    row = lambda i: (i, 0)
    per_tile = nt // SUBLANES
    last = t_total // SUBLANES - 1
    halo_spec = lambda f: pl.BlockSpec((BATCH, SUBLANES, D_MODEL), f)
    out_specs = [pl.BlockSpec((tm, D_RNN), row), pl.BlockSpec((tm, D_RNN), row), pl.BlockSpec((tm, D_MODEL), row)]
    out_shape = [jax.ShapeDtypeStruct((n, D_RNN), BF16), jax.ShapeDtypeStruct((n, D_RNN), U_DTYPE),
                 jax.ShapeDtypeStruct((n, D_MODEL), F32)]
    if not need_gate:
        out_specs, out_shape = out_specs[1:2], out_shape[1:2]
    return pl.pallas_call(
        functools.partial(_lru_in_kernel, need_gate=need_gate),
        grid=(n // tm,),
        in_specs=[
            pl.BlockSpec((BATCH, nt, D_MODEL), lambda i: (0, i, 0)),
            halo_spec(lambda i: (0, jnp.maximum(i * per_tile - 1, 0), 0)),
            halo_spec(lambda i: (0, jnp.minimum((i + 1) * per_tile, last), 0)),
            _const_spec((1, N_MOD, SUBLANES, D_MODEL)),
            _const_spec((4, D_MODEL)),
            _const_spec((D_MODEL, 2 * D_RNN)),
            _const_spec((CONV_W, D_RNN)),
            _const_spec((1, D_RNN)),
        ],
        out_specs=out_specs,
        out_shape=out_shape,
        scratch_shapes=[
            pltpu.VMEM((tm + 2 * HALO, D_RNN), F32),
            pltpu.VMEM((2, tm // 2 + HALO, D_MODEL), BF16),
            pltpu.VMEM((N_SLAB, tm, LANES), F32),
            pltpu.VMEM((N_SLAB, SUBLANES * SUBLANES, LANES), F32),
            pltpu.VMEM((N_SLAB, SUBLANES * SUBLANES, LANES), F32),
        ],
        compiler_params=_params(("parallel",)),
        name="lru_in",
    )(x3, x3, x3, mod, g, w_in, conv_w, conv_b)


def _scan_kernel(uf_ref, ub_ref, h0_ref, wa_ref, ba_ref, wi_ref, bi_ref, lam_ref, yf_ref, yb_ref, ht_ref,
                 a_sc, bx_sc, h_sc):
    i = pl.program_id(0)
    n = pl.num_programs(0)
    rows = uf_ref.shape[0]
    nt = rows // SUBLANES
    s8 = SUBLANES

    @pl.when(i == 0)
    def _():
        h_sc[...] = h0_ref[...]

    for d, u_ref in enumerate((uf_ref, ub_ref)):
        for c in range(N_LRU_BLOCKS):
            cs = slice(c * LRU_BLOCK_W, (c + 1) * LRU_BLOCK_W)
            u16 = u_ref[:, cs].astype(BF16)
            u = u_ref[:, cs].astype(F32)
            ta = jnp.tanh(jnp.dot(u16, wa_ref[d, c], preferred_element_type=F32) + 0.5 * ba_ref[d, :, cs])
            ti = jnp.tanh(jnp.dot(u16, wi_ref[d, c], preferred_element_type=F32) + 0.5 * bi_ref[d, :, cs])
            neg_lam = -lam_ref[d, :, cs]
            softplus = jnp.maximum(neg_lam, 0.0) + jnp.log1p(jnp.exp(-jnp.abs(neg_lam)))
            k = (-0.5 * LRU_C * LOG2E) * softplus
            a = jnp.exp2(k * ta + k)
            w = 1.0 - a * a
            root = w * lax.rsqrt(jnp.maximum(w, 1e-30))
            a_sc[d, :, cs] = a
            bx_sc[d, :, cs] = root * (ti * u + u)

    def step(t, carry):
        hf, hb = carry
        rf = pl.multiple_of(t * 2 * s8, 2 * s8)
        rb = pl.multiple_of((nt - 2 - 2 * t) * s8, 2 * s8)
        hf1 = a_sc[0, pl.ds(rf, s8), :] * hf + bx_sc[0, pl.ds(rf, s8), :]
        hf2 = a_sc[0, pl.ds(rf + s8, s8), :] * hf1 + bx_sc[0, pl.ds(rf + s8, s8), :]
        yf_ref[pl.ds(rf, 2 * s8), :] = jnp.concatenate([hf1, hf2], axis=0).astype(yf_ref.dtype)
        hb1 = a_sc[1, pl.ds(rb + s8, s8), :] * hb + bx_sc[1, pl.ds(rb + s8, s8), :]
        hb2 = a_sc[1, pl.ds(rb, s8), :] * hb1 + bx_sc[1, pl.ds(rb, s8), :]
        yb_ref[pl.ds(rb, 2 * s8), :] = jnp.concatenate([hb2, hb1], axis=0).astype(yb_ref.dtype)
        return hf2, hb2

    hf, hb = lax.fori_loop(0, nt // 2, step, (h_sc[0], h_sc[1]), unroll=2)
    h_sc[0] = hf
    h_sc[1] = hb

    @pl.when(i == n - 1)
    def _():
        ht_ref[...] = h_sc[...]


def _scan_call(u2, h0, w_a, b_a, w_i, b_i, lam):
    rows_total = u2.shape[0]
    rows = SCAN_T * SUBLANES
    n = rows_total // rows
    w = D_RNN
    fwd = lambda i: (i, 0)
    bwd = lambda i: (n - 1 - i, 0)
    return pl.pallas_call(
        _scan_kernel,
        grid=(n,),
        in_specs=[
            pl.BlockSpec((rows, w), fwd),
            pl.BlockSpec((rows, w), bwd),
            _const_spec((2, SUBLANES, w)),
            _const_spec((2, N_LRU_BLOCKS, LRU_BLOCK_W, LRU_BLOCK_W)),
            _const_spec((2, 1, w)),
            _const_spec((2, N_LRU_BLOCKS, LRU_BLOCK_W, LRU_BLOCK_W)),
            _const_spec((2, 1, w)),
            _const_spec((2, 1, w)),
        ],
        out_specs=[
            pl.BlockSpec((rows, w), fwd),
            pl.BlockSpec((rows, w), bwd),
            pl.BlockSpec((2, SUBLANES, w), lambda i: (0, 0, 0)),
        ],
        out_shape=[
            jax.ShapeDtypeStruct((rows_total, w), Y_DTYPE),
            jax.ShapeDtypeStruct((rows_total, w), Y_DTYPE),
            jax.ShapeDtypeStruct((2, SUBLANES, w), F32),
        ],
        scratch_shapes=[
            pltpu.VMEM((2, rows, w), F32),
            pltpu.VMEM((2, rows, w), F32),
            pltpu.VMEM((2, SUBLANES, w), F32),
        ],
        compiler_params=_params(("arbitrary",)),
        name="lru_scan",
    )(u2, u2, h0, w_a, b_a, w_i, b_i, lam)


def _rope_tables():
    t = np.arange(SEQ)
    row = (t // GRID_W).astype(np.float64)
    col = (t % GRID_W).astype(np.float64)
    half = HEAD_DIM // 2
    inv = ROPE_BASE ** (-np.arange(0, half, 2, dtype=np.float64) / half)
    ang_r = row[:, None] * inv[None, :]
    ang_c = col[:, None] * inv[None, :]
    ang = np.concatenate([ang_r, ang_r, ang_c, ang_c], axis=-1)
    ang = np.tile(ang, (1, LANES // HEAD_DIM))
    low = (np.arange(LANES) % 32) < 16
    sin = np.sin(ang)
    tables = (np.cos(ang), np.where(low, -sin, 0.0), np.where(low, 0.0, sin))
    return tuple(jnp.asarray(a, dtype=F32) for a in tables)


def kernel(x, c, ctx, c_ctx, ada_w, ada_b, norm_g, mlp_w1, mlp_w2, attn_w_qkv, attn_w_o, attn_sink,
           lru_w_in, lru_conv_w, lru_conv_b, lru_w_a, lru_b_a, lru_w_i, lru_b_i, lru_lam, lru_w_out):
    n_lat = BATCH * SEQ
    n_ctx = BATCH * CTX_LEN

    c16 = jnp.zeros((16, D_MODEL), F32).at[:BATCH].set(c).at[BATCH].set(c_ctx)
    mods = _mod_call(c16, ada_w, ada_b).reshape(2, 16, N_MOD, D_MODEL)

    def slab_bmajor(m):
        return jnp.broadcast_to(m[:, :, None, :], (BATCH, N_MOD, SUBLANES, D_MODEL))

    def slab_ctx(m):
        return jnp.broadcast_to(m[None, :, None, :], (1, N_MOD, SUBLANES, D_MODEL))

    mod_x0 = slab_bmajor(mods[0, :BATCH])
    mod_c0 = slab_ctx(mods[0, BATCH])
    w_qkv = attn_w_qkv[0]
    w_qkv = jnp.concatenate([w_qkv[:, :D_Q] * (HEAD_DIM ** -0.5 * LOG2E), w_qkv[:, D_Q:]], axis=1).astype(BF16)
    sink2 = attn_sink[0] * LOG2E
    w_o = attn_w_o[0].astype(BF16)
    w1_all, w2_all = mlp_w1.astype(BF16), mlp_w2.astype(BF16)
    g0 = norm_g[0]

    x2 = x.reshape(n_lat, D_MODEL)
    c2 = ctx.reshape(n_ctx, D_MODEL)
    q, k, v = _qkv_call(x2, mod_x0, g0, w_qkv, _rope_tables(), SEQ)
    qc, kc, vc = _qkv_call(c2, mod_c0, g0, w_qkv, None, n_ctx)
    kc3 = kc.reshape(BATCH, CTX_LEN, D_K2)
    att = _attn_call(sink2, q, k.reshape(BATCH, SEQ, D_K2), v, kc3, vc)
    att_c = _ctx_attn_call(sink2, qc, kc3, vc)
    x2 = _post_call(x2, [att], mod_x0, g0, w_o, w1_all, w2_all, 0, SEQ, lru=False)
    c2 = _post_call(c2, [att_c], mod_c0, g0, w_o, w1_all, w2_all, 0, n_ctx, lru=False)

    x3 = x2.reshape(BATCH, SEQ, D_MODEL)
    c3 = c2.reshape(BATCH, CTX_LEN, D_MODEL)
    mod_x1 = mods[1, :BATCH].transpose(1, 0, 2)[None]
    mod_c1 = slab_ctx(mods[1, BATCH])
    g1 = norm_g[1]
    w_in = lru_w_in[0].astype(BF16)
    conv_w = 0.5 * lru_conv_w[0]
    conv_b = 0.5 * lru_conv_b[0].reshape(1, D_RNN)
    w_a, w_i = lru_w_a[0].astype(BF16), lru_w_i[0].astype(BF16)
    b_a, b_i = lru_b_a[0].reshape(2, 1, D_RNN), lru_b_i[0].reshape(2, 1, D_RNN)
    lam = lru_lam[0].reshape(2, 1, D_RNN)
    scan = functools.partial(_scan_call, w_a=w_a, b_a=b_a, w_i=w_i, b_i=b_i, lam=lam)

    (u_c,) = _lru_in_call(c3, mod_c1, g1, w_in, conv_w, conv_b, need_gate=False)
    _, _, h_ctx = scan(u_c, jnp.zeros((2, SUBLANES, D_RNN), F32))
    gate_x, u_x, x_t = _lru_in_call(x3, mod_x1, g1, w_in, conv_w, conv_b, need_gate=True)
    yf, yb, _ = scan(u_x, h_ctx)
    return _post_call(x_t, [gate_x, yf, yb], mod_x1, g1, lru_w_out[0].astype(BF16),
                      w1_all, w2_all, 1, n_lat, lru=True)
```

```python
import functools

import jax
import jax.numpy as jnp
import numpy as np
from jax import lax
from jax.experimental import pallas as pl
from jax.experimental.pallas import tpu as pltpu

D_MODEL = 1024
BATCH = 8
SEQ = 2048
GRID_W = 64
CTX_LEN = 256
HEAD_DIM = 64
N_HEADS = 16
N_KV_HEADS = 4
GQA_GROUP = N_HEADS // N_KV_HEADS
WINDOW = 128
BLOCK = 128
ROPE_BASE = 10000.0
D_RNN = 1280
LRU_BLOCK_W = 256
N_LRU_BLOCKS = D_RNN // LRU_BLOCK_W
CONV_W = 4
LRU_C = 8.0
D_FF = 4 * D_MODEL
N_MOD = 6
EPS = 1e-6
NEG_INF = -1e30

D_Q = N_HEADS * HEAD_DIM
D_KV = N_KV_HEADS * HEAD_DIM
D_K2 = 2 * D_KV
LANES = 128
SUBLANES = 8
N_SLAB = D_MODEL // LANES
TOKEN_TILE = 512
QKV_TILE = 1024
MLP_TILE = 1024
LRU_IN_TILE = 512
FF_CHUNK = 1024
ATTN_Q_PER_STEP = 16
SCAN_T = 128
HALO = 16
U_DTYPE = jnp.float32
Y_DTYPE = jnp.bfloat16
LOG2E = 1.4426950408889634
VMEM_LIMIT = 60 * 1024 * 1024

F32 = jnp.float32
BF16 = jnp.bfloat16


def _rms(x, g):
    ms = jnp.mean(x * x, axis=-1, keepdims=True)
    return x * lax.rsqrt(ms + EPS) * g


def _slab(x):
    return x.reshape(x.shape[0] // SUBLANES, SUBLANES, x.shape[1])


def _modulate(h, shift8, scale8):
    out = _slab(h) * (1.0 + scale8)[None] + shift8[None]
    return out.reshape(h.shape)


def _gated_add(x, gate8, y):
    out = _slab(x) + gate8[None] * _slab(y)
    return out.reshape(x.shape)


def _const_spec(shape):
    n = len(shape)
    return pl.BlockSpec(shape, lambda *_: (0,) * n, pipeline_mode=pl.Buffered(1))


def _params(sem):
    return pltpu.CompilerParams(dimension_semantics=sem, vmem_limit_bytes=VMEM_LIMIT)


def _mod_kernel(c_ref, w_ref, b_ref, o_ref):
    s = jax.nn.silu(c_ref[...]).astype(BF16)
    o_ref[0] = jnp.dot(s, w_ref[0].astype(BF16), preferred_element_type=F32) + b_ref[0]


def _mod_call(c16, ada_w, ada_b):
    depth = ada_w.shape[0]
    nt = 3072
    return pl.pallas_call(
        _mod_kernel,
        grid=(depth, N_MOD * D_MODEL // nt),
        in_specs=[
            pl.BlockSpec((16, D_MODEL), lambda l, j: (0, 0)),
            pl.BlockSpec((1, D_MODEL, nt), lambda l, j: (l, 0, j)),
            pl.BlockSpec((1, 1, nt), lambda l, j: (l, 0, j)),
        ],
        out_specs=pl.BlockSpec((1, 16, nt), lambda l, j: (l, 0, j)),
        out_shape=jax.ShapeDtypeStruct((depth, 16, N_MOD * D_MODEL), F32),
        compiler_params=_params(("arbitrary", "arbitrary")),
        name="adaln_mod",
    )(c16, ada_w, ada_b.reshape(depth, 1, N_MOD * D_MODEL))


def _qkv_kernel(*refs, rope):
    if rope:
        x_ref, mod_ref, g_ref, w_ref, cos_ref, sa_ref, sb_ref, q_ref, k_ref, v_ref, h_sc, y_sc = refs
    else:
        x_ref, mod_ref, g_ref, w_ref, q_ref, k_ref, v_ref, h_sc, y_sc = refs
    half = x_ref.shape[0] // 2
    low = lax.broadcasted_iota(jnp.int32, (half, LANES), 1) < HEAD_DIM
    for r in range(2):
        rs = slice(r * half, (r + 1) * half)
        h_sc[r] = _modulate(_rms(x_ref[rs, :], g_ref[0:1, :]), mod_ref[0, 0], mod_ref[0, 1]).astype(BF16)
    for r in range(2):
        y_sc[r] = jnp.dot(h_sc[r], w_ref[...], preferred_element_type=F32)
    for r in range(2):
        rs = slice(r * half, (r + 1) * half)
        if rope:
            cos, sa, sb = cos_ref[rs, :], sa_ref[rs, :], sb_ref[rs, :]
        for c in range((D_Q + D_KV) // LANES):
            yc = y_sc[r, :, c * LANES:(c + 1) * LANES]
            if rope:
                yc = yc * cos + pltpu.roll(yc, LANES - 16, 1) * sa + pltpu.roll(yc, 16, 1) * sb
            if c < D_Q // LANES:
                q_ref[rs, c * LANES:(c + 1) * LANES] = yc.astype(BF16)
            else:
                c2 = 2 * (c - D_Q // LANES)
                swapped = pltpu.roll(yc, HEAD_DIM, 1)
                k_ref[rs, c2 * LANES:(c2 + 1) * LANES] = jnp.where(low, yc, swapped).astype(BF16)
                k_ref[rs, (c2 + 1) * LANES:(c2 + 2) * LANES] = jnp.where(low, swapped, yc).astype(BF16)
        for blk in range(half // BLOCK):
            v_ref[r * (half // BLOCK) + blk] = y_sc[r, blk * BLOCK:(blk + 1) * BLOCK, D_Q + D_KV:].T.astype(BF16)


def _qkv_call(x2, mod, g, w_qkv, tables, rows_per_group):
    n = x2.shape[0]
    rope = tables is not None
    tm = QKV_TILE if rope else TOKEN_TILE
    tiles_per_group = rows_per_group // tm
    in_specs = [
        pl.BlockSpec((tm, D_MODEL), lambda i: (i, 0)),
        pl.BlockSpec((1, N_MOD, SUBLANES, D_MODEL), lambda i: (i // tiles_per_group, 0, 0, 0)),
        _const_spec((4, D_MODEL)),
        _const_spec((D_MODEL, D_Q + 2 * D_KV)),
    ]
    args = [x2, mod, g, w_qkv]
    if rope:
        nt = SEQ // tm
        in_specs += [pl.BlockSpec((tm, LANES), lambda i: (i % nt, 0))] * 3
        args += list(tables)
    return pl.pallas_call(
        functools.partial(_qkv_kernel, rope=rope),
        grid=(n // tm,),
        in_specs=in_specs,
        out_specs=[
            pl.BlockSpec((tm, D_Q), lambda i: (i, 0)),
            pl.BlockSpec((tm, D_K2), lambda i: (i, 0)),
            pl.BlockSpec((tm // BLOCK, D_KV, BLOCK), lambda i: (i, 0, 0)),
        ],
        out_shape=[
            jax.ShapeDtypeStruct((n, D_Q), BF16),
            jax.ShapeDtypeStruct((n, D_K2), BF16),
            jax.ShapeDtypeStruct((n // BLOCK, D_KV, BLOCK), BF16),
        ],
        scratch_shapes=[
            pltpu.VMEM((2, tm // 2, D_MODEL), BF16),
            pltpu.VMEM((2, tm // 2, D_Q + 2 * D_KV), F32),
        ],
        compiler_params=_params(("parallel",)),
        name="qkv_rope" if rope else "qkv_ctx",
    )(*args)


VT_ROWS = HEAD_DIM + 16


def _attn_kernel(*refs, local, q_per_step):
    if local:
        sink_ref, q_ref, k_ref, v_ref, kc_ref, vc_ref, o_ref, s_sc, p_sc, bias_sc = refs
    else:
        sink_ref, q_ref, kc_ref, vc_ref, o_ref, s_sc, p_sc, bias_sc = refs
    j = pl.program_id(1)
    seq_blocks = SEQ // BLOCK if local else 0
    ctx_blocks = CTX_LEN // BLOCK
    n_keys = s_sc.shape[1]
    pad_row = lax.broadcasted_iota(jnp.int32, (VT_ROWS - HEAD_DIM, n_keys), 0)
    vt_pad = jnp.where(pad_row == 0, 1.0, 0.0).astype(BF16)

    nt = (((1,), (1,)), ((), ()))
    n_band = 3
    n_chunks = D_Q // LANES
    lane = lax.broadcasted_iota(jnp.int32, (BLOCK, LANES), 1)
    first_head = lax.broadcasted_iota(jnp.int32, (1, 2 * BLOCK), 1) < BLOCK

    def block_params(qb):
        jq = j * q_per_step + qb
        rows = pl.ds(pl.multiple_of(qb * BLOCK, BLOCK), BLOCK)
        if not local:
            return rows, None, None, None, qb // ctx_blocks
        blk0 = jnp.clip(jq - 1, 0, seq_blocks - n_band)
        return rows, blk0, pl.multiple_of(blk0 * BLOCK, BLOCK), qb % 2, 0

    def store_bias(qb):
        if local:
            jq = j * q_per_step + qb
            start = jnp.clip(jq - 1, 0, seq_blocks - n_band) * BLOCK
            kpos = start + lax.broadcasted_iota(jnp.int32, (n_band * BLOCK, BLOCK), 0)
            qpos = jq * BLOCK + lax.broadcasted_iota(jnp.int32, (n_band * BLOCK, BLOCK), 1)
            bias_sc[qb % 2] = jnp.where(jnp.abs(kpos - qpos) <= WINDOW, 0.0, NEG_INF).astype(F32)

    def scores(c, params):
        rows, _, start, slot, cb = params
        kcols = slice((c // 2) * LANES, (c // 2 + 1) * LANES)
        qc = q_ref[rows, c * LANES:(c + 1) * LANES]
        zero = jnp.zeros_like(qc)
        q2 = jnp.concatenate([jnp.where(lane < HEAD_DIM, qc, zero), jnp.where(lane < HEAD_DIM, zero, qc)], axis=0)
        s_sc[c % 2, 0:CTX_LEN] = lax.dot_general(kc_ref[cb, :, kcols], q2, nt, preferred_element_type=F32)
        if local:
            s_band = lax.dot_general(k_ref[0, pl.ds(start, n_band * BLOCK), kcols], q2, nt,
                                     preferred_element_type=F32)
            for hd in range(2):
                s_sc[c % 2, CTX_LEN:, hd * BLOCK:(hd + 1) * BLOCK] = (
                    s_band[:, hd * BLOCK:(hd + 1) * BLOCK] + bias_sc[slot])

    def softmax(c):
        s = s_sc[c % 2]
        sink_row = jnp.where(first_head, sink_ref[2 * c], sink_ref[2 * c + 1])
        m = jnp.maximum(jnp.max(s, axis=0, keepdims=True), sink_row)
        p_sc[c % 2] = jnp.exp2(s - m).astype(BF16)
        return jnp.exp2(sink_row - m)

    def values(c, params, sink_term):
        rows, blk0, _, _, cb = params
        hrows = slice((c // 2) * HEAD_DIM, (c // 2 + 1) * HEAD_DIM)
        vt = [vc_ref[cb * ctx_blocks + i, hrows, :] for i in range(ctx_blocks)]
        if local:
            vt_band = v_ref[pl.ds(blk0, n_band), hrows, :]
            vt += [vt_band[i] for i in range(n_band)]
        vt_aug = jnp.concatenate([jnp.concatenate(vt, axis=1), vt_pad], axis=0)
        acc = jnp.dot(vt_aug, p_sc[c % 2], preferred_element_type=F32)
        out_t = acc[0:HEAD_DIM] * (1.0 / (acc[HEAD_DIM:HEAD_DIM + 1] + sink_term))
        both = jnp.concatenate([out_t[:, :BLOCK], out_t[:, BLOCK:]], axis=0)
        o_ref[rows, c * LANES:(c + 1) * LANES] = both.T.astype(o_ref.dtype)

    def query_block(qb, sink_term0):
        cur = block_params(qb)
        nxt = block_params(jnp.minimum(qb + 1, q_per_step - 1))
        store_bias(qb + 1)
        sink_terms = {0: sink_term0}
        for c in range(n_chunks):
            if c + 2 < n_chunks:
                scores(c + 2, cur)
            else:
                scores(c + 2 - n_chunks, nxt)
            sink_terms[c + 1] = softmax((c + 1) % n_chunks)
            values(c, cur, sink_terms[c])
        return sink_terms[n_chunks]

    first = block_params(0)
    store_bias(0)
    scores(0, first)
    scores(1, first)
    lax.fori_loop(0, q_per_step, query_block, softmax(0))


def _attn_scratch(n_keys):
    return [
        pltpu.VMEM((2, n_keys, 2 * BLOCK), F32),
        pltpu.VMEM((2, n_keys, 2 * BLOCK), BF16),
        pltpu.VMEM((2, 3 * BLOCK, BLOCK), F32),
    ]


def _attn_call(sink2, q, k, vt, kc, vtc):
    nb = SEQ // BLOCK
    nbc = CTX_LEN // BLOCK
    qps = ATTN_Q_PER_STEP
    steps = nb // qps
    return pl.pallas_call(
        functools.partial(_attn_kernel, local=True, q_per_step=qps),
        grid=(BATCH, steps),
        in_specs=[
            pl.BlockSpec(memory_space=pltpu.SMEM),
            pl.BlockSpec((qps * BLOCK, D_Q), lambda b, j: (b * steps + j, 0)),
            pl.BlockSpec((1, SEQ, D_K2), lambda b, j: (b, 0, 0)),
            pl.BlockSpec((nb, D_KV, BLOCK), lambda b, j: (b, 0, 0)),
            pl.BlockSpec((1, CTX_LEN, D_K2), lambda b, j: (b, 0, 0)),
            pl.BlockSpec((nbc, D_KV, BLOCK), lambda b, j: (b, 0, 0)),
        ],
        out_specs=pl.BlockSpec((qps * BLOCK, D_Q), lambda b, j: (b * steps + j, 0)),
        out_shape=jax.ShapeDtypeStruct((BATCH * SEQ, D_Q), BF16),
        scratch_shapes=_attn_scratch(CTX_LEN + 3 * BLOCK),
        compiler_params=_params(("parallel", "arbitrary")),
        name="band_attn",
    )(sink2, q, k, vt, kc, vtc)


def _ctx_attn_call(sink2, qc, kc, vtc):
    nb = BATCH * CTX_LEN // BLOCK
    whole = lambda shape: pl.BlockSpec(shape, lambda b, j: (0,) * len(shape))
    return pl.pallas_call(
        functools.partial(_attn_kernel, local=False, q_per_step=nb),
        grid=(1, 1),
        in_specs=[
            pl.BlockSpec(memory_space=pltpu.SMEM),
            whole((BATCH * CTX_LEN, D_Q)),
            whole((BATCH, CTX_LEN, D_K2)),
            whole((nb, D_KV, BLOCK)),
        ],
        out_specs=whole((BATCH * CTX_LEN, D_Q)),
        out_shape=jax.ShapeDtypeStruct((BATCH * CTX_LEN, D_Q), BF16),
        scratch_shapes=_attn_scratch(CTX_LEN),
        compiler_params=_params(("arbitrary", "arbitrary")),
        name="ctx_attn",
    )(sink2, qc, kc, vtc)


def _to_time_major(src_ref, sc_ref, t0=0, nt=None):
    nt = src_ref.shape[1] if nt is None else nt
    r0 = t0 * SUBLANES
    for b in range(BATCH):
        for s in range(N_SLAB):
            sc_ref[s, pl.ds(r0 + b, nt, stride=SUBLANES), :] = src_ref[b, t0:t0 + nt, s * LANES:(s + 1) * LANES]
    return jnp.concatenate([sc_ref[s, r0:r0 + nt * SUBLANES, :] for s in range(N_SLAB)], axis=1)


def _from_time_major(val, sc_ref, dst_ref, t0, nt):
    r0 = t0 * SUBLANES
    for s in range(N_SLAB):
        sc_ref[s, r0:r0 + nt * SUBLANES, :] = val[:, s * LANES:(s + 1) * LANES]
    for b in range(BATCH):
        for s in range(N_SLAB):
            dst_ref[b, t0:t0 + nt, s * LANES:(s + 1) * LANES] = sc_ref[s, pl.ds(r0 + b, nt, stride=SUBLANES), :]


def _post_kernel(*refs, lru):
    if lru:
        (x_ref, gate_ref, yf_ref, yb_ref, mod_ref, g_ref, wf_ref, w1_ref, w2_ref, o_ref,
         x1_sc, h_sc, acc_sc, tout_sc) = refs
    else:
        x_ref, a_ref, mod_ref, g_ref, wf_ref, w1_ref, w2_ref, o_ref, x1_sc, h_sc, acc_sc = refs
    half_rows = x_ref.shape[0] // 2
    half_t = half_rows // BATCH

    def head(r):
        rs = slice(r * half_rows, (r + 1) * half_rows)
        if lru:
            front = (gate_ref[rs, :].astype(F32)
                     * (yf_ref[rs, :].astype(F32) + yb_ref[rs, :].astype(F32))).astype(BF16)
        else:
            front = a_ref[rs, :]
        y = jnp.dot(front, wf_ref[...], preferred_element_type=F32)
        x1 = _gated_add(x_ref[rs, :], mod_ref[0, 2], _rms(y, g_ref[1:2, :]))
        x1_sc[r] = x1
        h_sc[r] = _modulate(_rms(x1, g_ref[2:3, :]), mod_ref[0, 3], mod_ref[0, 4]).astype(BF16)

    def mlp(r):
        acc = jnp.zeros((half_rows, D_MODEL), F32)
        for c in range(D_FF // FF_CHUNK):
            hid = jnp.dot(h_sc[r], w1_ref[:, c * FF_CHUNK:(c + 1) * FF_CHUNK], preferred_element_type=F32)
            hid = jnp.square(jnp.maximum(hid, 0.0)).astype(BF16)
            acc = acc + jnp.dot(hid, w2_ref[c * FF_CHUNK:(c + 1) * FF_CHUNK, :], preferred_element_type=F32)
        acc_sc[r] = acc

    def tail(r):
        out = _gated_add(x1_sc[r], mod_ref[0, 5], _rms(acc_sc[r], g_ref[3:4, :]))
        if lru:
            _from_time_major(out, tout_sc, o_ref, r * half_t, half_t)
        else:
            o_ref[r * half_rows:(r + 1) * half_rows, :] = out

    head(0)
    head(1)
    mlp(0)
    tail(0)
    mlp(1)
    tail(1)


def _post_call(x, fronts, mod, g, w_front, w1, w2, layer, rows_per_group, lru):
    tm = MLP_TILE if (not lru and x.shape[0] >= 8 * MLP_TILE) else TOKEN_TILE
    tiles_per_group = rows_per_group // tm
    layer_spec = lambda shape: pl.BlockSpec((None,) + shape[1:], lambda i: (layer, 0, 0),
                                            pipeline_mode=pl.Buffered(1))
    row = lambda i: (i, 0)
    n = x.shape[0]
    x_spec = pl.BlockSpec((tm, D_MODEL), row)
    if lru:
        out_spec = pl.BlockSpec((BATCH, tm // BATCH, D_MODEL), lambda i: (0, i, 0))
        out_shape = jax.ShapeDtypeStruct((BATCH, n // BATCH, D_MODEL), F32)
        scratch = [pltpu.VMEM((N_SLAB, tm, LANES), F32)]
    else:
        out_spec, out_shape = x_spec, jax.ShapeDtypeStruct(x.shape, F32)
        scratch = []
    scratch = [
        pltpu.VMEM((2, tm // 2, D_MODEL), F32),
        pltpu.VMEM((2, tm // 2, D_MODEL), BF16),
        pltpu.VMEM((2, tm // 2, D_MODEL), F32),
    ] + scratch
    in_specs = [x_spec]
    in_specs += [pl.BlockSpec((tm, f.shape[1]), row) for f in fronts]
    in_specs += [
        pl.BlockSpec((1, N_MOD, SUBLANES, D_MODEL), lambda i: (i // tiles_per_group, 0, 0, 0)),
        _const_spec((4, D_MODEL)),
        _const_spec(w_front.shape),
        layer_spec(w1.shape),
        layer_spec(w2.shape),
    ]
    return pl.pallas_call(
        functools.partial(_post_kernel, lru=lru),
        grid=(n // tm,),
        in_specs=in_specs,
        out_specs=out_spec,
        out_shape=out_shape,
        scratch_shapes=scratch,
        compiler_params=_params(("parallel",)),
        name="lru_out_mlp" if lru else "attn_out_mlp",
    )(x, *fronts, mod, g, w_front, w1, w2)


def _lru_in_kernel(*refs, need_gate):
    if need_gate:
        (x_ref, xp_ref, xn_ref, mod_ref, g_ref, w_ref, cw_ref, cb_ref, gate_ref, u_ref, xt_ref,
         v_sc, h_sc, t_sc, tp_sc, tn_sc) = refs
    else:
        x_ref, xp_ref, xn_ref, mod_ref, g_ref, w_ref, cw_ref, cb_ref, u_ref, v_sc, h_sc, t_sc, tp_sc, tn_sc = refs
    i = pl.program_id(0)
    n = pl.num_programs(0)
    rows = x_ref.shape[0] * x_ref.shape[1]
    half = rows // 2
    half_t = x_ref.shape[1] // 2
    s8 = SUBLANES

    def pre(x):
        return _modulate(_rms(x, g_ref[0:1, :]), mod_ref[0, 0], mod_ref[0, 1]).astype(BF16)

    h_sc[0, 0:HALO] = pre(_to_time_major(xp_ref, tp_sc)[SUBLANES * SUBLANES - HALO:])
    x_halves = [_to_time_major(x_ref, t_sc, r * half_t, half_t) for r in range(2)]
    if need_gate:
        for r in range(2):
            xt_ref[r * half:(r + 1) * half, :] = x_halves[r]
    h_sc[0, HALO:] = pre(x_halves[0])
    h_sc[1, 0:half] = pre(x_halves[1])
    h_sc[1, half:] = pre(_to_time_major(xn_ref, tn_sc)[:HALO])
    ext = half + HALO
    for r in range(2):
        v_sc[r * ext:(r + 1) * ext] = jnp.dot(h_sc[r], w_ref[:, D_RNN:], preferred_element_type=F32)
    v_sc[0:HALO] = v_sc[0:HALO] * (i > 0).astype(F32)
    v_sc[HALO + rows:HALO + rows + s8] = v_sc[HALO + rows:HALO + rows + s8] * (i < n - 1).astype(F32)
    for r in range(2):
        if need_gate:
            h_r = h_sc[0, HALO:] if r == 0 else h_sc[1, 0:half]
            gate_ref[r * half:(r + 1) * half, :] = jax.nn.gelu(
                jnp.dot(h_r, w_ref[:, :D_RNN], preferred_element_type=F32)).astype(BF16)
        base = HALO + r * half
        u_ref[r * half:(r + 1) * half, :] = (
            cb_ref[...]
            + cw_ref[0:1, :] * v_sc[base - 2 * s8:base - 2 * s8 + half]
            + cw_ref[1:2, :] * v_sc[base - s8:base - s8 + half]
            + cw_ref[2:3, :] * v_sc[base:base + half]
            + cw_ref[3:4, :] * v_sc[base + s8:base + s8 + half]).astype(u_ref.dtype)


def _lru_in_call(x3, mod, g, w_in, conv_w, conv_b, need_gate):
    t_total = x3.shape[1]
    n = BATCH * t_total
    tm = LRU_IN_TILE
    nt = tm // BATCH
    row = lambda i: (i, 0)
    per_tile = nt // SUBLANES
    last = t_total // SUBLANES - 1
    halo_spec = lambda f: pl.BlockSpec((BATCH, SUBLANES, D_MODEL), f)
    out_specs = [pl.BlockSpec((tm, D_RNN), row), pl.BlockSpec((tm, D_RNN), row), pl.BlockSpec((tm, D_MODEL), row)]
    out_shape = [jax.ShapeDtypeStruct((n, D_RNN), BF16), jax.ShapeDtypeStruct((n, D_RNN), U_DTYPE),
                 jax.ShapeDtypeStruct((n, D_MODEL), F32)]
    if not need_gate:
        out_specs, out_shape = out_specs[1:2], out_shape[1:2]
    return pl.pallas_call(
        functools.partial(_lru_in_kernel, need_gate=need_gate),
        grid=(n // tm,),
        in_specs=[
            pl.BlockSpec((BATCH, nt, D_MODEL), lambda i: (0, i, 0)),
            halo_spec(lambda i: (0, jnp.maximum(i * per_tile - 1, 0), 0)),
            halo_spec(lambda i: (0, jnp.minimum((i + 1) * per_tile, last), 0)),
            _const_spec((1, N_MOD, SUBLANES, D_MODEL)),
            _const_spec((4, D_MODEL)),
            _const_spec((D_MODEL, 2 * D_RNN)),
            _const_spec((CONV_W, D_RNN)),
            _const_spec((1, D_RNN)),
        ],
        out_specs=out_specs,
        out_shape=out_shape,
        scratch_shapes=[
            pltpu.VMEM((tm + 2 * HALO, D_RNN), F32),
            pltpu.VMEM((2, tm // 2 + HALO, D_MODEL), BF16),
            pltpu.VMEM((N_SLAB, tm, LANES), F32),
            pltpu.VMEM((N_SLAB, SUBLANES * SUBLANES, LANES), F32),
            pltpu.VMEM((N_SLAB, SUBLANES * SUBLANES, LANES), F32),
        ],
        compiler_params=_params(("parallel",)),
        name="lru_in",
    )(x3, x3, x3, mod, g, w_in, conv_w, conv_b)


def _scan_kernel(uf_ref, ub_ref, h0_ref, wa_ref, ba_ref, wi_ref, bi_ref, lam_ref, yf_ref, yb_ref, ht_ref,
                 a_sc, bx_sc, h_sc):
    i = pl.program_id(0)
    n = pl.num_programs(0)
    rows = uf_ref.shape[0]
    nt = rows // SUBLANES
    s8 = SUBLANES

    @pl.when(i == 0)
    def _():
        h_sc[...] = h0_ref[...]

    for d, u_ref in enumerate((uf_ref, ub_ref)):
        for c in range(N_LRU_BLOCKS):
            cs = slice(c * LRU_BLOCK_W, (c + 1) * LRU_BLOCK_W)
            u16 = u_ref[:, cs].astype(BF16)
            u = u_ref[:, cs].astype(F32)
            ta = jnp.tanh(jnp.dot(u16, wa_ref[d, c], preferred_element_type=F32) + 0.5 * ba_ref[d, :, cs])
            ti = jnp.tanh(jnp.dot(u16, wi_ref[d, c], preferred_element_type=F32) + 0.5 * bi_ref[d, :, cs])
            neg_lam = -lam_ref[d, :, cs]
            softplus = jnp.maximum(neg_lam, 0.0) + jnp.log1p(jnp.exp(-jnp.abs(neg_lam)))
            k = (-0.5 * LRU_C * LOG2E) * softplus
            a = jnp.exp2(k * ta + k)
            w = 1.0 - a * a
            root = w * lax.rsqrt(jnp.maximum(w, 1e-30))
            a_sc[d, :, cs] = a
            bx_sc[d, :, cs] = root * (ti * u + u)

    def step(t, carry):
        hf, hb = carry
        rf = pl.multiple_of(t * 2 * s8, 2 * s8)
        rb = pl.multiple_of((nt - 2 - 2 * t) * s8, 2 * s8)
        hf1 = a_sc[0, pl.ds(rf, s8), :] * hf + bx_sc[0, pl.ds(rf, s8), :]
        hf2 = a_sc[0, pl.ds(rf + s8, s8), :] * hf1 + bx_sc[0, pl.ds(rf + s8, s8), :]
        yf_ref[pl.ds(rf, 2 * s8), :] = jnp.concatenate([hf1, hf2], axis=0).astype(yf_ref.dtype)
        hb1 = a_sc[1, pl.ds(rb + s8, s8), :] * hb + bx_sc[1, pl.ds(rb + s8, s8), :]
        hb2 = a_sc[1, pl.ds(rb, s8), :] * hb1 + bx_sc[1, pl.ds(rb, s8), :]
        yb_ref[pl.ds(rb, 2 * s8), :] = jnp.concatenate([hb2, hb1], axis=0).astype(yb_ref.dtype)
        return hf2, hb2

    hf, hb = lax.fori_loop(0, nt // 2, step, (h_sc[0], h_sc[1]), unroll=4)
    h_sc[0] = hf
    h_sc[1] = hb

    @pl.when(i == n - 1)
    def _():
        ht_ref[...] = h_sc[...]


def _scan_call(u2, h0, w_a, b_a, w_i, b_i, lam):
    rows_total = u2.shape[0]
    rows = SCAN_T * SUBLANES
    n = rows_total // rows
    w = D_RNN
    fwd = lambda i: (i, 0)
    bwd = lambda i: (n - 1 - i, 0)
    return pl.pallas_call(
        _scan_kernel,
        grid=(n,),
        in_specs=[
            pl.BlockSpec((rows, w), fwd),
            pl.BlockSpec((rows, w), bwd),
            _const_spec((2, SUBLANES, w)),
            _const_spec((2, N_LRU_BLOCKS, LRU_BLOCK_W, LRU_BLOCK_W)),
            _const_spec((2, 1, w)),
            _const_spec((2, N_LRU_BLOCKS, LRU_BLOCK_W, LRU_BLOCK_W)),
            _const_spec((2, 1, w)),
            _const_spec((2, 1, w)),
        ],
        out_specs=[
            pl.BlockSpec((rows, w), fwd),
            pl.BlockSpec((rows, w), bwd),
            pl.BlockSpec((2, SUBLANES, w), lambda i: (0, 0, 0)),
        ],
        out_shape=[
            jax.ShapeDtypeStruct((rows_total, w), Y_DTYPE),
            jax.ShapeDtypeStruct((rows_total, w), Y_DTYPE),
            jax.ShapeDtypeStruct((2, SUBLANES, w), F32),
        ],
        scratch_shapes=[
            pltpu.VMEM((2, rows, w), F32),
            pltpu.VMEM((2, rows, w), F32),
            pltpu.VMEM((2, SUBLANES, w), F32),
        ],
        compiler_params=_params(("arbitrary",)),
        name="lru_scan",
    )(u2, u2, h0, w_a, b_a, w_i, b_i, lam)


def _rope_tables():
    t = np.arange(SEQ)
    row = (t // GRID_W).astype(np.float64)
    col = (t % GRID_W).astype(np.float64)
    half = HEAD_DIM // 2
    inv = ROPE_BASE ** (-np.arange(0, half, 2, dtype=np.float64) / half)
    ang_r = row[:, None] * inv[None, :]
    ang_c = col[:, None] * inv[None, :]
    ang = np.concatenate([ang_r, ang_r, ang_c, ang_c], axis=-1)
    ang = np.tile(ang, (1, LANES // HEAD_DIM))
    low = (np.arange(LANES) % 32) < 16
    sin = np.sin(ang)
    tables = (np.cos(ang), np.where(low, -sin, 0.0), np.where(low, 0.0, sin))
    return tuple(jnp.asarray(a, dtype=F32) for a in tables)


def kernel(x, c, ctx, c_ctx, ada_w, ada_b, norm_g, mlp_w1, mlp_w2, attn_w_qkv, attn_w_o, attn_sink,
           lru_w_in, lru_conv_w, lru_conv_b, lru_w_a, lru_b_a, lru_w_i, lru_b_i, lru_lam, lru_w_out):
    n_lat = BATCH * SEQ
    n_ctx = BATCH * CTX_LEN

    c16 = jnp.zeros((16, D_MODEL), F32).at[:BATCH].set(c).at[BATCH].set(c_ctx)
    mods = _mod_call(c16, ada_w, ada_b).reshape(2, 16, N_MOD, D_MODEL)

    def slab_bmajor(m):
        return jnp.broadcast_to(m[:, :, None, :], (BATCH, N_MOD, SUBLANES, D_MODEL))

    def slab_ctx(m):
        return jnp.broadcast_to(m[None, :, None, :], (1, N_MOD, SUBLANES, D_MODEL))

    mod_x0 = slab_bmajor(mods[0, :BATCH])
    mod_c0 = slab_ctx(mods[0, BATCH])
    w_qkv = attn_w_qkv[0]
    w_qkv = jnp.concatenate([w_qkv[:, :D_Q] * (HEAD_DIM ** -0.5 * LOG2E), w_qkv[:, D_Q:]], axis=1).astype(BF16)
    sink2 = attn_sink[0] * LOG2E
    w_o = attn_w_o[0].astype(BF16)
    w1_all, w2_all = mlp_w1.astype(BF16), mlp_w2.astype(BF16)
    g0 = norm_g[0]

    x2 = x.reshape(n_lat, D_MODEL)
    c2 = ctx.reshape(n_ctx, D_MODEL)
    q, k, v = _qkv_call(x2, mod_x0, g0, w_qkv, _rope_tables(), SEQ)
    qc, kc, vc = _qkv_call(c2, mod_c0, g0, w_qkv, None, n_ctx)
    kc3 = kc.reshape(BATCH, CTX_LEN, D_K2)
    att = _attn_call(sink2, q, k.reshape(BATCH, SEQ, D_K2), v, kc3, vc)
    att_c = _ctx_attn_call(sink2, qc, kc3, vc)
    x2 = _post_call(x2, [att], mod_x0, g0, w_o, w1_all, w2_all, 0, SEQ, lru=False)
    c2 = _post_call(c2, [att_c], mod_c0, g0, w_o, w1_all, w2_all, 0, n_ctx, lru=False)

    x3 = x2.reshape(BATCH, SEQ, D_MODEL)
    c3 = c2.reshape(BATCH, CTX_LEN, D_MODEL)
    mod_x1 = mods[1, :BATCH].transpose(1, 0, 2)[None]
    mod_c1 = slab_ctx(mods[1, BATCH])
    g1 = norm_g[1]
    w_in = lru_w_in[0].astype(BF16)
    conv_w = 0.5 * lru_conv_w[0]
    conv_b = 0.5 * lru_conv_b[0].reshape(1, D_RNN)
    w_a, w_i = lru_w_a[0].astype(BF16), lru_w_i[0].astype(BF16)
    b_a, b_i = lru_b_a[0].reshape(2, 1, D_RNN), lru_b_i[0].reshape(2, 1, D_RNN)
    lam = lru_lam[0].reshape(2, 1, D_RNN)
    scan = functools.partial(_scan_call, w_a=w_a, b_a=b_a, w_i=w_i, b_i=b_i, lam=lam)

    (u_c,) = _lru_in_call(c3, mod_c1, g1, w_in, conv_w, conv_b, need_gate=False)
    _, _, h_ctx = scan(u_c, jnp.zeros((2, SUBLANES, D_RNN), F32))
    gate_x, u_x, x_t = _lru_in_call(x3, mod_x1, g1, w_in, conv_w, conv_b, need_gate=True)
    yf, yb, _ = scan(u_x, h_ctx)
    return _post_call(x_t, [gate_x, yf, yb], mod_x1, g1, lru_w_out[0].astype(BF16),
                      w1_all, w2_all, 1, n_lat, lru=True)
```

```python
import functools

import jax
import jax.numpy as jnp
import numpy as np
from jax import lax
from jax.experimental import pallas as pl
from jax.experimental.pallas import tpu as pltpu

D_MODEL = 1024
BATCH = 8
SEQ = 2048
GRID_W = 64
CTX_LEN = 256
HEAD_DIM = 64
N_HEADS = 16
N_KV_HEADS = 4
GQA_GROUP = N_HEADS // N_KV_HEADS
WINDOW = 128
BLOCK = 128
ROPE_BASE = 10000.0
D_RNN = 1280
LRU_BLOCK_W = 256
N_LRU_BLOCKS = D_RNN // LRU_BLOCK_W
CONV_W = 4
LRU_C = 8.0
D_FF = 4 * D_MODEL
N_MOD = 6
EPS = 1e-6
NEG_INF = -1e30

D_Q = N_HEADS * HEAD_DIM
D_KV = N_KV_HEADS * HEAD_DIM
D_K2 = 2 * D_KV
LANES = 128
SUBLANES = 8
N_SLAB = D_MODEL // LANES
TOKEN_TILE = 512
QKV_TILE = 1024
MLP_TILE = 1024
LRU_IN_TILE = 512
FF_CHUNK = 1024
ATTN_Q_PER_STEP = 16
SCAN_T = 128
HALO = 16
U_DTYPE = jnp.float32
Y_DTYPE = jnp.bfloat16
LOG2E = 1.4426950408889634
VMEM_LIMIT = 60 * 1024 * 1024

F32 = jnp.float32
BF16 = jnp.bfloat16


def _rms(x, g):
    ms = jnp.mean(x * x, axis=-1, keepdims=True)
    return x * lax.rsqrt(ms + EPS) * g


def _slab(x):
    return x.reshape(x.shape[0] // SUBLANES, SUBLANES, x.shape[1])


def _modulate(h, shift8, scale8):
    out = _slab(h) * (1.0 + scale8)[None] + shift8[None]
    return out.reshape(h.shape)


def _gated_add(x, gate8, y):
    out = _slab(x) + gate8[None] * _slab(y)
    return out.reshape(x.shape)


def _const_spec(shape):
    n = len(shape)
    return pl.BlockSpec(shape, lambda *_: (0,) * n, pipeline_mode=pl.Buffered(1))


def _params(sem):
    return pltpu.CompilerParams(dimension_semantics=sem, vmem_limit_bytes=VMEM_LIMIT)


def _mod_kernel(c_ref, w_ref, b_ref, o_ref):
    s = jax.nn.silu(c_ref[...]).astype(BF16)
    o_ref[0] = jnp.dot(s, w_ref[0].astype(BF16), preferred_element_type=F32) + b_ref[0]


def _mod_call(c16, ada_w, ada_b):
    depth = ada_w.shape[0]
    nt = 2048
    return pl.pallas_call(
        _mod_kernel,
        grid=(depth, N_MOD * D_MODEL // nt),
        in_specs=[
            pl.BlockSpec((16, D_MODEL), lambda l, j: (0, 0)),
            pl.BlockSpec((1, D_MODEL, nt), lambda l, j: (l, 0, j)),
            pl.BlockSpec((1, 1, nt), lambda l, j: (l, 0, j)),
        ],
        out_specs=pl.BlockSpec((1, 16, nt), lambda l, j: (l, 0, j)),
        out_shape=jax.ShapeDtypeStruct((depth, 16, N_MOD * D_MODEL), F32),
        compiler_params=_params(("arbitrary", "arbitrary")),
        name="adaln_mod",
    )(c16, ada_w, ada_b.reshape(depth, 1, N_MOD * D_MODEL))


def _qkv_kernel(*refs, rope):
    if rope:
        x_ref, mod_ref, g_ref, w_ref, cos_ref, sa_ref, sb_ref, q_ref, k_ref, v_ref, h_sc, y_sc = refs
    else:
        x_ref, mod_ref, g_ref, w_ref, q_ref, k_ref, v_ref, h_sc, y_sc = refs
    half = x_ref.shape[0] // 2
    low = lax.broadcasted_iota(jnp.int32, (half, LANES), 1) < HEAD_DIM
    for r in range(2):
        rs = slice(r * half, (r + 1) * half)
        h_sc[r] = _modulate(_rms(x_ref[rs, :], g_ref[0:1, :]), mod_ref[0, 0], mod_ref[0, 1]).astype(BF16)
    for r in range(2):
        y_sc[r] = jnp.dot(h_sc[r], w_ref[...], preferred_element_type=F32)
    for r in range(2):
        rs = slice(r * half, (r + 1) * half)
        if rope:
            cos, sa, sb = cos_ref[rs, :], sa_ref[rs, :], sb_ref[rs, :]
        for c in range((D_Q + D_KV) // LANES):
            yc = y_sc[r, :, c * LANES:(c + 1) * LANES]
            if rope:
                yc = yc * cos + pltpu.roll(yc, LANES - 16, 1) * sa + pltpu.roll(yc, 16, 1) * sb
            if c < D_Q // LANES:
                q_ref[rs, c * LANES:(c + 1) * LANES] = yc.astype(BF16)
            else:
                c2 = 2 * (c - D_Q // LANES)
                swapped = pltpu.roll(yc, HEAD_DIM, 1)
                k_ref[rs, c2 * LANES:(c2 + 1) * LANES] = jnp.where(low, yc, swapped).astype(BF16)
                k_ref[rs, (c2 + 1) * LANES:(c2 + 2) * LANES] = jnp.where(low, swapped, yc).astype(BF16)
        for blk in range(half // BLOCK):
            v_ref[r * (half // BLOCK) + blk] = y_sc[r, blk * BLOCK:(blk + 1) * BLOCK, D_Q + D_KV:].T.astype(BF16)


def _qkv_call(x2, mod, g, w_qkv, tables, rows_per_group):
    n = x2.shape[0]
    rope = tables is not None
    tm = QKV_TILE if rope else TOKEN_TILE
    tiles_per_group = rows_per_group // tm
    in_specs = [
        pl.BlockSpec((tm, D_MODEL), lambda i: (i, 0)),
        pl.BlockSpec((1, N_MOD, SUBLANES, D_MODEL), lambda i: (i // tiles_per_group, 0, 0, 0)),
        _const_spec((4, D_MODEL)),
        _const_spec((D_MODEL, D_Q + 2 * D_KV)),
    ]
    args = [x2, mod, g, w_qkv]
    if rope:
        nt = SEQ // tm
        in_specs += [pl.BlockSpec((tm, LANES), lambda i: (i % nt, 0))] * 3
        args += list(tables)
    return pl.pallas_call(
        functools.partial(_qkv_kernel, rope=rope),
        grid=(n // tm,),
        in_specs=in_specs,
        out_specs=[
            pl.BlockSpec((tm, D_Q), lambda i: (i, 0)),
            pl.BlockSpec((tm, D_K2), lambda i: (i, 0)),
            pl.BlockSpec((tm // BLOCK, D_KV, BLOCK), lambda i: (i, 0, 0)),
        ],
        out_shape=[
            jax.ShapeDtypeStruct((n, D_Q), BF16),
            jax.ShapeDtypeStruct((n, D_K2), BF16),
            jax.ShapeDtypeStruct((n // BLOCK, D_KV, BLOCK), BF16),
        ],
        scratch_shapes=[
            pltpu.VMEM((2, tm // 2, D_MODEL), BF16),
            pltpu.VMEM((2, tm // 2, D_Q + 2 * D_KV), F32),
        ],
        compiler_params=_params(("parallel",)),
        name="qkv_rope" if rope else "qkv_ctx",
    )(*args)


VT_ROWS = HEAD_DIM + 16


def _attn_kernel(*refs, local, q_per_step):
    if local:
        sink_ref, q_ref, k_ref, v_ref, kc_ref, vc_ref, o_ref, s_sc, p_sc, bias_sc = refs
    else:
        sink_ref, q_ref, kc_ref, vc_ref, o_ref, s_sc, p_sc, bias_sc = refs
    j = pl.program_id(1)
    seq_blocks = SEQ // BLOCK if local else 0
    ctx_blocks = CTX_LEN // BLOCK
    n_keys = s_sc.shape[1]
    pad_row = lax.broadcasted_iota(jnp.int32, (VT_ROWS - HEAD_DIM, n_keys), 0)
    vt_pad = jnp.where(pad_row == 0, 1.0, 0.0).astype(BF16)

    nt = (((1,), (1,)), ((), ()))
    n_band = 3
    n_chunks = D_Q // LANES
    lane = lax.broadcasted_iota(jnp.int32, (BLOCK, LANES), 1)
    first_head = lax.broadcasted_iota(jnp.int32, (1, 2 * BLOCK), 1) < BLOCK

    def block_params(qb):
        jq = j * q_per_step + qb
        rows = pl.ds(pl.multiple_of(qb * BLOCK, BLOCK), BLOCK)
        if not local:
            return rows, None, None, None, qb // ctx_blocks
        blk0 = jnp.clip(jq - 1, 0, seq_blocks - n_band)
        return rows, blk0, pl.multiple_of(blk0 * BLOCK, BLOCK), qb % 2, 0

    def store_bias(qb):
        if local:
            jq = j * q_per_step + qb
            start = jnp.clip(jq - 1, 0, seq_blocks - n_band) * BLOCK
            kpos = start + lax.broadcasted_iota(jnp.int32, (n_band * BLOCK, BLOCK), 0)
            qpos = jq * BLOCK + lax.broadcasted_iota(jnp.int32, (n_band * BLOCK, BLOCK), 1)
            bias_sc[qb % 2] = jnp.where(jnp.abs(kpos - qpos) <= WINDOW, 0.0, NEG_INF).astype(F32)

    def scores(c, params):
        rows, _, start, slot, cb = params
        kcols = slice((c // 2) * LANES, (c // 2 + 1) * LANES)
        qc = q_ref[rows, c * LANES:(c + 1) * LANES]
        zero = jnp.zeros_like(qc)
        q2 = jnp.concatenate([jnp.where(lane < HEAD_DIM, qc, zero), jnp.where(lane < HEAD_DIM, zero, qc)], axis=0)
        s_sc[c % 2, 0:CTX_LEN] = lax.dot_general(kc_ref[cb, :, kcols], q2, nt, preferred_element_type=F32)
        if local:
            s_band = lax.dot_general(k_ref[0, pl.ds(start, n_band * BLOCK), kcols], q2, nt,
                                     preferred_element_type=F32)
            for hd in range(2):
                s_sc[c % 2, CTX_LEN:, hd * BLOCK:(hd + 1) * BLOCK] = (
                    s_band[:, hd * BLOCK:(hd + 1) * BLOCK] + bias_sc[slot])

    def softmax(c):
        s = s_sc[c % 2]
        sink_row = jnp.where(first_head, sink_ref[2 * c], sink_ref[2 * c + 1])
        m = jnp.maximum(jnp.max(s, axis=0, keepdims=True), sink_row)
        p_sc[c % 2] = jnp.exp2(s - m).astype(BF16)
        return jnp.exp2(sink_row - m)

    def values(c, params, sink_term):
        rows, blk0, _, _, cb = params
        hrows = slice((c // 2) * HEAD_DIM, (c // 2 + 1) * HEAD_DIM)
        vt = [vc_ref[cb * ctx_blocks + i, hrows, :] for i in range(ctx_blocks)]
        if local:
            vt_band = v_ref[pl.ds(blk0, n_band), hrows, :]
            vt += [vt_band[i] for i in range(n_band)]
        vt_aug = jnp.concatenate([jnp.concatenate(vt, axis=1), vt_pad], axis=0)
        acc = jnp.dot(vt_aug, p_sc[c % 2], preferred_element_type=F32)
        out_t = acc[0:HEAD_DIM] * (1.0 / (acc[HEAD_DIM:HEAD_DIM + 1] + sink_term))
        both = jnp.concatenate([out_t[:, :BLOCK], out_t[:, BLOCK:]], axis=0)
        o_ref[rows, c * LANES:(c + 1) * LANES] = both.T.astype(o_ref.dtype)

    def query_block(qb, sink_term0):
        cur = block_params(qb)
        nxt = block_params(jnp.minimum(qb + 1, q_per_step - 1))
        store_bias(qb + 1)
        sink_terms = {0: sink_term0}
        for c in range(n_chunks):
            if c + 2 < n_chunks:
                scores(c + 2, cur)
            else:
                scores(c + 2 - n_chunks, nxt)
            sink_terms[c + 1] = softmax((c + 1) % n_chunks)
            values(c, cur, sink_terms[c])
        return sink_terms[n_chunks]

    first = block_params(0)
    store_bias(0)
    scores(0, first)
    scores(1, first)
    lax.fori_loop(0, q_per_step, query_block, softmax(0))


def _attn_scratch(n_keys):
    return [
        pltpu.VMEM((2, n_keys, 2 * BLOCK), F32),
        pltpu.VMEM((2, n_keys, 2 * BLOCK), BF16),
        pltpu.VMEM((2, 3 * BLOCK, BLOCK), F32),
    ]


def _attn_call(sink2, q, k, vt, kc, vtc):
    nb = SEQ // BLOCK
    nbc = CTX_LEN // BLOCK
    qps = ATTN_Q_PER_STEP
    steps = nb // qps
    return pl.pallas_call(
        functools.partial(_attn_kernel, local=True, q_per_step=qps),
        grid=(BATCH, steps),
        in_specs=[
            pl.BlockSpec(memory_space=pltpu.SMEM),
            pl.BlockSpec((qps * BLOCK, D_Q), lambda b, j: (b * steps + j, 0)),
            pl.BlockSpec((1, SEQ, D_K2), lambda b, j: (b, 0, 0)),
            pl.BlockSpec((nb, D_KV, BLOCK), lambda b, j: (b, 0, 0)),
            pl.BlockSpec((1, CTX_LEN, D_K2), lambda b, j: (b, 0, 0)),
            pl.BlockSpec((nbc, D_KV, BLOCK), lambda b, j: (b, 0, 0)),
        ],
        out_specs=pl.BlockSpec((qps * BLOCK, D_Q), lambda b, j: (b * steps + j, 0)),
        out_shape=jax.ShapeDtypeStruct((BATCH * SEQ, D_Q), BF16),
        scratch_shapes=_attn_scratch(CTX_LEN + 3 * BLOCK),
        compiler_params=_params(("parallel", "arbitrary")),
        name="band_attn",
    )(sink2, q, k, vt, kc, vtc)


def _ctx_attn_call(sink2, qc, kc, vtc):
    nb = BATCH * CTX_LEN // BLOCK
    whole = lambda shape: pl.BlockSpec(shape, lambda b, j: (0,) * len(shape))
    return pl.pallas_call(
        functools.partial(_attn_kernel, local=False, q_per_step=nb),
        grid=(1, 1),
        in_specs=[
            pl.BlockSpec(memory_space=pltpu.SMEM),
            whole((BATCH * CTX_LEN, D_Q)),
            whole((BATCH, CTX_LEN, D_K2)),
            whole((nb, D_KV, BLOCK)),
        ],
        out_specs=whole((BATCH * CTX_LEN, D_Q)),
        out_shape=jax.ShapeDtypeStruct((BATCH * CTX_LEN, D_Q), BF16),
        scratch_shapes=_attn_scratch(CTX_LEN),
        compiler_params=_params(("arbitrary", "arbitrary")),
        name="ctx_attn",
    )(sink2, qc, kc, vtc)


def _to_time_major(src_ref, sc_ref, t0=0, nt=None):
    nt = src_ref.shape[1] if nt is None else nt
    r0 = t0 * SUBLANES
    for b in range(BATCH):
        for s in range(N_SLAB):
            sc_ref[s, pl.ds(r0 + b, nt, stride=SUBLANES), :] = src_ref[b, t0:t0 + nt, s * LANES:(s + 1) * LANES]
    return jnp.concatenate([sc_ref[s, r0:r0 + nt * SUBLANES, :] for s in range(N_SLAB)], axis=1)


def _from_time_major(val, sc_ref, dst_ref, t0, nt):
    r0 = t0 * SUBLANES
    for s in range(N_SLAB):
        sc_ref[s, r0:r0 + nt * SUBLANES, :] = val[:, s * LANES:(s + 1) * LANES]
    for b in range(BATCH):
        for s in range(N_SLAB):
            dst_ref[b, t0:t0 + nt, s * LANES:(s + 1) * LANES] = sc_ref[s, pl.ds(r0 + b, nt, stride=SUBLANES), :]


def _post_kernel(*refs, lru):
    if lru:
        (x_ref, gate_ref, yf_ref, yb_ref, mod_ref, g_ref, wf_ref, w1_ref, w2_ref, o_ref,
         x1_sc, h_sc, acc_sc, tout_sc) = refs
    else:
        x_ref, a_ref, mod_ref, g_ref, wf_ref, w1_ref, w2_ref, o_ref, x1_sc, h_sc, acc_sc = refs
    half_rows = x_ref.shape[0] // 2
    half_t = half_rows // BATCH

    def head(r):
        rs = slice(r * half_rows, (r + 1) * half_rows)
        if lru:
            front = (gate_ref[rs, :].astype(F32)
                     * (yf_ref[rs, :].astype(F32) + yb_ref[rs, :].astype(F32))).astype(BF16)
        else:
            front = a_ref[rs, :]
        y = jnp.dot(front, wf_ref[...], preferred_element_type=F32)
        x1 = _gated_add(x_ref[rs, :], mod_ref[0, 2], _rms(y, g_ref[1:2, :]))
        x1_sc[r] = x1
        h_sc[r] = _modulate(_rms(x1, g_ref[2:3, :]), mod_ref[0, 3], mod_ref[0, 4]).astype(BF16)

    def mlp(r):
        acc = jnp.zeros((half_rows, D_MODEL), F32)
        for c in range(D_FF // FF_CHUNK):
            hid = jnp.dot(h_sc[r], w1_ref[:, c * FF_CHUNK:(c + 1) * FF_CHUNK], preferred_element_type=F32)
            hid = jnp.square(jnp.maximum(hid, 0.0)).astype(BF16)
            acc = acc + jnp.dot(hid, w2_ref[c * FF_CHUNK:(c + 1) * FF_CHUNK, :], preferred_element_type=F32)
        acc_sc[r] = acc

    def tail(r):
        out = _gated_add(x1_sc[r], mod_ref[0, 5], _rms(acc_sc[r], g_ref[3:4, :]))
        if lru:
            _from_time_major(out, tout_sc, o_ref, r * half_t, half_t)
        else:
            o_ref[r * half_rows:(r + 1) * half_rows, :] = out

    head(0)
    head(1)
    mlp(0)
    tail(0)
    mlp(1)
    tail(1)


def _post_call(x, fronts, mod, g, w_front, w1, w2, layer, rows_per_group, lru):
    tm = MLP_TILE if (not lru and x.shape[0] >= 8 * MLP_TILE) else TOKEN_TILE
    tiles_per_group = rows_per_group // tm
    layer_spec = lambda shape: pl.BlockSpec((None,) + shape[1:], lambda i: (layer, 0, 0),
                                            pipeline_mode=pl.Buffered(1))
    row = lambda i: (i, 0)
    n = x.shape[0]
    x_spec = pl.BlockSpec((tm, D_MODEL), row)
    if lru:
        out_spec = pl.BlockSpec((BATCH, tm // BATCH, D_MODEL), lambda i: (0, i, 0))
        out_shape = jax.ShapeDtypeStruct((BATCH, n // BATCH, D_MODEL), F32)
        scratch = [pltpu.VMEM((N_SLAB, tm, LANES), F32)]
    else:
        out_spec, out_shape = x_spec, jax.ShapeDtypeStruct(x.shape, F32)
        scratch = []
    scratch = [
        pltpu.VMEM((2, tm // 2, D_MODEL), F32),
        pltpu.VMEM((2, tm // 2, D_MODEL), BF16),
        pltpu.VMEM((2, tm // 2, D_MODEL), F32),
    ] + scratch
    in_specs = [x_spec]
    in_specs += [pl.BlockSpec((tm, f.shape[1]), row) for f in fronts]
    in_specs += [
        pl.BlockSpec((1, N_MOD, SUBLANES, D_MODEL), lambda i: (i // tiles_per_group, 0, 0, 0)),
        _const_spec((4, D_MODEL)),
        _const_spec(w_front.shape),
        layer_spec(w1.shape),
        layer_spec(w2.shape),
    ]
    return pl.pallas_call(
        functools.partial(_post_kernel, lru=lru),
        grid=(n // tm,),
        in_specs=in_specs,
        out_specs=out_spec,
        out_shape=out_shape,
        scratch_shapes=scratch,
        compiler_params=pltpu.CompilerParams(
            dimension_semantics=("parallel",), vmem_limit_bytes=VMEM_LIMIT,
            allow_input_fusion=[False] * (len(in_specs) - 3) + [True] * 3),
        name="lru_out_mlp" if lru else "attn_out_mlp",
    )(x, *fronts, mod, g, w_front, w1, w2)


def _lru_in_kernel(*refs, need_gate):
    if need_gate:
        (x_ref, xp_ref, xn_ref, mod_ref, g_ref, w_ref, cw_ref, cb_ref, gate_ref, u_ref, xt_ref,
         v_sc, h_sc, t_sc, tp_sc, tn_sc) = refs
    else:
        x_ref, xp_ref, xn_ref, mod_ref, g_ref, w_ref, cw_ref, cb_ref, u_ref, v_sc, h_sc, t_sc, tp_sc, tn_sc = refs
    i = pl.program_id(0)
    n = pl.num_programs(0)
    rows = x_ref.shape[0] * x_ref.shape[1]
    half = rows // 2
    half_t = x_ref.shape[1] // 2
    s8 = SUBLANES

    def pre(x):
        return _modulate(_rms(x, g_ref[0:1, :]), mod_ref[0, 0], mod_ref[0, 1]).astype(BF16)

    h_sc[0, 0:HALO] = pre(_to_time_major(xp_ref, tp_sc)[SUBLANES * SUBLANES - HALO:])
    x_halves = [_to_time_major(x_ref, t_sc, r * half_t, half_t) for r in range(2)]
    if need_gate:
        for r in range(2):
            xt_ref[r * half:(r + 1) * half, :] = x_halves[r]
    h_sc[0, HALO:] = pre(x_halves[0])
    h_sc[1, 0:half] = pre(x_halves[1])
    h_sc[1, half:] = pre(_to_time_major(xn_ref, tn_sc)[:HALO])
    ext = half + HALO
    for r in range(2):
        v_sc[r * ext:(r + 1) * ext] = jnp.dot(h_sc[r], w_ref[:, D_RNN:], preferred_element_type=F32)
    v_sc[0:HALO] = v_sc[0:HALO] * (i > 0).astype(F32)
    v_sc[HALO + rows:HALO + rows + s8] = v_sc[HALO + rows:HALO + rows + s8] * (i < n - 1).astype(F32)
    for r in range(2):
        if need_gate:
            h_r = h_sc[0, HALO:] if r == 0 else h_sc[1, 0:half]
            gate_ref[r * half:(r + 1) * half, :] = jax.nn.gelu(
                jnp.dot(h_r, w_ref[:, :D_RNN], preferred_element_type=F32)).astype(BF16)
        base = HALO + r * half
        u_ref[r * half:(r + 1) * half, :] = (
            cb_ref[...]
            + cw_ref[0:1, :] * v_sc[base - 2 * s8:base - 2 * s8 + half]
            + cw_ref[1:2, :] * v_sc[base - s8:base - s8 + half]
            + cw_ref[2:3, :] * v_sc[base:base + half]
            + cw_ref[3:4, :] * v_sc[base + s8:base + s8 + half]).astype(u_ref.dtype)


def _lru_in_call(x3, mod, g, w_in, conv_w, conv_b, need_gate):
    t_total = x3.shape[1]
    n = BATCH * t_total
    tm = LRU_IN_TILE
    nt = tm // BATCH
    row = lambda i: (i, 0)
    per_tile = nt // SUBLANES
    last = t_total // SUBLANES - 1
    halo_spec = lambda f: pl.BlockSpec((BATCH, SUBLANES, D_MODEL), f)
    out_specs = [pl.BlockSpec((tm, D_RNN), row), pl.BlockSpec((tm, D_RNN), row), pl.BlockSpec((tm, D_MODEL), row)]
    out_shape = [jax.ShapeDtypeStruct((n, D_RNN), BF16), jax.ShapeDtypeStruct((n, D_RNN), U_DTYPE),
                 jax.ShapeDtypeStruct((n, D_MODEL), F32)]
    if not need_gate:
        out_specs, out_shape = out_specs[1:2], out_shape[1:2]
    return pl.pallas_call(
        functools.partial(_lru_in_kernel, need_gate=need_gate),
        grid=(n // tm,),
        in_specs=[
            pl.BlockSpec((BATCH, nt, D_MODEL), lambda i: (0, i, 0)),
            halo_spec(lambda i: (0, jnp.maximum(i * per_tile - 1, 0), 0)),
            halo_spec(lambda i: (0, jnp.minimum((i + 1) * per_tile, last), 0)),
            _const_spec((1, N_MOD, SUBLANES, D_MODEL)),
            _const_spec((4, D_MODEL)),
            _const_spec((D_MODEL, 2 * D_RNN)),
            _const_spec((CONV_W, D_RNN)),
            _const_spec((1, D_RNN)),
        ],
        out_specs=out_specs,
        out_shape=out_shape,
        scratch_shapes=[
            pltpu.VMEM((tm + 2 * HALO, D_RNN), F32),
            pltpu.VMEM((2, tm // 2 + HALO, D_MODEL), BF16),
            pltpu.VMEM((N_SLAB, tm, LANES), F32),
            pltpu.VMEM((N_SLAB, SUBLANES * SUBLANES, LANES), F32),
            pltpu.VMEM((N_SLAB, SUBLANES * SUBLANES, LANES), F32),
        ],
        compiler_params=_params(("parallel",)),
        name="lru_in",
    )(x3, x3, x3, mod, g, w_in, conv_w, conv_b)


def _scan_kernel(uf_ref, ub_ref, h0_ref, wa_ref, ba_ref, wi_ref, bi_ref, lam_ref, yf_ref, yb_ref, ht_ref,
                 a_sc, bx_sc, h_sc):
    i = pl.program_id(0)
    n = pl.num_programs(0)
    rows = uf_ref.shape[0]
    nt = rows // SUBLANES
    s8 = SUBLANES

    @pl.when(i == 0)
    def _():
        h_sc[...] = h0_ref[...]

    for d, u_ref in enumerate((uf_ref, ub_ref)):
        for c in range(N_LRU_BLOCKS):
            cs = slice(c * LRU_BLOCK_W, (c + 1) * LRU_BLOCK_W)
            u16 = u_ref[:, cs].astype(BF16)
            u = u_ref[:, cs].astype(F32)
            ta = jnp.tanh(jnp.dot(u16, wa_ref[d, c], preferred_element_type=F32) + 0.5 * ba_ref[d, :, cs])
            ti = jnp.tanh(jnp.dot(u16, wi_ref[d, c], preferred_element_type=F32) + 0.5 * bi_ref[d, :, cs])
            neg_lam = -lam_ref[d, :, cs]
            softplus = jnp.maximum(neg_lam, 0.0) + jnp.log1p(jnp.exp(-jnp.abs(neg_lam)))
            k = (-0.5 * LRU_C * LOG2E) * softplus
            a = jnp.exp2(k * ta + k)
            w = 1.0 - a * a
            root = w * lax.rsqrt(jnp.maximum(w, 1e-30))
            a_sc[d, :, cs] = a
            bx_sc[d, :, cs] = root * (ti * u + u)

    def step(t, carry):
        hf, hb = carry
        rf = pl.multiple_of(t * 2 * s8, 2 * s8)
        rb = pl.multiple_of((nt - 2 - 2 * t) * s8, 2 * s8)
        hf1 = a_sc[0, pl.ds(rf, s8), :] * hf + bx_sc[0, pl.ds(rf, s8), :]
        hf2 = a_sc[0, pl.ds(rf + s8, s8), :] * hf1 + bx_sc[0, pl.ds(rf + s8, s8), :]
        yf_ref[pl.ds(rf, 2 * s8), :] = jnp.concatenate([hf1, hf2], axis=0).astype(yf_ref.dtype)
        hb1 = a_sc[1, pl.ds(rb + s8, s8), :] * hb + bx_sc[1, pl.ds(rb + s8, s8), :]
        hb2 = a_sc[1, pl.ds(rb, s8), :] * hb1 + bx_sc[1, pl.ds(rb, s8), :]
        yb_ref[pl.ds(rb, 2 * s8), :] = jnp.concatenate([hb2, hb1], axis=0).astype(yb_ref.dtype)
        return hf2, hb2

    hf, hb = lax.fori_loop(0, nt // 2, step, (h_sc[0], h_sc[1]), unroll=2)
    h_sc[0] = hf
    h_sc[1] = hb

    @pl.when(i == n - 1)
    def _():
        ht_ref[...] = h_sc[...]


def _scan_call(u2, h0, w_a, b_a, w_i, b_i, lam):
    rows_total = u2.shape[0]
    rows = SCAN_T * SUBLANES
    n = rows_total // rows
    w = D_RNN
    fwd = lambda i: (i, 0)
    bwd = lambda i: (n - 1 - i, 0)
    return pl.pallas_call(
        _scan_kernel,
        grid=(n,),
        in_specs=[
            pl.BlockSpec((rows, w), fwd),
            pl.BlockSpec((rows, w), bwd),
            _const_spec((2, SUBLANES, w)),
            _const_spec((2, N_LRU_BLOCKS, LRU_BLOCK_W, LRU_BLOCK_W)),
            _const_spec((2, 1, w)),
            _const_spec((2, N_LRU_BLOCKS, LRU_BLOCK_W, LRU_BLOCK_W)),
            _const_spec((2, 1, w)),
            _const_spec((2, 1, w)),
        ],
        out_specs=[
            pl.BlockSpec((rows, w), fwd),
            pl.BlockSpec((rows, w), bwd),
            pl.BlockSpec((2, SUBLANES, w), lambda i: (0, 0, 0)),
        ],
        out_shape=[
            jax.ShapeDtypeStruct((rows_total, w), Y_DTYPE),
            jax.ShapeDtypeStruct((rows_total, w), Y_DTYPE),
            jax.ShapeDtypeStruct((2, SUBLANES, w), F32),
        ],
        scratch_shapes=[
            pltpu.VMEM((2, rows, w), F32),
            pltpu.VMEM((2, rows, w), F32),
            pltpu.VMEM((2, SUBLANES, w), F32),
        ],
        compiler_params=_params(("arbitrary",)),
        name="lru_scan",
    )(u2, u2, h0, w_a, b_a, w_i, b_i, lam)


def _rope_tables():
    t = np.arange(SEQ)
    row = (t // GRID_W).astype(np.float64)
    col = (t % GRID_W).astype(np.float64)
    half = HEAD_DIM // 2
    inv = ROPE_BASE ** (-np.arange(0, half, 2, dtype=np.float64) / half)
    ang_r = row[:, None] * inv[None, :]
    ang_c = col[:, None] * inv[None, :]
    ang = np.concatenate([ang_r, ang_r, ang_c, ang_c], axis=-1)
    ang = np.tile(ang, (1, LANES // HEAD_DIM))
    low = (np.arange(LANES) % 32) < 16
    sin = np.sin(ang)
    tables = (np.cos(ang), np.where(low, -sin, 0.0), np.where(low, 0.0, sin))
    return tuple(jnp.asarray(a, dtype=F32) for a in tables)


def kernel(x, c, ctx, c_ctx, ada_w, ada_b, norm_g, mlp_w1, mlp_w2, attn_w_qkv, attn_w_o, attn_sink,
           lru_w_in, lru_conv_w, lru_conv_b, lru_w_a, lru_b_a, lru_w_i, lru_b_i, lru_lam, lru_w_out):
    n_lat = BATCH * SEQ
    n_ctx = BATCH * CTX_LEN

    c16 = jnp.zeros((16, D_MODEL), F32).at[:BATCH].set(c).at[BATCH].set(c_ctx)
    mods = _mod_call(c16, ada_w, ada_b).reshape(2, 16, N_MOD, D_MODEL)

    def slab_bmajor(m):
        return jnp.broadcast_to(m[:, :, None, :], (BATCH, N_MOD, SUBLANES, D_MODEL))

    def slab_ctx(m):
        return jnp.broadcast_to(m[None, :, None, :], (1, N_MOD, SUBLANES, D_MODEL))

    mod_x0 = slab_bmajor(mods[0, :BATCH])
    mod_c0 = slab_ctx(mods[0, BATCH])
    w_qkv = attn_w_qkv[0]
    w_qkv = jnp.concatenate([w_qkv[:, :D_Q] * (HEAD_DIM ** -0.5 * LOG2E), w_qkv[:, D_Q:]], axis=1).astype(BF16)
    sink2 = attn_sink[0] * LOG2E
    w_o = attn_w_o[0].astype(BF16)
    w1_all, w2_all = mlp_w1.astype(BF16), mlp_w2.astype(BF16)
    g0 = norm_g[0]

    x2 = x.reshape(n_lat, D_MODEL)
    c2 = ctx.reshape(n_ctx, D_MODEL)
    q, k, v = _qkv_call(x2, mod_x0, g0, w_qkv, _rope_tables(), SEQ)
    qc, kc, vc = _qkv_call(c2, mod_c0, g0, w_qkv, None, n_ctx)
    kc3 = kc.reshape(BATCH, CTX_LEN, D_K2)
    att = _attn_call(sink2, q, k.reshape(BATCH, SEQ, D_K2), v, kc3, vc)
    att_c = _ctx_attn_call(sink2, qc, kc3, vc)
    x2 = _post_call(x2, [att], mod_x0, g0, w_o, w1_all, w2_all, 0, SEQ, lru=False)
    c2 = _post_call(c2, [att_c], mod_c0, g0, w_o, w1_all, w2_all, 0, n_ctx, lru=False)

    x3 = x2.reshape(BATCH, SEQ, D_MODEL)
    c3 = c2.reshape(BATCH, CTX_LEN, D_MODEL)
    mod_x1 = mods[1, :BATCH].transpose(1, 0, 2)[None]
    mod_c1 = slab_ctx(mods[1, BATCH])
    g1 = norm_g[1]
    w_in = lru_w_in[0].astype(BF16)
    conv_w = 0.5 * lru_conv_w[0]
    conv_b = 0.5 * lru_conv_b[0].reshape(1, D_RNN)
    w_a, w_i = lru_w_a[0].astype(BF16), lru_w_i[0].astype(BF16)
    b_a, b_i = lru_b_a[0].reshape(2, 1, D_RNN), lru_b_i[0].reshape(2, 1, D_RNN)
    lam = lru_lam[0].reshape(2, 1, D_RNN)
    scan = functools.partial(_scan_call, w_a=w_a, b_a=b_a, w_i=w_i, b_i=b_i, lam=lam)

    (u_c,) = _lru_in_call(c3, mod_c1, g1, w_in, conv_w, conv_b, need_gate=False)
    _, _, h_ctx = scan(u_c, jnp.zeros((2, SUBLANES, D_RNN), F32))
    gate_x, u_x, x_t = _lru_in_call(x3, mod_x1, g1, w_in, conv_w, conv_b, need_gate=True)
    yf, yb, _ = scan(u_x, h_ctx)
    return _post_call(x_t, [gate_x, yf, yb], mod_x1, g1, lru_w_out[0].astype(BF16),
                      w1_all, w2_all, 1, n_lat, lru=True)
```
